```python
import jax, jax.numpy as jnp
from jax import lax
import numpy as np

D_MODEL = 1024
BATCH = 2
SEQ = 8192
DEPTH = 2

GRID_W = 64
CTX_LEN = 256

ATTN_HEAD_DIM = 64
ATTN_HEADS = 8
ATTN_KV_HEADS = 2
ATTN_Q_PER_KV = ATTN_HEADS // ATTN_KV_HEADS
ATTN_WIDTH = ATTN_HEADS * ATTN_HEAD_DIM
ATTN_KV_WIDTH = ATTN_KV_HEADS * ATTN_HEAD_DIM
Q_BLOCK = 128
ROPE_THETA = 10000.0
ROPE_AXIS_DIM = ATTN_HEAD_DIM // 2

GLA_HEADS = 4
GLA_DK = 64
GLA_DV = 128
GLA_K_WIDTH = GLA_HEADS * GLA_DK
GLA_V_WIDTH = GLA_HEADS * GLA_DV
GLA_GATE_RANK = 16
GLA_GATE_TEMP = 16.0
GLA_CHUNK = 64

SGU_GROUPS = 4
SGU_CHUNK = 128
SGU_WIDTH = 512
SGU_GROUP_DIM = SGU_WIDTH // SGU_GROUPS

DEEPNORM_ALPHA = (2 * DEPTH) ** 0.25
DEEPNORM_BETA = (8 * DEPTH) ** -0.25
EPS = 1e-6

SPLIT_SIZES = (
    ATTN_WIDTH, ATTN_KV_WIDTH, ATTN_KV_WIDTH, ATTN_WIDTH,
    GLA_K_WIDTH, GLA_K_WIDTH, GLA_V_WIDTH, GLA_GATE_RANK, GLA_GATE_RANK,
    GLA_V_WIDTH,
    SGU_WIDTH, SGU_WIDTH, SGU_WIDTH,
    D_MODEL, D_MODEL, D_MODEL,
)
PROJ_WIDTH = sum(SPLIT_SIZES)

kernel_name = "hybrid_gqa_gla_sgu_prefix_trunk"


def _split_points():
    pts, acc = [], 0
    for s in SPLIT_SIZES[:-1]:
        acc += s
        pts.append(acc)
    return pts


def layer_norm(x, g, b):
    xf = x.astype(jnp.float32)
    mu = jnp.mean(xf, axis=-1, keepdims=True)
    var = jnp.mean(jnp.square(xf - mu), axis=-1, keepdims=True)
    return ((xf - mu) * lax.rsqrt(var + EPS) * g + b).astype(x.dtype)


def rms_norm(x, g):
    xf = x.astype(jnp.float32)
    y = xf * lax.rsqrt(jnp.mean(jnp.square(xf), axis=-1, keepdims=True) + EPS)
    return (y * g).astype(x.dtype)


def axial_rope_angles(rows):
    row = jnp.repeat(jnp.arange(rows, dtype=jnp.float32), GRID_W)
    col = jnp.tile(jnp.arange(GRID_W, dtype=jnp.float32), rows)
    freqs = ROPE_THETA ** (-jnp.arange(0, ROPE_AXIS_DIM, 2, dtype=jnp.float32) / ROPE_AXIS_DIM)
    return row[:, None] * freqs, col[:, None] * freqs


def rope_1d(x, ang):
    m = ang.shape[-1]
    cos = jnp.cos(ang)[None, :, None, :]
    sin = jnp.sin(ang)[None, :, None, :]
    x1, x2 = x[..., :m], x[..., m:]
    return jnp.concatenate([x1 * cos - x2 * sin, x1 * sin + x2 * cos], axis=-1)


def apply_axial_rope(x, ang_r, ang_c):
    xf = x.astype(jnp.float32)
    out = jnp.concatenate([rope_1d(xf[..., :ROPE_AXIS_DIM], ang_r),
                           rope_1d(xf[..., ROPE_AXIS_DIM:], ang_c)], axis=-1)
    return out.astype(x.dtype)


def gqa_project(pq, pk, pv, q_norm, k_norm):
    B, N, _ = pq.shape
    q = rms_norm(pq.reshape(B, N, ATTN_HEADS, ATTN_HEAD_DIM), q_norm)
    k = rms_norm(pk.reshape(B, N, ATTN_KV_HEADS, ATTN_HEAD_DIM), k_norm)
    v = pv.reshape(B, N, ATTN_KV_HEADS, ATTN_HEAD_DIM)
    return q, k, v


def attend(q, k, v):
    s = jnp.einsum('btgrd,blgd->bgrtl', q, k, preferred_element_type=jnp.float32) * (ATTN_HEAD_DIM ** -0.5)
    p = jax.nn.softmax(s, axis=-1)
    return jnp.einsum('bgrtl,blgd->btgrd', p.astype(v.dtype), v)


def blocked_attention(q, k, v):
    B, N = q.shape[:2]
    nb = N // Q_BLOCK
    qb = q.reshape(B, nb, Q_BLOCK, ATTN_KV_HEADS, ATTN_Q_PER_KV, ATTN_HEAD_DIM)
    qb = jnp.moveaxis(qb, 1, 0)
    ob = lax.map(lambda qi: attend(qi, k, v), qb)
    return jnp.moveaxis(ob, 0, 1).reshape(B, N, ATTN_WIDTH)


def gla_scan(q, k, v, g, s0):
    B, H, N, dk = q.shape
    dv = v.shape[-1]
    nc = N // GLA_CHUNK
    mask = jnp.tril(jnp.ones((GLA_CHUNK, GLA_CHUNK), dtype=bool))

    def chunks(t):
        return jnp.moveaxis(t.reshape(B, H, nc, GLA_CHUNK, t.shape[-1]), 2, 0)

    def step(s, inp):
        qc, kc, vc, gc = inp
        b = jnp.cumsum(gc, axis=2)
        b_last = b[:, :, -1:, :]
        inter = jnp.einsum('bhtd,bhde->bhte', qc * jnp.exp(b), s)
        rel = jnp.where(mask[:, :, None], b[:, :, :, None, :] - b[:, :, None, :, :], -jnp.inf)
        att = jnp.einsum('bhtd,bhsd,bhtsd->bhts', qc, kc, jnp.exp(rel))
        o = inter + jnp.einsum('bhts,bhse->bhte', att, vc)
        s_new = jnp.exp(b_last[:, :, 0, :])[..., None] * s + \
            jnp.einsum('bhsd,bhse->bhde', kc * jnp.exp(b_last - b), vc)
        return s_new, o

    s_fin, o = lax.scan(step, s0, (chunks(q), chunks(k), chunks(v), chunks(g)))
    return jnp.moveaxis(o, 0, 2).reshape(B, H, N, dv), s_fin


def gla_heads(pq, pk, pv, paf, pab, a2_f, ab_f, a2_b, ab_b):
    B, N, _ = pq.shape

    def heads(t, d):
        return t.reshape(B, N, GLA_HEADS, d).transpose(0, 2, 1, 3).astype(jnp.float32)

    def log_decay(pa, a2, ab):
        z = (pa @ a2 + ab).astype(jnp.float32)
        return heads(jax.nn.log_sigmoid(z) / GLA_GATE_TEMP, GLA_DK)

    q = heads(pq, GLA_DK) * (GLA_DK ** -0.5)
    k = heads(pk, GLA_DK)
    v = heads(pv, GLA_DV)
    return q, k, v, log_decay(paf, a2_f, ab_f), log_decay(pab, a2_b, ab_b)


def gla_finish(o, o_norm, dtype):
    B, H, N, dv = o.shape
    o = rms_norm(o.transpose(0, 2, 1, 3), o_norm)
    return o.reshape(B, N, H * dv).astype(dtype)


def sgu(pu, pv, ln_g, ln_b, w_s, b_s):
    B, N, _ = pu.shape
    nch = N // SGU_CHUNK
    vn = layer_norm(pv, ln_g, ln_b).reshape(B, nch, SGU_CHUNK, SGU_GROUPS, SGU_GROUP_DIM)
    mixed = jnp.einsum('gts,bnsgc->bntgc', w_s, vn) + b_s.T[None, None, :, :, None]
    return pu * mixed.reshape(B, N, SGU_WIDTH)


def trunk_layer(x, ctx, c, c_ctx, p, ang_r, ang_c, need_ctx):
    B = x.shape[0]
    shift, scale, gate = jnp.split(jax.nn.silu(c) @ p['w_ada'] + p['b_ada'], 3, axis=-1)
    shift_c, scale_c, gate_c = jnp.split(jax.nn.silu(c_ctx) @ p['w_ada'] + p['b_ada'], 3, axis=-1)
    h = x * (1.0 + scale[:, None, :]) + shift[:, None, :]
    hc = ctx * (1.0 + scale_c) + shift_c

    pts = _split_points()
    (a_q, a_k, a_v, a_g, g_q, g_k, g_v, g_af, g_ab, g_g,
     s_u, s_v, s_g, m_a, m_g, m_s) = jnp.split(h @ p['w_in'], pts, axis=-1)
    (ca_q, ca_k, ca_v, ca_g, cg_q, cg_k, cg_v, cg_af, cg_ab, cg_g,
     cs_u, cs_v, cs_g, cm_a, cm_g, cm_s) = jnp.split(hc @ p['w_in'], pts, axis=-1)

    ql, kl, vl = gqa_project(a_q, a_k, a_v, p['q_norm'], p['k_norm'])
    ql = apply_axial_rope(ql, ang_r, ang_c)
    kl = apply_axial_rope(kl, ang_r, ang_c)
    qc, kc, vc = gqa_project(ca_q, ca_k, ca_v, p['q_norm'], p['k_norm'])
    k_all = jnp.concatenate([kc, kl], axis=1)
    v_all = jnp.concatenate([vc, vl], axis=1)
    y_attn = blocked_attention(ql, k_all, v_all) * jax.nn.silu(a_g)

    qlg, klg, vlg, gfl, gbl = gla_heads(g_q, g_k, g_v, g_af, g_ab, p['a2_f'], p['ab_f'], p['a2_b'], p['ab_b'])
    qcg, kcg, vcg, gfc, gbc = gla_heads(cg_q, cg_k, cg_v, cg_af, cg_ab, p['a2_f'], p['ab_f'], p['a2_b'], p['ab_b'])
    zeros = jnp.zeros((B, GLA_HEADS, GLA_DK, GLA_DV), jnp.float32)
    flip = lambda t: jnp.flip(t, axis=2)
    oc_f, s_f = gla_scan(qcg, kcg, vcg, gfc, zeros)
    oc_b, s_b = gla_scan(flip(qcg), flip(kcg), flip(vcg), flip(gbc), zeros)
    ol_f, _ = gla_scan(qlg, klg, vlg, gfl, s_f)
    ol_b, _ = gla_scan(flip(qlg), flip(klg), flip(vlg), flip(gbl), s_b)
    y_gla = gla_finish(ol_f + flip(ol_b), p['gla_o_norm'], x.dtype) * jax.nn.silu(g_g)

    y_sgu = sgu(s_u, s_v, p['sgu_ln_g'], p['sgu_ln_b'], p['sgu_w'], p['sgu_b']) * jax.nn.silu(s_g)

    def merge(ya, yg, ys, ga, gg, gs):
        m = (jax.nn.sigmoid(ga) * (ya @ p['w_br_attn'])
             + jax.nn.sigmoid(gg) * (yg @ p['w_br_gla'])
             + jax.nn.sigmoid(gs) * (ys @ p['w_br_sgu']))
        return m @ p['w_out']

    out = merge(y_attn, y_gla, y_sgu, m_a, m_g, m_s)
    x_new = layer_norm(DEEPNORM_ALPHA * x + gate[:, None, :] * out, p['post_g'], p['post_b'])

    if not need_ctx:
        return x_new, ctx

    L = ctx.shape[1]
    yc_attn = attend(qc.reshape(B, L, ATTN_KV_HEADS, ATTN_Q_PER_KV, ATTN_HEAD_DIM), kc, vc)
    yc_attn = yc_attn.reshape(B, L, ATTN_WIDTH) * jax.nn.silu(ca_g)
    yc_gla = gla_finish(oc_f + flip(oc_b), p['gla_o_norm'], ctx.dtype) * jax.nn.silu(cg_g)
    yc_sgu = sgu(cs_u, cs_v, p['sgu_ln_g'], p['sgu_ln_b'], p['sgu_w'], p['sgu_b']) * jax.nn.silu(cs_g)
    out_c = merge(yc_attn, yc_gla, yc_sgu, cm_a, cm_g, cm_s)
    ctx_new = layer_norm(DEEPNORM_ALPHA * ctx + gate_c * out_c, p['post_g'], p['post_b'])
    return x_new, ctx_new


def setup_inputs(seed: int = 0) -> dict:
    key = jax.random.key(seed)
    ks = iter(jax.random.split(key, 32))

    def nrm(shape, s):
        return jax.random.normal(next(ks), shape, jnp.float32) * s

    L, D = DEPTH, D_MODEL
    return {
        "x": nrm((BATCH, SEQ, D), 1.0),
        "c": nrm((BATCH, D), 1.0),
        "ctx": nrm((BATCH, CTX_LEN, D), 1.0),
        "c_ctx": nrm((D,), 1.0),
        "w_ada": nrm((L, D, 3 * D), 0.5 * D ** -0.5),
        "b_ada": nrm((L, 3 * D), 0.02),
        "w_in": nrm((L, D, PROJ_WIDTH), D ** -0.5),
        "attn_q_norm": 1.0 + nrm((L, ATTN_HEAD_DIM), 0.02),
        "attn_k_norm": 1.0 + nrm((L, ATTN_HEAD_DIM), 0.02),
        "gla_a2_f": nrm((L, GLA_GATE_RANK, GLA_K_WIDTH), GLA_GATE_RANK ** -0.5),
        "gla_ab_f": nrm((L, GLA_K_WIDTH), 0.1),
        "gla_a2_b": nrm((L, GLA_GATE_RANK, GLA_K_WIDTH), GLA_GATE_RANK ** -0.5),
        "gla_ab_b": nrm((L, GLA_K_WIDTH), 0.1),
        "gla_o_norm": 1.0 + nrm((L, GLA_DV), 0.02),
        "sgu_ln_g": 1.0 + nrm((L, SGU_WIDTH), 0.02),
        "sgu_ln_b": nrm((L, SGU_WIDTH), 0.02),
        "sgu_w": nrm((L, SGU_GROUPS, SGU_CHUNK, SGU_CHUNK), SGU_CHUNK ** -0.5),
        "sgu_b": 1.0 + nrm((L, SGU_GROUPS, SGU_CHUNK), 0.02),
        "w_br_attn": nrm((L, ATTN_WIDTH, D), ATTN_WIDTH ** -0.5 * DEEPNORM_BETA),
        "w_br_gla": nrm((L, GLA_V_WIDTH, D), GLA_V_WIDTH ** -0.5 * DEEPNORM_BETA),
        "w_br_sgu": nrm((L, SGU_WIDTH, D), SGU_WIDTH ** -0.5 * DEEPNORM_BETA),
        "w_out": nrm((L, D, D), D ** -0.5 * DEEPNORM_BETA),
        "post_ln_g": 1.0 + nrm((L, D), 0.02),
        "post_ln_b": nrm((L, D), 0.02),
    }


def reference(x, c, ctx, c_ctx, w_ada, b_ada, w_in, attn_q_norm, attn_k_norm,
              gla_a2_f, gla_ab_f, gla_a2_b, gla_ab_b, gla_o_norm,
              sgu_ln_g, sgu_ln_b, sgu_w, sgu_b,
              w_br_attn, w_br_gla, w_br_sgu, w_out, post_ln_g, post_ln_b):
    n_lat = x.shape[1]
    rows = n_lat // GRID_W
    ang_r, ang_c = axial_rope_angles(rows)
    for i in range(DEPTH):
        p = {
            'w_ada': w_ada[i], 'b_ada': b_ada[i], 'w_in': w_in[i],
            'q_norm': attn_q_norm[i], 'k_norm': attn_k_norm[i],
            'a2_f': gla_a2_f[i], 'ab_f': gla_ab_f[i], 'a2_b': gla_a2_b[i], 'ab_b': gla_ab_b[i],
            'gla_o_norm': gla_o_norm[i],
            'sgu_ln_g': sgu_ln_g[i], 'sgu_ln_b': sgu_ln_b[i], 'sgu_w': sgu_w[i], 'sgu_b': sgu_b[i],
            'w_br_attn': w_br_attn[i], 'w_br_gla': w_br_gla[i], 'w_br_sgu': w_br_sgu[i],
            'w_out': w_out[i], 'post_g': post_ln_g[i], 'post_b': post_ln_b[i],
        }
        x, ctx = trunk_layer(x, ctx, c, c_ctx, p, ang_r, ang_c, need_ctx=(i < DEPTH - 1))
    return x
```

```python
import functools

import jax
import jax.numpy as jnp
from jax import lax
from jax.experimental import pallas as pl
from jax.experimental.pallas import tpu as pltpu

F32 = jnp.float32
BF16 = jnp.bfloat16

D_MODEL = 1024
GRID_W = 64
HEAD_DIM = 64
N_HEADS = 8
N_KV = 2
Q_PER_KV = N_HEADS // N_KV
ATTN_W = N_HEADS * HEAD_DIM
KV_W = N_KV * HEAD_DIM
ROPE_THETA = 10000.0
ROPE_AXIS = HEAD_DIM // 2
GLA_HEADS = 4
GLA_DK = 64
GLA_DV = 128
GLA_KW = GLA_HEADS * GLA_DK
GLA_VW = GLA_HEADS * GLA_DV
GLA_RANK = 16
GLA_TEMP = 16.0
SGU_GROUPS = 4
SGU_CHUNK = 128
SGU_W = 512
EPS = 1e-6

LANES = 128
V7X_VMEM_BYTES = 64 * 1024 * 1024

COL_MA, COL_MG, COL_MS = 0, 1024, 2048
COL_AQ, COL_AG = 3072, 3584
COL_GV, COL_GG = 4096, 4608
COL_SU, COL_SV, COL_SG = 5120, 5632, 6144
COL_GQK = 6656
COL_AKV = 7168
COL_GATE = 7424
PROJ_W = 7680

ATT_TQ = 256
ATT_TK = 256
GLA_C = 64
GLA_SUB = 16
V_ROWS = 80


def _cparams(sem, vmem_mb):
    return pltpu.CompilerParams(dimension_semantics=sem,
                                vmem_limit_bytes=min(vmem_mb * 1024 * 1024, V7X_VMEM_BYTES - (4 << 20)))


def _silu(x):
    return x * jax.nn.sigmoid(x)


def _dot(a, b):
    return jnp.dot(a, b, preferred_element_type=F32)


def _dot_nt(a, b):
    return lax.dot_general(a, b, (((1,), (1,)), ((), ())), preferred_element_type=F32)


def _ada_kernel(c_ref, w_ref, b_ref, o_ref):
    c = c_ref[...]
    o_ref[...] = _dot(_silu(c).astype(BF16), w_ref[...].astype(BF16)) + b_ref[...]


def _ada(cc, w_ada, b_ada):
    L, D = w_ada.shape[0], w_ada.shape[1]
    return pl.pallas_call(
        _ada_kernel,
        grid=(L, 3),
        in_specs=[pl.BlockSpec((8, D), lambda l, j: (0, 0)),
                  pl.BlockSpec((None, D, D), lambda l, j: (l, 0, j)),
                  pl.BlockSpec((None, 1, D), lambda l, j: (l, 0, j))],
        out_specs=pl.BlockSpec((None, 8, D), lambda l, j: (l, 0, j)),
        out_shape=jax.ShapeDtypeStruct((L, 8, 3 * D), F32),
        compiler_params=_cparams(("arbitrary", "arbitrary"), 32),
        name="ada_mod",
    )(cc, w_ada, b_ada)


def _inproj_kernel(x_ref, mod_ref, w_ref, o_ref, h_ref, *, tm, ctx_len):
    i = pl.program_id(1)
    j = pl.program_id(2)

    @pl.when(j == 0)
    def _():
        D = x_ref.shape[-1]
        row = i * tm + lax.broadcasted_iota(jnp.int32, (tm, 1), 0)
        is_ctx = row < ctx_len
        shift = jnp.where(is_ctx, mod_ref[1:2, 0:D], mod_ref[0:1, 0:D])
        scale = jnp.where(is_ctx, mod_ref[1:2, D:2 * D], mod_ref[0:1, D:2 * D])
        h_ref[...] = (x_ref[...] * (1.0 + scale) + shift).astype(BF16)

    o_ref[...] = _dot(h_ref[...], w_ref[...]).astype(BF16)


def _inproj(xall, modb, w, ctx_len, tm, tn):
    B, T, D = xall.shape
    return pl.pallas_call(
        functools.partial(_inproj_kernel, tm=tm, ctx_len=ctx_len),
        grid=(B, T // tm, PROJ_W // tn),
        in_specs=[pl.BlockSpec((None, tm, D), lambda b, i, j: (b, i, 0)),
                  pl.BlockSpec((None, 8, 3 * D), lambda b, i, j: (b, 0, 0)),
                  pl.BlockSpec((D, tn), lambda b, i, j: (0, j))],
        out_specs=pl.BlockSpec((None, tm, tn), lambda b, i, j: (b, i, j)),
        out_shape=jax.ShapeDtypeStruct((B, T, PROJ_W), BF16),
        scratch_shapes=[pltpu.VMEM((tm, D), BF16)],
        compiler_params=_cparams(("arbitrary", "arbitrary", "arbitrary"), 40),
        name="in_proj",
    )(xall, modb, w)


def _rope(y, cos, sin, width):
    lane = lax.broadcasted_iota(jnp.int32, (1, width), 1)
    first = jnp.bitwise_and(lane, ROPE_AXIS - 1) < (ROPE_AXIS // 2)
    up = pltpu.roll(y, width - ROPE_AXIS // 2, 1)
    dn = pltpu.roll(y, ROPE_AXIS // 2, 1)
    return y * cos + jnp.where(first, up, dn) * sin


def _qkprep_kernel(aq_ref, akv_ref, cos_ref, sin_ref, qn_ref, kn_ref, gq_ref, gk_ref,
                   eq_ref, ek_ref, qt_ref, k_ref, vt_ref, *, tm):
    cos = cos_ref[...]
    sin = sin_ref[...]
    aq = aq_ref[...].astype(F32)
    ssq = _dot((aq * aq).astype(BF16), gq_ref[...])
    yq = aq * lax.rsqrt(ssq * (1.0 / HEAD_DIM) + EPS) * jnp.concatenate([qn_ref[...]] * 4, axis=1)
    yq = _rope(yq, jnp.concatenate([cos] * 4, axis=1), jnp.concatenate([sin] * 4, axis=1), ATTN_W)
    yq = (yq * (HEAD_DIM ** -0.5)).astype(BF16)
    qt = _dot_nt(eq_ref[...], yq).astype(BF16)
    qt_ref[...] = qt.reshape(N_HEADS, HEAD_DIM, tm)
    akv = akv_ref[...].astype(F32)
    ak = akv[:, 0:KV_W]
    ssk = _dot((ak * ak).astype(BF16), gk_ref[...])
    yk = ak * lax.rsqrt(ssk * (1.0 / HEAD_DIM) + EPS) * kn_ref[...]
    yk = _rope(yk, cos, sin, KV_W).astype(BF16)
    nkb = tm // ATT_TK
    k_ref[...] = yk.reshape(nkb, ATT_TK, KV_W)
    av = akv_ref[:, KV_W:2 * KV_W]
    vt = _dot_nt(ek_ref[...], av).astype(BF16)
    ones = jnp.ones((V_ROWS - HEAD_DIM, ATT_TK), BF16)
    for g in range(N_KV):
        for n in range(nkb):
            vt_ref[g, n, 0:HEAD_DIM, :] = vt[g * HEAD_DIM:(g + 1) * HEAD_DIM, n * ATT_TK:(n + 1) * ATT_TK]
            vt_ref[g, n, HEAD_DIM:V_ROWS, :] = ones


def _qkprep(proj, cos_t, sin_t, qn, kn, consts, tm):
    B, T, _ = proj.shape
    nkb_t = T // ATT_TK
    nkb = tm // ATT_TK
    full = lambda shape: pl.BlockSpec(shape, lambda b, i: (0,) * len(shape))
    return pl.pallas_call(
        functools.partial(_qkprep_kernel, tm=tm),
        grid=(B, T // tm),
        in_specs=[pl.BlockSpec((None, tm, ATTN_W), lambda b, i: (b, i, COL_AQ // ATTN_W)),
                  pl.BlockSpec((None, tm, 2 * KV_W), lambda b, i: (b, i, COL_AKV // (2 * KV_W))),
                  pl.BlockSpec((tm, LANES), lambda b, i: (i, 0)),
                  pl.BlockSpec((tm, LANES), lambda b, i: (i, 0)),
                  full((1, LANES)), full((1, LANES)),
                  full((ATTN_W, ATTN_W)), full((KV_W, KV_W)),
                  full((ATTN_W, ATTN_W)), full((KV_W, KV_W))],
        out_specs=[pl.BlockSpec((None, N_HEADS, HEAD_DIM, tm), lambda b, i: (b, 0, 0, i)),
                   pl.BlockSpec((None, nkb, ATT_TK, KV_W), lambda b, i: (b, i, 0, 0)),
                   pl.BlockSpec((None, N_KV, nkb, V_ROWS, ATT_TK), lambda b, i: (b, 0, i, 0, 0))],
        out_shape=[jax.ShapeDtypeStruct((B, N_HEADS, HEAD_DIM, T), BF16),
                   jax.ShapeDtypeStruct((B, nkb_t, ATT_TK, KV_W), BF16),
                   jax.ShapeDtypeStruct((B, N_KV, nkb_t, V_ROWS, ATT_TK), BF16)],
        compiler_params=_cparams(("arbitrary", "arbitrary"), 40),
        name="qkv_prep",
    )(proj, proj, cos_t, sin_t, qn, kn, consts["gq"], consts["gk"], consts["eq"], consts["ek"])


def _attn_kernel(qt_ref, k_ref, vt_ref, ag_ref, eye_ref, o_ref, oall_ref, *, nkb, ctx_blocks):
    g = pl.program_id(1)
    i = pl.program_id(2)
    tq = qt_ref.shape[-1]
    nk = jnp.where(i < ctx_blocks, ctx_blocks, nkb)
    sub = lax.broadcasted_iota(jnp.int32, (KV_W, 1), 0)
    own = (sub >= g * HEAD_DIM) & (sub < (g + 1) * HEAD_DIM)
    for r in range(Q_PER_KV):
        q = qt_ref[r]
        qpad = jnp.where(own, jnp.concatenate([q, q], axis=0), jnp.zeros((), BF16))

        def body(j, carry):
            m, acc = carry
            s = _dot(k_ref[j], qpad)
            m_new = jnp.maximum(m, jnp.max(s, axis=0, keepdims=True))
            p = jnp.exp(s - m_new)
            alpha = jnp.exp(m - m_new)
            acc = alpha * acc + _dot(vt_ref[j], p.astype(BF16))
            return m_new, acc

        m0 = jnp.full((1, tq), -jnp.inf, F32)
        acc0 = jnp.zeros((V_ROWS, tq), F32)
        _, acc = lax.fori_loop(0, nk, body, (m0, acc0))
        o = acc[0:HEAD_DIM, :] * (1.0 / acc[HEAD_DIM:HEAD_DIM + 1, :])
        oall_ref[r * HEAD_DIM:(r + 1) * HEAD_DIM, :] = o.astype(BF16)
    o_t = _dot_nt(eye_ref[...], oall_ref[...])
    o_ref[...] = (o_t * _silu(ag_ref[...].astype(F32))).astype(BF16)


def _attention(qt, k, vt, proj, eye_tq, ctx_len):
    B, T = proj.shape[0], proj.shape[1]
    nkb = T // ATT_TK
    gw = Q_PER_KV * HEAD_DIM
    return pl.pallas_call(
        functools.partial(_attn_kernel, nkb=nkb, ctx_blocks=ctx_len // ATT_TK),
        grid=(B, N_KV, T // ATT_TQ),
        in_specs=[pl.BlockSpec((None, Q_PER_KV, HEAD_DIM, ATT_TQ), lambda b, g, i: (b, g, 0, i)),
                  pl.BlockSpec((None, nkb, ATT_TK, KV_W), lambda b, g, i: (b, 0, 0, 0)),
                  pl.BlockSpec((None, None, nkb, V_ROWS, ATT_TK), lambda b, g, i: (b, g, 0, 0, 0)),
                  pl.BlockSpec((None, ATT_TQ, gw), lambda b, g, i: (b, i, COL_AG // gw + g)),
                  pl.BlockSpec((ATT_TQ, ATT_TQ), lambda b, g, i: (0, 0))],
        out_specs=pl.BlockSpec((None, ATT_TQ, gw), lambda b, g, i: (b, i, g)),
        out_shape=jax.ShapeDtypeStruct((B, T, ATTN_W), BF16),
        scratch_shapes=[pltpu.VMEM((gw, ATT_TQ), BF16)],
        compiler_params=_cparams(("arbitrary", "arbitrary", "arbitrary"), 40),
        name="gqa_attention",
    )(qt, k, vt, proj, eye_tq)


def _gla_direction(qk, v, pa, a2, ab, st_ref, eye_v, fwd):
    C, c = GLA_C, GLA_SUB
    q = qk[:, 0:GLA_KW].astype(F32) * (GLA_DK ** -0.5)
    k = qk[:, GLA_KW:2 * GLA_KW].astype(F32)
    z = _dot(pa, a2) + ab
    g = (jnp.minimum(z, 0.0) - jnp.log(1.0 + jnp.exp(-jnp.abs(z)))) * (1.0 / GLA_TEMP)
    row = lax.broadcasted_iota(jnp.int32, (C, C), 0)
    col = lax.broadcasted_iota(jnp.int32, (C, C), 1)
    blk = jnp.bitwise_and(row, -c)
    if fwd:
        tri = col <= row
        inblk = tri & (col >= blk)
    else:
        tri = col >= row
        inblk = tri & (col < blk + c)
    sel = jnp.concatenate([jnp.where(tri, 1.0, 0.0), jnp.where(inblk, 1.0, 0.0)], axis=0).astype(BF16)
    g_hi = g.astype(BF16)
    g_lo = (g - g_hi.astype(F32)).astype(BF16)
    cs = _dot(sel, jnp.concatenate([g_hi, g_lo], axis=1))
    cs = cs[:, 0:GLA_KW] + cs[:, GLA_KW:2 * GLA_KW]
    b = cs[0:C]
    d = cs[C:2 * C]
    ref = b - d
    total = b[C - 1:C, :] if fwd else b[0:1, :]
    q_in = (q * jnp.exp(d)).astype(BF16)
    q_st = (q * jnp.exp(b)).astype(BF16)
    k_out = (k * jnp.exp(total - b)).astype(BF16)
    w_tot = jnp.exp(total)
    rowc = lax.broadcasted_iota(jnp.int32, (C, 1), 0)
    lane = lax.broadcasted_iota(jnp.int32, (1, LANES), 1)
    lo_half = lane < GLA_DK
    zero = jnp.zeros((), BF16)
    k_blk = []
    for i in range(C // c):
        valid = (rowc < (i + 1) * c) if fwd else (rowc >= i * c)
        e = jnp.where(valid, ref[i * c:i * c + 1, :] - b, 0.0)
        k_blk.append((k * jnp.exp(e)).astype(BF16))
    outs = []
    for h in range(GLA_HEADS):
        pr, half = divmod(h, 2)
        hmask = lo_half if half == 0 else jnp.logical_not(lo_half)
        ls = slice(pr * LANES, (pr + 1) * LANES)
        a_rows = []
        for i in range(C // c):
            qm = jnp.where(hmask, q_in[i * c:(i + 1) * c, ls], zero)
            a_rows.append(_dot_nt(qm, k_blk[i][:, ls]))
        a = jnp.concatenate(a_rows, axis=0)
        a = jnp.where(tri, a, 0.0).astype(BF16)
        vh = v[:, h * GLA_DV:(h + 1) * GLA_DV]
        qs = jnp.where(hmask, q_st[:, ls], zero)
        o = _dot(a, vh) + _dot_nt(qs, st_ref[pr].astype(BF16))
        outs.append(o)
    for pr in range(GLA_HEADS // 2):
        ls = slice(pr * LANES, (pr + 1) * LANES)
        upd = []
        for half in range(2):
            h = 2 * pr + half
            vt = _dot_nt(eye_v, v[:, h * GLA_DV:(h + 1) * GLA_DV]).astype(BF16)
            upd.append(_dot(vt, k_out[:, ls]))
        st_ref[pr] = st_ref[pr] * w_tot[:, ls] + jnp.where(lo_half, upd[0], upd[1])
    return jnp.concatenate(outs, axis=1)


def _gla_kernel(qkf_ref, vf_ref, paf_ref, qkb_ref, vb_ref, pab_ref, a2f_ref, abf_ref, a2b_ref, abb_ref,
                eye_ref, of_ref, ob_ref, stf_ref, stb_ref):
    @pl.when(pl.program_id(1) == 0)
    def _():
        stf_ref[...] = jnp.zeros_like(stf_ref)
        stb_ref[...] = jnp.zeros_like(stb_ref)

    eye_v = eye_ref[...]
    of_ref[...] = _gla_direction(qkf_ref[...], vf_ref[...], paf_ref[...], a2f_ref[...], abf_ref[...],
                                 stf_ref, eye_v, True)
    ob_ref[...] = _gla_direction(qkb_ref[...], vb_ref[...], pab_ref[...], a2b_ref[...], abb_ref[...],
                                 stb_ref, eye_v, False)


def _gla(proj, a2f, abf, a2b, abb, eye_v, ctx_len):
    B, T = proj.shape[0], proj.shape[1]
    C = GLA_C
    nch, nctx = T // C, ctx_len // C

    def cb(s):
        return jnp.where(s < nctx, nctx - 1 - s, nch + nctx - 1 - s)

    full = lambda shape: pl.BlockSpec(shape, lambda b, s: (0,) * len(shape))
    return pl.pallas_call(
        _gla_kernel,
        grid=(B, nch),
        in_specs=[pl.BlockSpec((None, C, 2 * GLA_KW), lambda b, s: (b, s, COL_GQK // (2 * GLA_KW))),
                  pl.BlockSpec((None, C, GLA_VW), lambda b, s: (b, s, COL_GV // GLA_VW)),
                  pl.BlockSpec((None, C, LANES), lambda b, s: (b, s, COL_GATE // LANES)),
                  pl.BlockSpec((None, C, 2 * GLA_KW), lambda b, s: (b, cb(s), COL_GQK // (2 * GLA_KW))),
                  pl.BlockSpec((None, C, GLA_VW), lambda b, s: (b, cb(s), COL_GV // GLA_VW)),
                  pl.BlockSpec((None, C, LANES), lambda b, s: (b, cb(s), COL_GATE // LANES)),
                  full((LANES, GLA_KW)), full((1, GLA_KW)), full((LANES, GLA_KW)), full((1, GLA_KW)),
                  full((GLA_DV, GLA_DV))],
        out_specs=[pl.BlockSpec((None, C, GLA_VW), lambda b, s: (b, s, 0)),
                   pl.BlockSpec((None, C, GLA_VW), lambda b, s: (b, cb(s), 0))],
        out_shape=[jax.ShapeDtypeStruct((B, T, GLA_VW), F32),
                   jax.ShapeDtypeStruct((B, T, GLA_VW), F32)],
        scratch_shapes=[pltpu.VMEM((GLA_HEADS // 2, GLA_DV, LANES), F32),
                        pltpu.VMEM((GLA_HEADS // 2, GLA_DV, LANES), F32)],
        compiler_params=_cparams(("arbitrary", "arbitrary"), 32),
        name="gla_scan",
    )(proj, proj, proj, proj, proj, proj, a2f, abf, a2b, abb, eye_v)


def _merge_kernel(x_ref, mod_ref, ma_ref, mg_ref, ms_ref, su_ref, sv_ref, sg_ref, gg_ref, ya_ref, of_ref, ob_ref,
                  onorm_ref, lng_ref, lnb_ref, sw_ref, sb_ref, wa_ref, wg_ref, ws_ref, wo_ref, pg_ref, pb_ref,
                  o_ref, *, tm, ctx_tiles, row_off, alpha):
    D = x_ref.shape[-1]
    i = pl.program_id(1) + row_off
    og = of_ref[...] + ob_ref[...]
    parts = []
    for h in range(GLA_HEADS):
        oh = og[:, h * GLA_DV:(h + 1) * GLA_DV]
        ms = jnp.mean(oh * oh, axis=-1, keepdims=True)
        parts.append(oh * lax.rsqrt(ms + EPS))
    yg = jnp.concatenate(parts, axis=1) * onorm_ref[...] * _silu(gg_ref[...].astype(F32))
    sv = sv_ref[...].astype(F32)
    mu = jnp.mean(sv, axis=-1, keepdims=True)
    dv = sv - mu
    var = jnp.mean(dv * dv, axis=-1, keepdims=True)
    vn = (dv * lax.rsqrt(var + EPS) * lng_ref[...] + lnb_ref[...]).astype(BF16)
    gw = SGU_W // SGU_GROUPS
    rows = []
    for n in range(tm // SGU_CHUNK):
        cols = []
        for g in range(SGU_GROUPS):
            blk = vn[n * SGU_CHUNK:(n + 1) * SGU_CHUNK, g * gw:(g + 1) * gw]
            cols.append(_dot(sw_ref[g], blk))
        rows.append(jnp.concatenate(cols, axis=1) + sb_ref[...])
    mixed = jnp.concatenate(rows, axis=0)
    ys = su_ref[...].astype(F32) * mixed * _silu(sg_ref[...].astype(F32))
    m = (jax.nn.sigmoid(ma_ref[...].astype(F32)) * _dot(ya_ref[...], wa_ref[...])
         + jax.nn.sigmoid(mg_ref[...].astype(F32)) * _dot(yg.astype(BF16), wg_ref[...])
         + jax.nn.sigmoid(ms_ref[...].astype(F32)) * _dot(ys.astype(BF16), ws_ref[...]))
    out = _dot(m.astype(BF16), wo_ref[...])
    gate = jnp.where(i < ctx_tiles, mod_ref[1:2, 2 * D:3 * D], mod_ref[0:1, 2 * D:3 * D])
    r = alpha * x_ref[...] + gate * out
    mu = jnp.mean(r, axis=-1, keepdims=True)
    dr = r - mu
    var = jnp.mean(dr * dr, axis=-1, keepdims=True)
    o_ref[...] = dr * lax.rsqrt(var + EPS) * pg_ref[...] + pb_ref[...]


def _merge(xall, modb, proj, ya, of, ob, p, ctx_len, tm, skip_ctx, alpha):
    B, T, D = xall.shape
    row_off = ctx_len // tm if skip_ctx else 0
    nt = T // tm - row_off
    rows = lambda w, c: pl.BlockSpec((None, tm, w), lambda b, i: (b, i + row_off, c))
    full = lambda shape: pl.BlockSpec(shape, lambda b, i: (0,) * len(shape))
    return pl.pallas_call(
        functools.partial(_merge_kernel, tm=tm, ctx_tiles=ctx_len // tm, row_off=row_off, alpha=alpha),
        grid=(B, nt),
        in_specs=[rows(D, 0),
                  pl.BlockSpec((None, 8, 3 * D), lambda b, i: (b, 0, 0)),
                  rows(D, COL_MA // D), rows(D, COL_MG // D), rows(D, COL_MS // D),
                  rows(SGU_W, COL_SU // SGU_W), rows(SGU_W, COL_SV // SGU_W), rows(SGU_W, COL_SG // SGU_W),
                  rows(GLA_VW, COL_GG // GLA_VW),
                  rows(ATTN_W, 0), rows(GLA_VW, 0), rows(GLA_VW, 0),
                  full((1, GLA_VW)), full((1, SGU_W)), full((1, SGU_W)),
                  full((SGU_GROUPS, SGU_CHUNK, SGU_CHUNK)), full((SGU_CHUNK, SGU_W)),
                  full((ATTN_W, D)), full((GLA_VW, D)), full((SGU_W, D)), full((D, D)),
                  full((1, D)), full((1, D))],
        out_specs=pl.BlockSpec((None, tm, D), lambda b, i: (b, i, 0)),
        out_shape=jax.ShapeDtypeStruct((B, nt * tm, D), F32),
        compiler_params=_cparams(("arbitrary", "arbitrary"), 48),
        name="merge_out",
    )(xall, modb, proj, proj, proj, proj, proj, proj, proj, ya, of, ob,
      p["onorm"], p["lng"], p["lnb"], p["sw"], p["sb"], p["wa"], p["wg"], p["ws"], p["wo"], p["pg"], p["pb"])


def _regroup_w_in(w):
    D = w.shape[0]
    sizes = (512, 128, 128, 512, 256, 256, 512, 16, 16, 512, 512, 512, 512, 1024, 1024, 1024)
    names = ("aq", "ak", "av", "ag", "gq", "gk", "gv", "af", "ab", "gg", "su", "sv", "sg", "ma", "mg", "ms")
    seg, off = {}, 0
    for n, s in zip(names, sizes):
        seg[n] = w[:, off:off + s]
        off += s
    order = ("ma", "mg", "ms", "aq", "ag", "gv", "gg", "su", "sv", "sg", "gq", "gk", "ak", "av", "af", "ab")
    used = sum(seg[n].shape[1] for n in order)
    cols = [seg[n] for n in order] + [jnp.zeros((D, PROJ_W - used), w.dtype)]
    return jnp.concatenate(cols, axis=1).astype(BF16)


def _rope_tables(n_lat, ctx_len):
    rows = n_lat // GRID_W
    row = jnp.repeat(jnp.arange(rows, dtype=F32), GRID_W)
    col = jnp.tile(jnp.arange(GRID_W, dtype=F32), rows)
    freqs = ROPE_THETA ** (-jnp.arange(0, ROPE_AXIS, 2, dtype=F32) / ROPE_AXIS)
    ang_r, ang_c = row[:, None] * freqs, col[:, None] * freqs
    cos = jnp.concatenate([jnp.cos(ang_r)] * 2 + [jnp.cos(ang_c)] * 2, axis=1)
    sin = jnp.concatenate([-jnp.sin(ang_r), jnp.sin(ang_r), -jnp.sin(ang_c), jnp.sin(ang_c)], axis=1)
    cos = jnp.concatenate([jnp.ones((ctx_len, HEAD_DIM), F32), cos], axis=0)
    sin = jnp.concatenate([jnp.zeros((ctx_len, HEAD_DIM), F32), sin], axis=0)
    return jnp.concatenate([cos, cos], axis=1), jnp.concatenate([sin, sin], axis=1)


def _largest_tile(total, cap, mult):
    best = mult
    for t in range(mult, cap + 1, mult):
        if total % t == 0:
            best = t
    return best


def kernel(x, c, ctx, c_ctx, w_ada, b_ada, w_in, attn_q_norm, attn_k_norm, gla_a2_f, gla_ab_f, gla_a2_b, gla_ab_b, gla_o_norm, sgu_ln_g, sgu_ln_b, sgu_w, sgu_b, w_br_attn, w_br_gla, w_br_sgu, w_out, post_ln_g, post_ln_b):
    B, n_lat, D = x.shape
    ctx_len = ctx.shape[1]
    depth = w_in.shape[0]
    T = ctx_len + n_lat
    assert D == D_MODEL and B <= 7
    assert ctx_len % ATT_TK == 0 and n_lat % ATT_TK == 0 and n_lat % GRID_W == 0
    alpha = (2 * depth) ** 0.25

    tm_proj = _largest_tile(T, 1056, 16)
    tn_proj = 768
    tm_prep = _largest_tile(T, 768, ATT_TK)
    tm_merge = ATT_TK

    xall = jnp.concatenate([ctx, x], axis=1)
    cc = jnp.zeros((8, D), F32).at[0:B].set(c).at[B].set(c_ctx)
    mod = _ada(cc, w_ada, b_ada.reshape(depth, 1, 3 * D))
    cos_t, sin_t = _rope_tables(n_lat, ctx_len)
    consts = {
        "gq": jnp.kron(jnp.eye(N_HEADS, dtype=F32), jnp.ones((HEAD_DIM, HEAD_DIM), F32)).astype(BF16),
        "gk": jnp.kron(jnp.eye(N_KV, dtype=F32), jnp.ones((HEAD_DIM, HEAD_DIM), F32)).astype(BF16),
        "eq": jnp.eye(ATTN_W, dtype=BF16),
        "ek": jnp.eye(KV_W, dtype=BF16),
    }
    eye_tq = jnp.eye(ATT_TQ, dtype=BF16)
    eye_v = jnp.eye(GLA_DV, dtype=BF16)

    for l in range(depth):
        last = l == depth - 1
        modb = jnp.zeros((B, 8, 3 * D), F32).at[:, 0].set(mod[l, 0:B]).at[:, 1].set(mod[l, B][None])
        w = _regroup_w_in(w_in[l])
        proj = _inproj(xall, modb, w, ctx_len, tm_proj, tn_proj)
        qn = jnp.concatenate([attn_q_norm[l]] * 2)[None]
        kn = jnp.concatenate([attn_k_norm[l]] * 2)[None]
        qt, kk, vt = _qkprep(proj, cos_t, sin_t, qn, kn, consts, tm_prep)
        ya = _attention(qt, kk, vt, proj, eye_tq, ctx_len)
        a2f = jnp.zeros((LANES, GLA_KW), F32).at[0:GLA_RANK].set(gla_a2_f[l]).astype(BF16)
        a2b = jnp.zeros((LANES, GLA_KW), F32).at[GLA_RANK:2 * GLA_RANK].set(gla_a2_b[l]).astype(BF16)
        of, ob = _gla(proj, a2f, gla_ab_f[l][None], a2b, gla_ab_b[l][None], eye_v, ctx_len)
        p = {
            "onorm": jnp.concatenate([gla_o_norm[l]] * GLA_HEADS)[None],
            "lng": sgu_ln_g[l][None], "lnb": sgu_ln_b[l][None],
            "sw": sgu_w[l].astype(BF16),
            "sb": jnp.repeat(sgu_b[l].T, SGU_W // SGU_GROUPS, axis=1),
            "wa": w_br_attn[l].astype(BF16), "wg": w_br_gla[l].astype(BF16), "ws": w_br_sgu[l].astype(BF16),
            "wo": w_out[l].astype(BF16), "pg": post_ln_g[l][None], "pb": post_ln_b[l][None],
        }
        xall = _merge(xall, modb, proj, ya, of, ob, p, ctx_len, tm_merge, last, alpha)
    return xall
```

```python
import functools

import jax
import jax.numpy as jnp
from jax import lax
from jax.experimental import pallas as pl
from jax.experimental.pallas import tpu as pltpu

F32 = jnp.float32
BF16 = jnp.bfloat16

D_MODEL = 1024
GRID_W = 64
HEAD_DIM = 64
N_HEADS = 8
N_KV = 2
Q_PER_KV = N_HEADS // N_KV
ATTN_W = N_HEADS * HEAD_DIM
KV_W = N_KV * HEAD_DIM
ROPE_THETA = 10000.0
ROPE_AXIS = HEAD_DIM // 2
GLA_HEADS = 4
GLA_DK = 64
GLA_DV = 128
GLA_KW = GLA_HEADS * GLA_DK
GLA_VW = GLA_HEADS * GLA_DV
GLA_RANK = 16
GLA_TEMP = 16.0
SGU_GROUPS = 4
SGU_CHUNK = 128
SGU_W = 512
EPS = 1e-6

LANES = 128
V7X_VMEM_BYTES = 64 * 1024 * 1024

COL_MA, COL_MG, COL_MS = 0, 1024, 2048
COL_AQ, COL_AG = 3072, 3584
COL_GV, COL_GG = 4096, 4608
COL_SU, COL_SV, COL_SG = 5120, 5632, 6144
COL_GQK = 6656
COL_AKV = 7168
COL_GATE = 7424
PROJ_W = 7680

ATT_TQ = 256
ATT_TK = 256
GLA_C = 128
GLA_SUB = 16
V_ROWS = 80
ATT_UNROLL = 4
Q_SCALE = 1.4426950408889634 * HEAD_DIM ** -0.5
SCORE_LIMIT = 96.0


def _cparams(sem, vmem_mb):
    return pltpu.CompilerParams(dimension_semantics=sem,
                                vmem_limit_bytes=min(vmem_mb * 1024 * 1024, V7X_VMEM_BYTES - (4 << 20)))


def _silu(x):
    return x * jax.nn.sigmoid(x)


def _dot(a, b):
    return jnp.dot(a, b, preferred_element_type=F32)


def _dot_nt(a, b):
    return lax.dot_general(a, b, (((1,), (1,)), ((), ())), preferred_element_type=F32)


def _ada_kernel(c_ref, w_ref, b_ref, o_ref):
    c = c_ref[...]
    o_ref[...] = _dot(_silu(c).astype(BF16), w_ref[...].astype(BF16)) + b_ref[...]


def _ada(cc, w_ada, b_ada):
    L, D = w_ada.shape[0], w_ada.shape[1]
    return pl.pallas_call(
        _ada_kernel,
        grid=(L, 3),
        in_specs=[pl.BlockSpec((8, D), lambda l, j: (0, 0)),
                  pl.BlockSpec((None, D, D), lambda l, j: (l, 0, j)),
                  pl.BlockSpec((None, 1, D), lambda l, j: (l, 0, j))],
        out_specs=pl.BlockSpec((None, 8, D), lambda l, j: (l, 0, j)),
        out_shape=jax.ShapeDtypeStruct((L, 8, 3 * D), F32),
        compiler_params=_cparams(("arbitrary", "arbitrary"), 32),
        name="ada_mod",
    )(cc, w_ada, b_ada)


def _inproj_kernel(x_ref, mod_ref, w_ref, o_ref, h_ref, *, tm, ctx_len):
    i = pl.program_id(1)
    j = pl.program_id(2)

    @pl.when(j == 0)
    def _():
        D = x_ref.shape[-1]
        row = i * tm + lax.broadcasted_iota(jnp.int32, (tm, 1), 0)
        is_ctx = row < ctx_len
        shift = jnp.where(is_ctx, mod_ref[1:2, 0:D], mod_ref[0:1, 0:D])
        scale = jnp.where(is_ctx, mod_ref[1:2, D:2 * D], mod_ref[0:1, D:2 * D])
        h_ref[...] = (x_ref[...] * (1.0 + scale) + shift).astype(BF16)

    o_ref[...] = _dot(h_ref[...], w_ref[...]).astype(BF16)


def _inproj(xall, modb, w, ctx_len, tm, tn):
    B, T, D = xall.shape
    return pl.pallas_call(
        functools.partial(_inproj_kernel, tm=tm, ctx_len=ctx_len),
        grid=(B, T // tm, PROJ_W // tn),
        in_specs=[pl.BlockSpec((None, tm, D), lambda b, i, j: (b, i, 0)),
                  pl.BlockSpec((None, 8, 3 * D), lambda b, i, j: (b, 0, 0)),
                  pl.BlockSpec((D, tn), lambda b, i, j: (0, j))],
        out_specs=pl.BlockSpec((None, tm, tn), lambda b, i, j: (b, i, j)),
        out_shape=jax.ShapeDtypeStruct((B, T, PROJ_W), BF16),
        scratch_shapes=[pltpu.VMEM((tm, D), BF16)],
        compiler_params=_cparams(("arbitrary", "arbitrary", "arbitrary"), 40),
        name="in_proj",
    )(xall, modb, w)


def _rope(y, cos, sin, width):
    lane = lax.broadcasted_iota(jnp.int32, (1, width), 1)
    first = jnp.bitwise_and(lane, ROPE_AXIS - 1) < (ROPE_AXIS // 2)
    up = pltpu.roll(y, width - ROPE_AXIS // 2, 1)
    dn = pltpu.roll(y, ROPE_AXIS // 2, 1)
    return y * cos + jnp.where(first, up, dn) * sin


def _qkprep_kernel(aq_ref, akv_ref, cos_ref, sin_ref, qn_ref, kn_ref, gq_ref, gk_ref,
                   eq_ref, ek_ref, qt_ref, k_ref, vt_ref, *, tm):
    cos = cos_ref[...]
    sin = sin_ref[...]
    aq = aq_ref[...].astype(F32)
    ssq = _dot((aq * aq).astype(BF16), gq_ref[...])
    yq = aq * lax.rsqrt(ssq * (1.0 / HEAD_DIM) + EPS) * jnp.concatenate([qn_ref[...]] * 4, axis=1)
    yq = _rope(yq, jnp.concatenate([cos] * 4, axis=1), jnp.concatenate([sin] * 4, axis=1), ATTN_W)
    yq = (yq * Q_SCALE).astype(BF16)
    qt = _dot_nt(eq_ref[...], yq).astype(BF16)
    qt_ref[...] = qt.reshape(N_HEADS, HEAD_DIM, tm)
    akv = akv_ref[...].astype(F32)
    ak = akv[:, 0:KV_W]
    ssk = _dot((ak * ak).astype(BF16), gk_ref[...])
    yk = ak * lax.rsqrt(ssk * (1.0 / HEAD_DIM) + EPS) * kn_ref[...]
    yk = _rope(yk, cos, sin, KV_W).astype(BF16)
    nkb = tm // ATT_TK
    k_ref[...] = yk.reshape(nkb, ATT_TK, KV_W)
    av = akv_ref[:, KV_W:2 * KV_W]
    vt = _dot_nt(ek_ref[...], av).astype(BF16)
    ones = jnp.ones((V_ROWS - HEAD_DIM, ATT_TK), BF16)
    for g in range(N_KV):
        for n in range(nkb):
            vt_ref[g, n, 0:HEAD_DIM, :] = vt[g * HEAD_DIM:(g + 1) * HEAD_DIM, n * ATT_TK:(n + 1) * ATT_TK]
            vt_ref[g, n, HEAD_DIM:V_ROWS, :] = ones


def _qkprep(proj, cos_t, sin_t, qn, kn, consts, tm):
    B, T, _ = proj.shape
    nkb_t = T // ATT_TK
    nkb = tm // ATT_TK
    full = lambda shape: pl.BlockSpec(shape, lambda b, i: (0,) * len(shape))
    return pl.pallas_call(
        functools.partial(_qkprep_kernel, tm=tm),
        grid=(B, T // tm),
        in_specs=[pl.BlockSpec((None, tm, ATTN_W), lambda b, i: (b, i, COL_AQ // ATTN_W)),
                  pl.BlockSpec((None, tm, 2 * KV_W), lambda b, i: (b, i, COL_AKV // (2 * KV_W))),
                  pl.BlockSpec((tm, LANES), lambda b, i: (i, 0)),
                  pl.BlockSpec((tm, LANES), lambda b, i: (i, 0)),
                  full((1, LANES)), full((1, LANES)),
                  full((ATTN_W, ATTN_W)), full((KV_W, KV_W)),
                  full((ATTN_W, ATTN_W)), full((KV_W, KV_W))],
        out_specs=[pl.BlockSpec((None, N_HEADS, HEAD_DIM, tm), lambda b, i: (b, 0, 0, i)),
                   pl.BlockSpec((None, nkb, ATT_TK, KV_W), lambda b, i: (b, i, 0, 0)),
                   pl.BlockSpec((None, N_KV, nkb, V_ROWS, ATT_TK), lambda b, i: (b, 0, i, 0, 0))],
        out_shape=[jax.ShapeDtypeStruct((B, N_HEADS, HEAD_DIM, T), BF16),
                   jax.ShapeDtypeStruct((B, nkb_t, ATT_TK, KV_W), BF16),
                   jax.ShapeDtypeStruct((B, N_KV, nkb_t, V_ROWS, ATT_TK), BF16)],
        compiler_params=_cparams(("arbitrary", "arbitrary"), 40),
        name="qkv_prep",
    )(proj, proj, cos_t, sin_t, qn, kn, consts["gq"], consts["gk"], consts["eq"], consts["ek"])


def _attn_kernel(bounded_ref, qt_ref, k_ref, vt_ref, ag_ref, eye_ref, o_ref, q_ref, p_ref, acc_ref, oall_ref,
                 *, nkb, ctx_blocks):
    g = pl.program_id(1)
    i = pl.program_id(2)
    tq = qt_ref.shape[-1]
    sub = lax.broadcasted_iota(jnp.int32, (KV_W, 1), 0)
    own = (sub >= g * HEAD_DIM) & (sub < (g + 1) * HEAD_DIM)
    for r in range(Q_PER_KV):
        q = qt_ref[r]
        q_ref[:, r * tq:(r + 1) * tq] = jnp.where(own, jnp.concatenate([q, q], axis=0), jnp.zeros((), BF16))
    acc_ref[...] = jnp.zeros_like(acc_ref)

    def weights(j, slot):
        p_ref[slot] = jnp.exp2(_dot(k_ref[j], q_ref[...])).astype(BF16)

    def weighted_values(j, slot):
        return _dot(vt_ref[j], p_ref[slot])

    def run(n):
        U = ATT_UNROLL
        weights(0, 0)

        def group(jj, carry):
            j = U * jj
            part = None
            for u in range(U):
                weights(j + u + 1, (u + 1) % U)
                wv = weighted_values(j + u, u)
                part = wv if part is None else part + wv
            acc_ref[...] += part
            return carry

        groups = (n - 1) // U
        lax.fori_loop(0, groups, group, 0)
        for t in range(U * groups, n):
            if t + 1 < n:
                weights(t + 1, (t + 1) % U)
            acc_ref[...] += weighted_values(t, t % U)

    def run_online(n):
        for r in range(Q_PER_KV):
            q = q_ref[:, r * tq:(r + 1) * tq]

            def body(j, carry):
                m, acc = carry
                s = _dot(k_ref[j], q)
                m_new = jnp.maximum(m, jnp.max(s, axis=0, keepdims=True))
                p = jnp.exp2(s - m_new)
                return m_new, jnp.exp2(m - m_new) * acc + _dot(vt_ref[j], p.astype(BF16))

            init = (jnp.full((1, tq), -jnp.inf, F32), jnp.zeros((V_ROWS, tq), F32))
            acc_ref[:, r * tq:(r + 1) * tq] = lax.fori_loop(0, n, body, init)[1]

    bounded = bounded_ref[0] == 1
    is_ctx = i < ctx_blocks
    for cond, fn, n in ((bounded & is_ctx, run, ctx_blocks),
                        (bounded & jnp.logical_not(is_ctx), run, nkb),
                        (jnp.logical_not(bounded) & is_ctx, run_online, ctx_blocks),
                        (jnp.logical_not(bounded) & jnp.logical_not(is_ctx), run_online, nkb)):
        pl.when(cond)(functools.partial(fn, n))

    for r in range(Q_PER_KV):
        acc = acc_ref[:, r * tq:(r + 1) * tq]
        o = acc[0:HEAD_DIM, :] * (1.0 / acc[HEAD_DIM:HEAD_DIM + 1, :])
        oall_ref[r * HEAD_DIM:(r + 1) * HEAD_DIM, :] = o.astype(BF16)
    o_t = _dot_nt(eye_ref[...], oall_ref[...])
    o_ref[...] = (o_t * _silu(ag_ref[...].astype(F32))).astype(BF16)


def _attention(bounded, qt, k, vt, proj, eye_tq, ctx_len):
    B, T = proj.shape[0], proj.shape[1]
    nkb = T // ATT_TK
    gw = Q_PER_KV * HEAD_DIM
    grid_spec = pltpu.PrefetchScalarGridSpec(
        num_scalar_prefetch=1,
        grid=(B, N_KV, T // ATT_TQ),
        in_specs=[pl.BlockSpec((None, Q_PER_KV, HEAD_DIM, ATT_TQ), lambda b, g, i, f: (b, g, 0, i)),
                  pl.BlockSpec((None, nkb, ATT_TK, KV_W), lambda b, g, i, f: (b, 0, 0, 0)),
                  pl.BlockSpec((None, None, nkb, V_ROWS, ATT_TK), lambda b, g, i, f: (b, g, 0, 0, 0)),
                  pl.BlockSpec((None, ATT_TQ, gw), lambda b, g, i, f: (b, i, COL_AG // gw + g)),
                  pl.BlockSpec((ATT_TQ, ATT_TQ), lambda b, g, i, f: (0, 0))],
        out_specs=pl.BlockSpec((None, ATT_TQ, gw), lambda b, g, i, f: (b, i, g)),
        scratch_shapes=[pltpu.VMEM((KV_W, Q_PER_KV * ATT_TQ), BF16),
                        pltpu.VMEM((ATT_UNROLL, ATT_TK, Q_PER_KV * ATT_TQ), BF16),
                        pltpu.VMEM((V_ROWS, Q_PER_KV * ATT_TQ), F32),
                        pltpu.VMEM((gw, ATT_TQ), BF16)])
    return pl.pallas_call(
        functools.partial(_attn_kernel, nkb=nkb, ctx_blocks=ctx_len // ATT_TK),
        grid_spec=grid_spec,
        out_shape=jax.ShapeDtypeStruct((B, T, ATTN_W), BF16),
        compiler_params=_cparams(("arbitrary", "arbitrary", "arbitrary"), 40),
        name="gqa_attention",
    )(bounded, qt, k, vt, proj, eye_tq)


def _gla_direction(qk, v, pa, a2, ab, st_ref, fwd):
    C, c = GLA_C, GLA_SUB
    q = qk[:, 0:GLA_KW].astype(F32) * (GLA_DK ** -0.5)
    k = qk[:, GLA_KW:2 * GLA_KW].astype(F32)
    z = _dot(pa, a2) + ab
    g = (jnp.minimum(z, 0.0) - jnp.log(1.0 + jnp.exp(-jnp.abs(z)))) * (1.0 / GLA_TEMP)
    row = lax.broadcasted_iota(jnp.int32, (C, C), 0)
    col = lax.broadcasted_iota(jnp.int32, (C, C), 1)
    blk = jnp.bitwise_and(row, -c)
    if fwd:
        tri = col <= row
        inblk = tri & (col >= blk)
    else:
        tri = col >= row
        inblk = tri & (col < blk + c)
    sel = jnp.concatenate([jnp.where(tri, 1.0, 0.0), jnp.where(inblk, 1.0, 0.0)], axis=0).astype(BF16)
    g_hi = g.astype(BF16)
    g_lo = (g - g_hi.astype(F32)).astype(BF16)
    cs = _dot(sel, jnp.concatenate([g_hi, g_lo], axis=1))
    cs = cs[:, 0:GLA_KW] + cs[:, GLA_KW:2 * GLA_KW]
    b = cs[0:C]
    d = cs[C:2 * C]
    ref = b - d
    total = b[C - 1:C, :] if fwd else b[0:1, :]
    q_in = (q * jnp.exp(d)).astype(BF16)
    q_st = (q * jnp.exp(b)).astype(BF16)
    k_out = (k * jnp.exp(total - b)).astype(BF16)
    w_tot = jnp.exp(total)
    rowc = lax.broadcasted_iota(jnp.int32, (C, 1), 0)
    lane = lax.broadcasted_iota(jnp.int32, (1, LANES), 1)
    lo_half = lane < GLA_DK
    zero = jnp.zeros((), BF16)
    k_blk = []
    for i in range(C // c):
        valid = (rowc < (i + 1) * c) if fwd else (rowc >= i * c)
        e = jnp.where(valid, ref[i * c:i * c + 1, :] - b, 0.0)
        k_blk.append((k * jnp.exp(e)).astype(BF16))
    outs = []
    for pr in range(GLA_HEADS // 2):
        ls = slice(pr * LANES, (pr + 1) * LANES)
        a_blk = []
        for i in range(C // c):
            qb = q_in[i * c:(i + 1) * c, ls]
            qm = jnp.concatenate([jnp.where(lo_half, qb, zero), jnp.where(lo_half, zero, qb)], axis=0)
            a_blk.append(_dot_nt(qm, k_blk[i][:, ls]))
        st = st_ref[pr].astype(BF16)
        upd = []
        for half in range(2):
            h = 2 * pr + half
            a = jnp.concatenate([blk[half * c:(half + 1) * c] for blk in a_blk], axis=0)
            a = jnp.where(tri, a, 0.0).astype(BF16)
            vh = v[:, h * GLA_DV:(h + 1) * GLA_DV]
            qs = jnp.where(lo_half if half == 0 else jnp.logical_not(lo_half), q_st[:, ls], zero)
            outs.append(_dot(a, vh) + _dot_nt(qs, st))
            upd.append(lax.dot_general(vh, k_out[:, ls], (((0,), (0,)), ((), ())),
                                       preferred_element_type=F32))
        st_ref[pr] = st_ref[pr] * w_tot[:, ls] + jnp.where(lo_half, upd[0], upd[1])
    return jnp.concatenate(outs, axis=1)


def _gla_kernel(qkf_ref, vf_ref, paf_ref, qkb_ref, vb_ref, pab_ref, a2f_ref, abf_ref, a2b_ref, abb_ref,
                of_ref, ob_ref, stf_ref, stb_ref):
    @pl.when(pl.program_id(0) == 0)
    def _():
        stf_ref[...] = jnp.zeros_like(stf_ref)
        stb_ref[...] = jnp.zeros_like(stb_ref)

    for bi in range(qkf_ref.shape[0]):
        of_ref[bi] = _gla_direction(qkf_ref[bi], vf_ref[bi], paf_ref[bi], a2f_ref[...], abf_ref[...],
                                    stf_ref.at[bi], True)
        ob_ref[bi] = _gla_direction(qkb_ref[bi], vb_ref[bi], pab_ref[bi], a2b_ref[...], abb_ref[...],
                                    stb_ref.at[bi], False)


def _gla(proj, a2f, abf, a2b, abb, ctx_len):
    B, T = proj.shape[0], proj.shape[1]
    C = GLA_C
    nch, nctx = T // C, ctx_len // C

    def cb(s):
        return jnp.where(s < nctx, nctx - 1 - s, nch + nctx - 1 - s)

    full = lambda shape: pl.BlockSpec(shape, lambda s: (0,) * len(shape))
    return pl.pallas_call(
        _gla_kernel,
        grid=(nch,),
        in_specs=[pl.BlockSpec((B, C, 2 * GLA_KW), lambda s: (0, s, COL_GQK // (2 * GLA_KW))),
                  pl.BlockSpec((B, C, GLA_VW), lambda s: (0, s, COL_GV // GLA_VW)),
                  pl.BlockSpec((B, C, LANES), lambda s: (0, s, COL_GATE // LANES)),
                  pl.BlockSpec((B, C, 2 * GLA_KW), lambda s: (0, cb(s), COL_GQK // (2 * GLA_KW))),
                  pl.BlockSpec((B, C, GLA_VW), lambda s: (0, cb(s), COL_GV // GLA_VW)),
                  pl.BlockSpec((B, C, LANES), lambda s: (0, cb(s), COL_GATE // LANES)),
                  full((LANES, GLA_KW)), full((1, GLA_KW)), full((LANES, GLA_KW)), full((1, GLA_KW))],
        out_specs=[pl.BlockSpec((B, C, GLA_VW), lambda s: (0, s, 0)),
                   pl.BlockSpec((B, C, GLA_VW), lambda s: (0, cb(s), 0))],
        out_shape=[jax.ShapeDtypeStruct((B, T, GLA_VW), F32),
                   jax.ShapeDtypeStruct((B, T, GLA_VW), F32)],
        scratch_shapes=[pltpu.VMEM((B, GLA_HEADS // 2, GLA_DV, LANES), F32),
                        pltpu.VMEM((B, GLA_HEADS // 2, GLA_DV, LANES), F32)],
        compiler_params=_cparams(("arbitrary",), 32),
        name="gla_scan",
    )(proj, proj, proj, proj, proj, proj, a2f, abf, a2b, abb)


def _merge_kernel(x_ref, mod_ref, ma_ref, mg_ref, ms_ref, su_ref, sv_ref, sg_ref, gg_ref, ya_ref, of_ref, ob_ref,
                  onorm_ref, lng_ref, lnb_ref, sw_ref, sb_ref, wa_ref, wg_ref, ws_ref, wo_ref, pg_ref, pb_ref,
                  o_ref, *, tm, ctx_tiles, row_off, alpha):
    D = x_ref.shape[-1]
    i = pl.program_id(1) + row_off
    og = of_ref[...] + ob_ref[...]
    parts = []
    for h in range(GLA_HEADS):
        oh = og[:, h * GLA_DV:(h + 1) * GLA_DV]
        ms = jnp.mean(oh * oh, axis=-1, keepdims=True)
        parts.append(oh * lax.rsqrt(ms + EPS))
    yg = jnp.concatenate(parts, axis=1) * onorm_ref[...] * _silu(gg_ref[...].astype(F32))
    sv = sv_ref[...].astype(F32)
    mu = jnp.mean(sv, axis=-1, keepdims=True)
    dv = sv - mu
    var = jnp.mean(dv * dv, axis=-1, keepdims=True)
    vn = (dv * lax.rsqrt(var + EPS) * lng_ref[...] + lnb_ref[...]).astype(BF16)
    gw = SGU_W // SGU_GROUPS
    rows = []
    for n in range(tm // SGU_CHUNK):
        cols = []
        for g in range(SGU_GROUPS):
            blk = vn[n * SGU_CHUNK:(n + 1) * SGU_CHUNK, g * gw:(g + 1) * gw]
            cols.append(_dot(sw_ref[g], blk))
        rows.append(jnp.concatenate(cols, axis=1) + sb_ref[...])
    mixed = jnp.concatenate(rows, axis=0)
    ys = su_ref[...].astype(F32) * mixed * _silu(sg_ref[...].astype(F32))
    m = (jax.nn.sigmoid(ma_ref[...].astype(F32)) * _dot(ya_ref[...], wa_ref[...])
         + jax.nn.sigmoid(mg_ref[...].astype(F32)) * _dot(yg.astype(BF16), wg_ref[...])
         + jax.nn.sigmoid(ms_ref[...].astype(F32)) * _dot(ys.astype(BF16), ws_ref[...]))
    out = _dot(m.astype(BF16), wo_ref[...])
    gate = jnp.where(i < ctx_tiles, mod_ref[1:2, 2 * D:3 * D], mod_ref[0:1, 2 * D:3 * D])
    r = alpha * x_ref[...] + gate * out
    mu = jnp.mean(r, axis=-1, keepdims=True)
    dr = r - mu
    var = jnp.mean(dr * dr, axis=-1, keepdims=True)
    o_ref[...] = dr * lax.rsqrt(var + EPS) * pg_ref[...] + pb_ref[...]


def _merge(xall, modb, proj, ya, of, ob, p, ctx_len, tm, skip_ctx, alpha):
    B, T, D = xall.shape
    row_off = ctx_len // tm if skip_ctx else 0
    nt = T // tm - row_off
    rows = lambda w, c: pl.BlockSpec((None, tm, w), lambda b, i: (b, i + row_off, c))
    full = lambda shape: pl.BlockSpec(shape, lambda b, i: (0,) * len(shape))
    return pl.pallas_call(
        functools.partial(_merge_kernel, tm=tm, ctx_tiles=ctx_len // tm, row_off=row_off, alpha=alpha),
        grid=(B, nt),
        in_specs=[rows(D, 0),
                  pl.BlockSpec((None, 8, 3 * D), lambda b, i: (b, 0, 0)),
                  rows(D, COL_MA // D), rows(D, COL_MG // D), rows(D, COL_MS // D),
                  rows(SGU_W, COL_SU // SGU_W), rows(SGU_W, COL_SV // SGU_W), rows(SGU_W, COL_SG // SGU_W),
                  rows(GLA_VW, COL_GG // GLA_VW),
                  rows(ATTN_W, 0), rows(GLA_VW, 0), rows(GLA_VW, 0),
                  full((1, GLA_VW)), full((1, SGU_W)), full((1, SGU_W)),
                  full((SGU_GROUPS, SGU_CHUNK, SGU_CHUNK)), full((SGU_CHUNK, SGU_W)),
                  full((ATTN_W, D)), full((GLA_VW, D)), full((SGU_W, D)), full((D, D)),
                  full((1, D)), full((1, D))],
        out_specs=pl.BlockSpec((None, tm, D), lambda b, i: (b, i, 0)),
        out_shape=jax.ShapeDtypeStruct((B, nt * tm, D), F32),
        compiler_params=_cparams(("arbitrary", "arbitrary"), 48),
        name="merge_out",
    )(xall, modb, proj, proj, proj, proj, proj, proj, proj, ya, of, ob,
      p["onorm"], p["lng"], p["lnb"], p["sw"], p["sb"], p["wa"], p["wg"], p["ws"], p["wo"], p["pg"], p["pb"])


def _regroup_w_in(w):
    D = w.shape[0]
    sizes = (512, 128, 128, 512, 256, 256, 512, 16, 16, 512, 512, 512, 512, 1024, 1024, 1024)
    names = ("aq", "ak", "av", "ag", "gq", "gk", "gv", "af", "ab", "gg", "su", "sv", "sg", "ma", "mg", "ms")
    seg, off = {}, 0
    for n, s in zip(names, sizes):
        seg[n] = w[:, off:off + s]
        off += s
    order = ("ma", "mg", "ms", "aq", "ag", "gv", "gg", "su", "sv", "sg", "gq", "gk", "ak", "av", "af", "ab")
    used = sum(seg[n].shape[1] for n in order)
    cols = [seg[n] for n in order] + [jnp.zeros((D, PROJ_W - used), w.dtype)]
    return jnp.concatenate(cols, axis=1).astype(BF16)


def _rope_tables(n_lat, ctx_len):
    rows = n_lat // GRID_W
    row = jnp.repeat(jnp.arange(rows, dtype=F32), GRID_W)
    col = jnp.tile(jnp.arange(GRID_W, dtype=F32), rows)
    freqs = ROPE_THETA ** (-jnp.arange(0, ROPE_AXIS, 2, dtype=F32) / ROPE_AXIS)
    ang_r, ang_c = row[:, None] * freqs, col[:, None] * freqs
    cos = jnp.concatenate([jnp.cos(ang_r)] * 2 + [jnp.cos(ang_c)] * 2, axis=1)
    sin = jnp.concatenate([-jnp.sin(ang_r), jnp.sin(ang_r), -jnp.sin(ang_c), jnp.sin(ang_c)], axis=1)
    cos = jnp.concatenate([jnp.ones((ctx_len, HEAD_DIM), F32), cos], axis=0)
    sin = jnp.concatenate([jnp.zeros((ctx_len, HEAD_DIM), F32), sin], axis=0)
    return jnp.concatenate([cos, cos], axis=1), jnp.concatenate([sin, sin], axis=1)


def _largest_tile(total, cap, mult):
    best = mult
    for t in range(mult, cap + 1, mult):
        if total % t == 0:
            best = t
    return best


def kernel(x, c, ctx, c_ctx, w_ada, b_ada, w_in, attn_q_norm, attn_k_norm, gla_a2_f, gla_ab_f, gla_a2_b, gla_ab_b, gla_o_norm, sgu_ln_g, sgu_ln_b, sgu_w, sgu_b, w_br_attn, w_br_gla, w_br_sgu, w_out, post_ln_g, post_ln_b):
    B, n_lat, D = x.shape
    ctx_len = ctx.shape[1]
    depth = w_in.shape[0]
    T = ctx_len + n_lat
    assert D == D_MODEL and B <= 7
    assert ctx_len % ATT_TK == 0 and n_lat % ATT_TK == 0 and n_lat % GRID_W == 0
    alpha = (2 * depth) ** 0.25

    tm_proj = _largest_tile(T, 1056, 16)
    tn_proj = 768
    tm_prep = _largest_tile(T, 768, ATT_TK)
    tm_merge = ATT_TK

    xall = jnp.concatenate([ctx, x], axis=1)
    cc = jnp.zeros((8, D), F32).at[0:B].set(c).at[B].set(c_ctx)
    mod = _ada(cc, w_ada, b_ada.reshape(depth, 1, 3 * D))
    cos_t, sin_t = _rope_tables(n_lat, ctx_len)
    consts = {
        "gq": jnp.kron(jnp.eye(N_HEADS, dtype=F32), jnp.ones((HEAD_DIM, HEAD_DIM), F32)).astype(BF16),
        "gk": jnp.kron(jnp.eye(N_KV, dtype=F32), jnp.ones((HEAD_DIM, HEAD_DIM), F32)).astype(BF16),
        "eq": jnp.eye(ATTN_W, dtype=BF16),
        "ek": jnp.eye(KV_W, dtype=BF16),
    }
    eye_tq = jnp.eye(ATT_TQ, dtype=BF16)

    for l in range(depth):
        last = l == depth - 1
        modb = jnp.zeros((B, 8, 3 * D), F32).at[:, 0].set(mod[l, 0:B]).at[:, 1].set(mod[l, B][None])
        w = _regroup_w_in(w_in[l])
        proj = _inproj(xall, modb, w, ctx_len, tm_proj, tn_proj)
        qn = jnp.concatenate([attn_q_norm[l]] * 2)[None]
        kn = jnp.concatenate([attn_k_norm[l]] * 2)[None]
        qt, kk, vt = _qkprep(proj, cos_t, sin_t, qn, kn, consts, tm_prep)
        score_bound = HEAD_DIM * jnp.max(jnp.abs(attn_q_norm[l])) * jnp.max(jnp.abs(attn_k_norm[l])) * Q_SCALE
        bounded = (score_bound <= SCORE_LIMIT).astype(jnp.int32).reshape(1)
        ya = _attention(bounded, qt, kk, vt, proj, eye_tq, ctx_len)
        a2f = jnp.zeros((LANES, GLA_KW), F32).at[0:GLA_RANK].set(gla_a2_f[l]).astype(BF16)
        a2b = jnp.zeros((LANES, GLA_KW), F32).at[GLA_RANK:2 * GLA_RANK].set(gla_a2_b[l]).astype(BF16)
        of, ob = _gla(proj, a2f, gla_ab_f[l][None], a2b, gla_ab_b[l][None], ctx_len)
        p = {
            "onorm": jnp.concatenate([gla_o_norm[l]] * GLA_HEADS)[None],
            "lng": sgu_ln_g[l][None], "lnb": sgu_ln_b[l][None],
            "sw": sgu_w[l].astype(BF16),
            "sb": jnp.repeat(sgu_b[l].T, SGU_W // SGU_GROUPS, axis=1),
            "wa": w_br_attn[l].astype(BF16), "wg": w_br_gla[l].astype(BF16), "ws": w_br_sgu[l].astype(BF16),
            "wo": w_out[l].astype(BF16), "pg": post_ln_g[l][None], "pb": post_ln_b[l][None],
        }
        xall = _merge(xall, modb, proj, ya, of, ob, p, ctx_len, tm_merge, last, alpha)
    return xall
```

```python
import functools

import jax
import jax.numpy as jnp
from jax import lax
from jax.experimental import pallas as pl
from jax.experimental.pallas import tpu as pltpu

F32 = jnp.float32
BF16 = jnp.bfloat16

D_MODEL = 1024
GRID_W = 64
HEAD_DIM = 64
N_HEADS = 8
N_KV = 2
Q_PER_KV = N_HEADS // N_KV
ATTN_W = N_HEADS * HEAD_DIM
KV_W = N_KV * HEAD_DIM
ROPE_THETA = 10000.0
ROPE_AXIS = HEAD_DIM // 2
GLA_HEADS = 4
GLA_DK = 64
GLA_DV = 128
GLA_KW = GLA_HEADS * GLA_DK
GLA_VW = GLA_HEADS * GLA_DV
GLA_RANK = 16
GLA_TEMP = 16.0
SGU_GROUPS = 4
SGU_CHUNK = 128
SGU_W = 512
EPS = 1e-6

LANES = 128
V7X_VMEM_BYTES = 64 * 1024 * 1024

COL_MA, COL_MG, COL_MS = 0, 1024, 2048
COL_AQ, COL_AG = 3072, 3584
COL_GV, COL_GG = 4096, 4608
COL_SU, COL_SV, COL_SG = 5120, 5632, 6144
COL_GQK = 6656
COL_AKV = 7168
COL_GATE = 7424
PROJ_W = 7680

ATT_TQ = 256
ATT_TK = 256
GLA_C = 128
GLA_SUB = 16
V_ROWS = 80
ATT_UNROLL = 8
Q_SCALE = 1.4426950408889634 * HEAD_DIM ** -0.5
SCORE_LIMIT = 96.0


def _cparams(sem, vmem_mb):
    return pltpu.CompilerParams(dimension_semantics=sem,
                                vmem_limit_bytes=min(vmem_mb * 1024 * 1024, V7X_VMEM_BYTES - (4 << 20)))


def _silu(x):
    return x * jax.nn.sigmoid(x)


def _dot(a, b):
    return jnp.dot(a, b, preferred_element_type=F32)


def _dot_nt(a, b):
    return lax.dot_general(a, b, (((1,), (1,)), ((), ())), preferred_element_type=F32)


def _ada_kernel(c_ref, w_ref, b_ref, o_ref):
    c = c_ref[...]
    o_ref[...] = _dot(_silu(c).astype(BF16), w_ref[...].astype(BF16)) + b_ref[...]


def _ada(cc, w_ada, b_ada):
    L, D = w_ada.shape[0], w_ada.shape[1]
    return pl.pallas_call(
        _ada_kernel,
        grid=(L, 3),
        in_specs=[pl.BlockSpec((8, D), lambda l, j: (0, 0)),
                  pl.BlockSpec((None, D, D), lambda l, j: (l, 0, j)),
                  pl.BlockSpec((None, 1, D), lambda l, j: (l, 0, j))],
        out_specs=pl.BlockSpec((None, 8, D), lambda l, j: (l, 0, j)),
        out_shape=jax.ShapeDtypeStruct((L, 8, 3 * D), F32),
        compiler_params=_cparams(("arbitrary", "arbitrary"), 32),
        name="ada_mod",
    )(cc, w_ada, b_ada)


def _inproj_kernel(x_ref, mod_ref, w_ref, o_ref, h_ref, *, tm, ctx_len):
    i = pl.program_id(1)
    j = pl.program_id(2)

    @pl.when(j == 0)
    def _():
        D = x_ref.shape[-1]
        row = i * tm + lax.broadcasted_iota(jnp.int32, (tm, 1), 0)
        is_ctx = row < ctx_len
        shift = jnp.where(is_ctx, mod_ref[1:2, 0:D], mod_ref[0:1, 0:D])
        scale = jnp.where(is_ctx, mod_ref[1:2, D:2 * D], mod_ref[0:1, D:2 * D])
        h_ref[...] = (x_ref[...] * (1.0 + scale) + shift).astype(BF16)

    o_ref[...] = _dot(h_ref[...], w_ref[...]).astype(BF16)


def _inproj(xall, modb, w, ctx_len, tm, tn):
    B, T, D = xall.shape
    return pl.pallas_call(
        functools.partial(_inproj_kernel, tm=tm, ctx_len=ctx_len),
        grid=(B, T // tm, PROJ_W // tn),
        in_specs=[pl.BlockSpec((None, tm, D), lambda b, i, j: (b, i, 0)),
                  pl.BlockSpec((None, 8, 3 * D), lambda b, i, j: (b, 0, 0)),
                  pl.BlockSpec((D, tn), lambda b, i, j: (0, j))],
        out_specs=pl.BlockSpec((None, tm, tn), lambda b, i, j: (b, i, j)),
        out_shape=jax.ShapeDtypeStruct((B, T, PROJ_W), BF16),
        scratch_shapes=[pltpu.VMEM((tm, D), BF16)],
        compiler_params=_cparams(("arbitrary", "arbitrary", "arbitrary"), 40),
        name="in_proj",
    )(xall, modb, w)


def _rope(y, cos, sin, width):
    lane = lax.broadcasted_iota(jnp.int32, (1, width), 1)
    first = jnp.bitwise_and(lane, ROPE_AXIS - 1) < (ROPE_AXIS // 2)
    up = pltpu.roll(y, width - ROPE_AXIS // 2, 1)
    dn = pltpu.roll(y, ROPE_AXIS // 2, 1)
    return y * cos + jnp.where(first, up, dn) * sin


def _qkprep_kernel(aq_ref, akv_ref, cos_ref, sin_ref, qn_ref, kn_ref, gq_ref, gk_ref,
                   eq_ref, ek_ref, qt_ref, k_ref, vt_ref, *, tm):
    cos = cos_ref[...]
    sin = sin_ref[...]
    aq = aq_ref[...].astype(F32)
    ssq = _dot((aq * aq).astype(BF16), gq_ref[...])
    yq = aq * lax.rsqrt(ssq * (1.0 / HEAD_DIM) + EPS) * jnp.concatenate([qn_ref[...]] * 4, axis=1)
    yq = _rope(yq, jnp.concatenate([cos] * 4, axis=1), jnp.concatenate([sin] * 4, axis=1), ATTN_W)
    yq = (yq * Q_SCALE).astype(BF16)
    qt = _dot_nt(eq_ref[...], yq).astype(BF16)
    qt_ref[...] = qt.reshape(N_HEADS, HEAD_DIM, tm)
    akv = akv_ref[...].astype(F32)
    ak = akv[:, 0:KV_W]
    ssk = _dot((ak * ak).astype(BF16), gk_ref[...])
    yk = ak * lax.rsqrt(ssk * (1.0 / HEAD_DIM) + EPS) * kn_ref[...]
    yk = _rope(yk, cos, sin, KV_W).astype(BF16)
    nkb = tm // ATT_TK
    k_ref[...] = yk.reshape(nkb, ATT_TK, KV_W)
    av = akv_ref[:, KV_W:2 * KV_W]
    vt = _dot_nt(ek_ref[...], av).astype(BF16)
    ones = jnp.ones((V_ROWS - HEAD_DIM, ATT_TK), BF16)
    for g in range(N_KV):
        for n in range(nkb):
            vt_ref[g, n, 0:HEAD_DIM, :] = vt[g * HEAD_DIM:(g + 1) * HEAD_DIM, n * ATT_TK:(n + 1) * ATT_TK]
            vt_ref[g, n, HEAD_DIM:V_ROWS, :] = ones


def _qkprep(proj, cos_t, sin_t, qn, kn, consts, tm):
    B, T, _ = proj.shape
    nkb_t = T // ATT_TK
    nkb = tm // ATT_TK
    full = lambda shape: pl.BlockSpec(shape, lambda b, i: (0,) * len(shape))
    return pl.pallas_call(
        functools.partial(_qkprep_kernel, tm=tm),
        grid=(B, T // tm),
        in_specs=[pl.BlockSpec((None, tm, ATTN_W), lambda b, i: (b, i, COL_AQ // ATTN_W)),
                  pl.BlockSpec((None, tm, 2 * KV_W), lambda b, i: (b, i, COL_AKV // (2 * KV_W))),
                  pl.BlockSpec((tm, LANES), lambda b, i: (i, 0)),
                  pl.BlockSpec((tm, LANES), lambda b, i: (i, 0)),
                  full((1, LANES)), full((1, LANES)),
                  full((ATTN_W, ATTN_W)), full((KV_W, KV_W)),
                  full((ATTN_W, ATTN_W)), full((KV_W, KV_W))],
        out_specs=[pl.BlockSpec((None, N_HEADS, HEAD_DIM, tm), lambda b, i: (b, 0, 0, i)),
                   pl.BlockSpec((None, nkb, ATT_TK, KV_W), lambda b, i: (b, i, 0, 0)),
                   pl.BlockSpec((None, N_KV, nkb, V_ROWS, ATT_TK), lambda b, i: (b, 0, i, 0, 0))],
        out_shape=[jax.ShapeDtypeStruct((B, N_HEADS, HEAD_DIM, T), BF16),
                   jax.ShapeDtypeStruct((B, nkb_t, ATT_TK, KV_W), BF16),
                   jax.ShapeDtypeStruct((B, N_KV, nkb_t, V_ROWS, ATT_TK), BF16)],
        compiler_params=_cparams(("arbitrary", "arbitrary"), 40),
        name="qkv_prep",
    )(proj, proj, cos_t, sin_t, qn, kn, consts["gq"], consts["gk"], consts["eq"], consts["ek"])


def _attn_kernel(bounded_ref, qt_ref, k_ref, vt_ref, ag_ref, eye_ref, o_ref, q_ref, p_ref, acc_ref, oall_ref,
                 *, nkb, ctx_blocks):
    g = pl.program_id(1)
    i = pl.program_id(2)
    tq = qt_ref.shape[-1]
    sub = lax.broadcasted_iota(jnp.int32, (KV_W, 1), 0)
    own = (sub >= g * HEAD_DIM) & (sub < (g + 1) * HEAD_DIM)
    for r in range(Q_PER_KV):
        q = qt_ref[r]
        q_ref[:, r * tq:(r + 1) * tq] = jnp.where(own, jnp.concatenate([q, q], axis=0), jnp.zeros((), BF16))
    acc_ref[...] = jnp.zeros_like(acc_ref)

    def weights(j, slot):
        p_ref[slot] = jnp.exp2(_dot(k_ref[j], q_ref[...])).astype(BF16)

    def weighted_values(j, slot):
        return _dot(vt_ref[j], p_ref[slot])

    def run(n):
        U = ATT_UNROLL
        weights(0, 0)

        def group(jj, carry):
            j = U * jj
            part = None
            for u in range(U):
                weights(j + u + 1, (u + 1) % U)
                wv = weighted_values(j + u, u)
                part = wv if part is None else part + wv
            acc_ref[...] += part
            return carry

        groups = (n - 1) // U
        lax.fori_loop(0, groups, group, 0)
        for t in range(U * groups, n):
            if t + 1 < n:
                weights(t + 1, (t + 1) % U)
            acc_ref[...] += weighted_values(t, t % U)

    def run_online(n):
        for r in range(Q_PER_KV):
            q = q_ref[:, r * tq:(r + 1) * tq]

            def body(j, carry):
                m, acc = carry
                s = _dot(k_ref[j], q)
                m_new = jnp.maximum(m, jnp.max(s, axis=0, keepdims=True))
                p = jnp.exp2(s - m_new)
                return m_new, jnp.exp2(m - m_new) * acc + _dot(vt_ref[j], p.astype(BF16))

            init = (jnp.full((1, tq), -jnp.inf, F32), jnp.zeros((V_ROWS, tq), F32))
            acc_ref[:, r * tq:(r + 1) * tq] = lax.fori_loop(0, n, body, init)[1]

    bounded = bounded_ref[0] == 1
    is_ctx = i < ctx_blocks
    for cond, fn, n in ((bounded & is_ctx, run, ctx_blocks),
                        (bounded & jnp.logical_not(is_ctx), run, nkb),
                        (jnp.logical_not(bounded) & is_ctx, run_online, ctx_blocks),
                        (jnp.logical_not(bounded) & jnp.logical_not(is_ctx), run_online, nkb)):
        pl.when(cond)(functools.partial(fn, n))

    for r in range(Q_PER_KV):
        acc = acc_ref[:, r * tq:(r + 1) * tq]
        o = acc[0:HEAD_DIM, :] * (1.0 / acc[HEAD_DIM:HEAD_DIM + 1, :])
        oall_ref[r * HEAD_DIM:(r + 1) * HEAD_DIM, :] = o.astype(BF16)
    o_t = _dot_nt(eye_ref[...], oall_ref[...])
    o_ref[...] = (o_t * _silu(ag_ref[...].astype(F32))).astype(BF16)


def _attention(bounded, qt, k, vt, proj, eye_tq, ctx_len):
    B, T = proj.shape[0], proj.shape[1]
    nkb = T // ATT_TK
    gw = Q_PER_KV * HEAD_DIM
    grid_spec = pltpu.PrefetchScalarGridSpec(
        num_scalar_prefetch=1,
        grid=(B, N_KV, T // ATT_TQ),
        in_specs=[pl.BlockSpec((None, Q_PER_KV, HEAD_DIM, ATT_TQ), lambda b, g, i, f: (b, g, 0, i)),
                  pl.BlockSpec((None, nkb, ATT_TK, KV_W), lambda b, g, i, f: (b, 0, 0, 0)),
                  pl.BlockSpec((None, None, nkb, V_ROWS, ATT_TK), lambda b, g, i, f: (b, g, 0, 0, 0)),
                  pl.BlockSpec((None, ATT_TQ, gw), lambda b, g, i, f: (b, i, COL_AG // gw + g)),
                  pl.BlockSpec((ATT_TQ, ATT_TQ), lambda b, g, i, f: (0, 0))],
        out_specs=pl.BlockSpec((None, ATT_TQ, gw), lambda b, g, i, f: (b, i, g)),
        scratch_shapes=[pltpu.VMEM((KV_W, Q_PER_KV * ATT_TQ), BF16),
                        pltpu.VMEM((ATT_UNROLL, ATT_TK, Q_PER_KV * ATT_TQ), BF16),
                        pltpu.VMEM((V_ROWS, Q_PER_KV * ATT_TQ), F32),
                        pltpu.VMEM((gw, ATT_TQ), BF16)])
    return pl.pallas_call(
        functools.partial(_attn_kernel, nkb=nkb, ctx_blocks=ctx_len // ATT_TK),
        grid_spec=grid_spec,
        out_shape=jax.ShapeDtypeStruct((B, T, ATTN_W), BF16),
        compiler_params=_cparams(("arbitrary", "arbitrary", "arbitrary"), 40),
        name="gqa_attention",
    )(bounded, qt, k, vt, proj, eye_tq)


def _gla_direction(qk, v, pa, a2, ab, st_ref, o_ref, fwd):
    C, c = GLA_C, GLA_SUB
    q = qk[:, 0:GLA_KW].astype(F32) * (GLA_DK ** -0.5)
    k = qk[:, GLA_KW:2 * GLA_KW].astype(F32)
    z = _dot(pa, a2) + ab
    yield
    g =(jnp.minimum(z, 0.0) - jnp.log(1.0 + jnp.exp(-jnp.abs(z)))) * (1.0 / GLA_TEMP)
    row = lax.broadcasted_iota(jnp.int32, (C, C), 0)
    col = lax.broadcasted_iota(jnp.int32, (C, C), 1)
    blk = jnp.bitwise_and(row, -c)
    if fwd:
        tri = col <= row
        inblk = tri & (col >= blk)
    else:
        tri = col >= row
        inblk = tri & (col < blk + c)
    sel = jnp.concatenate([jnp.where(tri, 1.0, 0.0), jnp.where(inblk, 1.0, 0.0)], axis=0).astype(BF16)
    g_hi = g.astype(BF16)
    g_lo = (g - g_hi.astype(F32)).astype(BF16)
    cs = _dot(sel, jnp.concatenate([g_hi, g_lo], axis=1))
    yield
    cs =cs[:, 0:GLA_KW] + cs[:, GLA_KW:2 * GLA_KW]
    b = cs[0:C]
    d = cs[C:2 * C]
    ref = b - d
    total = b[C - 1:C, :] if fwd else b[0:1, :]
    q_in = (q * jnp.exp(d)).astype(BF16)
    q_st = (q * jnp.exp(b)).astype(BF16)
    k_out = (k * jnp.exp(total - b)).astype(BF16)
    w_tot = jnp.exp(total)
    zero = jnp.zeros((), BF16)
    nb = C // c
    k_blk = []
    for i in range(nb):
        lo, hi = (0, (i + 1) * c) if fwd else (i * c, C)
        kb = (k[lo:hi] * jnp.exp(ref[i * c:i * c + 1, :] - b[lo:hi])).astype(BF16)
        pad = jnp.zeros((C - (hi - lo), GLA_KW), BF16)
        k_blk.append(jnp.concatenate(([kb, pad] if fwd else [pad, kb]) if hi - lo < C else [kb], axis=0))
    head_of_lane = jnp.right_shift(lax.broadcasted_iota(jnp.int32, (1, GLA_KW), 1), GLA_DK.bit_length() - 1)
    a_two = []
    for m in range(nb // 2):
        qr = q_in[2 * m * c:(2 * m + 2) * c]
        q_heads = jnp.concatenate([jnp.where(head_of_lane == h, qr, zero) for h in range(GLA_HEADS)], axis=0)
        a_two.append(_dot_nt(q_heads, jnp.concatenate(k_blk[2 * m:2 * m + 2], axis=0)))
    yield

    def a_block(h, i):
        m, r = divmod(i, 2)
        return a_two[m][h * 2 * c + r * c:h * 2 * c + (r + 1) * c, r * C:(r + 1) * C]

    a_cat = jnp.concatenate(
        [jnp.where(tri, jnp.concatenate([a_block(h, i) for i in range(nb)], axis=0), 0.0)
         for h in range(GLA_HEADS)], axis=1).astype(BF16)
    vhead_of_lane = jnp.right_shift(lax.broadcasted_iota(jnp.int32, (1, GLA_VW), 1), GLA_DV.bit_length() - 1)
    v_bd = jnp.concatenate([jnp.where(vhead_of_lane == h, v, zero) for h in range(GLA_HEADS)], axis=0)
    st = st_ref[...]
    o = _dot(a_cat, v_bd) + _dot_nt(q_st, st.astype(BF16))
    upd = lax.dot_general(v, k_out, (((0,), (0,)), ((), ())), preferred_element_type=F32)
    same_head = jnp.right_shift(lax.broadcasted_iota(jnp.int32, (GLA_VW, 1), 0),
                                GLA_DV.bit_length() - 1) == head_of_lane
    st_ref[...] = jnp.where(same_head, st * w_tot + upd, 0.0)
    o_ref[...] = o
    yield


def _gla_kernel(qkf_ref, vf_ref, paf_ref, qkb_ref, vb_ref, pab_ref, a2f_ref, abf_ref, a2b_ref, abb_ref,
                of_ref, ob_ref, stf_ref, stb_ref):
    @pl.when(pl.program_id(0) == 0)
    def _():
        stf_ref[...] = jnp.zeros_like(stf_ref)
        stb_ref[...] = jnp.zeros_like(stb_ref)

    chains = []
    for bi in range(qkf_ref.shape[0]):
        chains.append(_gla_direction(qkf_ref[bi], vf_ref[bi], paf_ref[bi], a2f_ref[...], abf_ref[...],
                                     stf_ref.at[bi], of_ref.at[bi], True))
        chains.append(_gla_direction(qkb_ref[bi], vb_ref[bi], pab_ref[bi], a2b_ref[...], abb_ref[...],
                                     stb_ref.at[bi], ob_ref.at[bi], False))
    for _ in range(4):
        for chain in chains:
            next(chain)


def _gla(proj, a2f, abf, a2b, abb, ctx_len):
    B, T = proj.shape[0], proj.shape[1]
    C = GLA_C
    nch, nctx = T // C, ctx_len // C

    def cb(s):
        return jnp.where(s < nctx, nctx - 1 - s, nch + nctx - 1 - s)

    full = lambda shape: pl.BlockSpec(shape, lambda s: (0,) * len(shape))
    return pl.pallas_call(
        _gla_kernel,
        grid=(nch,),
        in_specs=[pl.BlockSpec((B, C, 2 * GLA_KW), lambda s: (0, s, COL_GQK // (2 * GLA_KW))),
                  pl.BlockSpec((B, C, GLA_VW), lambda s: (0, s, COL_GV // GLA_VW)),
                  pl.BlockSpec((B, C, LANES), lambda s: (0, s, COL_GATE // LANES)),
                  pl.BlockSpec((B, C, 2 * GLA_KW), lambda s: (0, cb(s), COL_GQK // (2 * GLA_KW))),
                  pl.BlockSpec((B, C, GLA_VW), lambda s: (0, cb(s), COL_GV // GLA_VW)),
                  pl.BlockSpec((B, C, LANES), lambda s: (0, cb(s), COL_GATE // LANES)),
                  full((LANES, GLA_KW)), full((1, GLA_KW)), full((LANES, GLA_KW)), full((1, GLA_KW))],
        out_specs=[pl.BlockSpec((B, C, GLA_VW), lambda s: (0, s, 0)),
                   pl.BlockSpec((B, C, GLA_VW), lambda s: (0, cb(s), 0))],
        out_shape=[jax.ShapeDtypeStruct((B, T, GLA_VW), F32),
                   jax.ShapeDtypeStruct((B, T, GLA_VW), F32)],
        scratch_shapes=[pltpu.VMEM((B, GLA_VW, GLA_KW), F32),
                        pltpu.VMEM((B, GLA_VW, GLA_KW), F32)],
        compiler_params=_cparams(("arbitrary",), 32),
        name="gla_scan",
    )(proj, proj, proj, proj, proj, proj, a2f, abf, a2b, abb)


def _merge_kernel(x_ref, mod_ref, ma_ref, mg_ref, ms_ref, su_ref, sv_ref, sg_ref, gg_ref, ya_ref, of_ref, ob_ref,
                  onorm_ref, lng_ref, lnb_ref, sw_ref, sb_ref, wa_ref, wg_ref, ws_ref, wo_ref, pg_ref, pb_ref,
                  o_ref, *, tm, ctx_tiles, row_off, alpha):
    D = x_ref.shape[-1]
    i = pl.program_id(1) + row_off
    og = of_ref[...] + ob_ref[...]
    parts = []
    for h in range(GLA_HEADS):
        oh = og[:, h * GLA_DV:(h + 1) * GLA_DV]
        ms = jnp.mean(oh * oh, axis=-1, keepdims=True)
        parts.append(oh * lax.rsqrt(ms + EPS))
    yg = jnp.concatenate(parts, axis=1) * onorm_ref[...] * _silu(gg_ref[...].astype(F32))
    sv = sv_ref[...].astype(F32)
    mu = jnp.mean(sv, axis=-1, keepdims=True)
    dv = sv - mu
    var = jnp.mean(dv * dv, axis=-1, keepdims=True)
    vn = (dv * lax.rsqrt(var + EPS) * lng_ref[...] + lnb_ref[...]).astype(BF16)
    gw = SGU_W // SGU_GROUPS
    rows = []
    for n in range(tm // SGU_CHUNK):
        cols = []
        for g in range(SGU_GROUPS):
            blk = vn[n * SGU_CHUNK:(n + 1) * SGU_CHUNK, g * gw:(g + 1) * gw]
            cols.append(_dot(sw_ref[g], blk))
        rows.append(jnp.concatenate(cols, axis=1) + sb_ref[...])
    mixed = jnp.concatenate(rows, axis=0)
    ys = su_ref[...].astype(F32) * mixed * _silu(sg_ref[...].astype(F32))
    m = (jax.nn.sigmoid(ma_ref[...].astype(F32)) * _dot(ya_ref[...], wa_ref[...])
         + jax.nn.sigmoid(mg_ref[...].astype(F32)) * _dot(yg.astype(BF16), wg_ref[...])
         + jax.nn.sigmoid(ms_ref[...].astype(F32)) * _dot(ys.astype(BF16), ws_ref[...]))
    out = _dot(m.astype(BF16), wo_ref[...])
    gate = jnp.where(i < ctx_tiles, mod_ref[1:2, 2 * D:3 * D], mod_ref[0:1, 2 * D:3 * D])
    r = alpha * x_ref[...] + gate * out
    mu = jnp.mean(r, axis=-1, keepdims=True)
    dr = r - mu
    var = jnp.mean(dr * dr, axis=-1, keepdims=True)
    o_ref[...] = dr * lax.rsqrt(var + EPS) * pg_ref[...] + pb_ref[...]


def _merge(xall, modb, proj, ya, of, ob, p, ctx_len, tm, skip_ctx, alpha):
    B, T, D = xall.shape
    row_off = ctx_len // tm if skip_ctx else 0
    nt = T // tm - row_off
    rows = lambda w, c: pl.BlockSpec((None, tm, w), lambda b, i: (b, i + row_off, c))
    full = lambda shape: pl.BlockSpec(shape, lambda b, i: (0,) * len(shape))
    return pl.pallas_call(
        functools.partial(_merge_kernel, tm=tm, ctx_tiles=ctx_len // tm, row_off=row_off, alpha=alpha),
        grid=(B, nt),
        in_specs=[rows(D, 0),
                  pl.BlockSpec((None, 8, 3 * D), lambda b, i: (b, 0, 0)),
                  rows(D, COL_MA // D), rows(D, COL_MG // D), rows(D, COL_MS // D),
                  rows(SGU_W, COL_SU // SGU_W), rows(SGU_W, COL_SV // SGU_W), rows(SGU_W, COL_SG // SGU_W),
                  rows(GLA_VW, COL_GG // GLA_VW),
                  rows(ATTN_W, 0), rows(GLA_VW, 0), rows(GLA_VW, 0),
                  full((1, GLA_VW)), full((1, SGU_W)), full((1, SGU_W)),
                  full((SGU_GROUPS, SGU_CHUNK, SGU_CHUNK)), full((SGU_CHUNK, SGU_W)),
                  full((ATTN_W, D)), full((GLA_VW, D)), full((SGU_W, D)), full((D, D)),
                  full((1, D)), full((1, D))],
        out_specs=pl.BlockSpec((None, tm, D), lambda b, i: (b, i, 0)),
        out_shape=jax.ShapeDtypeStruct((B, nt * tm, D), F32),
        compiler_params=_cparams(("arbitrary", "arbitrary"), 48),
        name="merge_out",
    )(xall, modb, proj, proj, proj, proj, proj, proj, proj, ya, of, ob,
      p["onorm"], p["lng"], p["lnb"], p["sw"], p["sb"], p["wa"], p["wg"], p["ws"], p["wo"], p["pg"], p["pb"])


def _regroup_w_in(w):
    D = w.shape[0]
    sizes = (512, 128, 128, 512, 256, 256, 512, 16, 16, 512, 512, 512, 512, 1024, 1024, 1024)
    names = ("aq", "ak", "av", "ag", "gq", "gk", "gv", "af", "ab", "gg", "su", "sv", "sg", "ma", "mg", "ms")
    seg, off = {}, 0
    for n, s in zip(names, sizes):
        seg[n] = w[:, off:off + s]
        off += s
    order = ("ma", "mg", "ms", "aq", "ag", "gv", "gg", "su", "sv", "sg", "gq", "gk", "ak", "av", "af", "ab")
    used = sum(seg[n].shape[1] for n in order)
    cols = [seg[n] for n in order] + [jnp.zeros((D, PROJ_W - used), w.dtype)]
    return jnp.concatenate(cols, axis=1).astype(BF16)


def _rope_tables(n_lat, ctx_len):
    rows = n_lat // GRID_W
    row = jnp.repeat(jnp.arange(rows, dtype=F32), GRID_W)
    col = jnp.tile(jnp.arange(GRID_W, dtype=F32), rows)
    freqs = ROPE_THETA ** (-jnp.arange(0, ROPE_AXIS, 2, dtype=F32) / ROPE_AXIS)
    ang_r, ang_c = row[:, None] * freqs, col[:, None] * freqs
    cos = jnp.concatenate([jnp.cos(ang_r)] * 2 + [jnp.cos(ang_c)] * 2, axis=1)
    sin = jnp.concatenate([-jnp.sin(ang_r), jnp.sin(ang_r), -jnp.sin(ang_c), jnp.sin(ang_c)], axis=1)
    cos = jnp.concatenate([jnp.ones((ctx_len, HEAD_DIM), F32), cos], axis=0)
    sin = jnp.concatenate([jnp.zeros((ctx_len, HEAD_DIM), F32), sin], axis=0)
    return jnp.concatenate([cos, cos], axis=1), jnp.concatenate([sin, sin], axis=1)


def _largest_tile(total, cap, mult):
    best = mult
    for t in range(mult, cap + 1, mult):
        if total % t == 0:
            best = t
    return best


def kernel(x, c, ctx, c_ctx, w_ada, b_ada, w_in, attn_q_norm, attn_k_norm, gla_a2_f, gla_ab_f, gla_a2_b, gla_ab_b, gla_o_norm, sgu_ln_g, sgu_ln_b, sgu_w, sgu_b, w_br_attn, w_br_gla, w_br_sgu, w_out, post_ln_g, post_ln_b):
    B, n_lat, D = x.shape
    ctx_len = ctx.shape[1]
    depth = w_in.shape[0]
    T = ctx_len + n_lat
    assert D == D_MODEL and B <= 7
    assert ctx_len % ATT_TK == 0 and n_lat % ATT_TK == 0 and n_lat % GRID_W == 0
    alpha = (2 * depth) ** 0.25

    tm_proj = _largest_tile(T, 1056, 16)
    tn_proj = 1536
    tm_prep = _largest_tile(T, 768, ATT_TK)
    tm_merge = ATT_TK

    xall = jnp.concatenate([ctx, x], axis=1)
    cc = jnp.zeros((8, D), F32).at[0:B].set(c).at[B].set(c_ctx)
    mod = _ada(cc, w_ada, b_ada.reshape(depth, 1, 3 * D))
    cos_t, sin_t = _rope_tables(n_lat, ctx_len)
    consts = {
        "gq": jnp.kron(jnp.eye(N_HEADS, dtype=F32), jnp.ones((HEAD_DIM, HEAD_DIM), F32)).astype(BF16),
        "gk": jnp.kron(jnp.eye(N_KV, dtype=F32), jnp.ones((HEAD_DIM, HEAD_DIM), F32)).astype(BF16),
        "eq": jnp.eye(ATTN_W, dtype=BF16),
        "ek": jnp.eye(KV_W, dtype=BF16),
    }
    eye_tq = jnp.eye(ATT_TQ, dtype=BF16)

    for l in range(depth):
        last = l == depth - 1
        modb = jnp.zeros((B, 8, 3 * D), F32).at[:, 0].set(mod[l, 0:B]).at[:, 1].set(mod[l, B][None])
        w = _regroup_w_in(w_in[l])
        proj = _inproj(xall, modb, w, ctx_len, tm_proj, tn_proj)
        qn = jnp.concatenate([attn_q_norm[l]] * 2)[None]
        kn = jnp.concatenate([attn_k_norm[l]] * 2)[None]
        qt, kk, vt = _qkprep(proj, cos_t, sin_t, qn, kn, consts, tm_prep)
        score_bound = HEAD_DIM * jnp.max(jnp.abs(attn_q_norm[l])) * jnp.max(jnp.abs(attn_k_norm[l])) * Q_SCALE
        bounded = (score_bound <= SCORE_LIMIT).astype(jnp.int32).reshape(1)
        ya = _attention(bounded, qt, kk, vt, proj, eye_tq, ctx_len)
        a2f = jnp.zeros((LANES, GLA_KW), F32).at[0:GLA_RANK].set(gla_a2_f[l]).astype(BF16)
        a2b = jnp.zeros((LANES, GLA_KW), F32).at[GLA_RANK:2 * GLA_RANK].set(gla_a2_b[l]).astype(BF16)
        of, ob = _gla(proj, a2f, gla_ab_f[l][None], a2b, gla_ab_b[l][None], ctx_len)
        p = {
            "onorm": jnp.concatenate([gla_o_norm[l]] * GLA_HEADS)[None],
            "lng": sgu_ln_g[l][None], "lnb": sgu_ln_b[l][None],
            "sw": sgu_w[l].astype(BF16),
            "sb": jnp.repeat(sgu_b[l].T, SGU_W // SGU_GROUPS, axis=1),
            "wa": w_br_attn[l].astype(BF16), "wg": w_br_gla[l].astype(BF16), "ws": w_br_sgu[l].astype(BF16),
            "wo": w_out[l].astype(BF16), "pg": post_ln_g[l][None], "pb": post_ln_b[l][None],
        }
        xall = _merge(xall, modb, proj, ya, of, ob, p, ctx_len, tm_merge, last, alpha)
    return xall
```

```python
import functools

import jax
import jax.numpy as jnp
from jax import lax
from jax.experimental import pallas as pl
from jax.experimental.pallas import tpu as pltpu

F32 = jnp.float32
BF16 = jnp.bfloat16

D_MODEL = 1024
GRID_W = 64
HEAD_DIM = 64
N_HEADS = 8
N_KV = 2
Q_PER_KV = N_HEADS // N_KV
ATTN_W = N_HEADS * HEAD_DIM
KV_W = N_KV * HEAD_DIM
ROPE_THETA = 10000.0
ROPE_AXIS = HEAD_DIM // 2
GLA_HEADS = 4
GLA_DK = 64
GLA_DV = 128
GLA_KW = GLA_HEADS * GLA_DK
GLA_VW = GLA_HEADS * GLA_DV
GLA_RANK = 16
GLA_TEMP = 16.0
SGU_GROUPS = 4
SGU_CHUNK = 128
SGU_W = 512
EPS = 1e-6

LANES = 128
V7X_VMEM_BYTES = 64 * 1024 * 1024

COL_MA, COL_MG, COL_MS = 0, 1024, 2048
COL_AQ, COL_AG = 3072, 3584
COL_GV, COL_GG = 4096, 4608
COL_SU, COL_SV, COL_SG = 5120, 5632, 6144
COL_GQK = 6656
COL_AKV = 7168
COL_GATE = 7424
PROJ_W = 7680

SUBLANES = 8
ATT_TQ = 256
ATT_TK_MAX = 768
GLA_C = 128
GLA_SUB = 16
ATT_UNROLL = 5
Q_SCALE = 1.4426950408889634 * HEAD_DIM ** -0.5
SCORE_LIMIT = 96.0


def _cparams(sem, vmem_mb):
    return pltpu.CompilerParams(dimension_semantics=sem,
                                vmem_limit_bytes=min(vmem_mb * 1024 * 1024, V7X_VMEM_BYTES - (4 << 20)))


def _silu(x):
    return x * jax.nn.sigmoid(x)


def _silu_from_half(u):
    return u + u * jnp.tanh(u)


def _dot(a, b):
    return jnp.dot(a, b, preferred_element_type=F32)


def _dot_nt(a, b):
    return lax.dot_general(a, b, (((1,), (1,)), ((), ())), preferred_element_type=F32)


def _ada_kernel(c_ref, w_ref, b_ref, o_ref):
    c = c_ref[...]
    o_ref[...] = _dot(_silu(c).astype(BF16), w_ref[...].astype(BF16)) + b_ref[...]


def _ada(cc, w_ada, b_ada):
    L, D = w_ada.shape[0], w_ada.shape[1]
    return pl.pallas_call(
        _ada_kernel,
        grid=(L, 3),
        in_specs=[pl.BlockSpec((8, D), lambda l, j: (0, 0)),
                  pl.BlockSpec((None, D, D), lambda l, j: (l, 0, j)),
                  pl.BlockSpec((None, 1, D), lambda l, j: (l, 0, j))],
        out_specs=pl.BlockSpec((None, 8, D), lambda l, j: (l, 0, j)),
        out_shape=jax.ShapeDtypeStruct((L, 8, 3 * D), F32),
        compiler_params=_cparams(("arbitrary", "arbitrary"), 32),
        name="ada_mod",
    )(cc, w_ada, b_ada)


def _inproj_kernel(x_ref, mod_ref, w_ref, o_ref, h_ref, *, tm, ctx_len):
    i = pl.program_id(1)
    j = pl.program_id(2)

    @pl.when(j == 0)
    def _():
        D = x_ref.shape[-1]
        row = i * tm + lax.broadcasted_iota(jnp.int32, (tm, 1), 0)
        is_ctx = row < ctx_len
        shift = jnp.where(is_ctx, mod_ref[1:2, 0:D], mod_ref[0:1, 0:D])
        scale = jnp.where(is_ctx, mod_ref[1:2, D:2 * D], mod_ref[0:1, D:2 * D])
        h_ref[...] = (x_ref[...] * (1.0 + scale) + shift).astype(BF16)

    o_ref[...] = _dot(h_ref[...], w_ref[...]).astype(BF16)


def _inproj(xall, modb, w, ctx_len, tm, tn):
    B, T, D = xall.shape
    return pl.pallas_call(
        functools.partial(_inproj_kernel, tm=tm, ctx_len=ctx_len),
        grid=(B, T // tm, PROJ_W // tn),
        in_specs=[pl.BlockSpec((None, tm, D), lambda b, i, j: (b, i, 0)),
                  pl.BlockSpec((None, 8, 3 * D), lambda b, i, j: (b, 0, 0)),
                  pl.BlockSpec((D, tn), lambda b, i, j: (0, j))],
        out_specs=pl.BlockSpec((None, tm, tn), lambda b, i, j: (b, i, j)),
        out_shape=jax.ShapeDtypeStruct((B, T, PROJ_W), BF16),
        scratch_shapes=[pltpu.VMEM((tm, D), BF16)],
        compiler_params=_cparams(("arbitrary", "arbitrary", "arbitrary"), 40),
        name="in_proj",
    )(xall, modb, w)


def _rope(y, cos, sin, width):
    lane = lax.broadcasted_iota(jnp.int32, (1, width), 1)
    first = jnp.bitwise_and(lane, ROPE_AXIS - 1) < (ROPE_AXIS // 2)
    up = pltpu.roll(y, width - ROPE_AXIS // 2, 1)
    dn = pltpu.roll(y, ROPE_AXIS // 2, 1)
    return y * cos + jnp.where(first, up, dn) * sin


def _qkprep_kernel(aq_ref, akv_ref, cos_ref, sin_ref, qn_ref, kn_ref, gq_ref, gk_ref,
                   eq_ref, ek_ref, qt_ref, k_ref, vt_ref, *, tm, tk):
    cos = cos_ref[...]
    sin = sin_ref[...]
    aq = aq_ref[...].astype(F32)
    ssq = _dot((aq * aq).astype(BF16), gq_ref[...])
    yq = aq * lax.rsqrt(ssq * (1.0 / HEAD_DIM) + EPS) * jnp.concatenate([qn_ref[...]] * 4, axis=1)
    yq = _rope(yq, jnp.concatenate([cos] * 4, axis=1), jnp.concatenate([sin] * 4, axis=1), ATTN_W)
    yq = (yq * Q_SCALE).astype(BF16)
    qt = _dot_nt(eq_ref[...], yq).astype(BF16)
    qt_ref[...] = qt.reshape(N_HEADS, HEAD_DIM, tm)
    akv = akv_ref[...].astype(F32)
    ak = akv[:, 0:KV_W]
    ssk = _dot((ak * ak).astype(BF16), gk_ref[...])
    yk = ak * lax.rsqrt(ssk * (1.0 / HEAD_DIM) + EPS) * kn_ref[...]
    yk = _rope(yk, cos, sin, KV_W).astype(BF16)
    nkb = tm // tk
    k_ref[...] = yk.reshape(nkb, tk, KV_W)
    av = akv_ref[:, KV_W:2 * KV_W]
    vt = _dot_nt(ek_ref[...], av).astype(BF16)
    for g in range(N_KV):
        for n in range(nkb):
            vt_ref[g, n] = vt[g * HEAD_DIM:(g + 1) * HEAD_DIM, n * tk:(n + 1) * tk]


def _qkprep(proj, cos_t, sin_t, qn, kn, consts, tm, tk):
    B, T, _ = proj.shape
    nkb_t = T // tk
    nkb = tm // tk
    full = lambda shape: pl.BlockSpec(shape, lambda b, i: (0,) * len(shape))
    return pl.pallas_call(
        functools.partial(_qkprep_kernel, tm=tm, tk=tk),
        grid=(B, T // tm),
        in_specs=[pl.BlockSpec((None, tm, ATTN_W), lambda b, i: (b, i, COL_AQ // ATTN_W)),
                  pl.BlockSpec((None, tm, 2 * KV_W), lambda b, i: (b, i, COL_AKV // (2 * KV_W))),
                  pl.BlockSpec((tm, LANES), lambda b, i: (i, 0)),
                  pl.BlockSpec((tm, LANES), lambda b, i: (i, 0)),
                  full((1, LANES)), full((1, LANES)),
                  full((ATTN_W, ATTN_W)), full((KV_W, KV_W)),
                  full((ATTN_W, ATTN_W)), full((KV_W, KV_W))],
        out_specs=[pl.BlockSpec((None, N_HEADS, HEAD_DIM, tm), lambda b, i: (b, 0, 0, i)),
                   pl.BlockSpec((None, nkb, tk, KV_W), lambda b, i: (b, i, 0, 0)),
                   pl.BlockSpec((None, N_KV, nkb, HEAD_DIM, tk), lambda b, i: (b, 0, i, 0, 0))],
        out_shape=[jax.ShapeDtypeStruct((B, N_HEADS, HEAD_DIM, T), BF16),
                   jax.ShapeDtypeStruct((B, nkb_t, tk, KV_W), BF16),
                   jax.ShapeDtypeStruct((B, N_KV, nkb_t, HEAD_DIM, tk), BF16)],
        compiler_params=_cparams(("arbitrary", "arbitrary"), 40),
        name="qkv_prep",
    )(proj, proj, cos_t, sin_t, qn, kn, consts["gq"], consts["gk"], consts["eq"], consts["ek"])


def _attn_kernel(bounded_ref, qt_ref, k_ref, vt_ref, ag_ref, eye_ref, o_ref, q_ref, p_ref, acc_ref, oall_ref,
                 *, ctx_len):
    g = pl.program_id(1)
    i = pl.program_id(2)
    tq = qt_ref.shape[-1]
    nkb, tk = k_ref.shape[0], k_ref.shape[1]
    ctx_blocks = ctx_len // tq
    sub = lax.broadcasted_iota(jnp.int32, (KV_W, 1), 0)
    own = (sub >= g * HEAD_DIM) & (sub < (g + 1) * HEAD_DIM)
    for r in range(Q_PER_KV):
        q = qt_ref[r]
        q_ref[:, r * tq:(r + 1) * tq] = jnp.where(own, jnp.concatenate([q, q], axis=0), jnp.zeros((), BF16))
    acc_ref[...] = jnp.zeros_like(acc_ref)
    nq = Q_PER_KV * tq
    sl = acc_ref.shape[0] - HEAD_DIM

    def weights(j, slot, rows):
        p = jnp.exp2(_dot(k_ref[j, 0:rows, :], q_ref[...]))
        p_ref[slot, 0:rows, :] = p.astype(BF16)
        return jnp.sum(p.reshape(rows // sl, sl, nq), axis=0)

    def weighted_values(j, slot, rows):
        return _dot(vt_ref[j, :, 0:rows], p_ref[slot, 0:rows, :])

    def run(n, rows):
        U = ATT_UNROLL
        den = weights(0, 0, rows)

        def group(jj, den):
            j = U * jj
            part = None
            for u in range(U):
                den = den + weights(j + u + 1, (u + 1) % U, rows)
                wv = weighted_values(j + u, u, rows)
                part = wv if part is None else part + wv
            acc_ref[0:HEAD_DIM, :] += part
            return den

        groups = (n - 1) // U
        den = lax.fori_loop(0, groups, group, den)
        for t in range(U * groups, n):
            if t + 1 < n:
                den = den + weights(t + 1, (t + 1) % U, rows)
            acc_ref[0:HEAD_DIM, :] += weighted_values(t, t % U, rows)
        acc_ref[HEAD_DIM:HEAD_DIM + sl, :] = den

    def run_online(n, rows):
        for r in range(Q_PER_KV):
            q = q_ref[:, r * tq:(r + 1) * tq]

            def body(j, carry):
                m, den, acc = carry
                s = _dot(k_ref[j, 0:rows, :], q)
                m_new = jnp.maximum(m, jnp.max(s, axis=0, keepdims=True))
                p = jnp.exp2(s - m_new)
                alpha = jnp.exp2(m - m_new)
                return (m_new, alpha * den + jnp.sum(p, axis=0, keepdims=True),
                        alpha * acc + _dot(vt_ref[j, :, 0:rows], p.astype(BF16)))

            init = (jnp.full((1, tq), -jnp.inf, F32), jnp.zeros((1, tq), F32), jnp.zeros((HEAD_DIM, tq), F32))
            _, den, acc = lax.fori_loop(0, n, body, init)
            acc_ref[0:HEAD_DIM, r * tq:(r + 1) * tq] = acc
            acc_ref[HEAD_DIM:HEAD_DIM + sl, r * tq:(r + 1) * tq] = jnp.broadcast_to(den * (1.0 / sl), (sl, tq))

    bounded = bounded_ref[0] == 1
    is_ctx = i < ctx_blocks
    for cond, fn, n, rows in ((bounded & is_ctx, run, 1, ctx_len),
                              (bounded & jnp.logical_not(is_ctx), run, nkb, tk),
                              (jnp.logical_not(bounded) & is_ctx, run_online, 1, ctx_len),
                              (jnp.logical_not(bounded) & jnp.logical_not(is_ctx), run_online, nkb, tk)):
        pl.when(cond)(functools.partial(fn, n, rows))

    for r in range(Q_PER_KV):
        acc = acc_ref[:, r * tq:(r + 1) * tq]
        den = jnp.sum(acc[HEAD_DIM:HEAD_DIM + sl, :], axis=0, keepdims=True)
        o = acc[0:HEAD_DIM, :] * (1.0 / den)
        oall_ref[r * HEAD_DIM:(r + 1) * HEAD_DIM, :] = o.astype(BF16)
    o_t = _dot_nt(eye_ref[...], oall_ref[...])
    o_ref[...] = (o_t * _silu_from_half(ag_ref[...].astype(F32))).astype(BF16)


def _attention(bounded, qt, k, vt, proj, eye_tq, ctx_len):
    B, T = proj.shape[0], proj.shape[1]
    nkb, tk = k.shape[1], k.shape[2]
    gw = Q_PER_KV * HEAD_DIM
    grid_spec = pltpu.PrefetchScalarGridSpec(
        num_scalar_prefetch=1,
        grid=(B, N_KV, T // ATT_TQ),
        in_specs=[pl.BlockSpec((None, Q_PER_KV, HEAD_DIM, ATT_TQ), lambda b, g, i, f: (b, g, 0, i)),
                  pl.BlockSpec((None, nkb, tk, KV_W), lambda b, g, i, f: (b, 0, 0, 0)),
                  pl.BlockSpec((None, None, nkb, HEAD_DIM, tk), lambda b, g, i, f: (b, g, 0, 0, 0)),
                  pl.BlockSpec((None, ATT_TQ, gw), lambda b, g, i, f: (b, i, COL_AG // gw + g)),
                  pl.BlockSpec((ATT_TQ, ATT_TQ), lambda b, g, i, f: (0, 0))],
        out_specs=pl.BlockSpec((None, ATT_TQ, gw), lambda b, g, i, f: (b, i, g)),
        scratch_shapes=[pltpu.VMEM((KV_W, Q_PER_KV * ATT_TQ), BF16),
                        pltpu.VMEM((ATT_UNROLL, tk, Q_PER_KV * ATT_TQ), BF16),
                        pltpu.VMEM((HEAD_DIM + SUBLANES, Q_PER_KV * ATT_TQ), F32),
                        pltpu.VMEM((gw, ATT_TQ), BF16)])
    return pl.pallas_call(
        functools.partial(_attn_kernel, ctx_len=ctx_len),
        grid_spec=grid_spec,
        out_shape=jax.ShapeDtypeStruct((B, T, ATTN_W), BF16),
        compiler_params=_cparams(("arbitrary", "arbitrary", "arbitrary"), 40),
        name="gqa_attention",
    )(bounded, qt, k, vt, proj, eye_tq)


def _gla_direction(qk, v, pa, a2, ab, st_ref, o_ref, fwd):
    C, c = GLA_C, GLA_SUB
    q = qk[:, 0:GLA_KW].astype(F32) * (GLA_DK ** -0.5)
    k = qk[:, GLA_KW:2 * GLA_KW].astype(F32)
    z = _dot(pa, a2) + ab
    yield
    g =(jnp.minimum(z, 0.0) - jnp.log(1.0 + jnp.exp(-jnp.abs(z)))) * (1.0 / GLA_TEMP)
    row = lax.broadcasted_iota(jnp.int32, (C, C), 0)
    col = lax.broadcasted_iota(jnp.int32, (C, C), 1)
    blk = jnp.bitwise_and(row, -c)
    if fwd:
        tri = col <= row
        inblk = tri & (col >= blk)
    else:
        tri = col >= row
        inblk = tri & (col < blk + c)
    sel = jnp.concatenate([jnp.where(tri, 1.0, 0.0), jnp.where(inblk, 1.0, 0.0)], axis=0).astype(BF16)
    g_hi = g.astype(BF16)
    g_lo = (g - g_hi.astype(F32)).astype(BF16)
    cs = _dot(sel, jnp.concatenate([g_hi, g_lo], axis=1))
    yield
    cs =cs[:, 0:GLA_KW] + cs[:, GLA_KW:2 * GLA_KW]
    b = cs[0:C]
    d = cs[C:2 * C]
    ref = b - d
    total = b[C - 1:C, :] if fwd else b[0:1, :]
    q_in = (q * jnp.exp(d)).astype(BF16)
    q_st = (q * jnp.exp(b)).astype(BF16)
    k_out = (k * jnp.exp(total - b)).astype(BF16)
    w_tot = jnp.exp(total)
    zero = jnp.zeros((), BF16)
    nb = C // c
    k_blk = []
    for i in range(nb):
        lo, hi = (0, (i + 1) * c) if fwd else (i * c, C)
        kb = (k[lo:hi] * jnp.exp(ref[i * c:i * c + 1, :] - b[lo:hi])).astype(BF16)
        pad = jnp.zeros((C - (hi - lo), GLA_KW), BF16)
        k_blk.append(jnp.concatenate(([kb, pad] if fwd else [pad, kb]) if hi - lo < C else [kb], axis=0))
    head_of_lane = jnp.right_shift(lax.broadcasted_iota(jnp.int32, (1, GLA_KW), 1), GLA_DK.bit_length() - 1)
    a_two = []
    for m in range(nb // 2):
        qr = q_in[2 * m * c:(2 * m + 2) * c]
        q_heads = jnp.concatenate([jnp.where(head_of_lane == h, qr, zero) for h in range(GLA_HEADS)], axis=0)
        a_two.append(_dot_nt(q_heads, jnp.concatenate(k_blk[2 * m:2 * m + 2], axis=0)))
    yield

    def a_block(h, i):
        m, r = divmod(i, 2)
        return a_two[m][h * 2 * c + r * c:h * 2 * c + (r + 1) * c, r * C:(r + 1) * C]

    a_cat = jnp.concatenate(
        [jnp.where(tri, jnp.concatenate([a_block(h, i) for i in range(nb)], axis=0), 0.0)
         for h in range(GLA_HEADS)], axis=1).astype(BF16)
    vhead_of_lane = jnp.right_shift(lax.broadcasted_iota(jnp.int32, (1, GLA_VW), 1), GLA_DV.bit_length() - 1)
    v_bd = jnp.concatenate([jnp.where(vhead_of_lane == h, v, zero) for h in range(GLA_HEADS)], axis=0)
    st = st_ref[...]
    o = _dot(a_cat, v_bd) + _dot_nt(q_st, st.astype(BF16))
    upd = lax.dot_general(v, k_out, (((0,), (0,)), ((), ())), preferred_element_type=F32)
    same_head = jnp.right_shift(lax.broadcasted_iota(jnp.int32, (GLA_VW, 1), 0),
                                GLA_DV.bit_length() - 1) == head_of_lane
    st_ref[...] = jnp.where(same_head, st * w_tot + upd, 0.0)
    o_ref[...] = o
    yield


def _gla_kernel(qkf_ref, vf_ref, paf_ref, qkb_ref, vb_ref, pab_ref, a2f_ref, abf_ref, a2b_ref, abb_ref,
                of_ref, ob_ref, stf_ref, stb_ref):
    @pl.when(pl.program_id(0) == 0)
    def _():
        stf_ref[...] = jnp.zeros_like(stf_ref)
        stb_ref[...] = jnp.zeros_like(stb_ref)

    chains = []
    for bi in range(qkf_ref.shape[0]):
        chains.append(_gla_direction(qkf_ref[bi], vf_ref[bi], paf_ref[bi], a2f_ref[...], abf_ref[...],
                                     stf_ref.at[bi], of_ref.at[bi], True))
        chains.append(_gla_direction(qkb_ref[bi], vb_ref[bi], pab_ref[bi], a2b_ref[...], abb_ref[...],
                                     stb_ref.at[bi], ob_ref.at[bi], False))
    for _ in range(4):
        for chain in chains:
            next(chain)


def _gla(proj, a2f, abf, a2b, abb, ctx_len):
    B, T = proj.shape[0], proj.shape[1]
    C = GLA_C
    nch, nctx = T // C, ctx_len // C

    def cb(s):
        return jnp.where(s < nctx, nctx - 1 - s, nch + nctx - 1 - s)

    full = lambda shape: pl.BlockSpec(shape, lambda s: (0,) * len(shape))
    return pl.pallas_call(
        _gla_kernel,
        grid=(nch,),
        in_specs=[pl.BlockSpec((B, C, 2 * GLA_KW), lambda s: (0, s, COL_GQK // (2 * GLA_KW))),
                  pl.BlockSpec((B, C, GLA_VW), lambda s: (0, s, COL_GV // GLA_VW)),
                  pl.BlockSpec((B, C, LANES), lambda s: (0, s, COL_GATE // LANES)),
                  pl.BlockSpec((B, C, 2 * GLA_KW), lambda s: (0, cb(s), COL_GQK // (2 * GLA_KW))),
                  pl.BlockSpec((B, C, GLA_VW), lambda s: (0, cb(s), COL_GV // GLA_VW)),
                  pl.BlockSpec((B, C, LANES), lambda s: (0, cb(s), COL_GATE // LANES)),
                  full((LANES, GLA_KW)), full((1, GLA_KW)), full((LANES, GLA_KW)), full((1, GLA_KW))],
        out_specs=[pl.BlockSpec((B, C, GLA_VW), lambda s: (0, s, 0)),
                   pl.BlockSpec((B, C, GLA_VW), lambda s: (0, cb(s), 0))],
        out_shape=[jax.ShapeDtypeStruct((B, T, GLA_VW), F32),
                   jax.ShapeDtypeStruct((B, T, GLA_VW), F32)],
        scratch_shapes=[pltpu.VMEM((B, GLA_VW, GLA_KW), F32),
                        pltpu.VMEM((B, GLA_VW, GLA_KW), F32)],
        compiler_params=_cparams(("arbitrary",), 32),
        name="gla_scan",
    )(proj, proj, proj, proj, proj, proj, a2f, abf, a2b, abb)


def _merge_kernel(x_ref, mod_ref, ma_ref, mg_ref, ms_ref, su_ref, sv_ref, sg_ref, gg_ref, ya_ref, of_ref, ob_ref,
                  onorm_ref, lng_ref, lnb_ref, sw_ref, sb_ref, wa_ref, wg_ref, ws_ref, wo_ref, pg_ref, pb_ref,
                  o_ref, *, tm, ctx_tiles, row_off, alpha):
    D = x_ref.shape[-1]
    i = pl.program_id(1) + row_off
    og = of_ref[...] + ob_ref[...]
    parts = []
    for h in range(GLA_HEADS):
        oh = og[:, h * GLA_DV:(h + 1) * GLA_DV]
        ms = jnp.mean(oh * oh, axis=-1, keepdims=True)
        parts.append(oh * lax.rsqrt(ms + EPS))
    yg = jnp.concatenate(parts, axis=1) * onorm_ref[...] * _silu_from_half(gg_ref[...]).astype(F32)
    sv = sv_ref[...].astype(F32)
    mu = jnp.mean(sv, axis=-1, keepdims=True)
    dv = sv - mu
    var = jnp.mean(dv * dv, axis=-1, keepdims=True)
    vn = (dv * lax.rsqrt(var + EPS) * lng_ref[...] + lnb_ref[...]).astype(BF16)
    gw = SGU_W // SGU_GROUPS
    rows = []
    for n in range(tm // SGU_CHUNK):
        cols = []
        for g in range(SGU_GROUPS):
            blk = vn[n * SGU_CHUNK:(n + 1) * SGU_CHUNK, g * gw:(g + 1) * gw]
            cols.append(_dot(sw_ref[g], blk))
        rows.append(jnp.concatenate(cols, axis=1) + sb_ref[...])
    mixed = jnp.concatenate(rows, axis=0)
    ys = su_ref[...].astype(F32) * mixed * _silu_from_half(sg_ref[...]).astype(F32)

    def gated(g_ref, y, w_ref):
        half = _dot(y, w_ref[...]).astype(BF16)
        return half + half * jnp.tanh(g_ref[...])

    m = (gated(ma_ref, ya_ref[...], wa_ref) + gated(mg_ref, yg.astype(BF16), wg_ref)
         + gated(ms_ref, ys.astype(BF16), ws_ref))
    out = _dot(m, wo_ref[...])
    gate = jnp.where(i < ctx_tiles, mod_ref[1:2, 2 * D:3 * D], mod_ref[0:1, 2 * D:3 * D])
    r = alpha * x_ref[...] + gate * out
    mu = jnp.mean(r, axis=-1, keepdims=True)
    dr = r - mu
    var = jnp.mean(dr * dr, axis=-1, keepdims=True)
    o_ref[...] = dr * lax.rsqrt(var + EPS) * pg_ref[...] + pb_ref[...]


def _merge(xall, modb, proj, ya, of, ob, p, ctx_len, tm, skip_ctx, alpha):
    B, T, D = xall.shape
    row_off = ctx_len // tm if skip_ctx else 0
    nt = T // tm - row_off
    rows = lambda w, c: pl.BlockSpec((None, tm, w), lambda b, i: (b, i + row_off, c))
    full = lambda shape: pl.BlockSpec(shape, lambda b, i: (0,) * len(shape))
    return pl.pallas_call(
        functools.partial(_merge_kernel, tm=tm, ctx_tiles=ctx_len // tm, row_off=row_off, alpha=alpha),
        grid=(B, nt),
        in_specs=[rows(D, 0),
                  pl.BlockSpec((None, 8, 3 * D), lambda b, i: (b, 0, 0)),
                  rows(D, COL_MA // D), rows(D, COL_MG // D), rows(D, COL_MS // D),
                  rows(SGU_W, COL_SU // SGU_W), rows(SGU_W, COL_SV // SGU_W), rows(SGU_W, COL_SG // SGU_W),
                  rows(GLA_VW, COL_GG // GLA_VW),
                  rows(ATTN_W, 0), rows(GLA_VW, 0), rows(GLA_VW, 0),
                  full((1, GLA_VW)), full((1, SGU_W)), full((1, SGU_W)),
                  full((SGU_GROUPS, SGU_CHUNK, SGU_CHUNK)), full((SGU_CHUNK, SGU_W)),
                  full((ATTN_W, D)), full((GLA_VW, D)), full((SGU_W, D)), full((D, D)),
                  full((1, D)), full((1, D))],
        out_specs=pl.BlockSpec((None, tm, D), lambda b, i: (b, i, 0)),
        out_shape=jax.ShapeDtypeStruct((B, nt * tm, D), F32),
        compiler_params=_cparams(("arbitrary", "arbitrary"), 48),
        name="merge_out",
    )(xall, modb, proj, proj, proj, proj, proj, proj, proj, ya, of, ob,
      p["onorm"], p["lng"], p["lnb"], p["sw"], p["sb"], p["wa"], p["wg"], p["ws"], p["wo"], p["pg"], p["pb"])


def _regroup_w_in(w):
    D = w.shape[0]
    sizes = (512, 128, 128, 512, 256, 256, 512, 16, 16, 512, 512, 512, 512, 1024, 1024, 1024)
    names = ("aq", "ak", "av", "ag", "gq", "gk", "gv", "af", "ab", "gg", "su", "sv", "sg", "ma", "mg", "ms")
    halved = ("ag", "gg", "sg", "ma", "mg", "ms")
    seg, off = {}, 0
    for n, s in zip(names, sizes):
        col = w[:, off:off + s]
        seg[n] = (col * 0.5 if n in halved else col).astype(BF16)
        off += s
    order = ("ma", "mg", "ms", "aq", "ag", "gv", "gg", "su", "sv", "sg", "gq", "gk", "ak", "av", "af", "ab")
    used = sum(seg[n].shape[1] for n in order)
    cols = [seg[n] for n in order] + [jnp.zeros((D, PROJ_W - used), BF16)]
    return jnp.concatenate(cols, axis=1)


def _rope_tables(n_lat, ctx_len):
    rows = n_lat // GRID_W
    row = jnp.repeat(jnp.arange(rows, dtype=F32), GRID_W)
    col = jnp.tile(jnp.arange(GRID_W, dtype=F32), rows)
    freqs = ROPE_THETA ** (-jnp.arange(0, ROPE_AXIS, 2, dtype=F32) / ROPE_AXIS)
    ang_r, ang_c = row[:, None] * freqs, col[:, None] * freqs
    cos = jnp.concatenate([jnp.cos(ang_r)] * 2 + [jnp.cos(ang_c)] * 2, axis=1)
    sin = jnp.concatenate([-jnp.sin(ang_r), jnp.sin(ang_r), -jnp.sin(ang_c), jnp.sin(ang_c)], axis=1)
    cos = jnp.concatenate([jnp.ones((ctx_len, HEAD_DIM), F32), cos], axis=0)
    sin = jnp.concatenate([jnp.zeros((ctx_len, HEAD_DIM), F32), sin], axis=0)
    return jnp.concatenate([cos, cos], axis=1), jnp.concatenate([sin, sin], axis=1)


def _largest_tile(total, cap, mult):
    best = mult
    for t in range(mult, cap + 1, mult):
        if total % t == 0:
            best = t
    return best


def kernel(x, c, ctx, c_ctx, w_ada, b_ada, w_in, attn_q_norm, attn_k_norm, gla_a2_f, gla_ab_f, gla_a2_b, gla_ab_b, gla_o_norm, sgu_ln_g, sgu_ln_b, sgu_w, sgu_b, w_br_attn, w_br_gla, w_br_sgu, w_out, post_ln_g, post_ln_b):
    B, n_lat, D = x.shape
    ctx_len = ctx.shape[1]
    depth = w_in.shape[0]
    T = ctx_len + n_lat
    assert D == D_MODEL and B <= 7
    assert ctx_len % ATT_TQ == 0 and n_lat % ATT_TQ == 0 and n_lat % GRID_W == 0
    alpha = (2 * depth) ** 0.25

    tm_proj = _largest_tile(T, 1056, 16)
    tn_proj = 1536
    tk_att = _largest_tile(T, ATT_TK_MAX, ATT_TQ)
    assert ctx_len <= tk_att
    tm_prep = tk_att
    tm_merge = ATT_TQ

    xall = jnp.concatenate([ctx, x], axis=1)
    cc = jnp.zeros((8, D), F32).at[0:B].set(c).at[B].set(c_ctx)
    mod = _ada(cc, w_ada, b_ada.reshape(depth, 1, 3 * D))
    cos_t, sin_t = _rope_tables(n_lat, ctx_len)
    consts = {
        "gq": jnp.kron(jnp.eye(N_HEADS, dtype=F32), jnp.ones((HEAD_DIM, HEAD_DIM), F32)).astype(BF16),
        "gk": jnp.kron(jnp.eye(N_KV, dtype=F32), jnp.ones((HEAD_DIM, HEAD_DIM), F32)).astype(BF16),
        "eq": jnp.eye(ATTN_W, dtype=BF16),
        "ek": jnp.eye(KV_W, dtype=BF16),
    }
    eye_tq = jnp.eye(ATT_TQ, dtype=BF16)

    for l in range(depth):
        last = l == depth - 1
        modb = jnp.zeros((B, 8, 3 * D), F32).at[:, 0].set(mod[l, 0:B]).at[:, 1].set(mod[l, B][None])
        w = _regroup_w_in(w_in[l])
        proj = _inproj(xall, modb, w, ctx_len, tm_proj, tn_proj)
        qn = jnp.concatenate([attn_q_norm[l]] * 2)[None]
        kn = jnp.concatenate([attn_k_norm[l]] * 2)[None]
        qt, kk, vt = _qkprep(proj, cos_t, sin_t, qn, kn, consts, tm_prep, tk_att)
        score_bound = HEAD_DIM * jnp.max(jnp.abs(attn_q_norm[l])) * jnp.max(jnp.abs(attn_k_norm[l])) * Q_SCALE
        bounded = (score_bound <= SCORE_LIMIT).astype(jnp.int32).reshape(1)
        ya = _attention(bounded, qt, kk, vt, proj, eye_tq, ctx_len)
        a2f = jnp.zeros((LANES, GLA_KW), F32).at[0:GLA_RANK].set(gla_a2_f[l]).astype(BF16)
        a2b = jnp.zeros((LANES, GLA_KW), F32).at[GLA_RANK:2 * GLA_RANK].set(gla_a2_b[l]).astype(BF16)
        of, ob = _gla(proj, a2f, gla_ab_f[l][None], a2b, gla_ab_b[l][None], ctx_len)
        p = {
            "onorm": jnp.concatenate([gla_o_norm[l]] * GLA_HEADS)[None],
            "lng": sgu_ln_g[l][None], "lnb": sgu_ln_b[l][None],
            "sw": sgu_w[l].astype(BF16),
            "sb": jnp.repeat(sgu_b[l].T, SGU_W // SGU_GROUPS, axis=1),
            "wa": (0.5 * w_br_attn[l]).astype(BF16), "wg": (0.5 * w_br_gla[l]).astype(BF16),
            "ws": (0.5 * w_br_sgu[l]).astype(BF16),
            "wo": w_out[l].astype(BF16), "pg": post_ln_g[l][None], "pb": post_ln_b[l][None],
        }
        xall = _merge(xall, modb, proj, ya, of, ob, p, ctx_len, tm_merge, last, alpha)
    return xall
```

```python
import functools

import jax
import jax.numpy as jnp
from jax import lax
from jax.experimental import pallas as pl
from jax.experimental.pallas import tpu as pltpu

F32 = jnp.float32
BF16 = jnp.bfloat16

D_MODEL = 1024
GRID_W = 64
HEAD_DIM = 64
N_HEADS = 8
N_KV = 2
Q_PER_KV = N_HEADS // N_KV
ATTN_W = N_HEADS * HEAD_DIM
KV_W = N_KV * HEAD_DIM
ROPE_THETA = 10000.0
ROPE_AXIS = HEAD_DIM // 2
GLA_HEADS = 4
GLA_DK = 64
GLA_DV = 128
GLA_KW = GLA_HEADS * GLA_DK
GLA_VW = GLA_HEADS * GLA_DV
GLA_RANK = 16
GLA_TEMP = 16.0
SGU_GROUPS = 4
SGU_CHUNK = 128
SGU_W = 512
EPS = 1e-6

LANES = 128
V7X_VMEM_BYTES = 64 * 1024 * 1024

COL_MA, COL_MG, COL_MS = 0, 1024, 2048
COL_AQ, COL_AG = 3072, 3584
COL_GV, COL_GG = 4096, 4608
COL_SU, COL_SV, COL_SG = 5120, 5632, 6144
COL_GQK = 6656
COL_AKV = 7168
COL_GATE = 7424
PROJ_W = 7680

SUBLANES = 8
ATT_TQ = 256
ATT_NSB = 2
ATT_TK_MAX = 768
GLA_C = 128
GLA_SUB = 16
ATT_UNROLL = 5
Q_SCALE = 1.4426950408889634 * HEAD_DIM ** -0.5
SCORE_LIMIT = 96.0


def _cparams(sem, vmem_mb):
    return pltpu.CompilerParams(dimension_semantics=sem,
                                vmem_limit_bytes=min(vmem_mb * 1024 * 1024, V7X_VMEM_BYTES - (4 << 20)))


def _silu(x):
    return x * jax.nn.sigmoid(x)


def _silu_from_half(u):
    return u + u * jnp.tanh(u)


def _dot(a, b):
    return jnp.dot(a, b, preferred_element_type=F32)


def _dot_nt(a, b):
    return lax.dot_general(a, b, (((1,), (1,)), ((), ())), preferred_element_type=F32)


def _ada_kernel(c_ref, w_ref, b_ref, o_ref):
    c = c_ref[...]
    o_ref[...] = _dot(_silu(c).astype(BF16), w_ref[...].astype(BF16)) + b_ref[...]


def _ada(cc, w_ada, b_ada):
    L, D = w_ada.shape[0], w_ada.shape[1]
    return pl.pallas_call(
        _ada_kernel,
        grid=(L, 3),
        in_specs=[pl.BlockSpec((8, D), lambda l, j: (0, 0)),
                  pl.BlockSpec((None, D, D), lambda l, j: (l, 0, j)),
                  pl.BlockSpec((None, 1, D), lambda l, j: (l, 0, j))],
        out_specs=pl.BlockSpec((None, 8, D), lambda l, j: (l, 0, j)),
        out_shape=jax.ShapeDtypeStruct((L, 8, 3 * D), F32),
        compiler_params=_cparams(("arbitrary", "arbitrary"), 32),
        name="ada_mod",
    )(cc, w_ada, b_ada)


def _inproj_kernel(x_ref, mod_ref, w_ref, o_ref, h_ref, *, tm, ctx_len):
    i = pl.program_id(1)
    j = pl.program_id(2)

    @pl.when(j == 0)
    def _():
        D = x_ref.shape[-1]
        row = i * tm + lax.broadcasted_iota(jnp.int32, (tm, 1), 0)
        is_ctx = row < ctx_len
        shift = jnp.where(is_ctx, mod_ref[1:2, 0:D], mod_ref[0:1, 0:D])
        scale = jnp.where(is_ctx, mod_ref[1:2, D:2 * D], mod_ref[0:1, D:2 * D])
        h_ref[...] = (x_ref[...] * (1.0 + scale) + shift).astype(BF16)

    o_ref[...] = _dot(h_ref[...], w_ref[...]).astype(BF16)


def _inproj(xall, modb, w_all, layer, ctx_len, tm, tn):
    B, T, D = xall.shape
    return pl.pallas_call(
        functools.partial(_inproj_kernel, tm=tm, ctx_len=ctx_len),
        grid=(B, T // tm, PROJ_W // tn),
        in_specs=[pl.BlockSpec((None, tm, D), lambda b, i, j: (b, i, 0)),
                  pl.BlockSpec((None, 8, 3 * D), lambda b, i, j: (b, 0, 0)),
                  pl.BlockSpec((None, D, tn), lambda b, i, j: (layer, 0, j))],
        out_specs=pl.BlockSpec((None, tm, tn), lambda b, i, j: (b, i, j)),
        out_shape=jax.ShapeDtypeStruct((B, T, PROJ_W), BF16),
        scratch_shapes=[pltpu.VMEM((tm, D), BF16)],
        compiler_params=_cparams(("arbitrary", "arbitrary", "arbitrary"), 40),
        name="in_proj",
    )(xall, modb, w_all)


def _rope(y, cos, sin, width):
    lane = lax.broadcasted_iota(jnp.int32, (1, width), 1)
    first = jnp.bitwise_and(lane, ROPE_AXIS - 1) < (ROPE_AXIS // 2)
    up = pltpu.roll(y, width - ROPE_AXIS // 2, 1)
    dn = pltpu.roll(y, ROPE_AXIS // 2, 1)
    return y * cos + jnp.where(first, up, dn) * sin


def _qkprep_kernel(aq_ref, akv_ref, cos_ref, sin_ref, qn_ref, kn_ref, gq_ref, gk_ref,
                   eq_ref, ek_ref, qt_ref, k_ref, vt_ref, *, tm, tk):
    cos = cos_ref[...]
    sin = sin_ref[...]
    aq = aq_ref[...].astype(F32)
    ssq = _dot((aq * aq).astype(BF16), gq_ref[...])
    yq = aq * lax.rsqrt(ssq * (1.0 / HEAD_DIM) + EPS) * jnp.concatenate([qn_ref[...]] * 4, axis=1)
    yq = _rope(yq, jnp.concatenate([cos] * 4, axis=1), jnp.concatenate([sin] * 4, axis=1), ATTN_W)
    yq = (yq * Q_SCALE).astype(BF16)
    qt = _dot_nt(eq_ref[...], yq).astype(BF16)
    zeros = jnp.zeros((HEAD_DIM, ATT_TQ), BF16)
    for g in range(N_KV):
        for qb in range(tm // ATT_TQ):
            for r in range(Q_PER_KV):
                h = g * Q_PER_KV + r
                cols = slice((qb * Q_PER_KV + r) * ATT_TQ, (qb * Q_PER_KV + r + 1) * ATT_TQ)
                qt_ref[g, g * HEAD_DIM:(g + 1) * HEAD_DIM, cols] = qt[h * HEAD_DIM:(h + 1) * HEAD_DIM,
                                                                      qb * ATT_TQ:(qb + 1) * ATT_TQ]
                qt_ref[g, (1 - g) * HEAD_DIM:(2 - g) * HEAD_DIM, cols] = zeros
    akv = akv_ref[...].astype(F32)
    ak = akv[:, 0:KV_W]
    ssk = _dot((ak * ak).astype(BF16), gk_ref[...])
    yk = ak * lax.rsqrt(ssk * (1.0 / HEAD_DIM) + EPS) * kn_ref[...]
    yk = _rope(yk, cos, sin, KV_W).astype(BF16)
    nkb = tm // tk
    k_ref[...] = yk.reshape(nkb, tk, KV_W)
    av = akv_ref[:, KV_W:2 * KV_W]
    vt = _dot_nt(ek_ref[...], av).astype(BF16)
    for g in range(N_KV):
        for n in range(nkb):
            vt_ref[g, n] = vt[g * HEAD_DIM:(g + 1) * HEAD_DIM, n * tk:(n + 1) * tk]


def _qkprep(proj, cos_t, sin_t, qn, kn, consts, tm, tk):
    B, T, _ = proj.shape
    nkb_t = T // tk
    nkb = tm // tk
    full = lambda shape: pl.BlockSpec(shape, lambda b, i: (0,) * len(shape))
    return pl.pallas_call(
        functools.partial(_qkprep_kernel, tm=tm, tk=tk),
        grid=(B, T // tm),
        in_specs=[pl.BlockSpec((None, tm, ATTN_W), lambda b, i: (b, i, COL_AQ // ATTN_W)),
                  pl.BlockSpec((None, tm, 2 * KV_W), lambda b, i: (b, i, COL_AKV // (2 * KV_W))),
                  pl.BlockSpec((tm, LANES), lambda b, i: (i, 0)),
                  pl.BlockSpec((tm, LANES), lambda b, i: (i, 0)),
                  full((1, LANES)), full((1, LANES)),
                  full((ATTN_W, ATTN_W)), full((KV_W, KV_W)),
                  full((ATTN_W, ATTN_W)), full((KV_W, KV_W))],
        out_specs=[pl.BlockSpec((None, N_KV, KV_W, Q_PER_KV * tm), lambda b, i: (b, 0, 0, i)),
                   pl.BlockSpec((None, nkb, tk, KV_W), lambda b, i: (b, i, 0, 0)),
                   pl.BlockSpec((None, N_KV, nkb, HEAD_DIM, tk), lambda b, i: (b, 0, i, 0, 0))],
        out_shape=[jax.ShapeDtypeStruct((B, N_KV, KV_W, Q_PER_KV * T), BF16),
                   jax.ShapeDtypeStruct((B, nkb_t, tk, KV_W), BF16),
                   jax.ShapeDtypeStruct((B, N_KV, nkb_t, HEAD_DIM, tk), BF16)],
        compiler_params=_cparams(("arbitrary", "arbitrary"), 40),
        name="qkv_prep",
    )(proj, proj, cos_t, sin_t, qn, kn, consts["gq"], consts["gk"], consts["eq"], consts["ek"])


def _attn_kernel(bounded_ref, *refs, nsb, n_tiles, rows):
    q_refs, (k_ref, vt_ref), ag_refs, eye_ref = refs[0:nsb], refs[nsb:nsb + 2], refs[nsb + 2:2 * nsb + 2], refs[2 * nsb + 2]
    o_ref, p_ref, acc_ref, oall_ref = refs[-4:]
    nq = q_refs[0].shape[-1]
    tq = nq // Q_PER_KV
    sl = acc_ref.shape[1] - HEAD_DIM
    U = p_ref.shape[0]
    acc_ref[...] = jnp.zeros_like(acc_ref)

    def weights(sb, j, slot):
        p = jnp.exp2(_dot(k_ref[j, 0:rows, :], q_refs[sb][...]))
        p_ref[slot, 0:rows, :] = p.astype(BF16)
        return jnp.sum(p.reshape(rows // sl, sl, nq), axis=0)

    def weighted_values(j, slot):
        return _dot(vt_ref[j, :, 0:rows], p_ref[slot, 0:rows, :])

    def run(sb):
        first = sb * n_tiles
        den = weights(sb, 0, first % U)

        def group(jj, den):
            j = U * jj
            part = None
            for u in range(U):
                den = den + weights(sb, j + u + 1, (first + u + 1) % U)
                wv = weighted_values(j + u, (first + u) % U)
                part = wv if part is None else part + wv
            acc_ref[sb, 0:HEAD_DIM, :] += part
            return den

        groups = (n_tiles - 1) // U
        den = lax.fori_loop(0, groups, group, den)
        for t in range(U * groups, n_tiles):
            if t + 1 < n_tiles:
                den = den + weights(sb, t + 1, (first + t + 1) % U)
            acc_ref[sb, 0:HEAD_DIM, :] += weighted_values(t, (first + t) % U)
        acc_ref[sb, HEAD_DIM:HEAD_DIM + sl, :] = den

    def run_online(sb):
        for r in range(Q_PER_KV):
            q = q_refs[sb][:, r * tq:(r + 1) * tq]

            def body(j, carry):
                m, den, acc = carry
                s = _dot(k_ref[j, 0:rows, :], q)
                m_new = jnp.maximum(m, jnp.max(s, axis=0, keepdims=True))
                p = jnp.exp2(s - m_new)
                alpha = jnp.exp2(m - m_new)
                return (m_new, alpha * den + jnp.sum(p, axis=0, keepdims=True),
                        alpha * acc + _dot(vt_ref[j, :, 0:rows], p.astype(BF16)))

            init = (jnp.full((1, tq), -jnp.inf, F32), jnp.zeros((1, tq), F32), jnp.zeros((HEAD_DIM, tq), F32))
            _, den, acc = lax.fori_loop(0, n_tiles, body, init)
            acc_ref[sb, 0:HEAD_DIM, r * tq:(r + 1) * tq] = acc
            acc_ref[sb, HEAD_DIM:HEAD_DIM + sl, r * tq:(r + 1) * tq] = jnp.broadcast_to(den * (1.0 / sl), (sl, tq))

    def finish(sb):
        for r in range(Q_PER_KV):
            acc = acc_ref[sb, :, r * tq:(r + 1) * tq]
            den = jnp.sum(acc[HEAD_DIM:HEAD_DIM + sl, :], axis=0, keepdims=True)
            o = acc[0:HEAD_DIM, :] * (1.0 / den)
            oall_ref[sb, r * HEAD_DIM:(r + 1) * HEAD_DIM, :] = o.astype(BF16)
        o_t = _dot_nt(eye_ref[...], oall_ref[sb])
        gate = _silu_from_half(ag_refs[sb][...].astype(F32))
        o_ref[sb * tq:(sb + 1) * tq, :] = (o_t * gate).astype(BF16)

    bounded = bounded_ref[0] == 1

    @pl.when(bounded)
    def _():
        for sb in range(nsb):
            run(sb)
            finish(sb)

    @pl.when(jnp.logical_not(bounded))
    def _():
        for sb in range(nsb):
            run_online(sb)
            finish(sb)


def _attention(bounded, qt, k, vt, proj, eye_tq, q_block0, n_qblocks, nsb, n_tiles, rows):
    B = proj.shape[0]
    nkb, tk = k.shape[1], k.shape[2]
    gw = Q_PER_KV * HEAD_DIM
    nq = Q_PER_KV * ATT_TQ
    assert n_qblocks % nsb == 0

    def qmap(sb):
        return lambda b, g, i, f: (b, g, 0, q_block0 + i * nsb + sb)

    def agmap(sb):
        return lambda b, g, i, f: (b, q_block0 + i * nsb + sb, COL_AG // gw + g)

    in_specs = ([pl.BlockSpec((None, None, KV_W, nq), qmap(sb)) for sb in range(nsb)]
                + [pl.BlockSpec((None, nkb, tk, KV_W), lambda b, g, i, f: (b, 0, 0, 0)),
                   pl.BlockSpec((None, None, nkb, HEAD_DIM, tk), lambda b, g, i, f: (b, g, 0, 0, 0))]
                + [pl.BlockSpec((None, ATT_TQ, gw), agmap(sb)) for sb in range(nsb)]
                + [pl.BlockSpec((ATT_TQ, ATT_TQ), lambda b, g, i, f: (0, 0))])
    args = [bounded] + [qt] * nsb + [k, vt] + [proj] * nsb + [eye_tq]
    grid_spec = pltpu.PrefetchScalarGridSpec(
        num_scalar_prefetch=1,
        grid=(B, N_KV, n_qblocks // nsb),
        in_specs=in_specs,
        out_specs=pl.BlockSpec((None, nsb * ATT_TQ, gw), lambda b, g, i, f: (b, i, g)),
        scratch_shapes=[pltpu.VMEM((ATT_UNROLL, tk, nq), BF16),
                        pltpu.VMEM((nsb, HEAD_DIM + SUBLANES, nq), F32),
                        pltpu.VMEM((nsb, gw, ATT_TQ), BF16)])
    return pl.pallas_call(
        functools.partial(_attn_kernel, nsb=nsb, n_tiles=n_tiles, rows=rows),
        grid_spec=grid_spec,
        out_shape=jax.ShapeDtypeStruct((B, n_qblocks * ATT_TQ, ATTN_W), BF16),
        compiler_params=_cparams(("arbitrary", "arbitrary", "arbitrary"), 40),
        name="gqa_attention",
    )(*args)


def _gla_direction(qk, v, pa, a2, ab, st_ref, o_ref, fwd):
    C, c = GLA_C, GLA_SUB
    q = qk[:, 0:GLA_KW].astype(F32) * (GLA_DK ** -0.5)
    k = qk[:, GLA_KW:2 * GLA_KW].astype(F32)
    z = _dot(pa, a2) + ab
    yield
    g =(jnp.minimum(z, 0.0) - jnp.log(1.0 + jnp.exp(-jnp.abs(z)))) * (1.0 / GLA_TEMP)
    row = lax.broadcasted_iota(jnp.int32, (C, C), 0)
    col = lax.broadcasted_iota(jnp.int32, (C, C), 1)
    blk = jnp.bitwise_and(row, -c)
    if fwd:
        tri = col <= row
        inblk = tri & (col >= blk)
    else:
        tri = col >= row
        inblk = tri & (col < blk + c)
    sel = jnp.concatenate([jnp.where(tri, 1.0, 0.0), jnp.where(inblk, 1.0, 0.0)], axis=0).astype(BF16)
    g_hi = g.astype(BF16)
    g_lo = (g - g_hi.astype(F32)).astype(BF16)
    cs = _dot(sel, jnp.concatenate([g_hi, g_lo], axis=1))
    yield
    cs =cs[:, 0:GLA_KW] + cs[:, GLA_KW:2 * GLA_KW]
    b = cs[0:C]
    d = cs[C:2 * C]
    ref = b - d
    total = b[C - 1:C, :] if fwd else b[0:1, :]
    q_in = (q * jnp.exp(d)).astype(BF16)
    q_st = (q * jnp.exp(b)).astype(BF16)
    k_out = (k * jnp.exp(total - b)).astype(BF16)
    w_tot = jnp.exp(total)
    zero = jnp.zeros((), BF16)
    nb = C // c
    k_blk = []
    for i in range(nb):
        lo, hi = (0, (i + 1) * c) if fwd else (i * c, C)
        kb = (k[lo:hi] * jnp.exp(ref[i * c:i * c + 1, :] - b[lo:hi])).astype(BF16)
        pad = jnp.zeros((C - (hi - lo), GLA_KW), BF16)
        k_blk.append(jnp.concatenate(([kb, pad] if fwd else [pad, kb]) if hi - lo < C else [kb], axis=0))
    head_of_lane = jnp.right_shift(lax.broadcasted_iota(jnp.int32, (1, GLA_KW), 1), GLA_DK.bit_length() - 1)
    a_two = []
    for m in range(nb // 2):
        qr = q_in[2 * m * c:(2 * m + 2) * c]
        q_heads = jnp.concatenate([jnp.where(head_of_lane == h, qr, zero) for h in range(GLA_HEADS)], axis=0)
        a_two.append(_dot_nt(q_heads, jnp.concatenate(k_blk[2 * m:2 * m + 2], axis=0)))
    yield

    def a_block(h, i):
        m, r = divmod(i, 2)
        return a_two[m][h * 2 * c + r * c:h * 2 * c + (r + 1) * c, r * C:(r + 1) * C]

    a_cat = jnp.concatenate(
        [jnp.where(tri, jnp.concatenate([a_block(h, i) for i in range(nb)], axis=0), 0.0)
         for h in range(GLA_HEADS)], axis=1).astype(BF16)
    vhead_of_lane = jnp.right_shift(lax.broadcasted_iota(jnp.int32, (1, GLA_VW), 1), GLA_DV.bit_length() - 1)
    v_bd = jnp.concatenate([jnp.where(vhead_of_lane == h, v, zero) for h in range(GLA_HEADS)], axis=0)
    st = st_ref[...]
    o = _dot(a_cat, v_bd) + _dot_nt(q_st, st.astype(BF16))
    upd = lax.dot_general(v, k_out, (((0,), (0,)), ((), ())), preferred_element_type=F32)
    same_head = jnp.right_shift(lax.broadcasted_iota(jnp.int32, (GLA_VW, 1), 0),
                                GLA_DV.bit_length() - 1) == head_of_lane
    st_ref[...] = jnp.where(same_head, st * w_tot + upd, 0.0)
    o_ref[...] = o
    yield


def _gla_kernel(qkf_ref, vf_ref, paf_ref, qkb_ref, vb_ref, pab_ref, a2f_ref, abf_ref, a2b_ref, abb_ref,
                of_ref, ob_ref, stf_ref, stb_ref):
    @pl.when(pl.program_id(0) == 0)
    def _():
        stf_ref[...] = jnp.zeros_like(stf_ref)
        stb_ref[...] = jnp.zeros_like(stb_ref)

    chains = []
    for bi in range(qkf_ref.shape[0]):
        chains.append(_gla_direction(qkf_ref[bi], vf_ref[bi], paf_ref[bi], a2f_ref[...], abf_ref[...],
                                     stf_ref.at[bi], of_ref.at[bi], True))
        chains.append(_gla_direction(qkb_ref[bi], vb_ref[bi], pab_ref[bi], a2b_ref[...], abb_ref[...],
                                     stb_ref.at[bi], ob_ref.at[bi], False))
    for _ in range(4):
        for chain in chains:
            next(chain)


def _gla(proj, a2f, abf, a2b, abb, ctx_len):
    B, T = proj.shape[0], proj.shape[1]
    C = GLA_C
    nch, nctx = T // C, ctx_len // C

    def cb(s):
        return jnp.where(s < nctx, nctx - 1 - s, nch + nctx - 1 - s)

    full = lambda shape: pl.BlockSpec(shape, lambda s: (0,) * len(shape))
    return pl.pallas_call(
        _gla_kernel,
        grid=(nch,),
        in_specs=[pl.BlockSpec((B, C, 2 * GLA_KW), lambda s: (0, s, COL_GQK // (2 * GLA_KW))),
                  pl.BlockSpec((B, C, GLA_VW), lambda s: (0, s, COL_GV // GLA_VW)),
                  pl.BlockSpec((B, C, LANES), lambda s: (0, s, COL_GATE // LANES)),
                  pl.BlockSpec((B, C, 2 * GLA_KW), lambda s: (0, cb(s), COL_GQK // (2 * GLA_KW))),
                  pl.BlockSpec((B, C, GLA_VW), lambda s: (0, cb(s), COL_GV // GLA_VW)),
                  pl.BlockSpec((B, C, LANES), lambda s: (0, cb(s), COL_GATE // LANES)),
                  full((LANES, GLA_KW)), full((1, GLA_KW)), full((LANES, GLA_KW)), full((1, GLA_KW))],
        out_specs=[pl.BlockSpec((B, C, GLA_VW), lambda s: (0, s, 0)),
                   pl.BlockSpec((B, C, GLA_VW), lambda s: (0, cb(s), 0))],
        out_shape=[jax.ShapeDtypeStruct((B, T, GLA_VW), F32),
                   jax.ShapeDtypeStruct((B, T, GLA_VW), F32)],
        scratch_shapes=[pltpu.VMEM((B, GLA_VW, GLA_KW), F32),
                        pltpu.VMEM((B, GLA_VW, GLA_KW), F32)],
        compiler_params=_cparams(("arbitrary",), 32),
        name="gla_scan",
    )(proj, proj, proj, proj, proj, proj, a2f, abf, a2b, abb)


def _merge_kernel(x_ref, mod_ref, ma_ref, mg_ref, ms_ref, su_ref, sv_ref, sg_ref, gg_ref, yac_ref, yal_ref,
                  of_ref, ob_ref,
                  onorm_ref, lng_ref, lnb_ref, sw_ref, sb_ref, wa_ref, wg_ref, ws_ref, wo_ref, pg_ref, pb_ref,
                  o_ref, *, tm, ctx_tiles, row_off, alpha):
    D = x_ref.shape[-1]
    i = pl.program_id(1) + row_off
    og = of_ref[...] + ob_ref[...]
    parts = []
    for h in range(GLA_HEADS):
        oh = og[:, h * GLA_DV:(h + 1) * GLA_DV]
        ms = jnp.mean(oh * oh, axis=-1, keepdims=True)
        parts.append(oh * lax.rsqrt(ms + EPS))
    yg = jnp.concatenate(parts, axis=1) * onorm_ref[...] * _silu_from_half(gg_ref[...]).astype(F32)
    sv = sv_ref[...].astype(F32)
    mu = jnp.mean(sv, axis=-1, keepdims=True)
    dv = sv - mu
    var = jnp.mean(dv * dv, axis=-1, keepdims=True)
    vn = (dv * lax.rsqrt(var + EPS) * lng_ref[...] + lnb_ref[...]).astype(BF16)
    gw = SGU_W // SGU_GROUPS
    rows = []
    for n in range(tm // SGU_CHUNK):
        cols = []
        for g in range(SGU_GROUPS):
            blk = vn[n * SGU_CHUNK:(n + 1) * SGU_CHUNK, g * gw:(g + 1) * gw]
            cols.append(_dot(sw_ref[g], blk))
        rows.append(jnp.concatenate(cols, axis=1) + sb_ref[...])
    mixed = jnp.concatenate(rows, axis=0)
    ys = su_ref[...].astype(F32) * mixed * _silu_from_half(sg_ref[...]).astype(F32)

    def gated(g_ref, y, w_ref):
        half = _dot(y, w_ref[...]).astype(BF16)
        return half + half * jnp.tanh(g_ref[...])

    ya = jnp.where(i < ctx_tiles, yac_ref[...], yal_ref[...])
    m = (gated(ma_ref, ya, wa_ref) + gated(mg_ref, yg.astype(BF16), wg_ref)
         + gated(ms_ref, ys.astype(BF16), ws_ref))
    out = _dot(m, wo_ref[...])
    gate = jnp.where(i < ctx_tiles, mod_ref[1:2, 2 * D:3 * D], mod_ref[0:1, 2 * D:3 * D])
    r = alpha * x_ref[...] + gate * out
    mu = jnp.mean(r, axis=-1, keepdims=True)
    dr = r - mu
    var = jnp.mean(dr * dr, axis=-1, keepdims=True)
    o_ref[...] = dr * lax.rsqrt(var + EPS) * pg_ref[...] + pb_ref[...]


def _merge(xall, modb, proj, ya_ctx, ya_lat, of, ob, p, ctx_len, tm, skip_ctx, alpha):
    B, T, D = xall.shape
    ctx_tiles = ctx_len // tm
    row_off = ctx_tiles if skip_ctx else 0
    nt = T // tm - row_off
    rows = lambda w, c: pl.BlockSpec((None, tm, w), lambda b, i: (b, i + row_off, c))
    ya_ctx_spec = pl.BlockSpec((None, tm, ATTN_W), lambda b, i: (b, jnp.minimum(i + row_off, ctx_tiles - 1), 0))
    ya_lat_spec = pl.BlockSpec((None, tm, ATTN_W), lambda b, i: (b, jnp.maximum(i + row_off - ctx_tiles, 0), 0))
    full = lambda shape: pl.BlockSpec(shape, lambda b, i: (0,) * len(shape))
    return pl.pallas_call(
        functools.partial(_merge_kernel, tm=tm, ctx_tiles=ctx_tiles, row_off=row_off, alpha=alpha),
        grid=(B, nt),
        in_specs=[rows(D, 0),
                  pl.BlockSpec((None, 8, 3 * D), lambda b, i: (b, 0, 0)),
                  rows(D, COL_MA // D), rows(D, COL_MG // D), rows(D, COL_MS // D),
                  rows(SGU_W, COL_SU // SGU_W), rows(SGU_W, COL_SV // SGU_W), rows(SGU_W, COL_SG // SGU_W),
                  rows(GLA_VW, COL_GG // GLA_VW),
                  ya_ctx_spec, ya_lat_spec, rows(GLA_VW, 0), rows(GLA_VW, 0),
                  full((1, GLA_VW)), full((1, SGU_W)), full((1, SGU_W)),
                  full((SGU_GROUPS, SGU_CHUNK, SGU_CHUNK)), full((SGU_CHUNK, SGU_W)),
                  full((ATTN_W, D)), full((GLA_VW, D)), full((SGU_W, D)), full((D, D)),
                  full((1, D)), full((1, D))],
        out_specs=pl.BlockSpec((None, tm, D), lambda b, i: (b, i, 0)),
        out_shape=jax.ShapeDtypeStruct((B, nt * tm, D), F32),
        compiler_params=_cparams(("arbitrary", "arbitrary"), 48),
        name="merge_out",
    )(xall, modb, proj, proj, proj, proj, proj, proj, proj, ya_ctx, ya_lat, of, ob,
      p["onorm"], p["lng"], p["lnb"], p["sw"], p["sb"], p["wa"], p["wg"], p["ws"], p["wo"], p["pg"], p["pb"])


W_BLK = 256
GATE_SHIFT = 2 * GLA_RANK


def _w_in_plan():
    sizes = (512, 128, 128, 512, 256, 256, 512, 16, 16, 512, 512, 512, 512, 1024, 1024, 1024)
    names = ("aq", "ak", "av", "ag", "gq", "gk", "gv", "af", "ab", "gg", "su", "sv", "sg", "ma", "mg", "ms")
    halved = ("ag", "gg", "sg", "ma", "mg", "ms")
    src, off = {}, 0
    for n, s in zip(names, sizes):
        src[n] = (off, s)
        off += s
    plan = []
    for n in ("ma", "mg", "ms", "aq", "ag", "gv", "gg", "su", "sv", "sg", "gq", "gk"):
        start, size = src[n]
        shift = start % W_BLK
        assert shift in (0, GATE_SHIFT) and size % W_BLK == 0
        for t in range(size // W_BLK):
            plan.append(((start - shift) // W_BLK + t, 1 if shift else 0, int(n in halved)))
    assert src["ak"][0] % W_BLK == 0 and src["av"][0] == src["ak"][0] + KV_W and 2 * KV_W == W_BLK
    plan.append((src["ak"][0] // W_BLK, 0, 0))
    assert src["af"][0] % W_BLK == 0 and src["ab"][0] == src["af"][0] + GLA_RANK
    plan.append((src["af"][0] // W_BLK, 2, 0))
    assert len(plan) * W_BLK == PROJ_W
    return plan, off


def _wprep_kernel(src_ref, kind_ref, half_ref, a_ref, b_ref, o_ref):
    j = pl.program_id(1)
    a = a_ref[...]
    ab = jnp.concatenate([a, b_ref[...]], axis=1)
    shifted = pltpu.roll(ab, 2 * W_BLK - GATE_SHIFT, 1)[:, 0:W_BLK]
    lane = lax.broadcasted_iota(jnp.int32, (1, W_BLK), 1)
    gates = jnp.where(lane < GATE_SHIFT, a, 0.0)
    kind = kind_ref[j]
    val = jnp.where(kind == 1, shifted, jnp.where(kind == 2, gates, a))
    o_ref[...] = (val * jnp.where(half_ref[j] == 1, 0.5, 1.0)).astype(BF16)


def _regroup_w_in(w_in):
    L, D, width = w_in.shape
    plan, used = _w_in_plan()
    assert used == width
    last_blk = (width - 1) // W_BLK
    src = jnp.array([p[0] for p in plan], jnp.int32)
    kind = jnp.array([p[1] for p in plan], jnp.int32)
    half = jnp.array([p[2] for p in plan], jnp.int32)
    grid_spec = pltpu.PrefetchScalarGridSpec(
        num_scalar_prefetch=3,
        grid=(L, len(plan)),
        in_specs=[pl.BlockSpec((None, D, W_BLK), lambda l, j, s, k, h: (l, 0, s[j])),
                  pl.BlockSpec((None, D, W_BLK), lambda l, j, s, k, h: (l, 0, jnp.minimum(s[j] + 1, last_blk)))],
        out_specs=pl.BlockSpec((None, D, W_BLK), lambda l, j, s, k, h: (l, 0, j)))
    return pl.pallas_call(
        _wprep_kernel,
        grid_spec=grid_spec,
        out_shape=jax.ShapeDtypeStruct((L, D, PROJ_W), BF16),
        compiler_params=_cparams(("arbitrary", "arbitrary"), 32),
        name="w_in_prep",
    )(src, kind, half, w_in, w_in)


def _rope_tables(n_lat, ctx_len):
    rows = n_lat // GRID_W
    row = jnp.repeat(jnp.arange(rows, dtype=F32), GRID_W)
    col = jnp.tile(jnp.arange(GRID_W, dtype=F32), rows)
    freqs = ROPE_THETA ** (-jnp.arange(0, ROPE_AXIS, 2, dtype=F32) / ROPE_AXIS)
    ang_r, ang_c = row[:, None] * freqs, col[:, None] * freqs
    cos = jnp.concatenate([jnp.cos(ang_r)] * 2 + [jnp.cos(ang_c)] * 2, axis=1)
    sin = jnp.concatenate([-jnp.sin(ang_r), jnp.sin(ang_r), -jnp.sin(ang_c), jnp.sin(ang_c)], axis=1)
    cos = jnp.concatenate([jnp.ones((ctx_len, HEAD_DIM), F32), cos], axis=0)
    sin = jnp.concatenate([jnp.zeros((ctx_len, HEAD_DIM), F32), sin], axis=0)
    return jnp.concatenate([cos, cos], axis=1), jnp.concatenate([sin, sin], axis=1)


def _largest_tile(total, cap, mult):
    best = mult
    for t in range(mult, cap + 1, mult):
        if total % t == 0:
            best = t
    return best


def kernel(x, c, ctx, c_ctx, w_ada, b_ada, w_in, attn_q_norm, attn_k_norm, gla_a2_f, gla_ab_f, gla_a2_b, gla_ab_b, gla_o_norm, sgu_ln_g, sgu_ln_b, sgu_w, sgu_b, w_br_attn, w_br_gla, w_br_sgu, w_out, post_ln_g, post_ln_b):
    B, n_lat, D = x.shape
    ctx_len = ctx.shape[1]
    depth = w_in.shape[0]
    T = ctx_len + n_lat
    assert D == D_MODEL and B <= 7
    assert ctx_len % ATT_TQ == 0 and n_lat % ATT_TQ == 0 and n_lat % GRID_W == 0
    alpha = (2 * depth) ** 0.25

    tm_proj = _largest_tile(T, 1056, 16)
    tn_proj = 1536
    tk_att = _largest_tile(T, ATT_TK_MAX, ATT_TQ)
    assert ctx_len <= tk_att
    tm_prep = tk_att
    tm_merge = ATT_TQ

    xall = jnp.concatenate([ctx, x], axis=1)
    cc = jnp.zeros((8, D), F32).at[0:B].set(c).at[B].set(c_ctx)
    mod = _ada(cc, w_ada, b_ada.reshape(depth, 1, 3 * D))
    cos_t, sin_t = _rope_tables(n_lat, ctx_len)
    consts = {
        "gq": jnp.kron(jnp.eye(N_HEADS, dtype=F32), jnp.ones((HEAD_DIM, HEAD_DIM), F32)).astype(BF16),
        "gk": jnp.kron(jnp.eye(N_KV, dtype=F32), jnp.ones((HEAD_DIM, HEAD_DIM), F32)).astype(BF16),
        "eq": jnp.eye(ATTN_W, dtype=BF16),
        "ek": jnp.eye(KV_W, dtype=BF16),
    }
    eye_tq = jnp.eye(ATT_TQ, dtype=BF16)
    w_all = _regroup_w_in(w_in)

    for l in range(depth):
        last = l == depth - 1
        modb = jnp.zeros((B, 8, 3 * D), F32).at[:, 0].set(mod[l, 0:B]).at[:, 1].set(mod[l, B][None])
        proj = _inproj(xall, modb, w_all, l, ctx_len, tm_proj, tn_proj)
        qn = jnp.concatenate([attn_q_norm[l]] * 2)[None]
        kn = jnp.concatenate([attn_k_norm[l]] * 2)[None]
        qt, kk, vt = _qkprep(proj, cos_t, sin_t, qn, kn, consts, tm_prep, tk_att)
        score_bound = HEAD_DIM * jnp.max(jnp.abs(attn_q_norm[l])) * jnp.max(jnp.abs(attn_k_norm[l])) * Q_SCALE
        bounded = (score_bound <= SCORE_LIMIT).astype(jnp.int32).reshape(1)
        ctx_blocks, lat_blocks = ctx_len // ATT_TQ, n_lat // ATT_TQ
        ya_lat = _attention(bounded, qt, kk, vt, proj, eye_tq, ctx_blocks, lat_blocks, ATT_NSB, T // tk_att, tk_att)
        ya_ctx = ya_lat if last else _attention(bounded, qt, kk, vt, proj, eye_tq, 0, ctx_blocks, 1, 1, ctx_len)
        a2f = jnp.zeros((LANES, GLA_KW), F32).at[0:GLA_RANK].set(gla_a2_f[l]).astype(BF16)
        a2b = jnp.zeros((LANES, GLA_KW), F32).at[GLA_RANK:2 * GLA_RANK].set(gla_a2_b[l]).astype(BF16)
        of, ob = _gla(proj, a2f, gla_ab_f[l][None], a2b, gla_ab_b[l][None], ctx_len)
        p = {
            "onorm": jnp.concatenate([gla_o_norm[l]] * GLA_HEADS)[None],
            "lng": sgu_ln_g[l][None], "lnb": sgu_ln_b[l][None],
            "sw": sgu_w[l].astype(BF16),
            "sb": jnp.repeat(sgu_b[l].T, SGU_W // SGU_GROUPS, axis=1),
            "wa": (0.5 * w_br_attn[l]).astype(BF16), "wg": (0.5 * w_br_gla[l]).astype(BF16),
            "ws": (0.5 * w_br_sgu[l]).astype(BF16),
            "wo": w_out[l].astype(BF16), "pg": post_ln_g[l][None], "pb": post_ln_b[l][None],
        }
        xall = _merge(xall, modb, proj, ya_ctx, ya_lat, of, ob, p, ctx_len, tm_merge, last, alpha)
    return xall
```

```python
import functools

import jax
import jax.numpy as jnp
import numpy as np
from jax import lax
from jax.experimental import pallas as pl
from jax.experimental.pallas import tpu as pltpu

F32 = jnp.float32
BF16 = jnp.bfloat16

D_MODEL = 1024
GRID_W = 64
HEAD_DIM = 64
N_HEADS = 8
N_KV = 2
Q_PER_KV = N_HEADS // N_KV
ATTN_W = N_HEADS * HEAD_DIM
KV_W = N_KV * HEAD_DIM
ROPE_THETA = 10000.0
ROPE_AXIS = HEAD_DIM // 2
GLA_HEADS = 4
GLA_DK = 64
GLA_DV = 128
GLA_KW = GLA_HEADS * GLA_DK
GLA_VW = GLA_HEADS * GLA_DV
GLA_RANK = 16
GLA_TEMP = 16.0
SGU_GROUPS = 4
SGU_CHUNK = 128
SGU_W = 512
EPS = 1e-6

LANES = 128
V7X_VMEM_BYTES = 64 * 1024 * 1024

COL_MA, COL_MG, COL_MS = 0, 1024, 2048
COL_AQ, COL_AG = 3072, 3584
COL_GV, COL_GG = 4096, 4608
COL_SU, COL_SV, COL_SG = 5120, 5632, 6144
COL_GQK = 6656
COL_AKV = 7168
COL_GATE = 7424
PROJ_W = 7680

SUBLANES = 8
ATT_TQ = 256
ATT_NSB = 2
ATT_TK_MAX = 768
GLA_C = 128
GLA_SUB = 16
ATT_UNROLL = 5
Q_SCALE = 1.4426950408889634 * HEAD_DIM ** -0.5
SCORE_LIMIT = 96.0


def _cparams(sem, vmem_mb):
    return pltpu.CompilerParams(dimension_semantics=sem,
                                vmem_limit_bytes=min(vmem_mb * 1024 * 1024, V7X_VMEM_BYTES - (4 << 20)))


def _silu(x):
    return x * jax.nn.sigmoid(x)


def _silu_from_half(u):
    return u + u * jnp.tanh(u)


def _dot(a, b):
    return jnp.dot(a, b, preferred_element_type=F32)


def _dot_nt(a, b):
    return lax.dot_general(a, b, (((1,), (1,)), ((), ())), preferred_element_type=F32)


def _ada_kernel(c_ref, w_ref, b_ref, o_ref):
    c = c_ref[...]
    o_ref[...] = _dot(_silu(c).astype(BF16), w_ref[...].astype(BF16)) + b_ref[...]


def _ada(cc, w_ada, b_ada):
    L, D = w_ada.shape[0], w_ada.shape[1]
    return pl.pallas_call(
        _ada_kernel,
        grid=(L, 3),
        in_specs=[pl.BlockSpec((8, D), lambda l, j: (0, 0)),
                  pl.BlockSpec((None, D, D), lambda l, j: (l, 0, j)),
                  pl.BlockSpec((None, 1, D), lambda l, j: (l, 0, j))],
        out_specs=pl.BlockSpec((None, 8, D), lambda l, j: (l, 0, j)),
        out_shape=jax.ShapeDtypeStruct((L, 8, 3 * D), F32),
        compiler_params=_cparams(("arbitrary", "arbitrary"), 32),
        name="ada_mod",
    )(cc, w_ada, b_ada)


def _inproj_kernel(x_ref, mod_ref, w_ref, o_ref, h_ref, *, tm, ctx_len):
    i = pl.program_id(1)
    j = pl.program_id(2)

    @pl.when(j == 0)
    def _():
        D = x_ref.shape[-1]
        row = i * tm + lax.broadcasted_iota(jnp.int32, (tm, 1), 0)
        is_ctx = row < ctx_len
        shift = jnp.where(is_ctx, mod_ref[1:2, 0:D], mod_ref[0:1, 0:D])
        scale = jnp.where(is_ctx, mod_ref[1:2, D:2 * D], mod_ref[0:1, D:2 * D])
        h_ref[...] = (x_ref[...] * (1.0 + scale) + shift).astype(BF16)

    o_ref[...] = _dot_nt(h_ref[...], w_ref[...]).astype(BF16)


def _inproj(xall, modb, w_all, layer, ctx_len, tm, tn):
    B, T, D = xall.shape
    return pl.pallas_call(
        functools.partial(_inproj_kernel, tm=tm, ctx_len=ctx_len),
        grid=(B, T // tm, PROJ_W // tn),
        in_specs=[pl.BlockSpec((None, tm, D), lambda b, i, j: (b, i, 0)),
                  pl.BlockSpec((None, 8, 3 * D), lambda b, i, j: (b, 0, 0)),
                  pl.BlockSpec((None, tn, D), lambda b, i, j: (layer, j, 0))],
        out_specs=pl.BlockSpec((None, tm, tn), lambda b, i, j: (b, i, j)),
        out_shape=jax.ShapeDtypeStruct((B, T, PROJ_W), BF16),
        scratch_shapes=[pltpu.VMEM((tm, D), BF16)],
        compiler_params=_cparams(("arbitrary", "arbitrary", "arbitrary"), 40),
        name="in_proj",
    )(xall, modb, w_all)


def _rope(y, cos, sin, width):
    lane = lax.broadcasted_iota(jnp.int32, (1, width), 1)
    first = jnp.bitwise_and(lane, ROPE_AXIS - 1) < (ROPE_AXIS // 2)
    up = pltpu.roll(y, width - ROPE_AXIS // 2, 1)
    dn = pltpu.roll(y, ROPE_AXIS // 2, 1)
    return y * cos + jnp.where(first, up, dn) * sin


def _qkprep_kernel(aq_ref, akv_ref, cos_ref, sin_ref, qn_ref, kn_ref, gq_ref, gk_ref,
                   eq_ref, ek_ref, qt_ref, k_ref, vt_ref, *, tm, tk):
    cos = cos_ref[...]
    sin = sin_ref[...]
    aq = aq_ref[...].astype(F32)
    ssq = _dot((aq * aq).astype(BF16), gq_ref[...])
    yq = aq * lax.rsqrt(ssq * (1.0 / HEAD_DIM) + EPS) * jnp.concatenate([qn_ref[...]] * 4, axis=1)
    yq = _rope(yq, jnp.concatenate([cos] * 4, axis=1), jnp.concatenate([sin] * 4, axis=1), ATTN_W)
    yq = (yq * Q_SCALE).astype(BF16)
    qt = _dot_nt(eq_ref[...], yq).astype(BF16)
    zeros = jnp.zeros((HEAD_DIM, ATT_TQ), BF16)
    for g in range(N_KV):
        for qb in range(tm // ATT_TQ):
            for r in range(Q_PER_KV):
                h = g * Q_PER_KV + r
                cols = slice((qb * Q_PER_KV + r) * ATT_TQ, (qb * Q_PER_KV + r + 1) * ATT_TQ)
                qt_ref[g, g * HEAD_DIM:(g + 1) * HEAD_DIM, cols] = qt[h * HEAD_DIM:(h + 1) * HEAD_DIM,
                                                                      qb * ATT_TQ:(qb + 1) * ATT_TQ]
                qt_ref[g, (1 - g) * HEAD_DIM:(2 - g) * HEAD_DIM, cols] = zeros
    akv = akv_ref[...].astype(F32)
    ak = akv[:, 0:KV_W]
    ssk = _dot((ak * ak).astype(BF16), gk_ref[...])
    yk = ak * lax.rsqrt(ssk * (1.0 / HEAD_DIM) + EPS) * kn_ref[...]
    yk = _rope(yk, cos, sin, KV_W).astype(BF16)
    nkb = tm // tk
    k_ref[...] = yk.reshape(nkb, tk, KV_W)
    av = akv_ref[:, KV_W:2 * KV_W]
    vt = _dot_nt(ek_ref[...], av).astype(BF16)
    for g in range(N_KV):
        for n in range(nkb):
            vt_ref[g, n] = vt[g * HEAD_DIM:(g + 1) * HEAD_DIM, n * tk:(n + 1) * tk]


def _qkprep(proj, cos_t, sin_t, qn, kn, consts, tm, tk):
    B, T, _ = proj.shape
    nkb_t = T // tk
    nkb = tm // tk
    full = lambda shape: pl.BlockSpec(shape, lambda b, i: (0,) * len(shape))
    return pl.pallas_call(
        functools.partial(_qkprep_kernel, tm=tm, tk=tk),
        grid=(B, T // tm),
        in_specs=[pl.BlockSpec((None, tm, ATTN_W), lambda b, i: (b, i, COL_AQ // ATTN_W)),
                  pl.BlockSpec((None, tm, 2 * KV_W), lambda b, i: (b, i, COL_AKV // (2 * KV_W))),
                  pl.BlockSpec((tm, LANES), lambda b, i: (i, 0)),
                  pl.BlockSpec((tm, LANES), lambda b, i: (i, 0)),
                  full((1, LANES)), full((1, LANES)),
                  full((ATTN_W, ATTN_W)), full((KV_W, KV_W)),
                  full((ATTN_W, ATTN_W)), full((KV_W, KV_W))],
        out_specs=[pl.BlockSpec((None, N_KV, KV_W, Q_PER_KV * tm), lambda b, i: (b, 0, 0, i)),
                   pl.BlockSpec((None, nkb, tk, KV_W), lambda b, i: (b, i, 0, 0)),
                   pl.BlockSpec((None, N_KV, nkb, HEAD_DIM, tk), lambda b, i: (b, 0, i, 0, 0))],
        out_shape=[jax.ShapeDtypeStruct((B, N_KV, KV_W, Q_PER_KV * T), BF16),
                   jax.ShapeDtypeStruct((B, nkb_t, tk, KV_W), BF16),
                   jax.ShapeDtypeStruct((B, N_KV, nkb_t, HEAD_DIM, tk), BF16)],
        compiler_params=_cparams(("arbitrary", "arbitrary"), 40),
        name="qkv_prep",
    )(proj, proj, cos_t, sin_t, qn, kn, consts["gq"], consts["gk"], consts["eq"], consts["ek"])


def _attn_kernel(bounded_ref, *refs, nsb, n_tiles, rows):
    q_refs, (k_ref, vt_ref), ag_refs, eye_ref = refs[0:nsb], refs[nsb:nsb + 2], refs[nsb + 2:2 * nsb + 2], refs[2 * nsb + 2]
    o_ref, p_ref, acc_ref, oall_ref = refs[-4:]
    nq = q_refs[0].shape[-1]
    tq = nq // Q_PER_KV
    sl = acc_ref.shape[1] - HEAD_DIM
    U = p_ref.shape[0]
    acc_ref[...] = jnp.zeros_like(acc_ref)

    def weights(sb, j, slot):
        p = jnp.exp2(_dot(k_ref[j, 0:rows, :], q_refs[sb][...]))
        p_ref[slot, 0:rows, :] = p.astype(BF16)
        return jnp.sum(p.reshape(rows // sl, sl, nq), axis=0)

    def weighted_values(j, slot):
        return _dot(vt_ref[j, :, 0:rows], p_ref[slot, 0:rows, :])

    def run(sb):
        first = sb * n_tiles
        den = weights(sb, 0, first % U)

        def group(jj, den):
            j = U * jj
            part = None
            for u in range(U):
                den = den + weights(sb, j + u + 1, (first + u + 1) % U)
                wv = weighted_values(j + u, (first + u) % U)
                part = wv if part is None else part + wv
            acc_ref[sb, 0:HEAD_DIM, :] += part
            return den

        groups = (n_tiles - 1) // U
        den = lax.fori_loop(0, groups, group, den)
        for t in range(U * groups, n_tiles):
            if t + 1 < n_tiles:
                den = den + weights(sb, t + 1, (first + t + 1) % U)
            acc_ref[sb, 0:HEAD_DIM, :] += weighted_values(t, (first + t) % U)
        acc_ref[sb, HEAD_DIM:HEAD_DIM + sl, :] = den

    def run_online(sb):
        for r in range(Q_PER_KV):
            q = q_refs[sb][:, r * tq:(r + 1) * tq]

            def body(j, carry):
                m, den, acc = carry
                s = _dot(k_ref[j, 0:rows, :], q)
                m_new = jnp.maximum(m, jnp.max(s, axis=0, keepdims=True))
                p = jnp.exp2(s - m_new)
                alpha = jnp.exp2(m - m_new)
                return (m_new, alpha * den + jnp.sum(p, axis=0, keepdims=True),
                        alpha * acc + _dot(vt_ref[j, :, 0:rows], p.astype(BF16)))

            init = (jnp.full((1, tq), -jnp.inf, F32), jnp.zeros((1, tq), F32), jnp.zeros((HEAD_DIM, tq), F32))
            _, den, acc = lax.fori_loop(0, n_tiles, body, init)
            acc_ref[sb, 0:HEAD_DIM, r * tq:(r + 1) * tq] = acc
            acc_ref[sb, HEAD_DIM:HEAD_DIM + sl, r * tq:(r + 1) * tq] = jnp.broadcast_to(den * (1.0 / sl), (sl, tq))

    def finish(sb):
        for r in range(Q_PER_KV):
            acc = acc_ref[sb, :, r * tq:(r + 1) * tq]
            den = jnp.sum(acc[HEAD_DIM:HEAD_DIM + sl, :], axis=0, keepdims=True)
            o = acc[0:HEAD_DIM, :] * (1.0 / den)
            oall_ref[sb, r * HEAD_DIM:(r + 1) * HEAD_DIM, :] = o.astype(BF16)
        o_t = _dot_nt(eye_ref[...], oall_ref[sb])
        gate = _silu_from_half(ag_refs[sb][...].astype(F32))
        o_ref[sb * tq:(sb + 1) * tq, :] = (o_t * gate).astype(BF16)

    bounded = bounded_ref[0] == 1

    @pl.when(bounded)
    def _():
        for sb in range(nsb):
            run(sb)
            finish(sb)

    @pl.when(jnp.logical_not(bounded))
    def _():
        for sb in range(nsb):
            run_online(sb)
            finish(sb)


def _attention(bounded, qt, k, vt, proj, eye_tq, q_block0, n_qblocks, nsb, n_tiles, rows):
    B = proj.shape[0]
    nkb, tk = k.shape[1], k.shape[2]
    gw = Q_PER_KV * HEAD_DIM
    nq = Q_PER_KV * ATT_TQ
    assert n_qblocks % nsb == 0

    def qmap(sb):
        return lambda b, g, i, f: (b, g, 0, q_block0 + i * nsb + sb)

    def agmap(sb):
        return lambda b, g, i, f: (b, q_block0 + i * nsb + sb, COL_AG // gw + g)

    in_specs = ([pl.BlockSpec((None, None, KV_W, nq), qmap(sb)) for sb in range(nsb)]
                + [pl.BlockSpec((None, nkb, tk, KV_W), lambda b, g, i, f: (b, 0, 0, 0)),
                   pl.BlockSpec((None, None, nkb, HEAD_DIM, tk), lambda b, g, i, f: (b, g, 0, 0, 0))]
                + [pl.BlockSpec((None, ATT_TQ, gw), agmap(sb)) for sb in range(nsb)]
                + [pl.BlockSpec((ATT_TQ, ATT_TQ), lambda b, g, i, f: (0, 0))])
    args = [bounded] + [qt] * nsb + [k, vt] + [proj] * nsb + [eye_tq]
    grid_spec = pltpu.PrefetchScalarGridSpec(
        num_scalar_prefetch=1,
        grid=(B, N_KV, n_qblocks // nsb),
        in_specs=in_specs,
        out_specs=pl.BlockSpec((None, nsb * ATT_TQ, gw), lambda b, g, i, f: (b, i, g)),
        scratch_shapes=[pltpu.VMEM((ATT_UNROLL, tk, nq), BF16),
                        pltpu.VMEM((nsb, HEAD_DIM + SUBLANES, nq), F32),
                        pltpu.VMEM((nsb, gw, ATT_TQ), BF16)])
    return pl.pallas_call(
        functools.partial(_attn_kernel, nsb=nsb, n_tiles=n_tiles, rows=rows),
        grid_spec=grid_spec,
        out_shape=jax.ShapeDtypeStruct((B, n_qblocks * ATT_TQ, ATTN_W), BF16),
        compiler_params=_cparams(("arbitrary", "arbitrary", "arbitrary"), 40),
        name="gqa_attention",
    )(*args)


def _gla_direction(qk, v, pa, a2, ab, st_ref, o_ref, fwd):
    C, c = GLA_C, GLA_SUB
    q = qk[:, 0:GLA_KW].astype(F32) * (GLA_DK ** -0.5)
    k = qk[:, GLA_KW:2 * GLA_KW].astype(F32)
    z = _dot(pa, a2) + ab
    yield
    g =(jnp.minimum(z, 0.0) - jnp.log(1.0 + jnp.exp(-jnp.abs(z)))) * (1.0 / GLA_TEMP)
    row = lax.broadcasted_iota(jnp.int32, (C, C), 0)
    col = lax.broadcasted_iota(jnp.int32, (C, C), 1)
    blk = jnp.bitwise_and(row, -c)
    if fwd:
        tri = col <= row
        inblk = tri & (col >= blk)
    else:
        tri = col >= row
        inblk = tri & (col < blk + c)
    sel = jnp.concatenate([jnp.where(tri, 1.0, 0.0), jnp.where(inblk, 1.0, 0.0)], axis=0).astype(BF16)
    g_hi = g.astype(BF16)
    g_lo = (g - g_hi.astype(F32)).astype(BF16)
    cs = _dot(sel, jnp.concatenate([g_hi, g_lo], axis=1))
    yield
    cs =cs[:, 0:GLA_KW] + cs[:, GLA_KW:2 * GLA_KW]
    b = cs[0:C]
    d = cs[C:2 * C]
    ref = b - d
    total = b[C - 1:C, :] if fwd else b[0:1, :]
    q_in = (q * jnp.exp(d)).astype(BF16)
    q_st = (q * jnp.exp(b)).astype(BF16)
    k_out = (k * jnp.exp(total - b)).astype(BF16)
    w_tot = jnp.exp(total)
    zero = jnp.zeros((), BF16)
    nb = C // c
    k_blk = []
    for i in range(nb):
        lo, hi = (0, (i + 1) * c) if fwd else (i * c, C)
        kb = (k[lo:hi] * jnp.exp(ref[i * c:i * c + 1, :] - b[lo:hi])).astype(BF16)
        pad = jnp.zeros((C - (hi - lo), GLA_KW), BF16)
        k_blk.append(jnp.concatenate(([kb, pad] if fwd else [pad, kb]) if hi - lo < C else [kb], axis=0))
    head_of_lane = jnp.right_shift(lax.broadcasted_iota(jnp.int32, (1, GLA_KW), 1), GLA_DK.bit_length() - 1)
    a_two = []
    for m in range(nb // 2):
        qr = q_in[2 * m * c:(2 * m + 2) * c]
        q_heads = jnp.concatenate([jnp.where(head_of_lane == h, qr, zero) for h in range(GLA_HEADS)], axis=0)
        a_two.append(_dot_nt(q_heads, jnp.concatenate(k_blk[2 * m:2 * m + 2], axis=0)))
    yield

    def a_block(h, i):
        m, r = divmod(i, 2)
        return a_two[m][h * 2 * c + r * c:h * 2 * c + (r + 1) * c, r * C:(r + 1) * C]

    a_cat = jnp.concatenate(
        [jnp.where(tri, jnp.concatenate([a_block(h, i) for i in range(nb)], axis=0), 0.0)
         for h in range(GLA_HEADS)], axis=1).astype(BF16)
    vhead_of_lane = jnp.right_shift(lax.broadcasted_iota(jnp.int32, (1, GLA_VW), 1), GLA_DV.bit_length() - 1)
    v_bd = jnp.concatenate([jnp.where(vhead_of_lane == h, v, zero) for h in range(GLA_HEADS)], axis=0)
    st = st_ref[...]
    o = _dot(a_cat, v_bd) + _dot_nt(q_st, st.astype(BF16))
    upd = lax.dot_general(v, k_out, (((0,), (0,)), ((), ())), preferred_element_type=F32)
    same_head = jnp.right_shift(lax.broadcasted_iota(jnp.int32, (GLA_VW, 1), 0),
                                GLA_DV.bit_length() - 1) == head_of_lane
    st_ref[...] = jnp.where(same_head, st * w_tot + upd, 0.0)
    o_ref[...] = o
    yield


def _gla_kernel(qkf_ref, vf_ref, paf_ref, qkb_ref, vb_ref, pab_ref, a2f_ref, abf_ref, a2b_ref, abb_ref,
                of_ref, ob_ref, stf_ref, stb_ref):
    @pl.when(pl.program_id(0) == 0)
    def _():
        stf_ref[...] = jnp.zeros_like(stf_ref)
        stb_ref[...] = jnp.zeros_like(stb_ref)

    chains = []
    for bi in range(qkf_ref.shape[0]):
        chains.append(_gla_direction(qkf_ref[bi], vf_ref[bi], paf_ref[bi], a2f_ref[...], abf_ref[...],
                                     stf_ref.at[bi], of_ref.at[bi], True))
        chains.append(_gla_direction(qkb_ref[bi], vb_ref[bi], pab_ref[bi], a2b_ref[...], abb_ref[...],
                                     stb_ref.at[bi], ob_ref.at[bi], False))
    for _ in range(4):
        for chain in chains:
            next(chain)


def _gla(proj, a2f, abf, a2b, abb, ctx_len):
    B, T = proj.shape[0], proj.shape[1]
    C = GLA_C
    nch, nctx = T // C, ctx_len // C

    def cb(s):
        return jnp.where(s < nctx, nctx - 1 - s, nch + nctx - 1 - s)

    full = lambda shape: pl.BlockSpec(shape, lambda s: (0,) * len(shape))
    return pl.pallas_call(
        _gla_kernel,
        grid=(nch,),
        in_specs=[pl.BlockSpec((B, C, 2 * GLA_KW), lambda s: (0, s, COL_GQK // (2 * GLA_KW))),
                  pl.BlockSpec((B, C, GLA_VW), lambda s: (0, s, COL_GV // GLA_VW)),
                  pl.BlockSpec((B, C, LANES), lambda s: (0, s, COL_GATE // LANES)),
                  pl.BlockSpec((B, C, 2 * GLA_KW), lambda s: (0, cb(s), COL_GQK // (2 * GLA_KW))),
                  pl.BlockSpec((B, C, GLA_VW), lambda s: (0, cb(s), COL_GV // GLA_VW)),
                  pl.BlockSpec((B, C, LANES), lambda s: (0, cb(s), COL_GATE // LANES)),
                  full((LANES, GLA_KW)), full((1, GLA_KW)), full((LANES, GLA_KW)), full((1, GLA_KW))],
        out_specs=[pl.BlockSpec((B, C, GLA_VW), lambda s: (0, s, 0)),
                   pl.BlockSpec((B, C, GLA_VW), lambda s: (0, cb(s), 0))],
        out_shape=[jax.ShapeDtypeStruct((B, T, GLA_VW), F32),
                   jax.ShapeDtypeStruct((B, T, GLA_VW), F32)],
        scratch_shapes=[pltpu.VMEM((B, GLA_VW, GLA_KW), F32),
                        pltpu.VMEM((B, GLA_VW, GLA_KW), F32)],
        compiler_params=_cparams(("arbitrary",), 32),
        name="gla_scan",
    )(proj, proj, proj, proj, proj, proj, a2f, abf, a2b, abb)


def _merge_kernel(x_ref, mod_ref, ma_ref, mg_ref, ms_ref, su_ref, sv_ref, sg_ref, gg_ref, yac_ref, yal_ref,
                  of_ref, ob_ref,
                  onorm_ref, lng_ref, lnb_ref, sw_ref, sb_ref, wa_ref, wg_ref, ws_ref, wo_ref, pg_ref, pb_ref,
                  o_ref, *, tm, ctx_tiles, row_off, alpha):
    D = x_ref.shape[-1]
    i = pl.program_id(1) + row_off
    og = of_ref[...] + ob_ref[...]
    parts = []
    for h in range(GLA_HEADS):
        oh = og[:, h * GLA_DV:(h + 1) * GLA_DV]
        ms = jnp.mean(oh * oh, axis=-1, keepdims=True)
        parts.append(oh * lax.rsqrt(ms + EPS))
    yg = jnp.concatenate(parts, axis=1) * onorm_ref[...] * _silu_from_half(gg_ref[...]).astype(F32)
    sv = sv_ref[...].astype(F32)
    mu = jnp.mean(sv, axis=-1, keepdims=True)
    dv = sv - mu
    var = jnp.mean(dv * dv, axis=-1, keepdims=True)
    vn = (dv * lax.rsqrt(var + EPS) * lng_ref[...] + lnb_ref[...]).astype(BF16)
    gw = SGU_W // SGU_GROUPS
    rows = []
    for n in range(tm // SGU_CHUNK):
        cols = []
        for g in range(SGU_GROUPS):
            blk = vn[n * SGU_CHUNK:(n + 1) * SGU_CHUNK, g * gw:(g + 1) * gw]
            cols.append(_dot(sw_ref[g], blk))
        rows.append(jnp.concatenate(cols, axis=1) + sb_ref[...])
    mixed = jnp.concatenate(rows, axis=0)
    ys = su_ref[...].astype(F32) * mixed * _silu_from_half(sg_ref[...]).astype(F32)

    def gated(g_ref, y, w_ref):
        half = _dot(y, w_ref[...]).astype(BF16)
        return half + half * jnp.tanh(g_ref[...])

    ya = jnp.where(i < ctx_tiles, yac_ref[...], yal_ref[...])
    m = (gated(ma_ref, ya, wa_ref) + gated(mg_ref, yg.astype(BF16), wg_ref)
         + gated(ms_ref, ys.astype(BF16), ws_ref))
    out = _dot(m, wo_ref[...])
    gate = jnp.where(i < ctx_tiles, mod_ref[1:2, 2 * D:3 * D], mod_ref[0:1, 2 * D:3 * D])
    r = alpha * x_ref[...] + gate * out
    mu = jnp.mean(r, axis=-1, keepdims=True)
    dr = r - mu
    var = jnp.mean(dr * dr, axis=-1, keepdims=True)
    o_ref[...] = dr * lax.rsqrt(var + EPS) * pg_ref[...] + pb_ref[...]


def _merge(xall, modb, proj, ya_ctx, ya_lat, of, ob, p, ctx_len, tm, skip_ctx, alpha):
    B, T, D = xall.shape
    ctx_tiles = ctx_len // tm
    row_off = ctx_tiles if skip_ctx else 0
    nt = T // tm - row_off
    rows = lambda w, c: pl.BlockSpec((None, tm, w), lambda b, i: (b, i + row_off, c))
    ya_ctx_spec = pl.BlockSpec((None, tm, ATTN_W), lambda b, i: (b, jnp.minimum(i + row_off, ctx_tiles - 1), 0))
    ya_lat_spec = pl.BlockSpec((None, tm, ATTN_W), lambda b, i: (b, jnp.maximum(i + row_off - ctx_tiles, 0), 0))
    full = lambda shape: pl.BlockSpec(shape, lambda b, i: (0,) * len(shape))
    return pl.pallas_call(
        functools.partial(_merge_kernel, tm=tm, ctx_tiles=ctx_tiles, row_off=row_off, alpha=alpha),
        grid=(B, nt),
        in_specs=[rows(D, 0),
                  pl.BlockSpec((None, 8, 3 * D), lambda b, i: (b, 0, 0)),
                  rows(D, COL_MA // D), rows(D, COL_MG // D), rows(D, COL_MS // D),
                  rows(SGU_W, COL_SU // SGU_W), rows(SGU_W, COL_SV // SGU_W), rows(SGU_W, COL_SG // SGU_W),
                  rows(GLA_VW, COL_GG // GLA_VW),
                  ya_ctx_spec, ya_lat_spec, rows(GLA_VW, 0), rows(GLA_VW, 0),
                  full((1, GLA_VW)), full((1, SGU_W)), full((1, SGU_W)),
                  full((SGU_GROUPS, SGU_CHUNK, SGU_CHUNK)), full((SGU_CHUNK, SGU_W)),
                  full((ATTN_W, D)), full((GLA_VW, D)), full((SGU_W, D)), full((D, D)),
                  full((1, D)), full((1, D))],
        out_specs=pl.BlockSpec((None, tm, D), lambda b, i: (b, i, 0)),
        out_shape=jax.ShapeDtypeStruct((B, nt * tm, D), F32),
        compiler_params=_cparams(("arbitrary", "arbitrary"), 48),
        name="merge_out",
    )(xall, modb, proj, proj, proj, proj, proj, proj, proj, ya_ctx, ya_lat, of, ob,
      p["onorm"], p["lng"], p["lnb"], p["sw"], p["sb"], p["wa"], p["wg"], p["ws"], p["wo"], p["pg"], p["pb"])


W_BLK = 256
GATE_SHIFT = 2 * GLA_RANK


def _w_in_plan():
    sizes = (512, 128, 128, 512, 256, 256, 512, 16, 16, 512, 512, 512, 512, 1024, 1024, 1024)
    names = ("aq", "ak", "av", "ag", "gq", "gk", "gv", "af", "ab", "gg", "su", "sv", "sg", "ma", "mg", "ms")
    halved = ("ag", "gg", "sg", "ma", "mg", "ms")
    src, off = {}, 0
    for n, s in zip(names, sizes):
        src[n] = (off, s)
        off += s
    plan = []
    for n in ("ma", "mg", "ms", "aq", "ag", "gv", "gg", "su", "sv", "sg", "gq", "gk"):
        start, size = src[n]
        shift = start % W_BLK
        assert shift in (0, GATE_SHIFT) and size % W_BLK == 0
        for t in range(size // W_BLK):
            plan.append(((start - shift) // W_BLK + t, 1 if shift else 0, int(n in halved)))
    assert src["ak"][0] % W_BLK == 0 and src["av"][0] == src["ak"][0] + KV_W and 2 * KV_W == W_BLK
    plan.append((src["ak"][0] // W_BLK, 0, 0))
    assert src["af"][0] % W_BLK == 0 and src["ab"][0] == src["af"][0] + GLA_RANK
    plan.append((src["af"][0] // W_BLK, 2, 0))
    assert len(plan) * W_BLK == PROJ_W
    return plan, off


def _wprep_kernel(src_ref, kind_ref, half_ref, a_ref, b_ref, o_ref):
    j = pl.program_id(1)
    a = a_ref[...]
    ab = jnp.concatenate([a, b_ref[...]], axis=0)
    shifted = ab[GATE_SHIFT:GATE_SHIFT + W_BLK, :]
    row = lax.broadcasted_iota(jnp.int32, (W_BLK, 1), 0)
    gates = jnp.where(row < GATE_SHIFT, a, 0.0)
    kind = kind_ref[j]
    val = jnp.where(kind == 1, shifted, jnp.where(kind == 2, gates, a))
    o_ref[...] = (val * jnp.where(half_ref[j] == 1, 0.5, 1.0)).astype(BF16)


def _regroup_w_in(w_in):
    L, D, width = w_in.shape
    plan, used = _w_in_plan()
    assert used == width
    last_blk = (width - 1) // W_BLK
    src = jnp.array([p[0] for p in plan], jnp.int32)
    kind = jnp.array([p[1] for p in plan], jnp.int32)
    half = jnp.array([p[2] for p in plan], jnp.int32)
    wt = jnp.swapaxes(w_in, 1, 2)
    grid_spec = pltpu.PrefetchScalarGridSpec(
        num_scalar_prefetch=3,
        grid=(L, len(plan)),
        in_specs=[pl.BlockSpec((None, W_BLK, D), lambda l, j, s, k, h: (l, s[j], 0)),
                  pl.BlockSpec((None, W_BLK, D), lambda l, j, s, k, h: (l, jnp.minimum(s[j] + 1, last_blk), 0))],
        out_specs=pl.BlockSpec((None, W_BLK, D), lambda l, j, s, k, h: (l, j, 0)))
    return pl.pallas_call(
        _wprep_kernel,
        grid_spec=grid_spec,
        out_shape=jax.ShapeDtypeStruct((L, PROJ_W, D), BF16),
        compiler_params=_cparams(("arbitrary", "arbitrary"), 32),
        name="w_in_prep",
    )(src, kind, half, wt, wt)


def _rope_tables(n_lat, ctx_len):
    rows = n_lat // GRID_W
    row = np.repeat(np.arange(rows, dtype=np.float64), GRID_W)
    col = np.tile(np.arange(GRID_W, dtype=np.float64), rows)
    freqs = ROPE_THETA ** (-np.arange(0, ROPE_AXIS, 2, dtype=np.float64) / ROPE_AXIS)
    ang_r, ang_c = row[:, None] * freqs, col[:, None] * freqs
    cos = np.concatenate([np.cos(ang_r)] * 2 + [np.cos(ang_c)] * 2, axis=1)
    sin = np.concatenate([-np.sin(ang_r), np.sin(ang_r), -np.sin(ang_c), np.sin(ang_c)], axis=1)
    cos = np.concatenate([np.ones((ctx_len, HEAD_DIM)), cos], axis=0)
    sin = np.concatenate([np.zeros((ctx_len, HEAD_DIM)), sin], axis=0)
    return (jnp.asarray(np.concatenate([cos, cos], axis=1), F32),
            jnp.asarray(np.concatenate([sin, sin], axis=1), F32))


def _largest_tile(total, cap, mult):
    best = mult
    for t in range(mult, cap + 1, mult):
        if total % t == 0:
            best = t
    return best


def kernel(x, c, ctx, c_ctx, w_ada, b_ada, w_in, attn_q_norm, attn_k_norm, gla_a2_f, gla_ab_f, gla_a2_b, gla_ab_b, gla_o_norm, sgu_ln_g, sgu_ln_b, sgu_w, sgu_b, w_br_attn, w_br_gla, w_br_sgu, w_out, post_ln_g, post_ln_b):
    B, n_lat, D = x.shape
    ctx_len = ctx.shape[1]
    depth = w_in.shape[0]
    T = ctx_len + n_lat
    assert D == D_MODEL and B <= 7
    assert ctx_len % ATT_TQ == 0 and n_lat % ATT_TQ == 0 and n_lat % GRID_W == 0
    alpha = (2 * depth) ** 0.25

    tm_proj = _largest_tile(T, 1056, 16)
    tn_proj = 1536
    tk_att = _largest_tile(T, ATT_TK_MAX, ATT_TQ)
    assert ctx_len <= tk_att
    tm_prep = tk_att
    tm_merge = ATT_TQ

    xall = jnp.concatenate([ctx, x], axis=1)
    cc = jnp.zeros((8, D), F32).at[0:B].set(c).at[B].set(c_ctx)
    mod = _ada(cc, w_ada, b_ada.reshape(depth, 1, 3 * D))
    cos_t, sin_t = _rope_tables(n_lat, ctx_len)
    consts = {
        "gq": jnp.kron(jnp.eye(N_HEADS, dtype=F32), jnp.ones((HEAD_DIM, HEAD_DIM), F32)).astype(BF16),
        "gk": jnp.kron(jnp.eye(N_KV, dtype=F32), jnp.ones((HEAD_DIM, HEAD_DIM), F32)).astype(BF16),
        "eq": jnp.eye(ATTN_W, dtype=BF16),
        "ek": jnp.eye(KV_W, dtype=BF16),
    }
    eye_tq = jnp.eye(ATT_TQ, dtype=BF16)
    w_all = _regroup_w_in(w_in)

    for l in range(depth):
        last = l == depth - 1
        modb = jnp.zeros((B, 8, 3 * D), F32).at[:, 0].set(mod[l, 0:B]).at[:, 1].set(mod[l, B][None])
        proj = _inproj(xall, modb, w_all, l, ctx_len, tm_proj, tn_proj)
        qn = jnp.concatenate([attn_q_norm[l]] * 2)[None]
        kn = jnp.concatenate([attn_k_norm[l]] * 2)[None]
        qt, kk, vt = _qkprep(proj, cos_t, sin_t, qn, kn, consts, tm_prep, tk_att)
        score_bound = HEAD_DIM * jnp.max(jnp.abs(attn_q_norm[l])) * jnp.max(jnp.abs(attn_k_norm[l])) * Q_SCALE
        bounded = (score_bound <= SCORE_LIMIT).astype(jnp.int32).reshape(1)
        ctx_blocks, lat_blocks = ctx_len // ATT_TQ, n_lat // ATT_TQ
        ya_lat = _attention(bounded, qt, kk, vt, proj, eye_tq, ctx_blocks, lat_blocks, ATT_NSB, T // tk_att, tk_att)
        ya_ctx = ya_lat if last else _attention(bounded, qt, kk, vt, proj, eye_tq, 0, ctx_blocks, 1, 1, ctx_len)
        a2f = jnp.zeros((LANES, GLA_KW), F32).at[0:GLA_RANK].set(gla_a2_f[l]).astype(BF16)
        a2b = jnp.zeros((LANES, GLA_KW), F32).at[GLA_RANK:2 * GLA_RANK].set(gla_a2_b[l]).astype(BF16)
        of, ob = _gla(proj, a2f, gla_ab_f[l][None], a2b, gla_ab_b[l][None], ctx_len)
        p = {
            "onorm": jnp.concatenate([gla_o_norm[l]] * GLA_HEADS)[None],
            "lng": sgu_ln_g[l][None], "lnb": sgu_ln_b[l][None],
            "sw": sgu_w[l].astype(BF16),
            "sb": jnp.repeat(sgu_b[l].T, SGU_W // SGU_GROUPS, axis=1),
            "wa": (0.5 * w_br_attn[l]).astype(BF16), "wg": (0.5 * w_br_gla[l]).astype(BF16),
            "ws": (0.5 * w_br_sgu[l]).astype(BF16),
            "wo": w_out[l].astype(BF16), "pg": post_ln_g[l][None], "pb": post_ln_b[l][None],
        }
        xall = _merge(xall, modb, proj, ya_ctx, ya_lat, of, ob, p, ctx_len, tm_merge, last, alpha)
    return xall
```

```python
import functools

import jax
import jax.numpy as jnp
import numpy as np
from jax import lax
from jax.experimental import pallas as pl
from jax.experimental.pallas import tpu as pltpu

F32 = jnp.float32
BF16 = jnp.bfloat16

D_MODEL = 1024
GRID_W = 64
HEAD_DIM = 64
N_HEADS = 8
N_KV = 2
Q_PER_KV = N_HEADS // N_KV
ATTN_W = N_HEADS * HEAD_DIM
KV_W = N_KV * HEAD_DIM
ROPE_THETA = 10000.0
ROPE_AXIS = HEAD_DIM // 2
GLA_HEADS = 4
GLA_DK = 64
GLA_DV = 128
GLA_KW = GLA_HEADS * GLA_DK
GLA_VW = GLA_HEADS * GLA_DV
GLA_RANK = 16
GLA_TEMP = 16.0
SGU_GROUPS = 4
SGU_CHUNK = 128
SGU_W = 512
EPS = 1e-6

LANES = 128
V7X_VMEM_BYTES = 64 * 1024 * 1024

COL_MA, COL_MG, COL_MS = 0, 1024, 2048
COL_AQ, COL_AG = 3072, 3584
COL_GV, COL_GG = 4096, 4608
COL_SU, COL_SV, COL_SG = 5120, 5632, 6144
COL_GQK = 6656
COL_AKV = 7168
COL_GATE = 7424
PROJ_W = 7680

SUBLANES = 8
ATT_TQ = 256
ATT_NSB = 2
ATT_TK_MAX = 768
GLA_C = 128
GLA_SUB = 16
ATT_UNROLL = 5
Q_SCALE = 1.4426950408889634 * HEAD_DIM ** -0.5
SCORE_LIMIT = 96.0


def _cparams(sem, vmem_mb):
    return pltpu.CompilerParams(dimension_semantics=sem,
                                vmem_limit_bytes=min(vmem_mb * 1024 * 1024, V7X_VMEM_BYTES - (4 << 20)))


def _silu(x):
    return x * jax.nn.sigmoid(x)


def _silu_from_half(u):
    return u + u * jnp.tanh(u)


def _dot(a, b):
    return jnp.dot(a, b, preferred_element_type=F32)


def _dot_nt(a, b):
    return lax.dot_general(a, b, (((1,), (1,)), ((), ())), preferred_element_type=F32)


def _ada_kernel(c_ref, w_ref, b_ref, o_ref):
    c = c_ref[...]
    o_ref[...] = _dot(_silu(c).astype(BF16), w_ref[...].astype(BF16)) + b_ref[...]


def _ada(cc, w_ada, b_ada):
    L, D = w_ada.shape[0], w_ada.shape[1]
    return pl.pallas_call(
        _ada_kernel,
        grid=(L, 3),
        in_specs=[pl.BlockSpec((8, D), lambda l, j: (0, 0)),
                  pl.BlockSpec((None, D, D), lambda l, j: (l, 0, j)),
                  pl.BlockSpec((None, 1, D), lambda l, j: (l, 0, j))],
        out_specs=pl.BlockSpec((None, 8, D), lambda l, j: (l, 0, j)),
        out_shape=jax.ShapeDtypeStruct((L, 8, 3 * D), F32),
        compiler_params=_cparams(("arbitrary", "arbitrary"), 32),
        name="ada_mod",
    )(cc, w_ada, b_ada)


def _inproj_kernel(x_ref, ctx_ref, mod_ref, w_ref, o_ref, h_ref, *, tm, ctx_len, split):
    i = pl.program_id(1)
    j = pl.program_id(2)
    D = x_ref.shape[-1]

    def modulated(v, r):
        return (v * (1.0 + mod_ref[r:r + 1, D:2 * D]) + mod_ref[r:r + 1, 0:D]).astype(BF16)

    if split:
        @pl.when((j == 0) & (i == 0))
        def _():
            h_ref[0:ctx_len, :] = modulated(ctx_ref[...], 1)
            h_ref[ctx_len:tm, :] = modulated(x_ref[0, 0:tm - ctx_len, :], 0)

        @pl.when((j == 0) & (i > 0))
        def _():
            h_ref[...] = modulated(x_ref[0], 0)
    else:
        @pl.when(j == 0)
        def _():
            row = i * tm + lax.broadcasted_iota(jnp.int32, (tm, 1), 0)
            is_ctx = row < ctx_len
            shift = jnp.where(is_ctx, mod_ref[1:2, 0:D], mod_ref[0:1, 0:D])
            scale = jnp.where(is_ctx, mod_ref[1:2, D:2 * D], mod_ref[0:1, D:2 * D])
            h_ref[...] = (x_ref[0] * (1.0 + scale) + shift).astype(BF16)

    o_ref[...] = _dot_nt(h_ref[...], w_ref[...]).astype(BF16)


def _inproj(x, ctx, ctx_len, modb, w_all, layer, tm, tn, split):
    B, _, D = x.shape
    T = x.shape[1] + ctx_len if split else x.shape[1]
    assert ctx_len < tm
    if split:
        x_spec = pl.BlockSpec((pl.Element(1), pl.Element(tm), pl.Element(D)),
                              lambda b, i, j: (b, pl.multiple_of(jnp.maximum(i * tm - ctx_len, 0), SUBLANES), 0))
    else:
        x_spec = pl.BlockSpec((1, tm, D), lambda b, i, j: (b, i, 0))
    return pl.pallas_call(
        functools.partial(_inproj_kernel, tm=tm, ctx_len=ctx_len, split=split),
        grid=(B, T // tm, PROJ_W // tn),
        in_specs=[x_spec,
                  pl.BlockSpec((None, ctx_len, D), lambda b, i, j: (b, 0, 0)),
                  pl.BlockSpec((None, 8, 3 * D), lambda b, i, j: (b, 0, 0)),
                  pl.BlockSpec((None, tn, D), lambda b, i, j: (layer, j, 0))],
        out_specs=pl.BlockSpec((None, tm, tn), lambda b, i, j: (b, i, j)),
        out_shape=jax.ShapeDtypeStruct((B, T, PROJ_W), BF16),
        scratch_shapes=[pltpu.VMEM((tm, D), BF16)],
        compiler_params=_cparams(("arbitrary", "arbitrary", "arbitrary"), 40),
        name="in_proj",
    )(x, ctx, modb, w_all)


def _rope(y, cos, sin, width):
    lane = lax.broadcasted_iota(jnp.int32, (1, width), 1)
    first = jnp.bitwise_and(lane, ROPE_AXIS - 1) < (ROPE_AXIS // 2)
    up = pltpu.roll(y, width - ROPE_AXIS // 2, 1)
    dn = pltpu.roll(y, ROPE_AXIS // 2, 1)
    return y * cos + jnp.where(first, up, dn) * sin


def _qkprep_kernel(aq_ref, akv_ref, cos_ref, sin_ref, qn_ref, kn_ref, gq_ref, gk_ref,
                   eq_ref, ek_ref, qt_ref, k_ref, vt_ref, *, tm, tk):
    cos = cos_ref[...]
    sin = sin_ref[...]
    aq = aq_ref[...].astype(F32)
    ssq = _dot((aq * aq).astype(BF16), gq_ref[...])
    yq = aq * lax.rsqrt(ssq * (1.0 / HEAD_DIM) + EPS) * jnp.concatenate([qn_ref[...]] * 4, axis=1)
    yq = _rope(yq, jnp.concatenate([cos] * 4, axis=1), jnp.concatenate([sin] * 4, axis=1), ATTN_W)
    yq = (yq * Q_SCALE).astype(BF16)
    qt = _dot_nt(eq_ref[...], yq).astype(BF16)
    zeros = jnp.zeros((HEAD_DIM, ATT_TQ), BF16)
    for g in range(N_KV):
        for qb in range(tm // ATT_TQ):
            for r in range(Q_PER_KV):
                h = g * Q_PER_KV + r
                cols = slice((qb * Q_PER_KV + r) * ATT_TQ, (qb * Q_PER_KV + r + 1) * ATT_TQ)
                qt_ref[g, g * HEAD_DIM:(g + 1) * HEAD_DIM, cols] = qt[h * HEAD_DIM:(h + 1) * HEAD_DIM,
                                                                      qb * ATT_TQ:(qb + 1) * ATT_TQ]
                qt_ref[g, (1 - g) * HEAD_DIM:(2 - g) * HEAD_DIM, cols] = zeros
    akv = akv_ref[...].astype(F32)
    ak = akv[:, 0:KV_W]
    ssk = _dot((ak * ak).astype(BF16), gk_ref[...])
    yk = ak * lax.rsqrt(ssk * (1.0 / HEAD_DIM) + EPS) * kn_ref[...]
    yk = _rope(yk, cos, sin, KV_W).astype(BF16)
    nkb = tm // tk
    k_ref[...] = yk.reshape(nkb, tk, KV_W)
    av = akv_ref[:, KV_W:2 * KV_W]
    vt = _dot_nt(ek_ref[...], av).astype(BF16)
    for g in range(N_KV):
        for n in range(nkb):
            vt_ref[g, n] = vt[g * HEAD_DIM:(g + 1) * HEAD_DIM, n * tk:(n + 1) * tk]


def _qkprep(proj, cos_t, sin_t, qn, kn, consts, tm, tk):
    B, T, _ = proj.shape
    nkb_t = T // tk
    nkb = tm // tk
    full = lambda shape: pl.BlockSpec(shape, lambda b, i: (0,) * len(shape))
    return pl.pallas_call(
        functools.partial(_qkprep_kernel, tm=tm, tk=tk),
        grid=(B, T // tm),
        in_specs=[pl.BlockSpec((None, tm, ATTN_W), lambda b, i: (b, i, COL_AQ // ATTN_W)),
                  pl.BlockSpec((None, tm, 2 * KV_W), lambda b, i: (b, i, COL_AKV // (2 * KV_W))),
                  pl.BlockSpec((tm, LANES), lambda b, i: (i, 0)),
                  pl.BlockSpec((tm, LANES), lambda b, i: (i, 0)),
                  full((1, LANES)), full((1, LANES)),
                  full((ATTN_W, ATTN_W)), full((KV_W, KV_W)),
                  full((ATTN_W, ATTN_W)), full((KV_W, KV_W))],
        out_specs=[pl.BlockSpec((None, N_KV, KV_W, Q_PER_KV * tm), lambda b, i: (b, 0, 0, i)),
                   pl.BlockSpec((None, nkb, tk, KV_W), lambda b, i: (b, i, 0, 0)),
                   pl.BlockSpec((None, N_KV, nkb, HEAD_DIM, tk), lambda b, i: (b, 0, i, 0, 0))],
        out_shape=[jax.ShapeDtypeStruct((B, N_KV, KV_W, Q_PER_KV * T), BF16),
                   jax.ShapeDtypeStruct((B, nkb_t, tk, KV_W), BF16),
                   jax.ShapeDtypeStruct((B, N_KV, nkb_t, HEAD_DIM, tk), BF16)],
        compiler_params=_cparams(("arbitrary", "arbitrary"), 40),
        name="qkv_prep",
    )(proj, proj, cos_t, sin_t, qn, kn, consts["gq"], consts["gk"], consts["eq"], consts["ek"])


def _attn_kernel(bounded_ref, *refs, nsb, n_tiles, rows):
    q_refs, (k_ref, vt_ref), ag_refs, eye_ref = refs[0:nsb], refs[nsb:nsb + 2], refs[nsb + 2:2 * nsb + 2], refs[2 * nsb + 2]
    o_ref, p_ref, acc_ref, oall_ref = refs[-4:]
    nq = q_refs[0].shape[-1]
    tq = nq // Q_PER_KV
    sl = acc_ref.shape[1] - HEAD_DIM
    U = p_ref.shape[0]
    acc_ref[...] = jnp.zeros_like(acc_ref)

    def weights(sb, j, slot):
        p = jnp.exp2(_dot(k_ref[j, 0:rows, :], q_refs[sb][...]))
        p_ref[slot, 0:rows, :] = p.astype(BF16)
        return jnp.sum(p.reshape(rows // sl, sl, nq), axis=0)

    def weighted_values(j, slot):
        return _dot(vt_ref[j, :, 0:rows], p_ref[slot, 0:rows, :])

    def run(sb):
        first = sb * n_tiles
        den = weights(sb, 0, first % U)

        def group(jj, den):
            j = U * jj
            part = None
            for u in range(U):
                den = den + weights(sb, j + u + 1, (first + u + 1) % U)
                wv = weighted_values(j + u, (first + u) % U)
                part = wv if part is None else part + wv
            acc_ref[sb, 0:HEAD_DIM, :] += part
            return den

        groups = (n_tiles - 1) // U
        den = lax.fori_loop(0, groups, group, den)
        for t in range(U * groups, n_tiles):
            if t + 1 < n_tiles:
                den = den + weights(sb, t + 1, (first + t + 1) % U)
            acc_ref[sb, 0:HEAD_DIM, :] += weighted_values(t, (first + t) % U)
        acc_ref[sb, HEAD_DIM:HEAD_DIM + sl, :] = den

    def run_online(sb):
        for r in range(Q_PER_KV):
            q = q_refs[sb][:, r * tq:(r + 1) * tq]

            def body(j, carry):
                m, den, acc = carry
                s = _dot(k_ref[j, 0:rows, :], q)
                m_new = jnp.maximum(m, jnp.max(s, axis=0, keepdims=True))
                p = jnp.exp2(s - m_new)
                alpha = jnp.exp2(m - m_new)
                return (m_new, alpha * den + jnp.sum(p, axis=0, keepdims=True),
                        alpha * acc + _dot(vt_ref[j, :, 0:rows], p.astype(BF16)))

            init = (jnp.full((1, tq), -jnp.inf, F32), jnp.zeros((1, tq), F32), jnp.zeros((HEAD_DIM, tq), F32))
            _, den, acc = lax.fori_loop(0, n_tiles, body, init)
            acc_ref[sb, 0:HEAD_DIM, r * tq:(r + 1) * tq] = acc
            acc_ref[sb, HEAD_DIM:HEAD_DIM + sl, r * tq:(r + 1) * tq] = jnp.broadcast_to(den * (1.0 / sl), (sl, tq))

    def finish(sb):
        for r in range(Q_PER_KV):
            acc = acc_ref[sb, :, r * tq:(r + 1) * tq]
            den = jnp.sum(acc[HEAD_DIM:HEAD_DIM + sl, :], axis=0, keepdims=True)
            o = acc[0:HEAD_DIM, :] * (1.0 / den)
            oall_ref[sb, r * HEAD_DIM:(r + 1) * HEAD_DIM, :] = o.astype(BF16)
        o_t = _dot_nt(eye_ref[...], oall_ref[sb])
        gate = _silu_from_half(ag_refs[sb][...].astype(F32))
        o_ref[sb * tq:(sb + 1) * tq, :] = (o_t * gate).astype(BF16)

    bounded = bounded_ref[0] == 1

    @pl.when(bounded)
    def _():
        for sb in range(nsb):
            run(sb)
            finish(sb)

    @pl.when(jnp.logical_not(bounded))
    def _():
        for sb in range(nsb):
            run_online(sb)
            finish(sb)


def _attention(bounded, qt, k, vt, proj, eye_tq, q_block0, n_qblocks, nsb, n_tiles, rows):
    B = proj.shape[0]
    nkb, tk = k.shape[1], k.shape[2]
    gw = Q_PER_KV * HEAD_DIM
    nq = Q_PER_KV * ATT_TQ
    assert n_qblocks % nsb == 0

    def qmap(sb):
        return lambda b, g, i, f: (b, g, 0, q_block0 + i * nsb + sb)

    def agmap(sb):
        return lambda b, g, i, f: (b, q_block0 + i * nsb + sb, COL_AG // gw + g)

    in_specs = ([pl.BlockSpec((None, None, KV_W, nq), qmap(sb)) for sb in range(nsb)]
                + [pl.BlockSpec((None, nkb, tk, KV_W), lambda b, g, i, f: (b, 0, 0, 0)),
                   pl.BlockSpec((None, None, nkb, HEAD_DIM, tk), lambda b, g, i, f: (b, g, 0, 0, 0))]
                + [pl.BlockSpec((None, ATT_TQ, gw), agmap(sb)) for sb in range(nsb)]
                + [pl.BlockSpec((ATT_TQ, ATT_TQ), lambda b, g, i, f: (0, 0))])
    args = [bounded] + [qt] * nsb + [k, vt] + [proj] * nsb + [eye_tq]
    grid_spec = pltpu.PrefetchScalarGridSpec(
        num_scalar_prefetch=1,
        grid=(B, N_KV, n_qblocks // nsb),
        in_specs=in_specs,
        out_specs=pl.BlockSpec((None, nsb * ATT_TQ, gw), lambda b, g, i, f: (b, i, g)),
        scratch_shapes=[pltpu.VMEM((ATT_UNROLL, tk, nq), BF16),
                        pltpu.VMEM((nsb, HEAD_DIM + SUBLANES, nq), F32),
                        pltpu.VMEM((nsb, gw, ATT_TQ), BF16)])
    return pl.pallas_call(
        functools.partial(_attn_kernel, nsb=nsb, n_tiles=n_tiles, rows=rows),
        grid_spec=grid_spec,
        out_shape=jax.ShapeDtypeStruct((B, n_qblocks * ATT_TQ, ATTN_W), BF16),
        compiler_params=_cparams(("arbitrary", "arbitrary", "arbitrary"), 40),
        name="gqa_attention",
    )(*args)


def _gla_direction(qk, v, pa, a2, ab, st_ref, o_ref, fwd):
    C, c = GLA_C, GLA_SUB
    q = qk[:, 0:GLA_KW].astype(F32) * (GLA_DK ** -0.5)
    k = qk[:, GLA_KW:2 * GLA_KW].astype(F32)
    z = _dot(pa, a2) + ab
    yield
    g =(jnp.minimum(z, 0.0) - jnp.log(1.0 + jnp.exp(-jnp.abs(z)))) * (1.0 / GLA_TEMP)
    row = lax.broadcasted_iota(jnp.int32, (C, C), 0)
    col = lax.broadcasted_iota(jnp.int32, (C, C), 1)
    blk = jnp.bitwise_and(row, -c)
    if fwd:
        tri = col <= row
        inblk = tri & (col >= blk)
    else:
        tri = col >= row
        inblk = tri & (col < blk + c)
    sel = jnp.concatenate([jnp.where(tri, 1.0, 0.0), jnp.where(inblk, 1.0, 0.0)], axis=0).astype(BF16)
    g_hi = g.astype(BF16)
    g_lo = (g - g_hi.astype(F32)).astype(BF16)
    cs = _dot(sel, jnp.concatenate([g_hi, g_lo], axis=1))
    yield
    cs =cs[:, 0:GLA_KW] + cs[:, GLA_KW:2 * GLA_KW]
    b = cs[0:C]
    d = cs[C:2 * C]
    ref = b - d
    total = b[C - 1:C, :] if fwd else b[0:1, :]
    q_in = (q * jnp.exp(d)).astype(BF16)
    q_st = (q * jnp.exp(b)).astype(BF16)
    k_out = (k * jnp.exp(total - b)).astype(BF16)
    w_tot = jnp.exp(total)
    zero = jnp.zeros((), BF16)
    nb = C // c
    k_blk = []
    for i in range(nb):
        lo, hi = (0, (i + 1) * c) if fwd else (i * c, C)
        kb = (k[lo:hi] * jnp.exp(ref[i * c:i * c + 1, :] - b[lo:hi])).astype(BF16)
        pad = jnp.zeros((C - (hi - lo), GLA_KW), BF16)
        k_blk.append(jnp.concatenate(([kb, pad] if fwd else [pad, kb]) if hi - lo < C else [kb], axis=0))
    head_of_lane = jnp.right_shift(lax.broadcasted_iota(jnp.int32, (1, GLA_KW), 1), GLA_DK.bit_length() - 1)
    a_two = []
    for m in range(nb // 2):
        qr = q_in[2 * m * c:(2 * m + 2) * c]
        q_heads = jnp.concatenate([jnp.where(head_of_lane == h, qr, zero) for h in range(GLA_HEADS)], axis=0)
        a_two.append(_dot_nt(q_heads, jnp.concatenate(k_blk[2 * m:2 * m + 2], axis=0)))
    yield

    def a_block(h, i):
        m, r = divmod(i, 2)
        return a_two[m][h * 2 * c + r * c:h * 2 * c + (r + 1) * c, r * C:(r + 1) * C]

    a_cat = jnp.concatenate(
        [jnp.where(tri, jnp.concatenate([a_block(h, i) for i in range(nb)], axis=0), 0.0)
         for h in range(GLA_HEADS)], axis=1).astype(BF16)
    vhead_of_lane = jnp.right_shift(lax.broadcasted_iota(jnp.int32, (1, GLA_VW), 1), GLA_DV.bit_length() - 1)
    v_bd = jnp.concatenate([jnp.where(vhead_of_lane == h, v, zero) for h in range(GLA_HEADS)], axis=0)
    st = st_ref[...]
    o = _dot(a_cat, v_bd) + _dot_nt(q_st, st.astype(BF16))
    upd = lax.dot_general(v, k_out, (((0,), (0,)), ((), ())), preferred_element_type=F32)
    same_head = jnp.right_shift(lax.broadcasted_iota(jnp.int32, (GLA_VW, 1), 0),
                                GLA_DV.bit_length() - 1) == head_of_lane
    st_ref[...] = jnp.where(same_head, st * w_tot + upd, 0.0)
    o_ref[...] = o
    yield


def _gla_kernel(qkf_ref, vf_ref, paf_ref, qkb_ref, vb_ref, pab_ref, a2f_ref, abf_ref, a2b_ref, abb_ref,
                of_ref, ob_ref, stf_ref, stb_ref):
    @pl.when(pl.program_id(0) == 0)
    def _():
        stf_ref[...] = jnp.zeros_like(stf_ref)
        stb_ref[...] = jnp.zeros_like(stb_ref)

    chains = []
    for bi in range(qkf_ref.shape[0]):
        chains.append(_gla_direction(qkf_ref[bi], vf_ref[bi], paf_ref[bi], a2f_ref[...], abf_ref[...],
                                     stf_ref.at[bi], of_ref.at[bi], True))
        chains.append(_gla_direction(qkb_ref[bi], vb_ref[bi], pab_ref[bi], a2b_ref[...], abb_ref[...],
                                     stb_ref.at[bi], ob_ref.at[bi], False))
    for _ in range(4):
        for chain in chains:
            next(chain)


def _gla(proj, a2f, abf, a2b, abb, ctx_len):
    B, T = proj.shape[0], proj.shape[1]
    C = GLA_C
    nch, nctx = T // C, ctx_len // C

    def cb(s):
        return jnp.where(s < nctx, nctx - 1 - s, nch + nctx - 1 - s)

    full = lambda shape: pl.BlockSpec(shape, lambda s: (0,) * len(shape))
    return pl.pallas_call(
        _gla_kernel,
        grid=(nch,),
        in_specs=[pl.BlockSpec((B, C, 2 * GLA_KW), lambda s: (0, s, COL_GQK // (2 * GLA_KW))),
                  pl.BlockSpec((B, C, GLA_VW), lambda s: (0, s, COL_GV // GLA_VW)),
                  pl.BlockSpec((B, C, LANES), lambda s: (0, s, COL_GATE // LANES)),
                  pl.BlockSpec((B, C, 2 * GLA_KW), lambda s: (0, cb(s), COL_GQK // (2 * GLA_KW))),
                  pl.BlockSpec((B, C, GLA_VW), lambda s: (0, cb(s), COL_GV // GLA_VW)),
                  pl.BlockSpec((B, C, LANES), lambda s: (0, cb(s), COL_GATE // LANES)),
                  full((LANES, GLA_KW)), full((1, GLA_KW)), full((LANES, GLA_KW)), full((1, GLA_KW))],
        out_specs=[pl.BlockSpec((B, C, GLA_VW), lambda s: (0, s, 0)),
                   pl.BlockSpec((B, C, GLA_VW), lambda s: (0, cb(s), 0))],
        out_shape=[jax.ShapeDtypeStruct((B, T, GLA_VW), F32),
                   jax.ShapeDtypeStruct((B, T, GLA_VW), F32)],
        scratch_shapes=[pltpu.VMEM((B, GLA_VW, GLA_KW), F32),
                        pltpu.VMEM((B, GLA_VW, GLA_KW), F32)],
        compiler_params=_cparams(("arbitrary",), 32),
        name="gla_scan",
    )(proj, proj, proj, proj, proj, proj, a2f, abf, a2b, abb)


def _merge_kernel(xc_ref, x_ref, mod_ref, ma_ref, mg_ref, ms_ref, su_ref, sv_ref, sg_ref, gg_ref, yac_ref, yal_ref,
                  of_ref, ob_ref,
                  onorm_ref, lng_ref, lnb_ref, sw_ref, sb_ref, wa_ref, wg_ref, ws_ref, wo_ref, pg_ref, pb_ref,
                  o_ref, *, tm, ctx_tiles, row_off, alpha):
    D = x_ref.shape[-1]
    i = pl.program_id(1) + row_off
    og = of_ref[...] + ob_ref[...]
    parts = []
    for h in range(GLA_HEADS):
        oh = og[:, h * GLA_DV:(h + 1) * GLA_DV]
        ms = jnp.mean(oh * oh, axis=-1, keepdims=True)
        parts.append(oh * lax.rsqrt(ms + EPS))
    yg = jnp.concatenate(parts, axis=1) * onorm_ref[...] * _silu_from_half(gg_ref[...]).astype(F32)
    sv = sv_ref[...].astype(F32)
    mu = jnp.mean(sv, axis=-1, keepdims=True)
    dv = sv - mu
    var = jnp.mean(dv * dv, axis=-1, keepdims=True)
    vn = (dv * lax.rsqrt(var + EPS) * lng_ref[...] + lnb_ref[...]).astype(BF16)
    gw = SGU_W // SGU_GROUPS
    rows = []
    for n in range(tm // SGU_CHUNK):
        cols = []
        for g in range(SGU_GROUPS):
            blk = vn[n * SGU_CHUNK:(n + 1) * SGU_CHUNK, g * gw:(g + 1) * gw]
            cols.append(_dot(sw_ref[g], blk))
        rows.append(jnp.concatenate(cols, axis=1) + sb_ref[...])
    mixed = jnp.concatenate(rows, axis=0)
    ys = su_ref[...].astype(F32) * mixed * _silu_from_half(sg_ref[...]).astype(F32)

    def gated(g_ref, y, w_ref):
        half = _dot(y, w_ref[...]).astype(BF16)
        return half + half * jnp.tanh(g_ref[...])

    ya = jnp.where(i < ctx_tiles, yac_ref[...], yal_ref[...])
    m = (gated(ma_ref, ya, wa_ref) + gated(mg_ref, yg.astype(BF16), wg_ref)
         + gated(ms_ref, ys.astype(BF16), ws_ref))
    out = _dot(m, wo_ref[...])
    gate = jnp.where(i < ctx_tiles, mod_ref[1:2, 2 * D:3 * D], mod_ref[0:1, 2 * D:3 * D])
    r = alpha * jnp.where(i < ctx_tiles, xc_ref[...], x_ref[...]) + gate * out
    mu = jnp.mean(r, axis=-1, keepdims=True)
    dr = r - mu
    var = jnp.mean(dr * dr, axis=-1, keepdims=True)
    o_ref[...] = dr * lax.rsqrt(var + EPS) * pg_ref[...] + pb_ref[...]


def _merge(x_ctx, x_lat, lat_tile0, modb, proj, ya_ctx, ya_lat, of, ob, p, ctx_len, tm, skip_ctx, alpha):
    B, T, D = proj.shape[0], proj.shape[1], x_lat.shape[-1]
    ctx_tiles = ctx_len // tm
    row_off = ctx_tiles if skip_ctx else 0
    nt = T // tm - row_off
    rows = lambda w, c: pl.BlockSpec((None, tm, w), lambda b, i: (b, i + row_off, c))
    ctx_rows = lambda w: pl.BlockSpec((None, tm, w), lambda b, i: (b, jnp.minimum(i + row_off, ctx_tiles - 1), 0))
    lat_rows = lambda w, t0: pl.BlockSpec(
        (None, tm, w), lambda b, i: (b, jnp.maximum(i + row_off - ctx_tiles, 0) + t0, 0))
    ya_ctx_spec, ya_lat_spec = ctx_rows(ATTN_W), lat_rows(ATTN_W, 0)
    full = lambda shape: pl.BlockSpec(shape, lambda b, i: (0,) * len(shape))
    return pl.pallas_call(
        functools.partial(_merge_kernel, tm=tm, ctx_tiles=ctx_tiles, row_off=row_off, alpha=alpha),
        grid=(B, nt),
        in_specs=[ctx_rows(D), lat_rows(D, lat_tile0),
                  pl.BlockSpec((None, 8, 3 * D), lambda b, i: (b, 0, 0)),
                  rows(D, COL_MA // D), rows(D, COL_MG // D), rows(D, COL_MS // D),
                  rows(SGU_W, COL_SU // SGU_W), rows(SGU_W, COL_SV // SGU_W), rows(SGU_W, COL_SG // SGU_W),
                  rows(GLA_VW, COL_GG // GLA_VW),
                  ya_ctx_spec, ya_lat_spec, rows(GLA_VW, 0), rows(GLA_VW, 0),
                  full((1, GLA_VW)), full((1, SGU_W)), full((1, SGU_W)),
                  full((SGU_GROUPS, SGU_CHUNK, SGU_CHUNK)), full((SGU_CHUNK, SGU_W)),
                  full((ATTN_W, D)), full((GLA_VW, D)), full((SGU_W, D)), full((D, D)),
                  full((1, D)), full((1, D))],
        out_specs=pl.BlockSpec((None, tm, D), lambda b, i: (b, i, 0)),
        out_shape=jax.ShapeDtypeStruct((B, nt * tm, D), F32),
        compiler_params=_cparams(("arbitrary", "arbitrary"), 48),
        name="merge_out",
    )(x_ctx, x_lat, modb, proj, proj, proj, proj, proj, proj, proj, ya_ctx, ya_lat, of, ob,
      p["onorm"], p["lng"], p["lnb"], p["sw"], p["sb"], p["wa"], p["wg"], p["ws"], p["wo"], p["pg"], p["pb"])


W_BLK = 256
GATE_SHIFT = 2 * GLA_RANK


def _w_in_plan():
    sizes = (512, 128, 128, 512, 256, 256, 512, 16, 16, 512, 512, 512, 512, 1024, 1024, 1024)
    names = ("aq", "ak", "av", "ag", "gq", "gk", "gv", "af", "ab", "gg", "su", "sv", "sg", "ma", "mg", "ms")
    halved = ("ag", "gg", "sg", "ma", "mg", "ms")
    src, off = {}, 0
    for n, s in zip(names, sizes):
        src[n] = (off, s)
        off += s
    plan = []
    for n in ("ma", "mg", "ms", "aq", "ag", "gv", "gg", "su", "sv", "sg", "gq", "gk"):
        start, size = src[n]
        shift = start % W_BLK
        assert shift in (0, GATE_SHIFT) and size % W_BLK == 0
        for t in range(size // W_BLK):
            plan.append(((start - shift) // W_BLK + t, 1 if shift else 0, int(n in halved)))
    assert src["ak"][0] % W_BLK == 0 and src["av"][0] == src["ak"][0] + KV_W and 2 * KV_W == W_BLK
    plan.append((src["ak"][0] // W_BLK, 0, 0))
    assert src["af"][0] % W_BLK == 0 and src["ab"][0] == src["af"][0] + GLA_RANK
    plan.append((src["af"][0] // W_BLK, 2, 0))
    assert len(plan) * W_BLK == PROJ_W
    return plan, off


def _wprep_kernel(src_ref, nxt_ref, kind_ref, half_ref, a_ref, b_ref, o_ref):
    j = pl.program_id(1)
    a = a_ref[...]
    ab = jnp.concatenate([a, b_ref[...]], axis=0)
    shifted = ab[GATE_SHIFT:GATE_SHIFT + W_BLK, :]
    row = lax.broadcasted_iota(jnp.int32, (W_BLK, 1), 0)
    gates = jnp.where(row < GATE_SHIFT, a, 0.0)
    kind = kind_ref[j]
    val = jnp.where(kind == 1, shifted, jnp.where(kind == 2, gates, a))
    o_ref[...] = (val * jnp.where(half_ref[j] == 1, 0.5, 1.0)).astype(BF16)


def _regroup_w_in(w_in):
    L, D, width = w_in.shape
    plan, used = _w_in_plan()
    assert used == width
    last_blk = (width - 1) // W_BLK
    src = jnp.array([p[0] for p in plan], jnp.int32)
    kind = jnp.array([p[1] for p in plan], jnp.int32)
    half = jnp.array([p[2] for p in plan], jnp.int32)
    nxt, cur = [], min(plan[0][0] + 1, last_blk)
    for s, k, _ in plan:
        cur = min(s + 1, last_blk) if k == 1 else cur
        nxt.append(cur)
    nxt = jnp.array(nxt, jnp.int32)
    wt = jnp.swapaxes(w_in, 1, 2)
    grid_spec = pltpu.PrefetchScalarGridSpec(
        num_scalar_prefetch=4,
        grid=(L, len(plan)),
        in_specs=[pl.BlockSpec((None, W_BLK, D), lambda l, j, s, n, k, h: (l, s[j], 0)),
                  pl.BlockSpec((None, W_BLK, D), lambda l, j, s, n, k, h: (l, n[j], 0))],
        out_specs=pl.BlockSpec((None, W_BLK, D), lambda l, j, s, n, k, h: (l, j, 0)))
    return pl.pallas_call(
        _wprep_kernel,
        grid_spec=grid_spec,
        out_shape=jax.ShapeDtypeStruct((L, PROJ_W, D), BF16),
        compiler_params=_cparams(("arbitrary", "arbitrary"), 32),
        name="w_in_prep",
    )(src, nxt, kind, half, wt, wt)


def _rope_tables(n_lat, ctx_len):
    rows = n_lat // GRID_W
    row = np.repeat(np.arange(rows, dtype=np.float64), GRID_W)
    col = np.tile(np.arange(GRID_W, dtype=np.float64), rows)
    freqs = ROPE_THETA ** (-np.arange(0, ROPE_AXIS, 2, dtype=np.float64) / ROPE_AXIS)
    ang_r, ang_c = row[:, None] * freqs, col[:, None] * freqs
    cos = np.concatenate([np.cos(ang_r)] * 2 + [np.cos(ang_c)] * 2, axis=1)
    sin = np.concatenate([-np.sin(ang_r), np.sin(ang_r), -np.sin(ang_c), np.sin(ang_c)], axis=1)
    cos = np.concatenate([np.ones((ctx_len, HEAD_DIM)), cos], axis=0)
    sin = np.concatenate([np.zeros((ctx_len, HEAD_DIM)), sin], axis=0)
    return (jnp.asarray(np.concatenate([cos, cos], axis=1), F32),
            jnp.asarray(np.concatenate([sin, sin], axis=1), F32))


def _largest_tile(total, cap, mult):
    best = mult
    for t in range(mult, cap + 1, mult):
        if total % t == 0:
            best = t
    return best


def kernel(x, c, ctx, c_ctx, w_ada, b_ada, w_in, attn_q_norm, attn_k_norm, gla_a2_f, gla_ab_f, gla_a2_b, gla_ab_b, gla_o_norm, sgu_ln_g, sgu_ln_b, sgu_w, sgu_b, w_br_attn, w_br_gla, w_br_sgu, w_out, post_ln_g, post_ln_b):
    B, n_lat, D = x.shape
    ctx_len = ctx.shape[1]
    depth = w_in.shape[0]
    T = ctx_len + n_lat
    assert D == D_MODEL and B <= 7
    assert ctx_len % ATT_TQ == 0 and n_lat % ATT_TQ == 0 and n_lat % GRID_W == 0
    alpha = (2 * depth) ** 0.25

    tm_proj = _largest_tile(T, 1056, 16)
    tn_proj = 1536
    tk_att = _largest_tile(T, ATT_TK_MAX, ATT_TQ)
    assert ctx_len <= tk_att
    tm_prep = tk_att
    tm_merge = ATT_TQ

    xall = None
    cc = jnp.zeros((8, D), F32).at[0:B].set(c).at[B].set(c_ctx)
    mod = _ada(cc, w_ada, b_ada.reshape(depth, 1, 3 * D))
    cos_t, sin_t = _rope_tables(n_lat, ctx_len)
    consts = {
        "gq": jnp.kron(jnp.eye(N_HEADS, dtype=F32), jnp.ones((HEAD_DIM, HEAD_DIM), F32)).astype(BF16),
        "gk": jnp.kron(jnp.eye(N_KV, dtype=F32), jnp.ones((HEAD_DIM, HEAD_DIM), F32)).astype(BF16),
        "eq": jnp.eye(ATTN_W, dtype=BF16),
        "ek": jnp.eye(KV_W, dtype=BF16),
    }
    eye_tq = jnp.eye(ATT_TQ, dtype=BF16)
    w_all = _regroup_w_in(w_in)

    for l in range(depth):
        last = l == depth - 1
        modb = jnp.zeros((B, 8, 3 * D), F32).at[:, 0].set(mod[l, 0:B]).at[:, 1].set(mod[l, B][None])
        first = xall is None
        proj = _inproj(x if first else xall, ctx if first else xall, ctx_len, modb, w_all, l, tm_proj, tn_proj,
                       split=first)
        qn = jnp.concatenate([attn_q_norm[l]] * 2)[None]
        kn = jnp.concatenate([attn_k_norm[l]] * 2)[None]
        qt, kk, vt = _qkprep(proj, cos_t, sin_t, qn, kn, consts, tm_prep, tk_att)
        score_bound = HEAD_DIM * jnp.max(jnp.abs(attn_q_norm[l])) * jnp.max(jnp.abs(attn_k_norm[l])) * Q_SCALE
        bounded = (score_bound <= SCORE_LIMIT).astype(jnp.int32).reshape(1)
        ctx_blocks, lat_blocks = ctx_len // ATT_TQ, n_lat // ATT_TQ
        ya_lat = _attention(bounded, qt, kk, vt, proj, eye_tq, ctx_blocks, lat_blocks, ATT_NSB, T // tk_att, tk_att)
        ya_ctx = ya_lat if last else _attention(bounded, qt, kk, vt, proj, eye_tq, 0, ctx_blocks, 1, 1, ctx_len)
        a2f = jnp.zeros((LANES, GLA_KW), F32).at[0:GLA_RANK].set(gla_a2_f[l]).astype(BF16)
        a2b = jnp.zeros((LANES, GLA_KW), F32).at[GLA_RANK:2 * GLA_RANK].set(gla_a2_b[l]).astype(BF16)
        of, ob = _gla(proj, a2f, gla_ab_f[l][None], a2b, gla_ab_b[l][None], ctx_len)
        p = {
            "onorm": jnp.concatenate([gla_o_norm[l]] * GLA_HEADS)[None],
            "lng": sgu_ln_g[l][None], "lnb": sgu_ln_b[l][None],
            "sw": sgu_w[l].astype(BF16),
            "sb": jnp.repeat(sgu_b[l].T, SGU_W // SGU_GROUPS, axis=1),
            "wa": (0.5 * w_br_attn[l]).astype(BF16), "wg": (0.5 * w_br_gla[l]).astype(BF16),
            "ws": (0.5 * w_br_sgu[l]).astype(BF16),
            "wo": w_out[l].astype(BF16), "pg": post_ln_g[l][None], "pb": post_ln_b[l][None],
        }
        xall = _merge(ctx if first else xall, x if first else xall, 0 if first else ctx_len // tm_merge,
                      modb, proj, ya_ctx, ya_lat, of, ob, p, ctx_len, tm_merge, last, alpha)
    return xall
```

```python
import functools

import jax
import jax.numpy as jnp
import numpy as np
from jax import lax
from jax.experimental import pallas as pl
from jax.experimental.pallas import tpu as pltpu

F32 = jnp.float32
BF16 = jnp.bfloat16

D_MODEL = 1024
GRID_W = 64
HEAD_DIM = 64
N_HEADS = 8
N_KV = 2
Q_PER_KV = N_HEADS // N_KV
ATTN_W = N_HEADS * HEAD_DIM
KV_W = N_KV * HEAD_DIM
ROPE_THETA = 10000.0
ROPE_AXIS = HEAD_DIM // 2
GLA_HEADS = 4
GLA_DK = 64
GLA_DV = 128
GLA_KW = GLA_HEADS * GLA_DK
GLA_VW = GLA_HEADS * GLA_DV
GLA_RANK = 16
GLA_TEMP = 16.0
SGU_GROUPS = 4
SGU_CHUNK = 128
SGU_W = 512
EPS = 1e-6

LANES = 128
V7X_VMEM_BYTES = 64 * 1024 * 1024

COL_MA, COL_MG, COL_MS = 0, 1024, 2048
COL_AQ, COL_AG = 3072, 3584
COL_GV, COL_GG = 4096, 4608
COL_SU, COL_SV, COL_SG = 5120, 5632, 6144
COL_GQK = 6656
COL_AKV = 7168
COL_GATE = 7424
PROJ_W = 7680

SUBLANES = 8
ATT_TQ = 256
ATT_NSB = 4
ATT_TK_MAX = 768
GLA_C = 128
GLA_SUB = 16
GLA_STEPS = 2
ATT_UNROLL = 5
Q_SCALE = 1.4426950408889634 * HEAD_DIM ** -0.5
SCORE_LIMIT = 96.0


def _cparams(sem, vmem_mb):
    return pltpu.CompilerParams(dimension_semantics=sem,
                                vmem_limit_bytes=min(vmem_mb * 1024 * 1024, V7X_VMEM_BYTES - (4 << 20)))


def _silu(x):
    return x * jax.nn.sigmoid(x)


def _silu_from_half(u):
    return u + u * jnp.tanh(u)


def _dot(a, b):
    return jnp.dot(a, b, preferred_element_type=F32)


def _dot_nt(a, b):
    return lax.dot_general(a, b, (((1,), (1,)), ((), ())), preferred_element_type=F32)


def _ada_kernel(c_ref, w_ref, b_ref, o_ref):
    c = c_ref[...]
    o_ref[...] = _dot(_silu(c).astype(BF16), w_ref[...].astype(BF16)) + b_ref[...]


def _ada(cc, w_ada, b_ada):
    L, D = w_ada.shape[0], w_ada.shape[1]
    return pl.pallas_call(
        _ada_kernel,
        grid=(L, 3),
        in_specs=[pl.BlockSpec((8, D), lambda l, j: (0, 0)),
                  pl.BlockSpec((None, D, D), lambda l, j: (l, 0, j)),
                  pl.BlockSpec((None, 1, D), lambda l, j: (l, 0, j))],
        out_specs=pl.BlockSpec((None, 8, D), lambda l, j: (l, 0, j)),
        out_shape=jax.ShapeDtypeStruct((L, 8, 3 * D), F32),
        compiler_params=_cparams(("arbitrary", "arbitrary"), 32),
        name="ada_mod",
    )(cc, w_ada, b_ada)


def _inproj_kernel(x_ref, ctx_ref, mod_ref, w_ref, o_ref, h_ref, *, tm, ctx_len, split):
    i = pl.program_id(1)
    j = pl.program_id(2)
    D = x_ref.shape[-1]

    def modulated(v, r):
        return (v * (1.0 + mod_ref[r:r + 1, D:2 * D]) + mod_ref[r:r + 1, 0:D]).astype(BF16)

    if split:
        @pl.when((j == 0) & (i == 0))
        def _():
            h_ref[0:ctx_len, :] = modulated(ctx_ref[...], 1)
            h_ref[ctx_len:tm, :] = modulated(x_ref[0, 0:tm - ctx_len, :], 0)

        @pl.when((j == 0) & (i > 0))
        def _():
            h_ref[...] = modulated(x_ref[0], 0)
    else:
        @pl.when(j == 0)
        def _():
            row = i * tm + lax.broadcasted_iota(jnp.int32, (tm, 1), 0)
            is_ctx = row < ctx_len
            shift = jnp.where(is_ctx, mod_ref[1:2, 0:D], mod_ref[0:1, 0:D])
            scale = jnp.where(is_ctx, mod_ref[1:2, D:2 * D], mod_ref[0:1, D:2 * D])
            h_ref[...] = (x_ref[0] * (1.0 + scale) + shift).astype(BF16)

    o_ref[...] = _dot_nt(h_ref[...], w_ref[...]).astype(BF16)


def _inproj(x, ctx, ctx_len, modb, w_all, layer, tm, tn, split):
    B, _, D = x.shape
    T = x.shape[1] + ctx_len if split else x.shape[1]
    assert ctx_len < tm
    if split:
        x_spec = pl.BlockSpec((pl.Element(1), pl.Element(tm), pl.Element(D)),
                              lambda b, i, j: (b, pl.multiple_of(jnp.maximum(i * tm - ctx_len, 0), SUBLANES), 0))
    else:
        x_spec = pl.BlockSpec((1, tm, D), lambda b, i, j: (b, i, 0))
    return pl.pallas_call(
        functools.partial(_inproj_kernel, tm=tm, ctx_len=ctx_len, split=split),
        grid=(B, T // tm, PROJ_W // tn),
        in_specs=[x_spec,
                  pl.BlockSpec((None, ctx_len, D), lambda b, i, j: (b, 0, 0)),
                  pl.BlockSpec((None, 8, 3 * D), lambda b, i, j: (b, 0, 0)),
                  pl.BlockSpec((None, tn, D), lambda b, i, j: (layer, j, 0))],
        out_specs=pl.BlockSpec((None, tm, tn), lambda b, i, j: (b, i, j)),
        out_shape=jax.ShapeDtypeStruct((B, T, PROJ_W), BF16),
        scratch_shapes=[pltpu.VMEM((tm, D), BF16)],
        compiler_params=_cparams(("arbitrary", "arbitrary", "arbitrary"), 40),
        name="in_proj",
    )(x, ctx, modb, w_all)


def _rope(y, cos, sin, width):
    lane = lax.broadcasted_iota(jnp.int32, (1, width), 1)
    first = jnp.bitwise_and(lane, ROPE_AXIS - 1) < (ROPE_AXIS // 2)
    up = pltpu.roll(y, width - ROPE_AXIS // 2, 1)
    dn = pltpu.roll(y, ROPE_AXIS // 2, 1)
    return y * cos + jnp.where(first, up, dn) * sin


def _qkprep_kernel(aq_ref, akv_ref, cos_ref, sin_ref, qn_ref, kn_ref, gq_ref, gk_ref,
                   eq_ref, ek_ref, qt_ref, k_ref, vt_ref, *, tm, tk):
    cos = cos_ref[...]
    sin = sin_ref[...]
    aq = aq_ref[...].astype(F32)
    ssq = _dot((aq * aq).astype(BF16), gq_ref[...])
    yq = aq * lax.rsqrt(ssq * (1.0 / HEAD_DIM) + EPS) * jnp.concatenate([qn_ref[...]] * 4, axis=1)
    yq = _rope(yq, jnp.concatenate([cos] * 4, axis=1), jnp.concatenate([sin] * 4, axis=1), ATTN_W)
    yq = (yq * Q_SCALE).astype(BF16)
    qt = _dot_nt(eq_ref[...], yq).astype(BF16)
    zeros = jnp.zeros((HEAD_DIM, ATT_TQ), BF16)
    for g in range(N_KV):
        for qb in range(tm // ATT_TQ):
            for r in range(Q_PER_KV):
                h = g * Q_PER_KV + r
                cols = slice((qb * Q_PER_KV + r) * ATT_TQ, (qb * Q_PER_KV + r + 1) * ATT_TQ)
                qt_ref[g, g * HEAD_DIM:(g + 1) * HEAD_DIM, cols] = qt[h * HEAD_DIM:(h + 1) * HEAD_DIM,
                                                                      qb * ATT_TQ:(qb + 1) * ATT_TQ]
                qt_ref[g, (1 - g) * HEAD_DIM:(2 - g) * HEAD_DIM, cols] = zeros
    akv = akv_ref[...].astype(F32)
    ak = akv[:, 0:KV_W]
    ssk = _dot((ak * ak).astype(BF16), gk_ref[...])
    yk = ak * lax.rsqrt(ssk * (1.0 / HEAD_DIM) + EPS) * kn_ref[...]
    yk = _rope(yk, cos, sin, KV_W).astype(BF16)
    nkb = tm // tk
    k_ref[...] = yk.reshape(nkb, tk, KV_W)
    av = akv_ref[:, KV_W:2 * KV_W]
    vt = _dot_nt(ek_ref[...], av).astype(BF16)
    for g in range(N_KV):
        for n in range(nkb):
            vt_ref[g, n] = vt[g * HEAD_DIM:(g + 1) * HEAD_DIM, n * tk:(n + 1) * tk]


def _qkprep(proj, cos_t, sin_t, qn, kn, consts, tm, tk):
    B, T, _ = proj.shape
    nkb_t = T // tk
    nkb = tm // tk
    full = lambda shape: pl.BlockSpec(shape, lambda b, i: (0,) * len(shape))
    return pl.pallas_call(
        functools.partial(_qkprep_kernel, tm=tm, tk=tk),
        grid=(B, T // tm),
        in_specs=[pl.BlockSpec((None, tm, ATTN_W), lambda b, i: (b, i, COL_AQ // ATTN_W)),
                  pl.BlockSpec((None, tm, 2 * KV_W), lambda b, i: (b, i, COL_AKV // (2 * KV_W))),
                  pl.BlockSpec((tm, LANES), lambda b, i: (i, 0)),
                  pl.BlockSpec((tm, LANES), lambda b, i: (i, 0)),
                  full((1, LANES)), full((1, LANES)),
                  full((ATTN_W, ATTN_W)), full((KV_W, KV_W)),
                  full((ATTN_W, ATTN_W)), full((KV_W, KV_W))],
        out_specs=[pl.BlockSpec((None, N_KV, KV_W, Q_PER_KV * tm), lambda b, i: (b, 0, 0, i)),
                   pl.BlockSpec((None, nkb, tk, KV_W), lambda b, i: (b, i, 0, 0)),
                   pl.BlockSpec((None, N_KV, nkb, HEAD_DIM, tk), lambda b, i: (b, 0, i, 0, 0))],
        out_shape=[jax.ShapeDtypeStruct((B, N_KV, KV_W, Q_PER_KV * T), BF16),
                   jax.ShapeDtypeStruct((B, nkb_t, tk, KV_W), BF16),
                   jax.ShapeDtypeStruct((B, N_KV, nkb_t, HEAD_DIM, tk), BF16)],
        compiler_params=_cparams(("arbitrary", "arbitrary"), 40),
        name="qkv_prep",
    )(proj, proj, cos_t, sin_t, qn, kn, consts["gq"], consts["gk"], consts["eq"], consts["ek"])


def _attn_kernel(bounded_ref, *refs, nsb, n_tiles, rows):
    q_refs, (k_ref, vt_ref), ag_refs, eye_ref = refs[0:nsb], refs[nsb:nsb + 2], refs[nsb + 2:2 * nsb + 2], refs[2 * nsb + 2]
    o_ref, p_ref, acc_ref, oall_ref = refs[-4:]
    nq = q_refs[0].shape[-1]
    tq = nq // Q_PER_KV
    sl = acc_ref.shape[1] - HEAD_DIM
    U = p_ref.shape[0]
    acc_ref[...] = jnp.zeros_like(acc_ref)

    def weights(sb, j, slot):
        p = jnp.exp2(_dot(k_ref[j, 0:rows, :], q_refs[sb][...]))
        p_ref[slot, 0:rows, :] = p.astype(BF16)
        return jnp.sum(p.reshape(rows // sl, sl, nq), axis=0)

    def weighted_values(j, slot):
        return _dot(vt_ref[j, :, 0:rows], p_ref[slot, 0:rows, :])

    def run(sb):
        first = sb * n_tiles
        den = weights(sb, 0, first % U)

        def group(jj, den):
            j = U * jj
            part = None
            for u in range(U):
                den = den + weights(sb, j + u + 1, (first + u + 1) % U)
                wv = weighted_values(j + u, (first + u) % U)
                part = wv if part is None else part + wv
            acc_ref[sb, 0:HEAD_DIM, :] += part
            return den

        groups = (n_tiles - 1) // U
        den = lax.fori_loop(0, groups, group, den)
        for t in range(U * groups, n_tiles):
            if t + 1 < n_tiles:
                den = den + weights(sb, t + 1, (first + t + 1) % U)
            acc_ref[sb, 0:HEAD_DIM, :] += weighted_values(t, (first + t) % U)
        acc_ref[sb, HEAD_DIM:HEAD_DIM + sl, :] = den

    def run_online(sb):
        for r in range(Q_PER_KV):
            q = q_refs[sb][:, r * tq:(r + 1) * tq]

            def body(j, carry):
                m, den, acc = carry
                s = _dot(k_ref[j, 0:rows, :], q)
                m_new = jnp.maximum(m, jnp.max(s, axis=0, keepdims=True))
                p = jnp.exp2(s - m_new)
                alpha = jnp.exp2(m - m_new)
                return (m_new, alpha * den + jnp.sum(p, axis=0, keepdims=True),
                        alpha * acc + _dot(vt_ref[j, :, 0:rows], p.astype(BF16)))

            init = (jnp.full((1, tq), -jnp.inf, F32), jnp.zeros((1, tq), F32), jnp.zeros((HEAD_DIM, tq), F32))
            _, den, acc = lax.fori_loop(0, n_tiles, body, init)
            acc_ref[sb, 0:HEAD_DIM, r * tq:(r + 1) * tq] = acc
            acc_ref[sb, HEAD_DIM:HEAD_DIM + sl, r * tq:(r + 1) * tq] = jnp.broadcast_to(den * (1.0 / sl), (sl, tq))

    def finish(sb):
        for r in range(Q_PER_KV):
            acc = acc_ref[sb, :, r * tq:(r + 1) * tq]
            den = jnp.sum(acc[HEAD_DIM:HEAD_DIM + sl, :], axis=0, keepdims=True)
            o = acc[0:HEAD_DIM, :] * (1.0 / den)
            oall_ref[sb, r * HEAD_DIM:(r + 1) * HEAD_DIM, :] = o.astype(BF16)
        o_t = _dot_nt(eye_ref[...], oall_ref[sb])
        gate = _silu_from_half(ag_refs[sb][...].astype(F32))
        o_ref[sb * tq:(sb + 1) * tq, :] = (o_t * gate).astype(BF16)

    bounded = bounded_ref[0] == 1

    @pl.when(bounded)
    def _():
        for sb in range(nsb):
            run(sb)
            finish(sb)

    @pl.when(jnp.logical_not(bounded))
    def _():
        for sb in range(nsb):
            run_online(sb)
            finish(sb)


def _attention(bounded, qt, k, vt, proj, eye_tq, q_block0, n_qblocks, nsb, n_tiles, rows):
    B = proj.shape[0]
    nkb, tk = k.shape[1], k.shape[2]
    gw = Q_PER_KV * HEAD_DIM
    nq = Q_PER_KV * ATT_TQ
    assert n_qblocks % nsb == 0

    def qmap(sb):
        return lambda b, g, i, f: (b, g, 0, q_block0 + i * nsb + sb)

    def agmap(sb):
        return lambda b, g, i, f: (b, q_block0 + i * nsb + sb, COL_AG // gw + g)

    in_specs = ([pl.BlockSpec((None, None, KV_W, nq), qmap(sb)) for sb in range(nsb)]
                + [pl.BlockSpec((None, nkb, tk, KV_W), lambda b, g, i, f: (b, 0, 0, 0)),
                   pl.BlockSpec((None, None, nkb, HEAD_DIM, tk), lambda b, g, i, f: (b, g, 0, 0, 0))]
                + [pl.BlockSpec((None, ATT_TQ, gw), agmap(sb)) for sb in range(nsb)]
                + [pl.BlockSpec((ATT_TQ, ATT_TQ), lambda b, g, i, f: (0, 0))])
    args = [bounded] + [qt] * nsb + [k, vt] + [proj] * nsb + [eye_tq]
    grid_spec = pltpu.PrefetchScalarGridSpec(
        num_scalar_prefetch=1,
        grid=(B, N_KV, n_qblocks // nsb),
        in_specs=in_specs,
        out_specs=pl.BlockSpec((None, nsb * ATT_TQ, gw), lambda b, g, i, f: (b, i, g)),
        scratch_shapes=[pltpu.VMEM((ATT_UNROLL, tk, nq), BF16),
                        pltpu.VMEM((nsb, HEAD_DIM + SUBLANES, nq), F32),
                        pltpu.VMEM((nsb, gw, ATT_TQ), BF16)])
    return pl.pallas_call(
        functools.partial(_attn_kernel, nsb=nsb, n_tiles=n_tiles, rows=rows),
        grid_spec=grid_spec,
        out_shape=jax.ShapeDtypeStruct((B, n_qblocks * ATT_TQ, ATTN_W), BF16),
        compiler_params=_cparams(("arbitrary", "arbitrary", "arbitrary"), 40),
        name="gqa_attention",
    )(*args)


def _gla_direction(qk, v, pa, a2, ab, st_ref, o_ref, fwd):
    C, c = GLA_C, GLA_SUB
    q = qk[:, 0:GLA_KW].astype(F32) * (GLA_DK ** -0.5)
    k = qk[:, GLA_KW:2 * GLA_KW].astype(F32)
    z = _dot(pa, a2) + ab
    yield
    g =(jnp.minimum(z, 0.0) - jnp.log(1.0 + jnp.exp(-jnp.abs(z)))) * (1.0 / GLA_TEMP)
    row = lax.broadcasted_iota(jnp.int32, (C, C), 0)
    col = lax.broadcasted_iota(jnp.int32, (C, C), 1)
    blk = jnp.bitwise_and(row, -c)
    if fwd:
        tri = col <= row
        inblk = tri & (col >= blk)
    else:
        tri = col >= row
        inblk = tri & (col < blk + c)
    sel = jnp.concatenate([jnp.where(tri, 1.0, 0.0), jnp.where(inblk, 1.0, 0.0)], axis=0).astype(BF16)
    g_hi = g.astype(BF16)
    g_lo = (g - g_hi.astype(F32)).astype(BF16)
    cs = _dot(sel, jnp.concatenate([g_hi, g_lo], axis=1))
    yield
    cs =cs[:, 0:GLA_KW] + cs[:, GLA_KW:2 * GLA_KW]
    b = cs[0:C]
    d = cs[C:2 * C]
    ref = b - d
    total = b[C - 1:C, :] if fwd else b[0:1, :]
    q_in = (q * jnp.exp(d)).astype(BF16)
    q_st = (q * jnp.exp(b)).astype(BF16)
    k_out = (k * jnp.exp(total - b)).astype(BF16)
    w_tot = jnp.exp(total)
    zero = jnp.zeros((), BF16)
    nb = C // c
    k_blk = []
    for i in range(nb):
        lo, hi = (0, (i + 1) * c) if fwd else (i * c, C)
        kb = (k[lo:hi] * jnp.exp(ref[i * c:i * c + 1, :] - b[lo:hi])).astype(BF16)
        pad = jnp.zeros((C - (hi - lo), GLA_KW), BF16)
        k_blk.append(jnp.concatenate(([kb, pad] if fwd else [pad, kb]) if hi - lo < C else [kb], axis=0))
    head_of_lane = jnp.right_shift(lax.broadcasted_iota(jnp.int32, (1, GLA_KW), 1), GLA_DK.bit_length() - 1)
    a_two = []
    for m in range(nb // 2):
        qr = q_in[2 * m * c:(2 * m + 2) * c]
        q_heads = jnp.concatenate([jnp.where(head_of_lane == h, qr, zero) for h in range(GLA_HEADS)], axis=0)
        a_two.append(_dot_nt(q_heads, jnp.concatenate(k_blk[2 * m:2 * m + 2], axis=0)))
    yield

    def a_block(h, i):
        m, r = divmod(i, 2)
        return a_two[m][h * 2 * c + r * c:h * 2 * c + (r + 1) * c, r * C:(r + 1) * C]

    a_cat = jnp.concatenate(
        [jnp.where(tri, jnp.concatenate([a_block(h, i) for i in range(nb)], axis=0), 0.0)
         for h in range(GLA_HEADS)], axis=1).astype(BF16)
    vhead_of_lane = jnp.right_shift(lax.broadcasted_iota(jnp.int32, (1, GLA_VW), 1), GLA_DV.bit_length() - 1)
    v_bd = jnp.concatenate([jnp.where(vhead_of_lane == h, v, zero) for h in range(GLA_HEADS)], axis=0)
    st = st_ref[...]
    o = _dot(a_cat, v_bd) + _dot_nt(q_st, st.astype(BF16))
    upd = lax.dot_general(v, k_out, (((0,), (0,)), ((), ())), preferred_element_type=F32)
    same_head = jnp.right_shift(lax.broadcasted_iota(jnp.int32, (GLA_VW, 1), 0),
                                GLA_DV.bit_length() - 1) == head_of_lane
    st_ref[...] = jnp.where(same_head, st * w_tot + upd, 0.0)
    o_ref[...] = o
    yield


def _gla_kernel(qkf_ref, vf_ref, paf_ref, qkb_ref, vb_ref, pab_ref, a2f_ref, abf_ref, a2b_ref, abb_ref,
                of_ref, ob_ref, stf_ref, stb_ref):
    @pl.when(pl.program_id(0) == 0)
    def _():
        stf_ref[...] = jnp.zeros_like(stf_ref)
        stb_ref[...] = jnp.zeros_like(stb_ref)

    C = GLA_C
    waves = []
    for t in range(GLA_STEPS):
        fs = slice(t * C, (t + 1) * C)
        bs = slice((GLA_STEPS - 1 - t) * C, (GLA_STEPS - t) * C)
        chains = []
        for bi in range(qkf_ref.shape[0]):
            chains.append(_gla_direction(qkf_ref[bi, fs], vf_ref[bi, fs], paf_ref[bi, fs], a2f_ref[...], abf_ref[...],
                                         stf_ref.at[bi], of_ref.at[bi, fs], True))
            chains.append(_gla_direction(qkb_ref[bi, bs], vb_ref[bi, bs], pab_ref[bi, bs], a2b_ref[...], abb_ref[...],
                                         stb_ref.at[bi], ob_ref.at[bi, bs], False))
        waves.append(chains)
    for _ in range(3):
        for chains in waves:
            for chain in chains:
                next(chain)
    for chains in waves:
        for chain in chains:
            next(chain)


def _gla(proj, a2f, abf, a2b, abb, ctx_len):
    B, T = proj.shape[0], proj.shape[1]
    C = GLA_C * GLA_STEPS
    assert T % C == 0 and ctx_len % C == 0
    nch, nctx = T // C, ctx_len // C

    def cb(s):
        return jnp.where(s < nctx, nctx - 1 - s, nch + nctx - 1 - s)

    full = lambda shape: pl.BlockSpec(shape, lambda s: (0,) * len(shape))
    return pl.pallas_call(
        _gla_kernel,
        grid=(nch,),
        in_specs=[pl.BlockSpec((B, C, 2 * GLA_KW), lambda s: (0, s, COL_GQK // (2 * GLA_KW))),
                  pl.BlockSpec((B, C, GLA_VW), lambda s: (0, s, COL_GV // GLA_VW)),
                  pl.BlockSpec((B, C, LANES), lambda s: (0, s, COL_GATE // LANES)),
                  pl.BlockSpec((B, C, 2 * GLA_KW), lambda s: (0, cb(s), COL_GQK // (2 * GLA_KW))),
                  pl.BlockSpec((B, C, GLA_VW), lambda s: (0, cb(s), COL_GV // GLA_VW)),
                  pl.BlockSpec((B, C, LANES), lambda s: (0, cb(s), COL_GATE // LANES)),
                  full((LANES, GLA_KW)), full((1, GLA_KW)), full((LANES, GLA_KW)), full((1, GLA_KW))],
        out_specs=[pl.BlockSpec((B, C, GLA_VW), lambda s: (0, s, 0)),
                   pl.BlockSpec((B, C, GLA_VW), lambda s: (0, cb(s), 0))],
        out_shape=[jax.ShapeDtypeStruct((B, T, GLA_VW), F32),
                   jax.ShapeDtypeStruct((B, T, GLA_VW), F32)],
        scratch_shapes=[pltpu.VMEM((B, GLA_VW, GLA_KW), F32),
                        pltpu.VMEM((B, GLA_VW, GLA_KW), F32)],
        compiler_params=_cparams(("arbitrary",), 32),
        name="gla_scan",
    )(proj, proj, proj, proj, proj, proj, a2f, abf, a2b, abb)


def _merge_kernel(xc_ref, x_ref, mod_ref, ma_ref, mg_ref, ms_ref, su_ref, sv_ref, sg_ref, gg_ref, yac_ref, yal_ref,
                  of_ref, ob_ref,
                  onorm_ref, lng_ref, lnb_ref, sw_ref, sb_ref, wa_ref, wg_ref, ws_ref, wo_ref, pg_ref, pb_ref,
                  o_ref, *, tm, ctx_tiles, row_off, alpha):
    D = x_ref.shape[-1]
    i = pl.program_id(1) + row_off
    og = of_ref[...] + ob_ref[...]
    parts = []
    for h in range(GLA_HEADS):
        oh = og[:, h * GLA_DV:(h + 1) * GLA_DV]
        ms = jnp.mean(oh * oh, axis=-1, keepdims=True)
        parts.append(oh * lax.rsqrt(ms + EPS))
    yg = jnp.concatenate(parts, axis=1) * onorm_ref[...] * _silu_from_half(gg_ref[...]).astype(F32)
    sv = sv_ref[...].astype(F32)
    mu = jnp.mean(sv, axis=-1, keepdims=True)
    dv = sv - mu
    var = jnp.mean(dv * dv, axis=-1, keepdims=True)
    vn = (dv * lax.rsqrt(var + EPS) * lng_ref[...] + lnb_ref[...]).astype(BF16)
    gw = SGU_W // SGU_GROUPS
    rows = []
    for n in range(tm // SGU_CHUNK):
        cols = []
        for g in range(SGU_GROUPS):
            blk = vn[n * SGU_CHUNK:(n + 1) * SGU_CHUNK, g * gw:(g + 1) * gw]
            cols.append(_dot(sw_ref[g], blk))
        rows.append(jnp.concatenate(cols, axis=1) + sb_ref[...])
    mixed = jnp.concatenate(rows, axis=0)
    ys = su_ref[...].astype(F32) * mixed * _silu_from_half(sg_ref[...]).astype(F32)

    def gated(g_ref, y, w_ref):
        half = _dot(y, w_ref[...]).astype(BF16)
        return half + half * jnp.tanh(g_ref[...])

    ya = jnp.where(i < ctx_tiles, yac_ref[...], yal_ref[...])
    m = (gated(ma_ref, ya, wa_ref) + gated(mg_ref, yg.astype(BF16), wg_ref)
         + gated(ms_ref, ys.astype(BF16), ws_ref))
    out = _dot(m, wo_ref[...])
    gate = jnp.where(i < ctx_tiles, mod_ref[1:2, 2 * D:3 * D], mod_ref[0:1, 2 * D:3 * D])
    r = alpha * jnp.where(i < ctx_tiles, xc_ref[...], x_ref[...]) + gate * out
    mu = jnp.mean(r, axis=-1, keepdims=True)
    dr = r - mu
    var = jnp.mean(dr * dr, axis=-1, keepdims=True)
    o_ref[...] = dr * lax.rsqrt(var + EPS) * pg_ref[...] + pb_ref[...]


def _merge(x_ctx, x_lat, lat_tile0, modb, proj, ya_ctx, ya_lat, of, ob, p, ctx_len, tm, skip_ctx, alpha):
    B, T, D = proj.shape[0], proj.shape[1], x_lat.shape[-1]
    ctx_tiles = ctx_len // tm
    row_off = ctx_tiles if skip_ctx else 0
    nt = T // tm - row_off
    rows = lambda w, c: pl.BlockSpec((None, tm, w), lambda b, i: (b, i + row_off, c))
    ctx_rows = lambda w: pl.BlockSpec((None, tm, w), lambda b, i: (b, jnp.minimum(i + row_off, ctx_tiles - 1), 0))
    lat_rows = lambda w, t0: pl.BlockSpec(
        (None, tm, w), lambda b, i: (b, jnp.maximum(i + row_off - ctx_tiles, 0) + t0, 0))
    ya_ctx_spec, ya_lat_spec = ctx_rows(ATTN_W), lat_rows(ATTN_W, 0)
    full = lambda shape: pl.BlockSpec(shape, lambda b, i: (0,) * len(shape))
    return pl.pallas_call(
        functools.partial(_merge_kernel, tm=tm, ctx_tiles=ctx_tiles, row_off=row_off, alpha=alpha),
        grid=(B, nt),
        in_specs=[ctx_rows(D), lat_rows(D, lat_tile0),
                  pl.BlockSpec((None, 8, 3 * D), lambda b, i: (b, 0, 0)),
                  rows(D, COL_MA // D), rows(D, COL_MG // D), rows(D, COL_MS // D),
                  rows(SGU_W, COL_SU // SGU_W), rows(SGU_W, COL_SV // SGU_W), rows(SGU_W, COL_SG // SGU_W),
                  rows(GLA_VW, COL_GG // GLA_VW),
                  ya_ctx_spec, ya_lat_spec, rows(GLA_VW, 0), rows(GLA_VW, 0),
                  full((1, GLA_VW)), full((1, SGU_W)), full((1, SGU_W)),
                  full((SGU_GROUPS, SGU_CHUNK, SGU_CHUNK)), full((SGU_CHUNK, SGU_W)),
                  full((ATTN_W, D)), full((GLA_VW, D)), full((SGU_W, D)), full((D, D)),
                  full((1, D)), full((1, D))],
        out_specs=pl.BlockSpec((None, tm, D), lambda b, i: (b, i, 0)),
        out_shape=jax.ShapeDtypeStruct((B, nt * tm, D), F32),
        compiler_params=_cparams(("arbitrary", "arbitrary"), 48),
        name="merge_out",
    )(x_ctx, x_lat, modb, proj, proj, proj, proj, proj, proj, proj, ya_ctx, ya_lat, of, ob,
      p["onorm"], p["lng"], p["lnb"], p["sw"], p["sb"], p["wa"], p["wg"], p["ws"], p["wo"], p["pg"], p["pb"])


W_BLK = 256
GATE_SHIFT = 2 * GLA_RANK


def _w_in_plan():
    sizes = (512, 128, 128, 512, 256, 256, 512, 16, 16, 512, 512, 512, 512, 1024, 1024, 1024)
    names = ("aq", "ak", "av", "ag", "gq", "gk", "gv", "af", "ab", "gg", "su", "sv", "sg", "ma", "mg", "ms")
    halved = ("ag", "gg", "sg", "ma", "mg", "ms")
    src, off = {}, 0
    for n, s in zip(names, sizes):
        src[n] = (off, s)
        off += s
    plan = []
    for n in ("ma", "mg", "ms", "aq", "ag", "gv", "gg", "su", "sv", "sg", "gq", "gk"):
        start, size = src[n]
        shift = start % W_BLK
        assert shift in (0, GATE_SHIFT) and size % W_BLK == 0
        for t in range(size // W_BLK):
            plan.append(((start - shift) // W_BLK + t, 1 if shift else 0, int(n in halved)))
    assert src["ak"][0] % W_BLK == 0 and src["av"][0] == src["ak"][0] + KV_W and 2 * KV_W == W_BLK
    plan.append((src["ak"][0] // W_BLK, 0, 0))
    assert src["af"][0] % W_BLK == 0 and src["ab"][0] == src["af"][0] + GLA_RANK
    plan.append((src["af"][0] // W_BLK, 2, 0))
    assert len(plan) * W_BLK == PROJ_W
    return plan, off


def _wprep_kernel(src_ref, nxt_ref, kind_ref, half_ref, a_ref, b_ref, o_ref):
    j = pl.program_id(1)
    a = a_ref[...]
    ab = jnp.concatenate([a, b_ref[...]], axis=0)
    shifted = ab[GATE_SHIFT:GATE_SHIFT + W_BLK, :]
    row = lax.broadcasted_iota(jnp.int32, (W_BLK, 1), 0)
    gates = jnp.where(row < GATE_SHIFT, a, 0.0)
    kind = kind_ref[j]
    val = jnp.where(kind == 1, shifted, jnp.where(kind == 2, gates, a))
    o_ref[...] = (val * jnp.where(half_ref[j] == 1, 0.5, 1.0)).astype(BF16)


def _regroup_w_in(w_in):
    L, D, width = w_in.shape
    plan, used = _w_in_plan()
    assert used == width
    last_blk = (width - 1) // W_BLK
    src = jnp.array([p[0] for p in plan], jnp.int32)
    kind = jnp.array([p[1] for p in plan], jnp.int32)
    half = jnp.array([p[2] for p in plan], jnp.int32)
    nxt, cur = [], min(plan[0][0] + 1, last_blk)
    for s, k, _ in plan:
        cur = min(s + 1, last_blk) if k == 1 else cur
        nxt.append(cur)
    nxt = jnp.array(nxt, jnp.int32)
    wt = jnp.swapaxes(w_in, 1, 2)
    grid_spec = pltpu.PrefetchScalarGridSpec(
        num_scalar_prefetch=4,
        grid=(L, len(plan)),
        in_specs=[pl.BlockSpec((None, W_BLK, D), lambda l, j, s, n, k, h: (l, s[j], 0)),
                  pl.BlockSpec((None, W_BLK, D), lambda l, j, s, n, k, h: (l, n[j], 0))],
        out_specs=pl.BlockSpec((None, W_BLK, D), lambda l, j, s, n, k, h: (l, j, 0)))
    return pl.pallas_call(
        _wprep_kernel,
        grid_spec=grid_spec,
        out_shape=jax.ShapeDtypeStruct((L, PROJ_W, D), BF16),
        compiler_params=_cparams(("arbitrary", "arbitrary"), 32),
        name="w_in_prep",
    )(src, nxt, kind, half, wt, wt)


def _rope_tables(n_lat, ctx_len):
    rows = n_lat // GRID_W
    row = np.repeat(np.arange(rows, dtype=np.float64), GRID_W)
    col = np.tile(np.arange(GRID_W, dtype=np.float64), rows)
    freqs = ROPE_THETA ** (-np.arange(0, ROPE_AXIS, 2, dtype=np.float64) / ROPE_AXIS)
    ang_r, ang_c = row[:, None] * freqs, col[:, None] * freqs
    cos = np.concatenate([np.cos(ang_r)] * 2 + [np.cos(ang_c)] * 2, axis=1)
    sin = np.concatenate([-np.sin(ang_r), np.sin(ang_r), -np.sin(ang_c), np.sin(ang_c)], axis=1)
    cos = np.concatenate([np.ones((ctx_len, HEAD_DIM)), cos], axis=0)
    sin = np.concatenate([np.zeros((ctx_len, HEAD_DIM)), sin], axis=0)
    return (jnp.asarray(np.concatenate([cos, cos], axis=1), F32),
            jnp.asarray(np.concatenate([sin, sin], axis=1), F32))


def _largest_tile(total, cap, mult):
    best = mult
    for t in range(mult, cap + 1, mult):
        if total % t == 0:
            best = t
    return best


def kernel(x, c, ctx, c_ctx, w_ada, b_ada, w_in, attn_q_norm, attn_k_norm, gla_a2_f, gla_ab_f, gla_a2_b, gla_ab_b, gla_o_norm, sgu_ln_g, sgu_ln_b, sgu_w, sgu_b, w_br_attn, w_br_gla, w_br_sgu, w_out, post_ln_g, post_ln_b):
    B, n_lat, D = x.shape
    ctx_len = ctx.shape[1]
    depth = w_in.shape[0]
    T = ctx_len + n_lat
    assert D == D_MODEL and B <= 7
    assert ctx_len % ATT_TQ == 0 and n_lat % ATT_TQ == 0 and n_lat % GRID_W == 0
    alpha = (2 * depth) ** 0.25

    tm_proj = _largest_tile(T, 1056, 16)
    tn_proj = 1536
    tk_att = _largest_tile(T, ATT_TK_MAX, ATT_TQ)
    assert ctx_len <= tk_att
    tm_prep = tk_att
    tm_merge = ATT_TQ

    xall = None
    cc = jnp.zeros((8, D), F32).at[0:B].set(c).at[B].set(c_ctx)
    mod = _ada(cc, w_ada, b_ada.reshape(depth, 1, 3 * D))
    cos_t, sin_t = _rope_tables(n_lat, ctx_len)
    consts = {
        "gq": jnp.kron(jnp.eye(N_HEADS, dtype=F32), jnp.ones((HEAD_DIM, HEAD_DIM), F32)).astype(BF16),
        "gk": jnp.kron(jnp.eye(N_KV, dtype=F32), jnp.ones((HEAD_DIM, HEAD_DIM), F32)).astype(BF16),
        "eq": jnp.eye(ATTN_W, dtype=BF16),
        "ek": jnp.eye(KV_W, dtype=BF16),
    }
    eye_tq = jnp.eye(ATT_TQ, dtype=BF16)
    w_all = _regroup_w_in(w_in)

    for l in range(depth):
        last = l == depth - 1
        modb = jnp.zeros((B, 8, 3 * D), F32).at[:, 0].set(mod[l, 0:B]).at[:, 1].set(mod[l, B][None])
        first = xall is None
        proj = _inproj(x if first else xall, ctx if first else xall, ctx_len, modb, w_all, l, tm_proj, tn_proj,
                       split=first)
        qn = jnp.concatenate([attn_q_norm[l]] * 2)[None]
        kn = jnp.concatenate([attn_k_norm[l]] * 2)[None]
        qt, kk, vt = _qkprep(proj, cos_t, sin_t, qn, kn, consts, tm_prep, tk_att)
        score_bound = HEAD_DIM * jnp.max(jnp.abs(attn_q_norm[l])) * jnp.max(jnp.abs(attn_k_norm[l])) * Q_SCALE
        bounded = (score_bound <= SCORE_LIMIT).astype(jnp.int32).reshape(1)
        ctx_blocks, lat_blocks = ctx_len // ATT_TQ, n_lat // ATT_TQ
        ya_lat = _attention(bounded, qt, kk, vt, proj, eye_tq, ctx_blocks, lat_blocks, ATT_NSB, T // tk_att, tk_att)
        ya_ctx = ya_lat if last else _attention(bounded, qt, kk, vt, proj, eye_tq, 0, ctx_blocks, 1, 1, ctx_len)
        a2f = jnp.zeros((LANES, GLA_KW), F32).at[0:GLA_RANK].set(gla_a2_f[l]).astype(BF16)
        a2b = jnp.zeros((LANES, GLA_KW), F32).at[GLA_RANK:2 * GLA_RANK].set(gla_a2_b[l]).astype(BF16)
        of, ob = _gla(proj, a2f, gla_ab_f[l][None], a2b, gla_ab_b[l][None], ctx_len)
        p = {
            "onorm": jnp.concatenate([gla_o_norm[l]] * GLA_HEADS)[None],
            "lng": sgu_ln_g[l][None], "lnb": sgu_ln_b[l][None],
            "sw": sgu_w[l].astype(BF16),
            "sb": jnp.repeat(sgu_b[l].T, SGU_W // SGU_GROUPS, axis=1),
            "wa": (0.5 * w_br_attn[l]).astype(BF16), "wg": (0.5 * w_br_gla[l]).astype(BF16),
            "ws": (0.5 * w_br_sgu[l]).astype(BF16),
            "wo": w_out[l].astype(BF16), "pg": post_ln_g[l][None], "pb": post_ln_b[l][None],
        }
        xall = _merge(ctx if first else xall, x if first else xall, 0 if first else ctx_len // tm_merge,
                      modb, proj, ya_ctx, ya_lat, of, ob, p, ctx_len, tm_merge, last, alpha)
    return xall
```

```python
import functools

import jax
import jax.numpy as jnp
import numpy as np
from jax import lax
from jax.experimental import pallas as pl
from jax.experimental.pallas import tpu as pltpu

F32 = jnp.float32
BF16 = jnp.bfloat16

D_MODEL = 1024
GRID_W = 64
HEAD_DIM = 64
N_HEADS = 8
N_KV = 2
Q_PER_KV = N_HEADS // N_KV
ATTN_W = N_HEADS * HEAD_DIM
KV_W = N_KV * HEAD_DIM
ROPE_THETA = 10000.0
ROPE_AXIS = HEAD_DIM // 2
GLA_HEADS = 4
GLA_DK = 64
GLA_DV = 128
GLA_KW = GLA_HEADS * GLA_DK
GLA_VW = GLA_HEADS * GLA_DV
GLA_RANK = 16
GLA_TEMP = 16.0
SGU_GROUPS = 4
SGU_CHUNK = 128
SGU_W = 512
EPS = 1e-6

LANES = 128
V7X_VMEM_BYTES = 64 * 1024 * 1024

COL_MA, COL_MG, COL_MS = 0, 1024, 2048
COL_AQ, COL_AG = 3072, 3584
COL_GV, COL_GG = 4096, 4608
COL_SU, COL_SV, COL_SG = 5120, 5632, 6144
COL_GQK = 6656
COL_AKV = 7168
COL_GATE = 7424
PROJ_W = 7680

SUBLANES = 8
ATT_TQ = 256
ATT_NSB = 2
ATT_TK_MAX = 768
GLA_C = 128
GLA_SUB = 16
GLA_STEPS = 2
ATT_UNROLL = 5
Q_SCALE = 1.4426950408889634 * HEAD_DIM ** -0.5
SCORE_LIMIT = 96.0


def _cparams(sem, vmem_mb):
    return pltpu.CompilerParams(dimension_semantics=sem,
                                vmem_limit_bytes=min(vmem_mb * 1024 * 1024, V7X_VMEM_BYTES - (4 << 20)))


def _silu(x):
    return x * jax.nn.sigmoid(x)


def _silu_from_half(u):
    return u + u * jnp.tanh(u)


def _dot(a, b):
    return jnp.dot(a, b, preferred_element_type=F32)


def _dot_nt(a, b):
    return lax.dot_general(a, b, (((1,), (1,)), ((), ())), preferred_element_type=F32)


def _ada_kernel(c_ref, w_ref, b_ref, o_ref):
    c = c_ref[...]
    o_ref[...] = _dot(_silu(c).astype(BF16), w_ref[...].astype(BF16)) + b_ref[...]


def _ada(cc, w_ada, b_ada):
    L, D = w_ada.shape[0], w_ada.shape[1]
    return pl.pallas_call(
        _ada_kernel,
        grid=(L, 3),
        in_specs=[pl.BlockSpec((8, D), lambda l, j: (0, 0)),
                  pl.BlockSpec((None, D, D), lambda l, j: (l, 0, j)),
                  pl.BlockSpec((None, 1, D), lambda l, j: (l, 0, j))],
        out_specs=pl.BlockSpec((None, 8, D), lambda l, j: (l, 0, j)),
        out_shape=jax.ShapeDtypeStruct((L, 8, 3 * D), F32),
        compiler_params=_cparams(("arbitrary", "arbitrary"), 32),
        name="ada_mod",
    )(cc, w_ada, b_ada)


def _inproj_kernel(x_ref, ctx_ref, mod_ref, w_ref, o_ref, h_ref, *, tm, ctx_len, split):
    i = pl.program_id(1)
    j = pl.program_id(2)
    D = x_ref.shape[-1]

    def modulated(v, r):
        return (v * (1.0 + mod_ref[r:r + 1, D:2 * D]) + mod_ref[r:r + 1, 0:D]).astype(BF16)

    if split:
        @pl.when((j == 0) & (i == 0))
        def _():
            h_ref[0:ctx_len, :] = modulated(ctx_ref[...], 1)
            h_ref[ctx_len:tm, :] = modulated(x_ref[0, 0:tm - ctx_len, :], 0)

        @pl.when((j == 0) & (i > 0))
        def _():
            h_ref[...] = modulated(x_ref[0], 0)
    else:
        @pl.when(j == 0)
        def _():
            row = i * tm + lax.broadcasted_iota(jnp.int32, (tm, 1), 0)
            is_ctx = row < ctx_len
            shift = jnp.where(is_ctx, mod_ref[1:2, 0:D], mod_ref[0:1, 0:D])
            scale = jnp.where(is_ctx, mod_ref[1:2, D:2 * D], mod_ref[0:1, D:2 * D])
            h_ref[...] = (x_ref[0] * (1.0 + scale) + shift).astype(BF16)

    o_ref[...] = _dot_nt(h_ref[...], w_ref[...]).astype(BF16)


def _inproj(x, ctx, ctx_len, modb, w_all, layer, tm, tn, split):
    B, _, D = x.shape
    T = x.shape[1] + ctx_len if split else x.shape[1]
    assert ctx_len < tm
    if split:
        x_spec = pl.BlockSpec((pl.Element(1), pl.Element(tm), pl.Element(D)),
                              lambda b, i, j: (b, pl.multiple_of(jnp.maximum(i * tm - ctx_len, 0), SUBLANES), 0))
    else:
        x_spec = pl.BlockSpec((1, tm, D), lambda b, i, j: (b, i, 0))
    return pl.pallas_call(
        functools.partial(_inproj_kernel, tm=tm, ctx_len=ctx_len, split=split),
        grid=(B, T // tm, PROJ_W // tn),
        in_specs=[x_spec,
                  pl.BlockSpec((None, ctx_len, D), lambda b, i, j: (b, 0, 0)),
                  pl.BlockSpec((None, 8, 3 * D), lambda b, i, j: (b, 0, 0)),
                  pl.BlockSpec((None, tn, D), lambda b, i, j: (layer, j, 0))],
        out_specs=pl.BlockSpec((None, tm, tn), lambda b, i, j: (b, i, j)),
        out_shape=jax.ShapeDtypeStruct((B, T, PROJ_W), BF16),
        scratch_shapes=[pltpu.VMEM((tm, D), BF16)],
        compiler_params=_cparams(("arbitrary", "arbitrary", "arbitrary"), 52),
        name="in_proj",
    )(x, ctx, modb, w_all)


def _rope(y, cos, sin, width):
    lane = lax.broadcasted_iota(jnp.int32, (1, width), 1)
    first = jnp.bitwise_and(lane, ROPE_AXIS - 1) < (ROPE_AXIS // 2)
    up = pltpu.roll(y, width - ROPE_AXIS // 2, 1)
    dn = pltpu.roll(y, ROPE_AXIS // 2, 1)
    return y * cos + jnp.where(first, up, dn) * sin


def _qkprep_kernel(aq_ref, akv_ref, cos_ref, sin_ref, qn_ref, kn_ref, gq_ref, gk_ref,
                   eq_ref, ek_ref, qt_ref, k_ref, vt_ref, *, tm, tk):
    cos = cos_ref[...]
    sin = sin_ref[...]
    aq = aq_ref[...].astype(F32)
    ssq = _dot((aq * aq).astype(BF16), gq_ref[...])
    yq = aq * lax.rsqrt(ssq * (1.0 / HEAD_DIM) + EPS) * jnp.concatenate([qn_ref[...]] * 4, axis=1)
    yq = _rope(yq, jnp.concatenate([cos] * 4, axis=1), jnp.concatenate([sin] * 4, axis=1), ATTN_W)
    yq = (yq * Q_SCALE).astype(BF16)
    qt = _dot_nt(eq_ref[...], yq).astype(BF16)
    zeros = jnp.zeros((HEAD_DIM, ATT_TQ), BF16)
    for g in range(N_KV):
        for qb in range(tm // ATT_TQ):
            for r in range(Q_PER_KV):
                h = g * Q_PER_KV + r
                cols = slice((qb * Q_PER_KV + r) * ATT_TQ, (qb * Q_PER_KV + r + 1) * ATT_TQ)
                qt_ref[g, g * HEAD_DIM:(g + 1) * HEAD_DIM, cols] = qt[h * HEAD_DIM:(h + 1) * HEAD_DIM,
                                                                      qb * ATT_TQ:(qb + 1) * ATT_TQ]
                qt_ref[g, (1 - g) * HEAD_DIM:(2 - g) * HEAD_DIM, cols] = zeros
    akv = akv_ref[...].astype(F32)
    ak = akv[:, 0:KV_W]
    ssk = _dot((ak * ak).astype(BF16), gk_ref[...])
    yk = ak * lax.rsqrt(ssk * (1.0 / HEAD_DIM) + EPS) * kn_ref[...]
    yk = _rope(yk, cos, sin, KV_W).astype(BF16)
    nkb = tm // tk
    k_ref[...] = yk.reshape(nkb, tk, KV_W)
    av = akv_ref[:, KV_W:2 * KV_W]
    vt = _dot_nt(ek_ref[...], av).astype(BF16)
    for g in range(N_KV):
        for n in range(nkb):
            vt_ref[g, n] = vt[g * HEAD_DIM:(g + 1) * HEAD_DIM, n * tk:(n + 1) * tk]


def _qkprep(proj, cos_t, sin_t, qn, kn, consts, tm, tk):
    B, T, _ = proj.shape
    nkb_t = T // tk
    nkb = tm // tk
    full = lambda shape: pl.BlockSpec(shape, lambda b, i: (0,) * len(shape))
    return pl.pallas_call(
        functools.partial(_qkprep_kernel, tm=tm, tk=tk),
        grid=(B, T // tm),
        in_specs=[pl.BlockSpec((None, tm, ATTN_W), lambda b, i: (b, i, COL_AQ // ATTN_W)),
                  pl.BlockSpec((None, tm, 2 * KV_W), lambda b, i: (b, i, COL_AKV // (2 * KV_W))),
                  pl.BlockSpec((tm, LANES), lambda b, i: (i, 0)),
                  pl.BlockSpec((tm, LANES), lambda b, i: (i, 0)),
                  full((1, LANES)), full((1, LANES)),
                  full((ATTN_W, ATTN_W)), full((KV_W, KV_W)),
                  full((ATTN_W, ATTN_W)), full((KV_W, KV_W))],
        out_specs=[pl.BlockSpec((None, N_KV, KV_W, Q_PER_KV * tm), lambda b, i: (b, 0, 0, i)),
                   pl.BlockSpec((None, nkb, tk, KV_W), lambda b, i: (b, i, 0, 0)),
                   pl.BlockSpec((None, N_KV, nkb, HEAD_DIM, tk), lambda b, i: (b, 0, i, 0, 0))],
        out_shape=[jax.ShapeDtypeStruct((B, N_KV, KV_W, Q_PER_KV * T), BF16),
                   jax.ShapeDtypeStruct((B, nkb_t, tk, KV_W), BF16),
                   jax.ShapeDtypeStruct((B, N_KV, nkb_t, HEAD_DIM, tk), BF16)],
        compiler_params=_cparams(("arbitrary", "arbitrary"), 40),
        name="qkv_prep",
    )(proj, proj, cos_t, sin_t, qn, kn, consts["gq"], consts["gk"], consts["eq"], consts["ek"])


def _attn_kernel(bounded_ref, *refs, nsb, n_tiles, rows):
    q_refs, (k_ref, vt_ref), ag_refs, eye_ref = refs[0:nsb], refs[nsb:nsb + 2], refs[nsb + 2:2 * nsb + 2], refs[2 * nsb + 2]
    o_ref, p_ref, acc_ref, oall_ref = refs[-4:]
    nq = q_refs[0].shape[-1]
    tq = nq // Q_PER_KV
    sl = acc_ref.shape[1] - HEAD_DIM
    U = p_ref.shape[0]
    acc_ref[...] = jnp.zeros_like(acc_ref)

    def weights(sb, j, slot):
        p = jnp.exp2(_dot(k_ref[j, 0:rows, :], q_refs[sb][...]))
        p_ref[slot, 0:rows, :] = p.astype(BF16)
        return jnp.sum(p.reshape(rows // sl, sl, nq), axis=0)

    def weighted_values(j, slot):
        return _dot(vt_ref[j, :, 0:rows], p_ref[slot, 0:rows, :])

    def run(sb):
        first = sb * n_tiles
        den = weights(sb, 0, first % U)

        def group(jj, den):
            j = U * jj
            part = None
            for u in range(U):
                den = den + weights(sb, j + u + 1, (first + u + 1) % U)
                wv = weighted_values(j + u, (first + u) % U)
                part = wv if part is None else part + wv
            acc_ref[sb, 0:HEAD_DIM, :] += part
            return den

        groups = (n_tiles - 1) // U
        den = lax.fori_loop(0, groups, group, den)
        for t in range(U * groups, n_tiles):
            if t + 1 < n_tiles:
                den = den + weights(sb, t + 1, (first + t + 1) % U)
            acc_ref[sb, 0:HEAD_DIM, :] += weighted_values(t, (first + t) % U)
        acc_ref[sb, HEAD_DIM:HEAD_DIM + sl, :] = den

    def run_online(sb):
        for r in range(Q_PER_KV):
            q = q_refs[sb][:, r * tq:(r + 1) * tq]

            def body(j, carry):
                m, den, acc = carry
                s = _dot(k_ref[j, 0:rows, :], q)
                m_new = jnp.maximum(m, jnp.max(s, axis=0, keepdims=True))
                p = jnp.exp2(s - m_new)
                alpha = jnp.exp2(m - m_new)
                return (m_new, alpha * den + jnp.sum(p, axis=0, keepdims=True),
                        alpha * acc + _dot(vt_ref[j, :, 0:rows], p.astype(BF16)))

            init = (jnp.full((1, tq), -jnp.inf, F32), jnp.zeros((1, tq), F32), jnp.zeros((HEAD_DIM, tq), F32))
            _, den, acc = lax.fori_loop(0, n_tiles, body, init)
            acc_ref[sb, 0:HEAD_DIM, r * tq:(r + 1) * tq] = acc
            acc_ref[sb, HEAD_DIM:HEAD_DIM + sl, r * tq:(r + 1) * tq] = jnp.broadcast_to(den * (1.0 / sl), (sl, tq))

    def finish(sb):
        for r in range(Q_PER_KV):
            acc = acc_ref[sb, :, r * tq:(r + 1) * tq]
            den = jnp.sum(acc[HEAD_DIM:HEAD_DIM + sl, :], axis=0, keepdims=True)
            o = acc[0:HEAD_DIM, :] * (1.0 / den)
            oall_ref[sb, r * HEAD_DIM:(r + 1) * HEAD_DIM, :] = o.astype(BF16)
        o_t = _dot_nt(eye_ref[...], oall_ref[sb])
        gate = _silu_from_half(ag_refs[sb][...].astype(F32))
        o_ref[sb * tq:(sb + 1) * tq, :] = (o_t * gate).astype(BF16)

    bounded = bounded_ref[0] == 1

    @pl.when(bounded)
    def _():
        for sb in range(nsb):
            run(sb)
            finish(sb)

    @pl.when(jnp.logical_not(bounded))
    def _():
        for sb in range(nsb):
            run_online(sb)
            finish(sb)


def _attention(bounded, qt, k, vt, proj, eye_tq, q_block0, n_qblocks, nsb, n_tiles, rows):
    B = proj.shape[0]
    nkb, tk = k.shape[1], k.shape[2]
    gw = Q_PER_KV * HEAD_DIM
    nq = Q_PER_KV * ATT_TQ
    assert n_qblocks % nsb == 0

    def qmap(sb):
        return lambda b, g, i, f: (b, g, 0, q_block0 + i * nsb + sb)

    def agmap(sb):
        return lambda b, g, i, f: (b, q_block0 + i * nsb + sb, COL_AG // gw + g)

    in_specs = ([pl.BlockSpec((None, None, KV_W, nq), qmap(sb)) for sb in range(nsb)]
                + [pl.BlockSpec((None, nkb, tk, KV_W), lambda b, g, i, f: (b, 0, 0, 0)),
                   pl.BlockSpec((None, None, nkb, HEAD_DIM, tk), lambda b, g, i, f: (b, g, 0, 0, 0))]
                + [pl.BlockSpec((None, ATT_TQ, gw), agmap(sb)) for sb in range(nsb)]
                + [pl.BlockSpec((ATT_TQ, ATT_TQ), lambda b, g, i, f: (0, 0))])
    args = [bounded] + [qt] * nsb + [k, vt] + [proj] * nsb + [eye_tq]
    grid_spec = pltpu.PrefetchScalarGridSpec(
        num_scalar_prefetch=1,
        grid=(B, N_KV, n_qblocks // nsb),
        in_specs=in_specs,
        out_specs=pl.BlockSpec((None, nsb * ATT_TQ, gw), lambda b, g, i, f: (b, i, g)),
        scratch_shapes=[pltpu.VMEM((ATT_UNROLL, tk, nq), BF16),
                        pltpu.VMEM((nsb, HEAD_DIM + SUBLANES, nq), F32),
                        pltpu.VMEM((nsb, gw, ATT_TQ), BF16)])
    return pl.pallas_call(
        functools.partial(_attn_kernel, nsb=nsb, n_tiles=n_tiles, rows=rows),
        grid_spec=grid_spec,
        out_shape=jax.ShapeDtypeStruct((B, n_qblocks * ATT_TQ, ATTN_W), BF16),
        compiler_params=_cparams(("arbitrary", "arbitrary", "arbitrary"), 40),
        name="gqa_attention",
    )(*args)


def _gla_direction(qk, v, pa, a2, ab, st_ref, o_ref, fwd):
    C, c = GLA_C, GLA_SUB
    q = qk[:, 0:GLA_KW].astype(F32) * (GLA_DK ** -0.5)
    k = qk[:, GLA_KW:2 * GLA_KW].astype(F32)
    z = _dot(pa, a2) + ab
    yield
    g =(jnp.minimum(z, 0.0) - jnp.log(1.0 + jnp.exp(-jnp.abs(z)))) * (1.0 / GLA_TEMP)
    row = lax.broadcasted_iota(jnp.int32, (C, C), 0)
    col = lax.broadcasted_iota(jnp.int32, (C, C), 1)
    blk = jnp.bitwise_and(row, -c)
    if fwd:
        tri = col <= row
        inblk = tri & (col >= blk)
    else:
        tri = col >= row
        inblk = tri & (col < blk + c)
    sel = jnp.concatenate([jnp.where(tri, 1.0, 0.0), jnp.where(inblk, 1.0, 0.0)], axis=0).astype(BF16)
    g_hi = g.astype(BF16)
    g_lo = (g - g_hi.astype(F32)).astype(BF16)
    cs = _dot(sel, jnp.concatenate([g_hi, g_lo], axis=1))
    yield
    cs =cs[:, 0:GLA_KW] + cs[:, GLA_KW:2 * GLA_KW]
    b = cs[0:C]
    d = cs[C:2 * C]
    ref = b - d
    total = b[C - 1:C, :] if fwd else b[0:1, :]
    q_in = (q * jnp.exp(d)).astype(BF16)
    q_st = (q * jnp.exp(b)).astype(BF16)
    k_out = (k * jnp.exp(total - b)).astype(BF16)
    w_tot = jnp.exp(total)
    zero = jnp.zeros((), BF16)
    nb = C // c
    k_blk = []
    for i in range(nb):
        lo, hi = (0, (i + 1) * c) if fwd else (i * c, C)
        kb = (k[lo:hi] * jnp.exp(ref[i * c:i * c + 1, :] - b[lo:hi])).astype(BF16)
        pad = jnp.zeros((C - (hi - lo), GLA_KW), BF16)
        k_blk.append(jnp.concatenate(([kb, pad] if fwd else [pad, kb]) if hi - lo < C else [kb], axis=0))
    head_of_lane = jnp.right_shift(lax.broadcasted_iota(jnp.int32, (1, GLA_KW), 1), GLA_DK.bit_length() - 1)
    a_two = []
    for m in range(nb // 2):
        qr = q_in[2 * m * c:(2 * m + 2) * c]
        q_heads = jnp.concatenate([jnp.where(head_of_lane == h, qr, zero) for h in range(GLA_HEADS)], axis=0)
        a_two.append(_dot_nt(q_heads, jnp.concatenate(k_blk[2 * m:2 * m + 2], axis=0)))
    yield

    def a_block(h, i):
        m, r = divmod(i, 2)
        return a_two[m][h * 2 * c + r * c:h * 2 * c + (r + 1) * c, r * C:(r + 1) * C]

    a_cat = jnp.concatenate(
        [jnp.where(tri, jnp.concatenate([a_block(h, i) for i in range(nb)], axis=0), 0.0)
         for h in range(GLA_HEADS)], axis=1).astype(BF16)
    vhead_of_lane = jnp.right_shift(lax.broadcasted_iota(jnp.int32, (1, GLA_VW), 1), GLA_DV.bit_length() - 1)
    v_bd = jnp.concatenate([jnp.where(vhead_of_lane == h, v, zero) for h in range(GLA_HEADS)], axis=0)
    st = st_ref[...]
    o = _dot(a_cat, v_bd) + _dot_nt(q_st, st.astype(BF16))
    upd = lax.dot_general(v, k_out, (((0,), (0,)), ((), ())), preferred_element_type=F32)
    same_head = jnp.right_shift(lax.broadcasted_iota(jnp.int32, (GLA_VW, 1), 0),
                                GLA_DV.bit_length() - 1) == head_of_lane
    st_ref[...] = jnp.where(same_head, st * w_tot + upd, 0.0)
    o_ref[...] = o
    yield


def _gla_kernel(qkf_ref, vf_ref, paf_ref, qkb_ref, vb_ref, pab_ref, a2f_ref, abf_ref, a2b_ref, abb_ref,
                of_ref, ob_ref, stf_ref, stb_ref):
    @pl.when(pl.program_id(0) == 0)
    def _():
        stf_ref[...] = jnp.zeros_like(stf_ref)
        stb_ref[...] = jnp.zeros_like(stb_ref)

    C = GLA_C
    waves = []
    for t in range(GLA_STEPS):
        fs = slice(t * C, (t + 1) * C)
        bs = slice((GLA_STEPS - 1 - t) * C, (GLA_STEPS - t) * C)
        chains = []
        for bi in range(qkf_ref.shape[0]):
            chains.append(_gla_direction(qkf_ref[bi, fs], vf_ref[bi, fs], paf_ref[bi, fs], a2f_ref[...], abf_ref[...],
                                         stf_ref.at[bi], of_ref.at[bi, fs], True))
            chains.append(_gla_direction(qkb_ref[bi, bs], vb_ref[bi, bs], pab_ref[bi, bs], a2b_ref[...], abb_ref[...],
                                         stb_ref.at[bi], ob_ref.at[bi, bs], False))
        waves.append(chains)
    for _ in range(3):
        for chains in waves:
            for chain in chains:
                next(chain)
    for chains in waves:
        for chain in chains:
            next(chain)


def _gla(proj, a2f, abf, a2b, abb, ctx_len):
    B, T = proj.shape[0], proj.shape[1]
    C = GLA_C * GLA_STEPS
    assert T % C == 0 and ctx_len % C == 0
    nch, nctx = T // C, ctx_len // C

    def cb(s):
        return jnp.where(s < nctx, nctx - 1 - s, nch + nctx - 1 - s)

    full = lambda shape: pl.BlockSpec(shape, lambda s: (0,) * len(shape))
    return pl.pallas_call(
        _gla_kernel,
        grid=(nch,),
        in_specs=[pl.BlockSpec((B, C, 2 * GLA_KW), lambda s: (0, s, COL_GQK // (2 * GLA_KW))),
                  pl.BlockSpec((B, C, GLA_VW), lambda s: (0, s, COL_GV // GLA_VW)),
                  pl.BlockSpec((B, C, LANES), lambda s: (0, s, COL_GATE // LANES)),
                  pl.BlockSpec((B, C, 2 * GLA_KW), lambda s: (0, cb(s), COL_GQK // (2 * GLA_KW))),
                  pl.BlockSpec((B, C, GLA_VW), lambda s: (0, cb(s), COL_GV // GLA_VW)),
                  pl.BlockSpec((B, C, LANES), lambda s: (0, cb(s), COL_GATE // LANES)),
                  full((LANES, GLA_KW)), full((1, GLA_KW)), full((LANES, GLA_KW)), full((1, GLA_KW))],
        out_specs=[pl.BlockSpec((B, C, GLA_VW), lambda s: (0, s, 0)),
                   pl.BlockSpec((B, C, GLA_VW), lambda s: (0, cb(s), 0))],
        out_shape=[jax.ShapeDtypeStruct((B, T, GLA_VW), F32),
                   jax.ShapeDtypeStruct((B, T, GLA_VW), F32)],
        scratch_shapes=[pltpu.VMEM((B, GLA_VW, GLA_KW), F32),
                        pltpu.VMEM((B, GLA_VW, GLA_KW), F32)],
        compiler_params=_cparams(("arbitrary",), 32),
        name="gla_scan",
    )(proj, proj, proj, proj, proj, proj, a2f, abf, a2b, abb)


def _merge_kernel(xc_ref, x_ref, mod_ref, ma_ref, mg_ref, ms_ref, su_ref, sv_ref, sg_ref, gg_ref, yac_ref, yal_ref,
                  of_ref, ob_ref,
                  onorm_ref, lng_ref, lnb_ref, sw_ref, sb_ref, wa_ref, wg_ref, ws_ref, wo_ref, pg_ref, pb_ref,
                  o_ref, *, tm, ctx_tiles, row_off, alpha):
    D = x_ref.shape[-1]
    i = pl.program_id(1) + row_off
    og = of_ref[...] + ob_ref[...]
    parts = []
    for h in range(GLA_HEADS):
        oh = og[:, h * GLA_DV:(h + 1) * GLA_DV]
        ms = jnp.mean(oh * oh, axis=-1, keepdims=True)
        parts.append(oh * lax.rsqrt(ms + EPS))
    yg = jnp.concatenate(parts, axis=1) * onorm_ref[...] * _silu_from_half(gg_ref[...]).astype(F32)
    sv = sv_ref[...].astype(F32)
    mu = jnp.mean(sv, axis=-1, keepdims=True)
    dv = sv - mu
    var = jnp.mean(dv * dv, axis=-1, keepdims=True)
    vn = (dv * lax.rsqrt(var + EPS) * lng_ref[...] + lnb_ref[...]).astype(BF16)
    gw = SGU_W // SGU_GROUPS
    rows = []
    for n in range(tm // SGU_CHUNK):
        cols = []
        for g in range(SGU_GROUPS):
            blk = vn[n * SGU_CHUNK:(n + 1) * SGU_CHUNK, g * gw:(g + 1) * gw]
            cols.append(_dot(sw_ref[g], blk))
        rows.append(jnp.concatenate(cols, axis=1) + sb_ref[...])
    mixed = jnp.concatenate(rows, axis=0)
    ys = su_ref[...].astype(F32) * mixed * _silu_from_half(sg_ref[...]).astype(F32)

    def gated(g_ref, y, w_ref):
        half = _dot(y, w_ref[...]).astype(BF16)
        return half + half * jnp.tanh(g_ref[...])

    ya = jnp.where(i < ctx_tiles, yac_ref[...], yal_ref[...])
    m = (gated(ma_ref, ya, wa_ref) + gated(mg_ref, yg.astype(BF16), wg_ref)
         + gated(ms_ref, ys.astype(BF16), ws_ref))
    out = _dot(m, wo_ref[...])
    gate = jnp.where(i < ctx_tiles, mod_ref[1:2, 2 * D:3 * D], mod_ref[0:1, 2 * D:3 * D])
    r = alpha * jnp.where(i < ctx_tiles, xc_ref[...], x_ref[...]) + gate * out
    mu = jnp.mean(r, axis=-1, keepdims=True)
    dr = r - mu
    var = jnp.mean(dr * dr, axis=-1, keepdims=True)
    o_ref[...] = dr * lax.rsqrt(var + EPS) * pg_ref[...] + pb_ref[...]


def _merge(x_ctx, x_lat, lat_tile0, modb, proj, ya_ctx, ya_lat, of, ob, p, ctx_len, tm, skip_ctx, alpha):
    B, T, D = proj.shape[0], proj.shape[1], x_lat.shape[-1]
    ctx_tiles = ctx_len // tm
    row_off = ctx_tiles if skip_ctx else 0
    nt = T // tm - row_off
    rows = lambda w, c: pl.BlockSpec((None, tm, w), lambda b, i: (b, i + row_off, c))
    ctx_rows = lambda w: pl.BlockSpec((None, tm, w), lambda b, i: (b, jnp.minimum(i + row_off, ctx_tiles - 1), 0))
    lat_rows = lambda w, t0: pl.BlockSpec(
        (None, tm, w), lambda b, i: (b, jnp.maximum(i + row_off - ctx_tiles, 0) + t0, 0))
    ya_ctx_spec, ya_lat_spec = ctx_rows(ATTN_W), lat_rows(ATTN_W, 0)
    full = lambda shape: pl.BlockSpec(shape, lambda b, i: (0,) * len(shape))
    return pl.pallas_call(
        functools.partial(_merge_kernel, tm=tm, ctx_tiles=ctx_tiles, row_off=row_off, alpha=alpha),
        grid=(B, nt),
        in_specs=[ctx_rows(D), lat_rows(D, lat_tile0),
                  pl.BlockSpec((None, 8, 3 * D), lambda b, i: (b, 0, 0)),
                  rows(D, COL_MA // D), rows(D, COL_MG // D), rows(D, COL_MS // D),
                  rows(SGU_W, COL_SU // SGU_W), rows(SGU_W, COL_SV // SGU_W), rows(SGU_W, COL_SG // SGU_W),
                  rows(GLA_VW, COL_GG // GLA_VW),
                  ya_ctx_spec, ya_lat_spec, rows(GLA_VW, 0), rows(GLA_VW, 0),
                  full((1, GLA_VW)), full((1, SGU_W)), full((1, SGU_W)),
                  full((SGU_GROUPS, SGU_CHUNK, SGU_CHUNK)), full((SGU_CHUNK, SGU_W)),
                  full((ATTN_W, D)), full((GLA_VW, D)), full((SGU_W, D)), full((D, D)),
                  full((1, D)), full((1, D))],
        out_specs=pl.BlockSpec((None, tm, D), lambda b, i: (b, i, 0)),
        out_shape=jax.ShapeDtypeStruct((B, nt * tm, D), F32),
        compiler_params=_cparams(("arbitrary", "arbitrary"), 48),
        name="merge_out",
    )(x_ctx, x_lat, modb, proj, proj, proj, proj, proj, proj, proj, ya_ctx, ya_lat, of, ob,
      p["onorm"], p["lng"], p["lnb"], p["sw"], p["sb"], p["wa"], p["wg"], p["ws"], p["wo"], p["pg"], p["pb"])


W_BLK = 256
GATE_SHIFT = 2 * GLA_RANK


def _w_in_plan():
    sizes = (512, 128, 128, 512, 256, 256, 512, 16, 16, 512, 512, 512, 512, 1024, 1024, 1024)
    names = ("aq", "ak", "av", "ag", "gq", "gk", "gv", "af", "ab", "gg", "su", "sv", "sg", "ma", "mg", "ms")
    halved = ("ag", "gg", "sg", "ma", "mg", "ms")
    src, off = {}, 0
    for n, s in zip(names, sizes):
        src[n] = (off, s)
        off += s
    plan = []
    for n in ("ma", "mg", "ms", "aq", "ag", "gv", "gg", "su", "sv", "sg", "gq", "gk"):
        start, size = src[n]
        shift = start % W_BLK
        assert shift in (0, GATE_SHIFT) and size % W_BLK == 0
        for t in range(size // W_BLK):
            plan.append(((start - shift) // W_BLK + t, 1 if shift else 0, int(n in halved)))
    assert src["ak"][0] % W_BLK == 0 and src["av"][0] == src["ak"][0] + KV_W and 2 * KV_W == W_BLK
    plan.append((src["ak"][0] // W_BLK, 0, 0))
    assert src["af"][0] % W_BLK == 0 and src["ab"][0] == src["af"][0] + GLA_RANK
    plan.append((src["af"][0] // W_BLK, 2, 0))
    assert len(plan) * W_BLK == PROJ_W
    return plan, off


def _wprep_kernel(src_ref, nxt_ref, kind_ref, half_ref, a_ref, b_ref, o_ref):
    j = pl.program_id(1)
    a = a_ref[...]
    ab = jnp.concatenate([a, b_ref[...]], axis=0)
    shifted = ab[GATE_SHIFT:GATE_SHIFT + W_BLK, :]
    row = lax.broadcasted_iota(jnp.int32, (W_BLK, 1), 0)
    gates = jnp.where(row < GATE_SHIFT, a, 0.0)
    kind = kind_ref[j]
    val = jnp.where(kind == 1, shifted, jnp.where(kind == 2, gates, a))
    o_ref[...] = (val * jnp.where(half_ref[j] == 1, 0.5, 1.0)).astype(BF16)


def _regroup_w_in(w_in):
    L, D, width = w_in.shape
    plan, used = _w_in_plan()
    assert used == width
    last_blk = (width - 1) // W_BLK
    src = jnp.array([p[0] for p in plan], jnp.int32)
    kind = jnp.array([p[1] for p in plan], jnp.int32)
    half = jnp.array([p[2] for p in plan], jnp.int32)
    nxt, cur = [], min(plan[0][0] + 1, last_blk)
    for s, k, _ in plan:
        cur = min(s + 1, last_blk) if k == 1 else cur
        nxt.append(cur)
    nxt = jnp.array(nxt, jnp.int32)
    wt = jnp.swapaxes(w_in, 1, 2)
    grid_spec = pltpu.PrefetchScalarGridSpec(
        num_scalar_prefetch=4,
        grid=(L, len(plan)),
        in_specs=[pl.BlockSpec((None, W_BLK, D), lambda l, j, s, n, k, h: (l, s[j], 0)),
                  pl.BlockSpec((None, W_BLK, D), lambda l, j, s, n, k, h: (l, n[j], 0))],
        out_specs=pl.BlockSpec((None, W_BLK, D), lambda l, j, s, n, k, h: (l, j, 0)))
    return pl.pallas_call(
        _wprep_kernel,
        grid_spec=grid_spec,
        out_shape=jax.ShapeDtypeStruct((L, PROJ_W, D), BF16),
        compiler_params=_cparams(("arbitrary", "arbitrary"), 32),
        name="w_in_prep",
    )(src, nxt, kind, half, wt, wt)


def _rope_tables(n_lat, ctx_len):
    rows = n_lat // GRID_W
    row = np.repeat(np.arange(rows, dtype=np.float64), GRID_W)
    col = np.tile(np.arange(GRID_W, dtype=np.float64), rows)
    freqs = ROPE_THETA ** (-np.arange(0, ROPE_AXIS, 2, dtype=np.float64) / ROPE_AXIS)
    ang_r, ang_c = row[:, None] * freqs, col[:, None] * freqs
    cos = np.concatenate([np.cos(ang_r)] * 2 + [np.cos(ang_c)] * 2, axis=1)
    sin = np.concatenate([-np.sin(ang_r), np.sin(ang_r), -np.sin(ang_c), np.sin(ang_c)], axis=1)
    cos = np.concatenate([np.ones((ctx_len, HEAD_DIM)), cos], axis=0)
    sin = np.concatenate([np.zeros((ctx_len, HEAD_DIM)), sin], axis=0)
    return (jnp.asarray(np.concatenate([cos, cos], axis=1), F32),
            jnp.asarray(np.concatenate([sin, sin], axis=1), F32))


def _largest_tile(total, cap, mult):
    best = mult
    for t in range(mult, cap + 1, mult):
        if total % t == 0:
            best = t
    return best


def kernel(x, c, ctx, c_ctx, w_ada, b_ada, w_in, attn_q_norm, attn_k_norm, gla_a2_f, gla_ab_f, gla_a2_b, gla_ab_b, gla_o_norm, sgu_ln_g, sgu_ln_b, sgu_w, sgu_b, w_br_attn, w_br_gla, w_br_sgu, w_out, post_ln_g, post_ln_b):
    B, n_lat, D = x.shape
    ctx_len = ctx.shape[1]
    depth = w_in.shape[0]
    T = ctx_len + n_lat
    assert D == D_MODEL and B <= 7
    assert ctx_len % ATT_TQ == 0 and n_lat % ATT_TQ == 0 and n_lat % GRID_W == 0
    alpha = (2 * depth) ** 0.25

    tm_proj = _largest_tile(T, 2112, 16)
    tn_proj = 1536
    tk_att = _largest_tile(T, ATT_TK_MAX, ATT_TQ)
    assert ctx_len <= tk_att
    tm_prep = tk_att
    tm_merge = ATT_TQ

    xall = None
    cc = jnp.zeros((8, D), F32).at[0:B].set(c).at[B].set(c_ctx)
    mod = _ada(cc, w_ada, b_ada.reshape(depth, 1, 3 * D))
    cos_t, sin_t = _rope_tables(n_lat, ctx_len)
    consts = {
        "gq": jnp.kron(jnp.eye(N_HEADS, dtype=F32), jnp.ones((HEAD_DIM, HEAD_DIM), F32)).astype(BF16),
        "gk": jnp.kron(jnp.eye(N_KV, dtype=F32), jnp.ones((HEAD_DIM, HEAD_DIM), F32)).astype(BF16),
        "eq": jnp.eye(ATTN_W, dtype=BF16),
        "ek": jnp.eye(KV_W, dtype=BF16),
    }
    eye_tq = jnp.eye(ATT_TQ, dtype=BF16)
    w_all = _regroup_w_in(w_in)

    for l in range(depth):
        last = l == depth - 1
        modb = jnp.zeros((B, 8, 3 * D), F32).at[:, 0].set(mod[l, 0:B]).at[:, 1].set(mod[l, B][None])
        first = xall is None
        proj = _inproj(x if first else xall, ctx if first else xall, ctx_len, modb, w_all, l, tm_proj, tn_proj,
                       split=first)
        qn = jnp.concatenate([attn_q_norm[l]] * 2)[None]
        kn = jnp.concatenate([attn_k_norm[l]] * 2)[None]
        qt, kk, vt = _qkprep(proj, cos_t, sin_t, qn, kn, consts, tm_prep, tk_att)
        score_bound = HEAD_DIM * jnp.max(jnp.abs(attn_q_norm[l])) * jnp.max(jnp.abs(attn_k_norm[l])) * Q_SCALE
        bounded = (score_bound <= SCORE_LIMIT).astype(jnp.int32).reshape(1)
        ctx_blocks, lat_blocks = ctx_len // ATT_TQ, n_lat // ATT_TQ
        ya_lat = _attention(bounded, qt, kk, vt, proj, eye_tq, ctx_blocks, lat_blocks, ATT_NSB, T // tk_att, tk_att)
        ya_ctx = ya_lat if last else _attention(bounded, qt, kk, vt, proj, eye_tq, 0, ctx_blocks, 1, 1, ctx_len)
        a2f = jnp.zeros((LANES, GLA_KW), F32).at[0:GLA_RANK].set(gla_a2_f[l]).astype(BF16)
        a2b = jnp.zeros((LANES, GLA_KW), F32).at[GLA_RANK:2 * GLA_RANK].set(gla_a2_b[l]).astype(BF16)
        of, ob = _gla(proj, a2f, gla_ab_f[l][None], a2b, gla_ab_b[l][None], ctx_len)
        p = {
            "onorm": jnp.concatenate([gla_o_norm[l]] * GLA_HEADS)[None],
            "lng": sgu_ln_g[l][None], "lnb": sgu_ln_b[l][None],
            "sw": sgu_w[l].astype(BF16),
            "sb": jnp.repeat(sgu_b[l].T, SGU_W // SGU_GROUPS, axis=1),
            "wa": (0.5 * w_br_attn[l]).astype(BF16), "wg": (0.5 * w_br_gla[l]).astype(BF16),
            "ws": (0.5 * w_br_sgu[l]).astype(BF16),
            "wo": w_out[l].astype(BF16), "pg": post_ln_g[l][None], "pb": post_ln_b[l][None],
        }
        xall = _merge(ctx if first else xall, x if first else xall, 0 if first else ctx_len // tm_merge,
                      modb, proj, ya_ctx, ya_lat, of, ob, p, ctx_len, tm_merge, last, alpha)
    return xall
```

```python
import functools

import jax
import jax.numpy as jnp
import numpy as np
from jax import lax
from jax.experimental import pallas as pl
from jax.experimental.pallas import tpu as pltpu

F32 = jnp.float32
BF16 = jnp.bfloat16

D_MODEL = 1024
GRID_W = 64
HEAD_DIM = 64
N_HEADS = 8
N_KV = 2
Q_PER_KV = N_HEADS // N_KV
ATTN_W = N_HEADS * HEAD_DIM
KV_W = N_KV * HEAD_DIM
ROPE_THETA = 10000.0
ROPE_AXIS = HEAD_DIM // 2
GLA_HEADS = 4
GLA_DK = 64
GLA_DV = 128
GLA_KW = GLA_HEADS * GLA_DK
GLA_VW = GLA_HEADS * GLA_DV
GLA_RANK = 16
GLA_TEMP = 16.0
SGU_GROUPS = 4
SGU_CHUNK = 128
SGU_W = 512
EPS = 1e-6

LANES = 128
V7X_VMEM_BYTES = 64 * 1024 * 1024

COL_MA, COL_MG, COL_MS = 0, 1024, 2048
COL_AQ, COL_AG = 3072, 3584
COL_GV, COL_GG = 4096, 4608
COL_SU, COL_SV, COL_SG = 5120, 5632, 6144
COL_GQK = 6656
COL_AKV = 7168
COL_GATE = 7424
PROJ_W = 7680

SUBLANES = 8
ATT_TQ = 256
ATT_NSB = 4
ATT_TK_MAX = 768
GLA_C = 128
GLA_SUB = 16
GLA_STEPS = 2
ATT_UNROLL = 4
Q_SCALE = 1.4426950408889634 * HEAD_DIM ** -0.5
SCORE_LIMIT = 96.0


def _cparams(sem, vmem_mb):
    return pltpu.CompilerParams(dimension_semantics=sem,
                                vmem_limit_bytes=min(vmem_mb * 1024 * 1024, V7X_VMEM_BYTES - (4 << 20)))


def _silu(x):
    return x * jax.nn.sigmoid(x)


def _silu_from_half(u):
    return u + u * jnp.tanh(u)


def _dot(a, b):
    return jnp.dot(a, b, preferred_element_type=F32)


def _dot_nt(a, b):
    return lax.dot_general(a, b, (((1,), (1,)), ((), ())), preferred_element_type=F32)


def _ada_kernel(c_ref, w_ref, b_ref, o_ref):
    c = c_ref[...]
    o_ref[...] = _dot(_silu(c).astype(BF16), w_ref[...].astype(BF16)) + b_ref[...]


def _ada(cc, w_ada, b_ada):
    L, D = w_ada.shape[0], w_ada.shape[1]
    return pl.pallas_call(
        _ada_kernel,
        grid=(L, 3),
        in_specs=[pl.BlockSpec((8, D), lambda l, j: (0, 0)),
                  pl.BlockSpec((None, D, D), lambda l, j: (l, 0, j)),
                  pl.BlockSpec((None, 1, D), lambda l, j: (l, 0, j))],
        out_specs=pl.BlockSpec((None, 8, D), lambda l, j: (l, 0, j)),
        out_shape=jax.ShapeDtypeStruct((L, 8, 3 * D), F32),
        compiler_params=_cparams(("arbitrary", "arbitrary"), 32),
        name="ada_mod",
    )(cc, w_ada, b_ada)


def _inproj_kernel(x_ref, ctx_ref, mod_ref, w_ref, o_ref, h_ref, *, tm, ctx_len, split):
    i = pl.program_id(1)
    j = pl.program_id(2)
    D = x_ref.shape[-1]

    def modulated(v, r):
        return (v * (1.0 + mod_ref[r:r + 1, D:2 * D]) + mod_ref[r:r + 1, 0:D]).astype(BF16)

    if split:
        @pl.when((j == 0) & (i == 0))
        def _():
            h_ref[0:ctx_len, :] = modulated(ctx_ref[...], 1)
            h_ref[ctx_len:tm, :] = modulated(x_ref[0, 0:tm - ctx_len, :], 0)

        @pl.when((j == 0) & (i > 0))
        def _():
            h_ref[...] = modulated(x_ref[0], 0)
    else:
        @pl.when(j == 0)
        def _():
            row = i * tm + lax.broadcasted_iota(jnp.int32, (tm, 1), 0)
            is_ctx = row < ctx_len
            shift = jnp.where(is_ctx, mod_ref[1:2, 0:D], mod_ref[0:1, 0:D])
            scale = jnp.where(is_ctx, mod_ref[1:2, D:2 * D], mod_ref[0:1, D:2 * D])
            h_ref[...] = (x_ref[0] * (1.0 + scale) + shift).astype(BF16)

    o_ref[...] = _dot_nt(h_ref[...], w_ref[...]).astype(BF16)


def _inproj(x, ctx, ctx_len, modb, w_all, layer, tm, tn, split):
    B, _, D = x.shape
    T = x.shape[1] + ctx_len if split else x.shape[1]
    assert ctx_len < tm
    if split:
        x_spec = pl.BlockSpec((pl.Element(1), pl.Element(tm), pl.Element(D)),
                              lambda b, i, j: (b, pl.multiple_of(jnp.maximum(i * tm - ctx_len, 0), SUBLANES), 0))
    else:
        x_spec = pl.BlockSpec((1, tm, D), lambda b, i, j: (b, i, 0))
    return pl.pallas_call(
        functools.partial(_inproj_kernel, tm=tm, ctx_len=ctx_len, split=split),
        grid=(B, T // tm, PROJ_W // tn),
        in_specs=[x_spec,
                  pl.BlockSpec((None, ctx_len, D), lambda b, i, j: (b, 0, 0)),
                  pl.BlockSpec((None, 8, 3 * D), lambda b, i, j: (b, 0, 0)),
                  pl.BlockSpec((None, tn, D), lambda b, i, j: (layer, j, 0))],
        out_specs=pl.BlockSpec((None, tm, tn), lambda b, i, j: (b, i, j)),
        out_shape=jax.ShapeDtypeStruct((B, T, PROJ_W), BF16),
        scratch_shapes=[pltpu.VMEM((tm, D), BF16)],
        compiler_params=_cparams(("arbitrary", "arbitrary", "arbitrary"), 52),
        name="in_proj",
    )(x, ctx, modb, w_all)


def _rope(y, cos, sin, width):
    lane = lax.broadcasted_iota(jnp.int32, (1, width), 1)
    first = jnp.bitwise_and(lane, ROPE_AXIS - 1) < (ROPE_AXIS // 2)
    up = pltpu.roll(y, width - ROPE_AXIS // 2, 1)
    dn = pltpu.roll(y, ROPE_AXIS // 2, 1)
    return y * cos + jnp.where(first, up, dn) * sin


def _qkprep_kernel(aq_ref, akv_ref, cos_ref, sin_ref, qn_ref, kn_ref, gq_ref, gk_ref,
                   eq_ref, ek_ref, qt_ref, k_ref, vt_ref, *, tm, tk):
    cos = cos_ref[...]
    sin = sin_ref[...]
    aq = aq_ref[...].astype(F32)
    ssq = _dot((aq * aq).astype(BF16), gq_ref[...])
    yq = aq * lax.rsqrt(ssq * (1.0 / HEAD_DIM) + EPS) * jnp.concatenate([qn_ref[...]] * 4, axis=1)
    yq = _rope(yq, jnp.concatenate([cos] * 4, axis=1), jnp.concatenate([sin] * 4, axis=1), ATTN_W)
    yq = (yq * Q_SCALE).astype(BF16)
    qt = _dot_nt(eq_ref[...], yq).astype(BF16)
    zeros = jnp.zeros((HEAD_DIM, ATT_TQ), BF16)
    for g in range(N_KV):
        for qb in range(tm // ATT_TQ):
            for r in range(Q_PER_KV):
                h = g * Q_PER_KV + r
                cols = slice((qb * Q_PER_KV + r) * ATT_TQ, (qb * Q_PER_KV + r + 1) * ATT_TQ)
                qt_ref[g, g * HEAD_DIM:(g + 1) * HEAD_DIM, cols] = qt[h * HEAD_DIM:(h + 1) * HEAD_DIM,
                                                                      qb * ATT_TQ:(qb + 1) * ATT_TQ]
                qt_ref[g, (1 - g) * HEAD_DIM:(2 - g) * HEAD_DIM, cols] = zeros
    akv = akv_ref[...].astype(F32)
    ak = akv[:, 0:KV_W]
    ssk = _dot((ak * ak).astype(BF16), gk_ref[...])
    yk = ak * lax.rsqrt(ssk * (1.0 / HEAD_DIM) + EPS) * kn_ref[...]
    yk = _rope(yk, cos, sin, KV_W).astype(BF16)
    nkb = tm // tk
    k_ref[...] = yk.reshape(nkb, tk, KV_W)
    av = akv_ref[:, KV_W:2 * KV_W]
    vt = _dot_nt(ek_ref[...], av).astype(BF16)
    for g in range(N_KV):
        for n in range(nkb):
            vt_ref[g, n] = vt[g * HEAD_DIM:(g + 1) * HEAD_DIM, n * tk:(n + 1) * tk]


def _qkprep(proj, cos_t, sin_t, qn, kn, consts, tm, tk):
    B, T, _ = proj.shape
    nkb_t = T // tk
    nkb = tm // tk
    full = lambda shape: pl.BlockSpec(shape, lambda b, i: (0,) * len(shape))
    return pl.pallas_call(
        functools.partial(_qkprep_kernel, tm=tm, tk=tk),
        grid=(B, T // tm),
        in_specs=[pl.BlockSpec((None, tm, ATTN_W), lambda b, i: (b, i, COL_AQ // ATTN_W)),
                  pl.BlockSpec((None, tm, 2 * KV_W), lambda b, i: (b, i, COL_AKV // (2 * KV_W))),
                  pl.BlockSpec((tm, LANES), lambda b, i: (i, 0)),
                  pl.BlockSpec((tm, LANES), lambda b, i: (i, 0)),
                  full((1, LANES)), full((1, LANES)),
                  full((ATTN_W, ATTN_W)), full((KV_W, KV_W)),
                  full((ATTN_W, ATTN_W)), full((KV_W, KV_W))],
        out_specs=[pl.BlockSpec((None, N_KV, KV_W, Q_PER_KV * tm), lambda b, i: (b, 0, 0, i)),
                   pl.BlockSpec((None, nkb, tk, KV_W), lambda b, i: (b, i, 0, 0)),
                   pl.BlockSpec((None, N_KV, nkb, HEAD_DIM, tk), lambda b, i: (b, 0, i, 0, 0))],
        out_shape=[jax.ShapeDtypeStruct((B, N_KV, KV_W, Q_PER_KV * T), BF16),
                   jax.ShapeDtypeStruct((B, nkb_t, tk, KV_W), BF16),
                   jax.ShapeDtypeStruct((B, N_KV, nkb_t, HEAD_DIM, tk), BF16)],
        compiler_params=_cparams(("arbitrary", "arbitrary"), 40),
        name="qkv_prep",
    )(proj, proj, cos_t, sin_t, qn, kn, consts["gq"], consts["gk"], consts["eq"], consts["ek"])


def _attn_kernel(bounded_ref, *refs, nsb, n_tiles, rows):
    q_refs, (k_ref, vt_ref), ag_refs, eye_ref = refs[0:nsb], refs[nsb:nsb + 2], refs[nsb + 2:2 * nsb + 2], refs[2 * nsb + 2]
    o_ref, p_ref, acc_ref, oall_ref, qall_ref = refs[-5:]
    nq = q_refs[0].shape[-1]
    tq = nq // Q_PER_KV
    sl = acc_ref.shape[1] - HEAD_DIM
    U = p_ref.shape[0]
    acc_ref[...] = jnp.zeros_like(acc_ref)

    def stage_of(s):
        if isinstance(s, int):
            return divmod(s, n_tiles)
        return lax.div(s, n_tiles), lax.rem(s, n_tiles)

    def weights(s, slot):
        sb, j = stage_of(s)
        p = jnp.exp2(_dot(k_ref[j, 0:rows, :], qall_ref[sb]))
        p_ref[slot, 0:rows, :] = p.astype(BF16)
        acc_ref[sb, HEAD_DIM:HEAD_DIM + sl, :] += jnp.sum(p.reshape(rows // sl, sl, nq), axis=0)

    def weighted_values(s, slot):
        sb, j = stage_of(s)
        acc_ref[sb, 0:HEAD_DIM, :] += _dot(vt_ref[j, :, 0:rows], p_ref[slot, 0:rows, :])

    def run():
        stages = nsb * n_tiles
        for sb in range(nsb):
            qall_ref[sb] = q_refs[sb][...]
        weights(0, 0)

        def group(i, carry):
            for u in range(U):
                weights(U * i + u + 1, (u + 1) % U)
                weighted_values(U * i + u, u)
            return carry

        groups = (stages - 1) // U
        lax.fori_loop(0, groups, group, 0)
        for s in range(U * groups, stages):
            if s + 1 < stages:
                weights(s + 1, (s + 1) % U)
            weighted_values(s, s % U)

    def run_online(sb):
        for r in range(Q_PER_KV):
            q = q_refs[sb][:, r * tq:(r + 1) * tq]

            def body(j, carry):
                m, den, acc = carry
                s = _dot(k_ref[j, 0:rows, :], q)
                m_new = jnp.maximum(m, jnp.max(s, axis=0, keepdims=True))
                p = jnp.exp2(s - m_new)
                alpha = jnp.exp2(m - m_new)
                return (m_new, alpha * den + jnp.sum(p, axis=0, keepdims=True),
                        alpha * acc + _dot(vt_ref[j, :, 0:rows], p.astype(BF16)))

            init = (jnp.full((1, tq), -jnp.inf, F32), jnp.zeros((1, tq), F32), jnp.zeros((HEAD_DIM, tq), F32))
            _, den, acc = lax.fori_loop(0, n_tiles, body, init)
            acc_ref[sb, 0:HEAD_DIM, r * tq:(r + 1) * tq] = acc
            acc_ref[sb, HEAD_DIM:HEAD_DIM + sl, r * tq:(r + 1) * tq] = jnp.broadcast_to(den * (1.0 / sl), (sl, tq))

    def finish(sb):
        for r in range(Q_PER_KV):
            acc = acc_ref[sb, :, r * tq:(r + 1) * tq]
            den = jnp.sum(acc[HEAD_DIM:HEAD_DIM + sl, :], axis=0, keepdims=True)
            o = acc[0:HEAD_DIM, :] * (1.0 / den)
            oall_ref[sb, r * HEAD_DIM:(r + 1) * HEAD_DIM, :] = o.astype(BF16)
        o_t = _dot_nt(eye_ref[...], oall_ref[sb])
        gate = _silu_from_half(ag_refs[sb][...].astype(F32))
        o_ref[sb * tq:(sb + 1) * tq, :] = (o_t * gate).astype(BF16)

    bounded = bounded_ref[0] == 1

    @pl.when(bounded)
    def _():
        run()
        for sb in range(nsb):
            finish(sb)

    @pl.when(jnp.logical_not(bounded))
    def _():
        for sb in range(nsb):
            run_online(sb)
            finish(sb)


def _attention(bounded, qt, k, vt, proj, eye_tq, q_block0, n_qblocks, nsb, n_tiles, rows):
    B = proj.shape[0]
    nkb, tk = k.shape[1], k.shape[2]
    gw = Q_PER_KV * HEAD_DIM
    nq = Q_PER_KV * ATT_TQ
    assert n_qblocks % nsb == 0

    def qmap(sb):
        return lambda b, g, i, f: (b, g, 0, q_block0 + i * nsb + sb)

    def agmap(sb):
        return lambda b, g, i, f: (b, q_block0 + i * nsb + sb, COL_AG // gw + g)

    in_specs = ([pl.BlockSpec((None, None, KV_W, nq), qmap(sb)) for sb in range(nsb)]
                + [pl.BlockSpec((None, nkb, tk, KV_W), lambda b, g, i, f: (b, 0, 0, 0)),
                   pl.BlockSpec((None, None, nkb, HEAD_DIM, tk), lambda b, g, i, f: (b, g, 0, 0, 0))]
                + [pl.BlockSpec((None, ATT_TQ, gw), agmap(sb)) for sb in range(nsb)]
                + [pl.BlockSpec((ATT_TQ, ATT_TQ), lambda b, g, i, f: (0, 0))])
    args = [bounded] + [qt] * nsb + [k, vt] + [proj] * nsb + [eye_tq]
    grid_spec = pltpu.PrefetchScalarGridSpec(
        num_scalar_prefetch=1,
        grid=(B, N_KV, n_qblocks // nsb),
        in_specs=in_specs,
        out_specs=pl.BlockSpec((None, nsb * ATT_TQ, gw), lambda b, g, i, f: (b, i, g)),
        scratch_shapes=[pltpu.VMEM((ATT_UNROLL, tk, nq), BF16),
                        pltpu.VMEM((nsb, HEAD_DIM + SUBLANES, nq), F32),
                        pltpu.VMEM((nsb, gw, ATT_TQ), BF16),
                        pltpu.VMEM((nsb, KV_W, nq), BF16)])
    return pl.pallas_call(
        functools.partial(_attn_kernel, nsb=nsb, n_tiles=n_tiles, rows=rows),
        grid_spec=grid_spec,
        out_shape=jax.ShapeDtypeStruct((B, n_qblocks * ATT_TQ, ATTN_W), BF16),
        compiler_params=_cparams(("arbitrary", "arbitrary", "arbitrary"), 40),
        name="gqa_attention",
    )(*args)


def _gla_direction(qk, v, pa, a2, ab, st_ref, o_ref, fwd):
    C, c = GLA_C, GLA_SUB
    q = qk[:, 0:GLA_KW].astype(F32) * (GLA_DK ** -0.5)
    k = qk[:, GLA_KW:2 * GLA_KW].astype(F32)
    z = _dot(pa, a2) + ab
    yield
    g =(jnp.minimum(z, 0.0) - jnp.log(1.0 + jnp.exp(-jnp.abs(z)))) * (1.0 / GLA_TEMP)
    row = lax.broadcasted_iota(jnp.int32, (C, C), 0)
    col = lax.broadcasted_iota(jnp.int32, (C, C), 1)
    blk = jnp.bitwise_and(row, -c)
    if fwd:
        tri = col <= row
        inblk = tri & (col >= blk)
    else:
        tri = col >= row
        inblk = tri & (col < blk + c)
    sel = jnp.concatenate([jnp.where(tri, 1.0, 0.0), jnp.where(inblk, 1.0, 0.0)], axis=0).astype(BF16)
    g_hi = g.astype(BF16)
    g_lo = (g - g_hi.astype(F32)).astype(BF16)
    cs = _dot(sel, jnp.concatenate([g_hi, g_lo], axis=1))
    yield
    cs =cs[:, 0:GLA_KW] + cs[:, GLA_KW:2 * GLA_KW]
    b = cs[0:C]
    d = cs[C:2 * C]
    ref = b - d
    total = b[C - 1:C, :] if fwd else b[0:1, :]
    q_in = (q * jnp.exp(d)).astype(BF16)
    q_st = (q * jnp.exp(b)).astype(BF16)
    k_out = (k * jnp.exp(total - b)).astype(BF16)
    w_tot = jnp.exp(total)
    zero = jnp.zeros((), BF16)
    nb = C // c
    k_blk = []
    for i in range(nb):
        lo, hi = (0, (i + 1) * c) if fwd else (i * c, C)
        kb = (k[lo:hi] * jnp.exp(ref[i * c:i * c + 1, :] - b[lo:hi])).astype(BF16)
        pad = jnp.zeros((C - (hi - lo), GLA_KW), BF16)
        k_blk.append(jnp.concatenate(([kb, pad] if fwd else [pad, kb]) if hi - lo < C else [kb], axis=0))
    head_of_lane = jnp.right_shift(lax.broadcasted_iota(jnp.int32, (1, GLA_KW), 1), GLA_DK.bit_length() - 1)
    a_two = []
    for m in range(nb // 2):
        qr = q_in[2 * m * c:(2 * m + 2) * c]
        q_heads = jnp.concatenate([jnp.where(head_of_lane == h, qr, zero) for h in range(GLA_HEADS)], axis=0)
        a_two.append(_dot_nt(q_heads, jnp.concatenate(k_blk[2 * m:2 * m + 2], axis=0)))
    yield

    def a_block(h, i):
        m, r = divmod(i, 2)
        return a_two[m][h * 2 * c + r * c:h * 2 * c + (r + 1) * c, r * C:(r + 1) * C]

    a_cat = jnp.concatenate(
        [jnp.where(tri, jnp.concatenate([a_block(h, i) for i in range(nb)], axis=0), 0.0)
         for h in range(GLA_HEADS)], axis=1).astype(BF16)
    vhead_of_lane = jnp.right_shift(lax.broadcasted_iota(jnp.int32, (1, GLA_VW), 1), GLA_DV.bit_length() - 1)
    v_bd = jnp.concatenate([jnp.where(vhead_of_lane == h, v, zero) for h in range(GLA_HEADS)], axis=0)
    st = st_ref[...]
    o = _dot(a_cat, v_bd) + _dot_nt(q_st, st.astype(BF16))
    upd = lax.dot_general(v, k_out, (((0,), (0,)), ((), ())), preferred_element_type=F32)
    same_head = jnp.right_shift(lax.broadcasted_iota(jnp.int32, (GLA_VW, 1), 0),
                                GLA_DV.bit_length() - 1) == head_of_lane
    st_ref[...] = jnp.where(same_head, st * w_tot + upd, 0.0)
    o_ref[...] = o
    yield


def _gla_kernel(qkf_ref, vf_ref, paf_ref, qkb_ref, vb_ref, pab_ref, a2f_ref, abf_ref, a2b_ref, abb_ref,
                of_ref, ob_ref, stf_ref, stb_ref):
    @pl.when(pl.program_id(0) == 0)
    def _():
        stf_ref[...] = jnp.zeros_like(stf_ref)
        stb_ref[...] = jnp.zeros_like(stb_ref)

    C = GLA_C
    waves = []
    for t in range(GLA_STEPS):
        fs = slice(t * C, (t + 1) * C)
        bs = slice((GLA_STEPS - 1 - t) * C, (GLA_STEPS - t) * C)
        chains = []
        for bi in range(qkf_ref.shape[0]):
            chains.append(_gla_direction(qkf_ref[bi, fs], vf_ref[bi, fs], paf_ref[bi, fs], a2f_ref[...], abf_ref[...],
                                         stf_ref.at[bi], of_ref.at[bi, fs], True))
            chains.append(_gla_direction(qkb_ref[bi, bs], vb_ref[bi, bs], pab_ref[bi, bs], a2b_ref[...], abb_ref[...],
                                         stb_ref.at[bi], ob_ref.at[bi, bs], False))
        waves.append(chains)
    for _ in range(3):
        for chains in waves:
            for chain in chains:
                next(chain)
    for chains in waves:
        for chain in chains:
            next(chain)


def _gla(proj, a2f, abf, a2b, abb, ctx_len):
    B, T = proj.shape[0], proj.shape[1]
    C = GLA_C * GLA_STEPS
    assert T % C == 0 and ctx_len % C == 0
    nch, nctx = T // C, ctx_len // C

    def cb(s):
        return jnp.where(s < nctx, nctx - 1 - s, nch + nctx - 1 - s)

    full = lambda shape: pl.BlockSpec(shape, lambda s: (0,) * len(shape))
    return pl.pallas_call(
        _gla_kernel,
        grid=(nch,),
        in_specs=[pl.BlockSpec((B, C, 2 * GLA_KW), lambda s: (0, s, COL_GQK // (2 * GLA_KW))),
                  pl.BlockSpec((B, C, GLA_VW), lambda s: (0, s, COL_GV // GLA_VW)),
                  pl.BlockSpec((B, C, LANES), lambda s: (0, s, COL_GATE // LANES)),
                  pl.BlockSpec((B, C, 2 * GLA_KW), lambda s: (0, cb(s), COL_GQK // (2 * GLA_KW))),
                  pl.BlockSpec((B, C, GLA_VW), lambda s: (0, cb(s), COL_GV // GLA_VW)),
                  pl.BlockSpec((B, C, LANES), lambda s: (0, cb(s), COL_GATE // LANES)),
                  full((LANES, GLA_KW)), full((1, GLA_KW)), full((LANES, GLA_KW)), full((1, GLA_KW))],
        out_specs=[pl.BlockSpec((B, C, GLA_VW), lambda s: (0, s, 0)),
                   pl.BlockSpec((B, C, GLA_VW), lambda s: (0, cb(s), 0))],
        out_shape=[jax.ShapeDtypeStruct((B, T, GLA_VW), F32),
                   jax.ShapeDtypeStruct((B, T, GLA_VW), F32)],
        scratch_shapes=[pltpu.VMEM((B, GLA_VW, GLA_KW), F32),
                        pltpu.VMEM((B, GLA_VW, GLA_KW), F32)],
        compiler_params=_cparams(("arbitrary",), 32),
        name="gla_scan",
    )(proj, proj, proj, proj, proj, proj, a2f, abf, a2b, abb)


def _merge_kernel(xc_ref, x_ref, mod_ref, ma_ref, mg_ref, ms_ref, su_ref, sv_ref, sg_ref, gg_ref, yac_ref, yal_ref,
                  of_ref, ob_ref,
                  onorm_ref, lng_ref, lnb_ref, sw_ref, sb_ref, wa_ref, wg_ref, ws_ref, wo_ref, pg_ref, pb_ref,
                  o_ref, *, tm, ctx_tiles, row_off, alpha):
    D = x_ref.shape[-1]
    i = pl.program_id(1) + row_off
    og = of_ref[...] + ob_ref[...]
    parts = []
    for h in range(GLA_HEADS):
        oh = og[:, h * GLA_DV:(h + 1) * GLA_DV]
        ms = jnp.mean(oh * oh, axis=-1, keepdims=True)
        parts.append(oh * lax.rsqrt(ms + EPS))
    yg = jnp.concatenate(parts, axis=1) * onorm_ref[...] * _silu_from_half(gg_ref[...]).astype(F32)
    sv = sv_ref[...].astype(F32)
    mu = jnp.mean(sv, axis=-1, keepdims=True)
    dv = sv - mu
    var = jnp.mean(dv * dv, axis=-1, keepdims=True)
    vn = (dv * lax.rsqrt(var + EPS) * lng_ref[...] + lnb_ref[...]).astype(BF16)
    gw = SGU_W // SGU_GROUPS
    rows = []
    for n in range(tm // SGU_CHUNK):
        cols = []
        for g in range(SGU_GROUPS):
            blk = vn[n * SGU_CHUNK:(n + 1) * SGU_CHUNK, g * gw:(g + 1) * gw]
            cols.append(_dot(sw_ref[g], blk))
        rows.append(jnp.concatenate(cols, axis=1) + sb_ref[...])
    mixed = jnp.concatenate(rows, axis=0)
    ys = su_ref[...].astype(F32) * mixed * _silu_from_half(sg_ref[...]).astype(F32)

    def gated(g_ref, y, w_ref):
        half = _dot(y, w_ref[...]).astype(BF16)
        return half + half * jnp.tanh(g_ref[...])

    ya = jnp.where(i < ctx_tiles, yac_ref[...], yal_ref[...])
    m = (gated(ma_ref, ya, wa_ref) + gated(mg_ref, yg.astype(BF16), wg_ref)
         + gated(ms_ref, ys.astype(BF16), ws_ref))
    out = _dot(m, wo_ref[...])
    gate = jnp.where(i < ctx_tiles, mod_ref[1:2, 2 * D:3 * D], mod_ref[0:1, 2 * D:3 * D])
    r = alpha * jnp.where(i < ctx_tiles, xc_ref[...], x_ref[...]) + gate * out
    mu = jnp.mean(r, axis=-1, keepdims=True)
    dr = r - mu
    var = jnp.mean(dr * dr, axis=-1, keepdims=True)
    o_ref[...] = dr * lax.rsqrt(var + EPS) * pg_ref[...] + pb_ref[...]


def _merge(x_ctx, x_lat, lat_tile0, modb, proj, ya_ctx, ya_lat, of, ob, p, ctx_len, tm, skip_ctx, alpha):
    B, T, D = proj.shape[0], proj.shape[1], x_lat.shape[-1]
    ctx_tiles = ctx_len // tm
    row_off = ctx_tiles if skip_ctx else 0
    nt = T // tm - row_off
    rows = lambda w, c: pl.BlockSpec((None, tm, w), lambda b, i: (b, i + row_off, c))
    ctx_rows = lambda w: pl.BlockSpec((None, tm, w), lambda b, i: (b, jnp.minimum(i + row_off, ctx_tiles - 1), 0))
    lat_rows = lambda w, t0: pl.BlockSpec(
        (None, tm, w), lambda b, i: (b, jnp.maximum(i + row_off - ctx_tiles, 0) + t0, 0))
    ya_ctx_spec, ya_lat_spec = ctx_rows(ATTN_W), lat_rows(ATTN_W, 0)
    full = lambda shape: pl.BlockSpec(shape, lambda b, i: (0,) * len(shape))
    return pl.pallas_call(
        functools.partial(_merge_kernel, tm=tm, ctx_tiles=ctx_tiles, row_off=row_off, alpha=alpha),
        grid=(B, nt),
        in_specs=[ctx_rows(D), lat_rows(D, lat_tile0),
                  pl.BlockSpec((None, 8, 3 * D), lambda b, i: (b, 0, 0)),
                  rows(D, COL_MA // D), rows(D, COL_MG // D), rows(D, COL_MS // D),
                  rows(SGU_W, COL_SU // SGU_W), rows(SGU_W, COL_SV // SGU_W), rows(SGU_W, COL_SG // SGU_W),
                  rows(GLA_VW, COL_GG // GLA_VW),
                  ya_ctx_spec, ya_lat_spec, rows(GLA_VW, 0), rows(GLA_VW, 0),
                  full((1, GLA_VW)), full((1, SGU_W)), full((1, SGU_W)),
                  full((SGU_GROUPS, SGU_CHUNK, SGU_CHUNK)), full((SGU_CHUNK, SGU_W)),
                  full((ATTN_W, D)), full((GLA_VW, D)), full((SGU_W, D)), full((D, D)),
                  full((1, D)), full((1, D))],
        out_specs=pl.BlockSpec((None, tm, D), lambda b, i: (b, i, 0)),
        out_shape=jax.ShapeDtypeStruct((B, nt * tm, D), F32),
        compiler_params=_cparams(("arbitrary", "arbitrary"), 48),
        name="merge_out",
    )(x_ctx, x_lat, modb, proj, proj, proj, proj, proj, proj, proj, ya_ctx, ya_lat, of, ob,
      p["onorm"], p["lng"], p["lnb"], p["sw"], p["sb"], p["wa"], p["wg"], p["ws"], p["wo"], p["pg"], p["pb"])


W_BLK = 256
GATE_SHIFT = 2 * GLA_RANK


def _w_in_plan():
    sizes = (512, 128, 128, 512, 256, 256, 512, 16, 16, 512, 512, 512, 512, 1024, 1024, 1024)
    names = ("aq", "ak", "av", "ag", "gq", "gk", "gv", "af", "ab", "gg", "su", "sv", "sg", "ma", "mg", "ms")
    halved = ("ag", "gg", "sg", "ma", "mg", "ms")
    src, off = {}, 0
    for n, s in zip(names, sizes):
        src[n] = (off, s)
        off += s
    plan = []
    for n in ("ma", "mg", "ms", "aq", "ag", "gv", "gg", "su", "sv", "sg", "gq", "gk"):
        start, size = src[n]
        shift = start % W_BLK
        assert shift in (0, GATE_SHIFT) and size % W_BLK == 0
        for t in range(size // W_BLK):
            plan.append(((start - shift) // W_BLK + t, 1 if shift else 0, int(n in halved)))
    assert src["ak"][0] % W_BLK == 0 and src["av"][0] == src["ak"][0] + KV_W and 2 * KV_W == W_BLK
    plan.append((src["ak"][0] // W_BLK, 0, 0))
    assert src["af"][0] % W_BLK == 0 and src["ab"][0] == src["af"][0] + GLA_RANK
    plan.append((src["af"][0] // W_BLK, 2, 0))
    assert len(plan) * W_BLK == PROJ_W
    return plan, off


def _wprep_kernel(src_ref, nxt_ref, kind_ref, half_ref, a_ref, b_ref, o_ref):
    j = pl.program_id(1)
    a = a_ref[...]
    ab = jnp.concatenate([a, b_ref[...]], axis=0)
    shifted = ab[GATE_SHIFT:GATE_SHIFT + W_BLK, :]
    row = lax.broadcasted_iota(jnp.int32, (W_BLK, 1), 0)
    gates = jnp.where(row < GATE_SHIFT, a, 0.0)
    kind = kind_ref[j]
    val = jnp.where(kind == 1, shifted, jnp.where(kind == 2, gates, a))
    o_ref[...] = (val * jnp.where(half_ref[j] == 1, 0.5, 1.0)).astype(BF16)


def _regroup_w_in(w_in):
    L, D, width = w_in.shape
    plan, used = _w_in_plan()
    assert used == width
    last_blk = (width - 1) // W_BLK
    src = jnp.array([p[0] for p in plan], jnp.int32)
    kind = jnp.array([p[1] for p in plan], jnp.int32)
    half = jnp.array([p[2] for p in plan], jnp.int32)
    nxt, cur = [], min(plan[0][0] + 1, last_blk)
    for s, k, _ in plan:
        cur = min(s + 1, last_blk) if k == 1 else cur
        nxt.append(cur)
    nxt = jnp.array(nxt, jnp.int32)
    wt = jnp.swapaxes(w_in, 1, 2)
    grid_spec = pltpu.PrefetchScalarGridSpec(
        num_scalar_prefetch=4,
        grid=(L, len(plan)),
        in_specs=[pl.BlockSpec((None, W_BLK, D), lambda l, j, s, n, k, h: (l, s[j], 0)),
                  pl.BlockSpec((None, W_BLK, D), lambda l, j, s, n, k, h: (l, n[j], 0))],
        out_specs=pl.BlockSpec((None, W_BLK, D), lambda l, j, s, n, k, h: (l, j, 0)))
    return pl.pallas_call(
        _wprep_kernel,
        grid_spec=grid_spec,
        out_shape=jax.ShapeDtypeStruct((L, PROJ_W, D), BF16),
        compiler_params=_cparams(("arbitrary", "arbitrary"), 32),
        name="w_in_prep",
    )(src, nxt, kind, half, wt, wt)


def _rope_tables(n_lat, ctx_len):
    rows = n_lat // GRID_W
    row = np.repeat(np.arange(rows, dtype=np.float64), GRID_W)
    col = np.tile(np.arange(GRID_W, dtype=np.float64), rows)
    freqs = ROPE_THETA ** (-np.arange(0, ROPE_AXIS, 2, dtype=np.float64) / ROPE_AXIS)
    ang_r, ang_c = row[:, None] * freqs, col[:, None] * freqs
    cos = np.concatenate([np.cos(ang_r)] * 2 + [np.cos(ang_c)] * 2, axis=1)
    sin = np.concatenate([-np.sin(ang_r), np.sin(ang_r), -np.sin(ang_c), np.sin(ang_c)], axis=1)
    cos = np.concatenate([np.ones((ctx_len, HEAD_DIM)), cos], axis=0)
    sin = np.concatenate([np.zeros((ctx_len, HEAD_DIM)), sin], axis=0)
    return (jnp.asarray(np.concatenate([cos, cos], axis=1), F32),
            jnp.asarray(np.concatenate([sin, sin], axis=1), F32))


def _largest_tile(total, cap, mult):
    best = mult
    for t in range(mult, cap + 1, mult):
        if total % t == 0:
            best = t
    return best


def kernel(x, c, ctx, c_ctx, w_ada, b_ada, w_in, attn_q_norm, attn_k_norm, gla_a2_f, gla_ab_f, gla_a2_b, gla_ab_b, gla_o_norm, sgu_ln_g, sgu_ln_b, sgu_w, sgu_b, w_br_attn, w_br_gla, w_br_sgu, w_out, post_ln_g, post_ln_b):
    B, n_lat, D = x.shape
    ctx_len = ctx.shape[1]
    depth = w_in.shape[0]
    T = ctx_len + n_lat
    assert D == D_MODEL and B <= 7
    assert ctx_len % ATT_TQ == 0 and n_lat % ATT_TQ == 0 and n_lat % GRID_W == 0
    alpha = (2 * depth) ** 0.25

    tm_proj = _largest_tile(T, 2112, 16)
    tn_proj = 1536
    tk_att = _largest_tile(T, ATT_TK_MAX, ATT_TQ)
    assert ctx_len <= tk_att
    tm_prep = tk_att
    tm_merge = ATT_TQ

    xall = None
    cc = jnp.zeros((8, D), F32).at[0:B].set(c).at[B].set(c_ctx)
    mod = _ada(cc, w_ada, b_ada.reshape(depth, 1, 3 * D))
    cos_t, sin_t = _rope_tables(n_lat, ctx_len)
    consts = {
        "gq": jnp.kron(jnp.eye(N_HEADS, dtype=F32), jnp.ones((HEAD_DIM, HEAD_DIM), F32)).astype(BF16),
        "gk": jnp.kron(jnp.eye(N_KV, dtype=F32), jnp.ones((HEAD_DIM, HEAD_DIM), F32)).astype(BF16),
        "eq": jnp.eye(ATTN_W, dtype=BF16),
        "ek": jnp.eye(KV_W, dtype=BF16),
    }
    eye_tq = jnp.eye(ATT_TQ, dtype=BF16)
    w_all = _regroup_w_in(w_in)

    for l in range(depth):
        last = l == depth - 1
        modb = jnp.zeros((B, 8, 3 * D), F32).at[:, 0].set(mod[l, 0:B]).at[:, 1].set(mod[l, B][None])
        first = xall is None
        proj = _inproj(x if first else xall, ctx if first else xall, ctx_len, modb, w_all, l, tm_proj, tn_proj,
                       split=first)
        qn = jnp.concatenate([attn_q_norm[l]] * 2)[None]
        kn = jnp.concatenate([attn_k_norm[l]] * 2)[None]
        qt, kk, vt = _qkprep(proj, cos_t, sin_t, qn, kn, consts, tm_prep, tk_att)
        score_bound = HEAD_DIM * jnp.max(jnp.abs(attn_q_norm[l])) * jnp.max(jnp.abs(attn_k_norm[l])) * Q_SCALE
        bounded = (score_bound <= SCORE_LIMIT).astype(jnp.int32).reshape(1)
        ctx_blocks, lat_blocks = ctx_len // ATT_TQ, n_lat // ATT_TQ
        ya_lat = _attention(bounded, qt, kk, vt, proj, eye_tq, ctx_blocks, lat_blocks, ATT_NSB, T // tk_att, tk_att)
        ya_ctx = ya_lat if last else _attention(bounded, qt, kk, vt, proj, eye_tq, 0, ctx_blocks, 1, 1, ctx_len)
        a2f = jnp.zeros((LANES, GLA_KW), F32).at[0:GLA_RANK].set(gla_a2_f[l]).astype(BF16)
        a2b = jnp.zeros((LANES, GLA_KW), F32).at[GLA_RANK:2 * GLA_RANK].set(gla_a2_b[l]).astype(BF16)
        of, ob = _gla(proj, a2f, gla_ab_f[l][None], a2b, gla_ab_b[l][None], ctx_len)
        p = {
            "onorm": jnp.concatenate([gla_o_norm[l]] * GLA_HEADS)[None],
            "lng": sgu_ln_g[l][None], "lnb": sgu_ln_b[l][None],
            "sw": sgu_w[l].astype(BF16),
            "sb": jnp.repeat(sgu_b[l].T, SGU_W // SGU_GROUPS, axis=1),
            "wa": (0.5 * w_br_attn[l]).astype(BF16), "wg": (0.5 * w_br_gla[l]).astype(BF16),
            "ws": (0.5 * w_br_sgu[l]).astype(BF16),
            "wo": w_out[l].astype(BF16), "pg": post_ln_g[l][None], "pb": post_ln_b[l][None],
        }
        xall = _merge(ctx if first else xall, x if first else xall, 0 if first else ctx_len // tm_merge,
                      modb, proj, ya_ctx, ya_lat, of, ob, p, ctx_len, tm_merge, last, alpha)
    return xall
```

```python
import functools

import jax
import jax.numpy as jnp
import numpy as np
from jax import lax
from jax.experimental import pallas as pl
from jax.experimental.pallas import tpu as pltpu

F32 = jnp.float32
BF16 = jnp.bfloat16

D_MODEL = 1024
GRID_W = 64
HEAD_DIM = 64
N_HEADS = 8
N_KV = 2
Q_PER_KV = N_HEADS // N_KV
ATTN_W = N_HEADS * HEAD_DIM
KV_W = N_KV * HEAD_DIM
ROPE_THETA = 10000.0
ROPE_AXIS = HEAD_DIM // 2
GLA_HEADS = 4
GLA_DK = 64
GLA_DV = 128
GLA_KW = GLA_HEADS * GLA_DK
GLA_VW = GLA_HEADS * GLA_DV
GLA_RANK = 16
GLA_TEMP = 16.0
SGU_GROUPS = 4
SGU_CHUNK = 128
SGU_W = 512
EPS = 1e-6

LANES = 128
V7X_VMEM_BYTES = 64 * 1024 * 1024

COL_MA, COL_MG, COL_MS = 0, 1024, 2048
COL_AQ, COL_AG = 3072, 3584
COL_GV, COL_GG = 4096, 4608
COL_SU, COL_SV, COL_SG = 5120, 5632, 6144
COL_GQK = 6656
COL_AKV = 7168
COL_GATE = 7424
PROJ_W = 7680

SUBLANES = 8
ATT_TQ = 256
ATT_NSB = 4
ATT_TK_MAX = 768
GLA_C = 128
GLA_SUB = 16
GLA_STEPS = 2
ATT_UNROLL = 4
Q_SCALE = 1.4426950408889634 * HEAD_DIM ** -0.5
SCORE_LIMIT = 96.0


def _cparams(sem, vmem_mb):
    return pltpu.CompilerParams(dimension_semantics=sem,
                                vmem_limit_bytes=min(vmem_mb * 1024 * 1024, V7X_VMEM_BYTES - (4 << 20)))


def _silu(x):
    return x * jax.nn.sigmoid(x)


def _silu_from_half(u):
    return u + u * jnp.tanh(u)


def _dot(a, b):
    return jnp.dot(a, b, preferred_element_type=F32)


def _dot_nt(a, b):
    return lax.dot_general(a, b, (((1,), (1,)), ((), ())), preferred_element_type=F32)


def _ada_kernel(c_ref, w_ref, b_ref, o_ref):
    c = c_ref[...]
    o_ref[...] = _dot(_silu(c).astype(BF16), w_ref[...].astype(BF16)) + b_ref[...]


def _ada(cc, w_ada, b_ada):
    L, D = w_ada.shape[0], w_ada.shape[1]
    return pl.pallas_call(
        _ada_kernel,
        grid=(L, 3),
        in_specs=[pl.BlockSpec((8, D), lambda l, j: (0, 0)),
                  pl.BlockSpec((None, D, D), lambda l, j: (l, 0, j)),
                  pl.BlockSpec((None, 1, D), lambda l, j: (l, 0, j))],
        out_specs=pl.BlockSpec((None, 8, D), lambda l, j: (l, 0, j)),
        out_shape=jax.ShapeDtypeStruct((L, 8, 3 * D), F32),
        compiler_params=_cparams(("arbitrary", "arbitrary"), 32),
        name="ada_mod",
    )(cc, w_ada, b_ada)


def _inproj_kernel(x_ref, ctx_ref, mod_ref, w_ref, o_ref, h_ref, *, tm, ctx_len):
    i = pl.program_id(1)
    j = pl.program_id(2)
    D = x_ref.shape[-1]

    def modulated(v, r):
        return (v * (1.0 + mod_ref[r:r + 1, D:2 * D]) + mod_ref[r:r + 1, 0:D]).astype(BF16)

    @pl.when((j == 0) & (i == 0))
    def _():
        h_ref[0:ctx_len, :] = modulated(ctx_ref[...], 1)
        h_ref[ctx_len:tm, :] = modulated(x_ref[0, 0:tm - ctx_len, :], 0)

    @pl.when((j == 0) & (i > 0))
    def _():
        h_ref[...] = modulated(x_ref[0], 0)

    o_ref[...] = _dot_nt(h_ref[...], w_ref[...]).astype(BF16)


def _inproj(x, ctx, modb, w_all, layer, tm, tn):
    B, n_lat, D = x.shape
    ctx_len = ctx.shape[1]
    T = n_lat + ctx_len
    assert ctx_len < tm and T % tm == 0 and (tm - ctx_len) % SUBLANES == 0
    x_spec = pl.BlockSpec((pl.Element(1), pl.Element(tm), pl.Element(D)),
                          lambda b, i, j: (b, pl.multiple_of(jnp.maximum(i * tm - ctx_len, 0), SUBLANES), 0))
    return pl.pallas_call(
        functools.partial(_inproj_kernel, tm=tm, ctx_len=ctx_len),
        grid=(B, T // tm, PROJ_W // tn),
        in_specs=[x_spec,
                  pl.BlockSpec((None, ctx_len, D), lambda b, i, j: (b, 0, 0)),
                  pl.BlockSpec((None, 8, 3 * D), lambda b, i, j: (b, 0, 0)),
                  pl.BlockSpec((None, tn, D), lambda b, i, j: (layer, j, 0))],
        out_specs=pl.BlockSpec((None, tm, tn), lambda b, i, j: (b, i, j)),
        out_shape=jax.ShapeDtypeStruct((B, T, PROJ_W), BF16),
        scratch_shapes=[pltpu.VMEM((tm, D), BF16)],
        compiler_params=_cparams(("arbitrary", "arbitrary", "arbitrary"), 52),
        name="in_proj",
    )(x, ctx, modb, w_all)


def _rope(y, cos, sin, width):
    lane = lax.broadcasted_iota(jnp.int32, (1, width), 1)
    first = jnp.bitwise_and(lane, ROPE_AXIS - 1) < (ROPE_AXIS // 2)
    up = pltpu.roll(y, width - ROPE_AXIS // 2, 1)
    dn = pltpu.roll(y, ROPE_AXIS // 2, 1)
    return y * cos + jnp.where(first, up, dn) * sin


def _qkprep_kernel(aq_ref, akv_ref, cos_ref, sin_ref, qn_ref, kn_ref, gq_ref, gk_ref,
                   eq_ref, ek_ref, qt_ref, k_ref, vt_ref, *, tm, tk):
    cos = cos_ref[...]
    sin = sin_ref[...]
    aq = aq_ref[...].astype(F32)
    ssq = _dot((aq * aq).astype(BF16), gq_ref[...])
    yq = aq * lax.rsqrt(ssq * (1.0 / HEAD_DIM) + EPS) * jnp.concatenate([qn_ref[...]] * 4, axis=1)
    yq = _rope(yq, jnp.concatenate([cos] * 4, axis=1), jnp.concatenate([sin] * 4, axis=1), ATTN_W)
    yq = (yq * Q_SCALE).astype(BF16)
    qt = _dot_nt(eq_ref[...], yq).astype(BF16)
    zeros = jnp.zeros((HEAD_DIM, ATT_TQ), BF16)
    for g in range(N_KV):
        for qb in range(tm // ATT_TQ):
            for r in range(Q_PER_KV):
                h = g * Q_PER_KV + r
                cols = slice((qb * Q_PER_KV + r) * ATT_TQ, (qb * Q_PER_KV + r + 1) * ATT_TQ)
                qt_ref[g, g * HEAD_DIM:(g + 1) * HEAD_DIM, cols] = qt[h * HEAD_DIM:(h + 1) * HEAD_DIM,
                                                                      qb * ATT_TQ:(qb + 1) * ATT_TQ]
                qt_ref[g, (1 - g) * HEAD_DIM:(2 - g) * HEAD_DIM, cols] = zeros
    akv = akv_ref[...].astype(F32)
    ak = akv[:, 0:KV_W]
    ssk = _dot((ak * ak).astype(BF16), gk_ref[...])
    yk = ak * lax.rsqrt(ssk * (1.0 / HEAD_DIM) + EPS) * kn_ref[...]
    yk = _rope(yk, cos, sin, KV_W).astype(BF16)
    nkb = tm // tk
    k_ref[...] = yk.reshape(nkb, tk, KV_W)
    av = akv_ref[:, KV_W:2 * KV_W]
    vt = _dot_nt(ek_ref[...], av).astype(BF16)
    for g in range(N_KV):
        for n in range(nkb):
            vt_ref[g, n] = vt[g * HEAD_DIM:(g + 1) * HEAD_DIM, n * tk:(n + 1) * tk]


def _qkprep(proj, cos_t, sin_t, qn, kn, consts, tm, tk):
    B, T, _ = proj.shape
    nkb_t = T // tk
    nkb = tm // tk
    full = lambda shape: pl.BlockSpec(shape, lambda b, i: (0,) * len(shape))
    return pl.pallas_call(
        functools.partial(_qkprep_kernel, tm=tm, tk=tk),
        grid=(B, T // tm),
        in_specs=[pl.BlockSpec((None, tm, ATTN_W), lambda b, i: (b, i, COL_AQ // ATTN_W)),
                  pl.BlockSpec((None, tm, 2 * KV_W), lambda b, i: (b, i, COL_AKV // (2 * KV_W))),
                  pl.BlockSpec((tm, LANES), lambda b, i: (i, 0)),
                  pl.BlockSpec((tm, LANES), lambda b, i: (i, 0)),
                  full((1, LANES)), full((1, LANES)),
                  full((ATTN_W, ATTN_W)), full((KV_W, KV_W)),
                  full((ATTN_W, ATTN_W)), full((KV_W, KV_W))],
        out_specs=[pl.BlockSpec((None, N_KV, KV_W, Q_PER_KV * tm), lambda b, i: (b, 0, 0, i)),
                   pl.BlockSpec((None, nkb, tk, KV_W), lambda b, i: (b, i, 0, 0)),
                   pl.BlockSpec((None, N_KV, nkb, HEAD_DIM, tk), lambda b, i: (b, 0, i, 0, 0))],
        out_shape=[jax.ShapeDtypeStruct((B, N_KV, KV_W, Q_PER_KV * T), BF16),
                   jax.ShapeDtypeStruct((B, nkb_t, tk, KV_W), BF16),
                   jax.ShapeDtypeStruct((B, N_KV, nkb_t, HEAD_DIM, tk), BF16)],
        compiler_params=_cparams(("arbitrary", "arbitrary"), 40),
        name="qkv_prep",
    )(proj, proj, cos_t, sin_t, qn, kn, consts["gq"], consts["gk"], consts["eq"], consts["ek"])


def _attn_kernel(bounded_ref, *refs, nsb, n_tiles, rows):
    q_refs, (k_ref, vt_ref), ag_refs, eye_ref = refs[0:nsb], refs[nsb:nsb + 2], refs[nsb + 2:2 * nsb + 2], refs[2 * nsb + 2]
    o_ref, p_ref, acc_ref, oall_ref, qall_ref = refs[-5:]
    nq = q_refs[0].shape[-1]
    tq = nq // Q_PER_KV
    sl = acc_ref.shape[1] - HEAD_DIM
    U = p_ref.shape[0]
    acc_ref[...] = jnp.zeros_like(acc_ref)

    def stage_of(s):
        if isinstance(s, int):
            return divmod(s, n_tiles)
        return lax.div(s, n_tiles), lax.rem(s, n_tiles)

    def weights(s, slot):
        sb, j = stage_of(s)
        p = jnp.exp2(_dot(k_ref[j, 0:rows, :], qall_ref[sb]))
        p_ref[slot, 0:rows, :] = p.astype(BF16)
        acc_ref[sb, HEAD_DIM:HEAD_DIM + sl, :] += jnp.sum(p.reshape(rows // sl, sl, nq), axis=0)

    def weighted_values(s, slot):
        sb, j = stage_of(s)
        acc_ref[sb, 0:HEAD_DIM, :] += _dot(vt_ref[j, :, 0:rows], p_ref[slot, 0:rows, :])

    def run():
        stages = nsb * n_tiles
        for sb in range(nsb):
            qall_ref[sb] = q_refs[sb][...]
        weights(0, 0)

        def group(i, carry):
            for u in range(U):
                weights(U * i + u + 1, (u + 1) % U)
                weighted_values(U * i + u, u)
            return carry

        groups = (stages - 1) // U
        lax.fori_loop(0, groups, group, 0)
        for s in range(U * groups, stages):
            if s + 1 < stages:
                weights(s + 1, (s + 1) % U)
            weighted_values(s, s % U)

    def run_online(sb):
        for r in range(Q_PER_KV):
            q = q_refs[sb][:, r * tq:(r + 1) * tq]

            def body(j, carry):
                m, den, acc = carry
                s = _dot(k_ref[j, 0:rows, :], q)
                m_new = jnp.maximum(m, jnp.max(s, axis=0, keepdims=True))
                p = jnp.exp2(s - m_new)
                alpha = jnp.exp2(m - m_new)
                return (m_new, alpha * den + jnp.sum(p, axis=0, keepdims=True),
                        alpha * acc + _dot(vt_ref[j, :, 0:rows], p.astype(BF16)))

            init = (jnp.full((1, tq), -jnp.inf, F32), jnp.zeros((1, tq), F32), jnp.zeros((HEAD_DIM, tq), F32))
            _, den, acc = lax.fori_loop(0, n_tiles, body, init)
            acc_ref[sb, 0:HEAD_DIM, r * tq:(r + 1) * tq] = acc
            acc_ref[sb, HEAD_DIM:HEAD_DIM + sl, r * tq:(r + 1) * tq] = jnp.broadcast_to(den * (1.0 / sl), (sl, tq))

    def finish(sb):
        for r in range(Q_PER_KV):
            acc = acc_ref[sb, :, r * tq:(r + 1) * tq]
            den = jnp.sum(acc[HEAD_DIM:HEAD_DIM + sl, :], axis=0, keepdims=True)
            o = acc[0:HEAD_DIM, :] * (1.0 / den)
            oall_ref[sb, r * HEAD_DIM:(r + 1) * HEAD_DIM, :] = o.astype(BF16)
        o_t = _dot_nt(eye_ref[...], oall_ref[sb])
        gate = _silu_from_half(ag_refs[sb][...].astype(F32))
        o_ref[sb * tq:(sb + 1) * tq, :] = (o_t * gate).astype(BF16)

    bounded = bounded_ref[0] == 1

    @pl.when(bounded)
    def _():
        run()
        for sb in range(nsb):
            finish(sb)

    @pl.when(jnp.logical_not(bounded))
    def _():
        for sb in range(nsb):
            run_online(sb)
            finish(sb)


def _attention(bounded, qt, k, vt, proj, eye_tq, q_block0, n_qblocks, nsb, n_tiles, rows):
    B = proj.shape[0]
    nkb, tk = k.shape[1], k.shape[2]
    gw = Q_PER_KV * HEAD_DIM
    nq = Q_PER_KV * ATT_TQ
    assert n_qblocks % nsb == 0

    def qmap(sb):
        return lambda b, g, i, f: (b, g, 0, q_block0 + i * nsb + sb)

    def agmap(sb):
        return lambda b, g, i, f: (b, q_block0 + i * nsb + sb, COL_AG // gw + g)

    in_specs = ([pl.BlockSpec((None, None, KV_W, nq), qmap(sb)) for sb in range(nsb)]
                + [pl.BlockSpec((None, nkb, tk, KV_W), lambda b, g, i, f: (b, 0, 0, 0)),
                   pl.BlockSpec((None, None, nkb, HEAD_DIM, tk), lambda b, g, i, f: (b, g, 0, 0, 0))]
                + [pl.BlockSpec((None, ATT_TQ, gw), agmap(sb)) for sb in range(nsb)]
                + [pl.BlockSpec((ATT_TQ, ATT_TQ), lambda b, g, i, f: (0, 0))])
    args = [bounded] + [qt] * nsb + [k, vt] + [proj] * nsb + [eye_tq]
    grid_spec = pltpu.PrefetchScalarGridSpec(
        num_scalar_prefetch=1,
        grid=(B, N_KV, n_qblocks // nsb),
        in_specs=in_specs,
        out_specs=pl.BlockSpec((None, nsb * ATT_TQ, gw), lambda b, g, i, f: (b, i, g)),
        scratch_shapes=[pltpu.VMEM((ATT_UNROLL, tk, nq), BF16),
                        pltpu.VMEM((nsb, HEAD_DIM + SUBLANES, nq), F32),
                        pltpu.VMEM((nsb, gw, ATT_TQ), BF16),
                        pltpu.VMEM((nsb, KV_W, nq), BF16)])
    return pl.pallas_call(
        functools.partial(_attn_kernel, nsb=nsb, n_tiles=n_tiles, rows=rows),
        grid_spec=grid_spec,
        out_shape=jax.ShapeDtypeStruct((B, n_qblocks * ATT_TQ, ATTN_W), BF16),
        compiler_params=_cparams(("arbitrary", "arbitrary", "arbitrary"), 40),
        name="gqa_attention",
    )(*args)


def _gla_direction(qk, v, pa, a2, ab, st_ref, o_ref, fwd):
    C, c = GLA_C, GLA_SUB
    q = qk[:, 0:GLA_KW].astype(F32) * (GLA_DK ** -0.5)
    k = qk[:, GLA_KW:2 * GLA_KW].astype(F32)
    z = _dot(pa, a2) + ab
    yield
    g =(jnp.minimum(z, 0.0) - jnp.log(1.0 + jnp.exp(-jnp.abs(z)))) * (1.0 / GLA_TEMP)
    row = lax.broadcasted_iota(jnp.int32, (C, C), 0)
    col = lax.broadcasted_iota(jnp.int32, (C, C), 1)
    blk = jnp.bitwise_and(row, -c)
    if fwd:
        tri = col <= row
        inblk = tri & (col >= blk)
    else:
        tri = col >= row
        inblk = tri & (col < blk + c)
    sel = jnp.concatenate([jnp.where(tri, 1.0, 0.0), jnp.where(inblk, 1.0, 0.0)], axis=0).astype(BF16)
    g_hi = g.astype(BF16)
    g_lo = (g - g_hi.astype(F32)).astype(BF16)
    cs = _dot(sel, jnp.concatenate([g_hi, g_lo], axis=1))
    yield
    cs =cs[:, 0:GLA_KW] + cs[:, GLA_KW:2 * GLA_KW]
    b = cs[0:C]
    d = cs[C:2 * C]
    ref = b - d
    total = b[C - 1:C, :] if fwd else b[0:1, :]
    q_in = (q * jnp.exp(d)).astype(BF16)
    q_st = (q * jnp.exp(b)).astype(BF16)
    k_out = (k * jnp.exp(total - b)).astype(BF16)
    w_tot = jnp.exp(total)
    zero = jnp.zeros((), BF16)
    nb = C // c
    k_blk = []
    for i in range(nb):
        lo, hi = (0, (i + 1) * c) if fwd else (i * c, C)
        kb = (k[lo:hi] * jnp.exp(ref[i * c:i * c + 1, :] - b[lo:hi])).astype(BF16)
        pad = jnp.zeros((C - (hi - lo), GLA_KW), BF16)
        k_blk.append(jnp.concatenate(([kb, pad] if fwd else [pad, kb]) if hi - lo < C else [kb], axis=0))
    head_of_lane = jnp.right_shift(lax.broadcasted_iota(jnp.int32, (1, GLA_KW), 1), GLA_DK.bit_length() - 1)
    a_two = []
    for m in range(nb // 2):
        qr = q_in[2 * m * c:(2 * m + 2) * c]
        q_heads = jnp.concatenate([jnp.where(head_of_lane == h, qr, zero) for h in range(GLA_HEADS)], axis=0)
        a_two.append(_dot_nt(q_heads, jnp.concatenate(k_blk[2 * m:2 * m + 2], axis=0)))
    yield

    def a_block(h, i):
        m, r = divmod(i, 2)
        return a_two[m][h * 2 * c + r * c:h * 2 * c + (r + 1) * c, r * C:(r + 1) * C]

    a_cat = jnp.concatenate(
        [jnp.where(tri, jnp.concatenate([a_block(h, i) for i in range(nb)], axis=0), 0.0)
         for h in range(GLA_HEADS)], axis=1).astype(BF16)
    vhead_of_lane = jnp.right_shift(lax.broadcasted_iota(jnp.int32, (1, GLA_VW), 1), GLA_DV.bit_length() - 1)
    v_bd = jnp.concatenate([jnp.where(vhead_of_lane == h, v, zero) for h in range(GLA_HEADS)], axis=0)
    st = st_ref[...]
    o = _dot(a_cat, v_bd) + _dot_nt(q_st, st.astype(BF16))
    upd = lax.dot_general(v, k_out, (((0,), (0,)), ((), ())), preferred_element_type=F32)
    same_head = jnp.right_shift(lax.broadcasted_iota(jnp.int32, (GLA_VW, 1), 0),
                                GLA_DV.bit_length() - 1) == head_of_lane
    st_ref[...] = jnp.where(same_head, st * w_tot + upd, 0.0)
    o_ref[...] = o
    yield


def _gla_kernel(qkf_ref, vf_ref, paf_ref, qkb_ref, vb_ref, pab_ref, a2f_ref, abf_ref, a2b_ref, abb_ref,
                of_ref, ob_ref, stf_ref, stb_ref):
    @pl.when(pl.program_id(0) == 0)
    def _():
        stf_ref[...] = jnp.zeros_like(stf_ref)
        stb_ref[...] = jnp.zeros_like(stb_ref)

    C = GLA_C
    waves = []
    for t in range(GLA_STEPS):
        fs = slice(t * C, (t + 1) * C)
        bs = slice((GLA_STEPS - 1 - t) * C, (GLA_STEPS - t) * C)
        chains = []
        for bi in range(qkf_ref.shape[0]):
            chains.append(_gla_direction(qkf_ref[bi, fs], vf_ref[bi, fs], paf_ref[bi, fs], a2f_ref[...], abf_ref[...],
                                         stf_ref.at[bi], of_ref.at[bi, fs], True))
            chains.append(_gla_direction(qkb_ref[bi, bs], vb_ref[bi, bs], pab_ref[bi, bs], a2b_ref[...], abb_ref[...],
                                         stb_ref.at[bi], ob_ref.at[bi, bs], False))
        waves.append(chains)
    for _ in range(3):
        for chains in waves:
            for chain in chains:
                next(chain)
    for chains in waves:
        for chain in chains:
            next(chain)


def _gla(proj, a2f, abf, a2b, abb, ctx_len):
    B, T = proj.shape[0], proj.shape[1]
    C = GLA_C * GLA_STEPS
    assert T % C == 0 and ctx_len % C == 0
    nch, nctx = T // C, ctx_len // C

    def cb(s):
        return jnp.where(s < nctx, nctx - 1 - s, nch + nctx - 1 - s)

    full = lambda shape: pl.BlockSpec(shape, lambda s: (0,) * len(shape))
    return pl.pallas_call(
        _gla_kernel,
        grid=(nch,),
        in_specs=[pl.BlockSpec((B, C, 2 * GLA_KW), lambda s: (0, s, COL_GQK // (2 * GLA_KW))),
                  pl.BlockSpec((B, C, GLA_VW), lambda s: (0, s, COL_GV // GLA_VW)),
                  pl.BlockSpec((B, C, LANES), lambda s: (0, s, COL_GATE // LANES)),
                  pl.BlockSpec((B, C, 2 * GLA_KW), lambda s: (0, cb(s), COL_GQK // (2 * GLA_KW))),
                  pl.BlockSpec((B, C, GLA_VW), lambda s: (0, cb(s), COL_GV // GLA_VW)),
                  pl.BlockSpec((B, C, LANES), lambda s: (0, cb(s), COL_GATE // LANES)),
                  full((LANES, GLA_KW)), full((1, GLA_KW)), full((LANES, GLA_KW)), full((1, GLA_KW))],
        out_specs=[pl.BlockSpec((B, C, GLA_VW), lambda s: (0, s, 0)),
                   pl.BlockSpec((B, C, GLA_VW), lambda s: (0, cb(s), 0))],
        out_shape=[jax.ShapeDtypeStruct((B, T, GLA_VW), F32),
                   jax.ShapeDtypeStruct((B, T, GLA_VW), F32)],
        scratch_shapes=[pltpu.VMEM((B, GLA_VW, GLA_KW), F32),
                        pltpu.VMEM((B, GLA_VW, GLA_KW), F32)],
        compiler_params=_cparams(("arbitrary",), 32),
        name="gla_scan",
    )(proj, proj, proj, proj, proj, proj, a2f, abf, a2b, abb)


def _merge_kernel(x_ref, mod_ref, ma_ref, mg_ref, ms_ref, su_ref, sv_ref, sg_ref, gg_ref, ya_ref, of_ref, ob_ref,
                  onorm_ref, lng_ref, lnb_ref, sw_ref, sb_ref, wa_ref, wg_ref, ws_ref, wo_ref, pg_ref, pb_ref,
                  o_ref, *, tm, mod_row, alpha):
    D = x_ref.shape[-1]
    ma_ref, mg_ref, ms_ref, su_ref, sv_ref, sg_ref, gg_ref, of_ref, ob_ref = (
        r.at[0] for r in (ma_ref, mg_ref, ms_ref, su_ref, sv_ref, sg_ref, gg_ref, of_ref, ob_ref))
    og = of_ref[...] + ob_ref[...]
    parts = []
    for h in range(GLA_HEADS):
        oh = og[:, h * GLA_DV:(h + 1) * GLA_DV]
        ms = jnp.mean(oh * oh, axis=-1, keepdims=True)
        parts.append(oh * lax.rsqrt(ms + EPS))
    yg = jnp.concatenate(parts, axis=1) * onorm_ref[...] * _silu_from_half(gg_ref[...]).astype(F32)
    sv = sv_ref[...].astype(F32)
    mu = jnp.mean(sv, axis=-1, keepdims=True)
    dv = sv - mu
    var = jnp.mean(dv * dv, axis=-1, keepdims=True)
    vn = (dv * lax.rsqrt(var + EPS) * lng_ref[...] + lnb_ref[...]).astype(BF16)
    gw = SGU_W // SGU_GROUPS
    rows = []
    for n in range(tm // SGU_CHUNK):
        cols = []
        for g in range(SGU_GROUPS):
            blk = vn[n * SGU_CHUNK:(n + 1) * SGU_CHUNK, g * gw:(g + 1) * gw]
            cols.append(_dot(sw_ref[g], blk))
        rows.append(jnp.concatenate(cols, axis=1) + sb_ref[...])
    mixed = jnp.concatenate(rows, axis=0)
    ys = su_ref[...].astype(F32) * mixed * _silu_from_half(sg_ref[...]).astype(F32)

    def gated(g_ref, y, w_ref):
        half = _dot(y, w_ref[...]).astype(BF16)
        return half + half * jnp.tanh(g_ref[...])

    m = (gated(ma_ref, ya_ref[...], wa_ref) + gated(mg_ref, yg.astype(BF16), wg_ref)
         + gated(ms_ref, ys.astype(BF16), ws_ref))
    out = _dot(m, wo_ref[...])
    r = alpha * x_ref[...] + mod_ref[mod_row:mod_row + 1, 2 * D:3 * D] * out
    mu = jnp.mean(r, axis=-1, keepdims=True)
    dr = r - mu
    var = jnp.mean(dr * dr, axis=-1, keepdims=True)
    o_ref[...] = dr * lax.rsqrt(var + EPS) * pg_ref[...] + pb_ref[...]


def _merge(x_seg, ya_seg, row0, mod_row, modb, proj, of, ob, p, tm, alpha):
    B, n_rows, D = x_seg.shape
    assert n_rows % tm == 0 and row0 % 16 == 0

    def stream_rows(w, col0):
        return pl.BlockSpec((pl.Element(1), pl.Element(tm), pl.Element(w)),
                            lambda b, i: (b, pl.multiple_of(row0 + i * tm, 16), col0))

    seg_rows = lambda w: pl.BlockSpec((None, tm, w), lambda b, i: (b, i, 0))
    full = lambda shape: pl.BlockSpec(shape, lambda b, i: (0,) * len(shape))
    return pl.pallas_call(
        functools.partial(_merge_kernel, tm=tm, mod_row=mod_row, alpha=alpha),
        grid=(B, n_rows // tm),
        in_specs=[seg_rows(D),
                  pl.BlockSpec((None, 8, 3 * D), lambda b, i: (b, 0, 0)),
                  stream_rows(D, COL_MA), stream_rows(D, COL_MG), stream_rows(D, COL_MS),
                  stream_rows(SGU_W, COL_SU), stream_rows(SGU_W, COL_SV), stream_rows(SGU_W, COL_SG),
                  stream_rows(GLA_VW, COL_GG),
                  seg_rows(ATTN_W), stream_rows(GLA_VW, 0), stream_rows(GLA_VW, 0),
                  full((1, GLA_VW)), full((1, SGU_W)), full((1, SGU_W)),
                  full((SGU_GROUPS, SGU_CHUNK, SGU_CHUNK)), full((SGU_CHUNK, SGU_W)),
                  full((ATTN_W, D)), full((GLA_VW, D)), full((SGU_W, D)), full((D, D)),
                  full((1, D)), full((1, D))],
        out_specs=pl.BlockSpec((None, tm, D), lambda b, i: (b, i, 0)),
        out_shape=jax.ShapeDtypeStruct((B, n_rows, D), F32),
        compiler_params=_cparams(("arbitrary", "arbitrary"), 56),
        name="merge_out",
    )(x_seg, modb, proj, proj, proj, proj, proj, proj, proj, ya_seg, of, ob,
      p["onorm"], p["lng"], p["lnb"], p["sw"], p["sb"], p["wa"], p["wg"], p["ws"], p["wo"], p["pg"], p["pb"])


W_BLK = 256
GATE_SHIFT = 2 * GLA_RANK


def _w_in_plan():
    sizes = (512, 128, 128, 512, 256, 256, 512, 16, 16, 512, 512, 512, 512, 1024, 1024, 1024)
    names = ("aq", "ak", "av", "ag", "gq", "gk", "gv", "af", "ab", "gg", "su", "sv", "sg", "ma", "mg", "ms")
    halved = ("ag", "gg", "sg", "ma", "mg", "ms")
    src, off = {}, 0
    for n, s in zip(names, sizes):
        src[n] = (off, s)
        off += s
    plan = []
    for n in ("ma", "mg", "ms", "aq", "ag", "gv", "gg", "su", "sv", "sg", "gq", "gk"):
        start, size = src[n]
        shift = start % W_BLK
        assert shift in (0, GATE_SHIFT) and size % W_BLK == 0
        for t in range(size // W_BLK):
            plan.append(((start - shift) // W_BLK + t, 1 if shift else 0, int(n in halved)))
    assert src["ak"][0] % W_BLK == 0 and src["av"][0] == src["ak"][0] + KV_W and 2 * KV_W == W_BLK
    plan.append((src["ak"][0] // W_BLK, 0, 0))
    assert src["af"][0] % W_BLK == 0 and src["ab"][0] == src["af"][0] + GLA_RANK
    plan.append((src["af"][0] // W_BLK, 2, 0))
    assert len(plan) * W_BLK == PROJ_W
    return plan, off


def _wprep_kernel(src_ref, nxt_ref, kind_ref, half_ref, a_ref, b_ref, o_ref):
    j = pl.program_id(1)
    a = a_ref[...]
    ab = jnp.concatenate([a, b_ref[...]], axis=0)
    shifted = ab[GATE_SHIFT:GATE_SHIFT + W_BLK, :]
    row = lax.broadcasted_iota(jnp.int32, (W_BLK, 1), 0)
    gates = jnp.where(row < GATE_SHIFT, a, 0.0)
    kind = kind_ref[j]
    val = jnp.where(kind == 1, shifted, jnp.where(kind == 2, gates, a))
    o_ref[...] = (val * jnp.where(half_ref[j] == 1, 0.5, 1.0)).astype(BF16)


def _regroup_w_in(w_in):
    L, D, width = w_in.shape
    plan, used = _w_in_plan()
    assert used == width
    last_blk = (width - 1) // W_BLK
    src = jnp.array([p[0] for p in plan], jnp.int32)
    kind = jnp.array([p[1] for p in plan], jnp.int32)
    half = jnp.array([p[2] for p in plan], jnp.int32)
    nxt, cur = [], min(plan[0][0] + 1, last_blk)
    for s, k, _ in plan:
        cur = min(s + 1, last_blk) if k == 1 else cur
        nxt.append(cur)
    nxt = jnp.array(nxt, jnp.int32)
    wt = jnp.swapaxes(w_in, 1, 2)
    grid_spec = pltpu.PrefetchScalarGridSpec(
        num_scalar_prefetch=4,
        grid=(L, len(plan)),
        in_specs=[pl.BlockSpec((None, W_BLK, D), lambda l, j, s, n, k, h: (l, s[j], 0)),
                  pl.BlockSpec((None, W_BLK, D), lambda l, j, s, n, k, h: (l, n[j], 0))],
        out_specs=pl.BlockSpec((None, W_BLK, D), lambda l, j, s, n, k, h: (l, j, 0)))
    return pl.pallas_call(
        _wprep_kernel,
        grid_spec=grid_spec,
        out_shape=jax.ShapeDtypeStruct((L, PROJ_W, D), BF16),
        compiler_params=_cparams(("arbitrary", "arbitrary"), 32),
        name="w_in_prep",
    )(src, nxt, kind, half, wt, wt)


def _rope_tables(n_lat, ctx_len):
    rows = n_lat // GRID_W
    row = np.repeat(np.arange(rows, dtype=np.float64), GRID_W)
    col = np.tile(np.arange(GRID_W, dtype=np.float64), rows)
    freqs = ROPE_THETA ** (-np.arange(0, ROPE_AXIS, 2, dtype=np.float64) / ROPE_AXIS)
    ang_r, ang_c = row[:, None] * freqs, col[:, None] * freqs
    cos = np.concatenate([np.cos(ang_r)] * 2 + [np.cos(ang_c)] * 2, axis=1)
    sin = np.concatenate([-np.sin(ang_r), np.sin(ang_r), -np.sin(ang_c), np.sin(ang_c)], axis=1)
    cos = np.concatenate([np.ones((ctx_len, HEAD_DIM)), cos], axis=0)
    sin = np.concatenate([np.zeros((ctx_len, HEAD_DIM)), sin], axis=0)
    return (jnp.asarray(np.concatenate([cos, cos], axis=1), F32),
            jnp.asarray(np.concatenate([sin, sin], axis=1), F32))


def _largest_tile(total, cap, mult):
    best = mult
    for t in range(mult, cap + 1, mult):
        if total % t == 0:
            best = t
    return best


def kernel(x, c, ctx, c_ctx, w_ada, b_ada, w_in, attn_q_norm, attn_k_norm, gla_a2_f, gla_ab_f, gla_a2_b, gla_ab_b, gla_o_norm, sgu_ln_g, sgu_ln_b, sgu_w, sgu_b, w_br_attn, w_br_gla, w_br_sgu, w_out, post_ln_g, post_ln_b):
    B, n_lat, D = x.shape
    ctx_len = ctx.shape[1]
    depth = w_in.shape[0]
    T = ctx_len + n_lat
    assert D == D_MODEL and B <= 7
    assert ctx_len % ATT_TQ == 0 and n_lat % ATT_TQ == 0 and n_lat % GRID_W == 0
    alpha = (2 * depth) ** 0.25

    tm_proj = _largest_tile(T, 2112, 16)
    tn_proj = 1536
    tk_att = _largest_tile(T, ATT_TK_MAX, ATT_TQ)
    assert ctx_len <= tk_att
    tm_prep = tk_att
    tm_merge_lat = _largest_tile(n_lat, 512, SGU_CHUNK)
    tm_merge_ctx = _largest_tile(ctx_len, 512, SGU_CHUNK)

    x_ctx, x_lat = ctx, x
    cc = jnp.zeros((8, D), F32).at[0:B].set(c).at[B].set(c_ctx)
    mod = _ada(cc, w_ada, b_ada.reshape(depth, 1, 3 * D))
    cos_t, sin_t = _rope_tables(n_lat, ctx_len)
    consts = {
        "gq": jnp.kron(jnp.eye(N_HEADS, dtype=F32), jnp.ones((HEAD_DIM, HEAD_DIM), F32)).astype(BF16),
        "gk": jnp.kron(jnp.eye(N_KV, dtype=F32), jnp.ones((HEAD_DIM, HEAD_DIM), F32)).astype(BF16),
        "eq": jnp.eye(ATTN_W, dtype=BF16),
        "ek": jnp.eye(KV_W, dtype=BF16),
    }
    eye_tq = jnp.eye(ATT_TQ, dtype=BF16)
    w_all = _regroup_w_in(w_in)

    for l in range(depth):
        last = l == depth - 1
        modb = jnp.zeros((B, 8, 3 * D), F32).at[:, 0].set(mod[l, 0:B]).at[:, 1].set(mod[l, B][None])
        proj = _inproj(x_lat, x_ctx, modb, w_all, l, tm_proj, tn_proj)
        qn = jnp.concatenate([attn_q_norm[l]] * 2)[None]
        kn = jnp.concatenate([attn_k_norm[l]] * 2)[None]
        qt, kk, vt = _qkprep(proj, cos_t, sin_t, qn, kn, consts, tm_prep, tk_att)
        score_bound = HEAD_DIM * jnp.max(jnp.abs(attn_q_norm[l])) * jnp.max(jnp.abs(attn_k_norm[l])) * Q_SCALE
        bounded = (score_bound <= SCORE_LIMIT).astype(jnp.int32).reshape(1)
        ctx_blocks, lat_blocks = ctx_len // ATT_TQ, n_lat // ATT_TQ
        ya_lat = _attention(bounded, qt, kk, vt, proj, eye_tq, ctx_blocks, lat_blocks, ATT_NSB, T // tk_att, tk_att)
        ya_ctx = None if last else _attention(bounded, qt, kk, vt, proj, eye_tq, 0, ctx_blocks, 1, 1, ctx_len)
        a2f = jnp.zeros((LANES, GLA_KW), F32).at[0:GLA_RANK].set(gla_a2_f[l]).astype(BF16)
        a2b = jnp.zeros((LANES, GLA_KW), F32).at[GLA_RANK:2 * GLA_RANK].set(gla_a2_b[l]).astype(BF16)
        of, ob = _gla(proj, a2f, gla_ab_f[l][None], a2b, gla_ab_b[l][None], ctx_len)
        p = {
            "onorm": jnp.concatenate([gla_o_norm[l]] * GLA_HEADS)[None],
            "lng": sgu_ln_g[l][None], "lnb": sgu_ln_b[l][None],
            "sw": sgu_w[l].astype(BF16),
            "sb": jnp.repeat(sgu_b[l].T, SGU_W // SGU_GROUPS, axis=1),
            "wa": (0.5 * w_br_attn[l]).astype(BF16), "wg": (0.5 * w_br_gla[l]).astype(BF16),
            "ws": (0.5 * w_br_sgu[l]).astype(BF16),
            "wo": w_out[l].astype(BF16), "pg": post_ln_g[l][None], "pb": post_ln_b[l][None],
        }
        if not last:
            x_ctx = _merge(x_ctx, ya_ctx, 0, 1, modb, proj, of, ob, p, tm_merge_ctx, alpha)
        x_lat = _merge(x_lat, ya_lat, ctx_len, 0, modb, proj, of, ob, p, tm_merge_lat, alpha)
    return x_lat
```

```python
import functools

import jax
import jax.numpy as jnp
import numpy as np
from jax import lax
from jax.experimental import pallas as pl
from jax.experimental.pallas import tpu as pltpu

F32 = jnp.float32
BF16 = jnp.bfloat16

D_MODEL = 1024
GRID_W = 64
HEAD_DIM = 64
N_HEADS = 8
N_KV = 2
Q_PER_KV = N_HEADS // N_KV
ATTN_W = N_HEADS * HEAD_DIM
KV_W = N_KV * HEAD_DIM
ROPE_THETA = 10000.0
ROPE_AXIS = HEAD_DIM // 2
GLA_HEADS = 4
GLA_DK = 64
GLA_DV = 128
GLA_KW = GLA_HEADS * GLA_DK
GLA_VW = GLA_HEADS * GLA_DV
GLA_RANK = 16
GLA_TEMP = 16.0
SGU_GROUPS = 4
SGU_CHUNK = 128
SGU_W = 512
EPS = 1e-6

LANES = 128
V7X_VMEM_BYTES = 64 * 1024 * 1024

COL_MA, COL_MG, COL_MS = 0, 1024, 2048
COL_AQ, COL_AG = 3072, 3584
COL_GV, COL_GG = 4096, 4608
COL_SU, COL_SV, COL_SG = 5120, 5632, 6144
COL_GQK = 6656
COL_AKV = 7168
COL_GATE = 7424
PROJ_W = 7680

SUBLANES = 8
ATT_TQ = 256
ATT_NSB = 8
IN_PROJ_CHUNKS = 4
ATT_TK_MAX = 768
GLA_C = 128
GLA_SUB = 16
GLA_STEPS = 2
ATT_UNROLL = 4
Q_SCALE = 1.4426950408889634 * HEAD_DIM ** -0.5
SCORE_LIMIT = 96.0


def _cparams(sem, vmem_mb):
    return pltpu.CompilerParams(dimension_semantics=sem,
                                vmem_limit_bytes=min(vmem_mb * 1024 * 1024, V7X_VMEM_BYTES - (4 << 20)))


def _silu(x):
    return x * jax.nn.sigmoid(x)


def _silu_from_half(u):
    return u + u * jnp.tanh(u)


def _dot(a, b):
    return jnp.dot(a, b, preferred_element_type=F32)


def _dot_nt(a, b):
    return lax.dot_general(a, b, (((1,), (1,)), ((), ())), preferred_element_type=F32)


def _ada_kernel(c_ref, w_ref, b_ref, o_ref):
    c = c_ref[...]
    o_ref[...] = _dot(_silu(c).astype(BF16), w_ref[...].astype(BF16)) + b_ref[...]


def _ada(cc, w_ada, b_ada):
    L, D = w_ada.shape[0], w_ada.shape[1]
    return pl.pallas_call(
        _ada_kernel,
        grid=(L, 3),
        in_specs=[pl.BlockSpec((8, D), lambda l, j: (0, 0)),
                  pl.BlockSpec((None, D, D), lambda l, j: (l, 0, j)),
                  pl.BlockSpec((None, 1, D), lambda l, j: (l, 0, j))],
        out_specs=pl.BlockSpec((None, 8, D), lambda l, j: (l, 0, j)),
        out_shape=jax.ShapeDtypeStruct((L, 8, 3 * D), F32),
        compiler_params=_cparams(("arbitrary", "arbitrary"), 32),
        name="ada_mod",
    )(cc, w_ada, b_ada)


def _inproj_kernel(x_ref, ctx_ref, mod_ref, w_ref, o_ref, h_ref, *, tm, ctx_len):
    i = pl.program_id(1)
    j = pl.program_id(2)
    D = x_ref.shape[-1]

    def modulated(v, r):
        return (v * (1.0 + mod_ref[r:r + 1, D:2 * D]) + mod_ref[r:r + 1, 0:D]).astype(BF16)

    w = w_ref[...]

    def project(r0, r1, v, r):
        h = modulated(v, r)
        h_ref[r0:r1, :] = h
        o_ref[r0:r1, :] = _dot_nt(h, w).astype(BF16)

    bounds = [tm * c // IN_PROJ_CHUNKS for c in range(IN_PROJ_CHUNKS + 1)]

    @pl.when((j == 0) & (i == 0))
    def _():
        project(0, ctx_len, ctx_ref[...], 1)
        for r0, r1 in zip(bounds[:-1], bounds[1:]):
            r0, r1 = max(r0, ctx_len), max(r1, ctx_len)
            if r1 > r0:
                project(r0, r1, x_ref[0, r0 - ctx_len:r1 - ctx_len, :], 0)

    @pl.when((j == 0) & (i > 0))
    def _():
        for r0, r1 in zip(bounds[:-1], bounds[1:]):
            project(r0, r1, x_ref[0, r0:r1, :], 0)

    @pl.when(j > 0)
    def _():
        o_ref[...] = _dot_nt(h_ref[...], w).astype(BF16)


def _inproj(x, ctx, modb, w_all, layer, tm, tn):
    B, n_lat, D = x.shape
    ctx_len = ctx.shape[1]
    T = n_lat + ctx_len
    assert ctx_len < tm and T % tm == 0 and (tm - ctx_len) % SUBLANES == 0
    x_spec = pl.BlockSpec((pl.Element(1), pl.Element(tm), pl.Element(D)),
                          lambda b, i, j: (b, pl.multiple_of(jnp.maximum(i * tm - ctx_len, 0), SUBLANES), 0))
    return pl.pallas_call(
        functools.partial(_inproj_kernel, tm=tm, ctx_len=ctx_len),
        grid=(B, T // tm, PROJ_W // tn),
        in_specs=[x_spec,
                  pl.BlockSpec((None, ctx_len, D), lambda b, i, j: (b, 0, 0)),
                  pl.BlockSpec((None, 8, 3 * D), lambda b, i, j: (b, 0, 0)),
                  pl.BlockSpec((None, tn, D), lambda b, i, j: (layer, j, 0))],
        out_specs=pl.BlockSpec((None, tm, tn), lambda b, i, j: (b, i, j)),
        out_shape=jax.ShapeDtypeStruct((B, T, PROJ_W), BF16),
        scratch_shapes=[pltpu.VMEM((tm, D), BF16)],
        compiler_params=_cparams(("arbitrary", "arbitrary", "arbitrary"), 58),
        name="in_proj",
    )(x, ctx, modb, w_all)


def _rope(y, cos, sin, width):
    lane = lax.broadcasted_iota(jnp.int32, (1, width), 1)
    first = jnp.bitwise_and(lane, ROPE_AXIS - 1) < (ROPE_AXIS // 2)
    up = pltpu.roll(y, width - ROPE_AXIS // 2, 1)
    dn = pltpu.roll(y, ROPE_AXIS // 2, 1)
    return y * cos + jnp.where(first, up, dn) * sin


def _qkprep_kernel(aq_ref, akv_ref, cos_ref, sin_ref, qn_ref, kn_ref, gq_ref, gk_ref,
                   eq_ref, ek_ref, qt_ref, k_ref, vt_ref, *, tm, tk):
    cos = cos_ref[...]
    sin = sin_ref[...]
    aq = aq_ref[...].astype(F32)
    ssq = _dot((aq * aq).astype(BF16), gq_ref[...])
    yq = aq * lax.rsqrt(ssq * (1.0 / HEAD_DIM) + EPS) * jnp.concatenate([qn_ref[...]] * 4, axis=1)
    yq = _rope(yq, jnp.concatenate([cos] * 4, axis=1), jnp.concatenate([sin] * 4, axis=1), ATTN_W)
    yq = (yq * Q_SCALE).astype(BF16)
    qt = _dot_nt(eq_ref[...], yq).astype(BF16)
    zeros = jnp.zeros((HEAD_DIM, ATT_TQ), BF16)
    for g in range(N_KV):
        for qb in range(tm // ATT_TQ):
            for r in range(Q_PER_KV):
                h = g * Q_PER_KV + r
                cols = slice((qb * Q_PER_KV + r) * ATT_TQ, (qb * Q_PER_KV + r + 1) * ATT_TQ)
                qt_ref[g, g * HEAD_DIM:(g + 1) * HEAD_DIM, cols] = qt[h * HEAD_DIM:(h + 1) * HEAD_DIM,
                                                                      qb * ATT_TQ:(qb + 1) * ATT_TQ]
                qt_ref[g, (1 - g) * HEAD_DIM:(2 - g) * HEAD_DIM, cols] = zeros
    akv = akv_ref[...].astype(F32)
    ak = akv[:, 0:KV_W]
    ssk = _dot((ak * ak).astype(BF16), gk_ref[...])
    yk = ak * lax.rsqrt(ssk * (1.0 / HEAD_DIM) + EPS) * kn_ref[...]
    yk = _rope(yk, cos, sin, KV_W).astype(BF16)
    nkb = tm // tk
    k_ref[...] = yk.reshape(nkb, tk, KV_W)
    av = akv_ref[:, KV_W:2 * KV_W]
    vt = _dot_nt(ek_ref[...], av).astype(BF16)
    for g in range(N_KV):
        for n in range(nkb):
            vt_ref[g, n] = vt[g * HEAD_DIM:(g + 1) * HEAD_DIM, n * tk:(n + 1) * tk]


def _qkprep(proj, cos_t, sin_t, qn, kn, consts, tm, tk):
    B, T, _ = proj.shape
    nkb_t = T // tk
    nkb = tm // tk
    full = lambda shape: pl.BlockSpec(shape, lambda b, i: (0,) * len(shape))
    return pl.pallas_call(
        functools.partial(_qkprep_kernel, tm=tm, tk=tk),
        grid=(B, T // tm),
        in_specs=[pl.BlockSpec((None, tm, ATTN_W), lambda b, i: (b, i, COL_AQ // ATTN_W)),
                  pl.BlockSpec((None, tm, 2 * KV_W), lambda b, i: (b, i, COL_AKV // (2 * KV_W))),
                  pl.BlockSpec((tm, LANES), lambda b, i: (i, 0)),
                  pl.BlockSpec((tm, LANES), lambda b, i: (i, 0)),
                  full((1, LANES)), full((1, LANES)),
                  full((ATTN_W, ATTN_W)), full((KV_W, KV_W)),
                  full((ATTN_W, ATTN_W)), full((KV_W, KV_W))],
        out_specs=[pl.BlockSpec((None, N_KV, KV_W, Q_PER_KV * tm), lambda b, i: (b, 0, 0, i)),
                   pl.BlockSpec((None, nkb, tk, KV_W), lambda b, i: (b, i, 0, 0)),
                   pl.BlockSpec((None, N_KV, nkb, HEAD_DIM, tk), lambda b, i: (b, 0, i, 0, 0))],
        out_shape=[jax.ShapeDtypeStruct((B, N_KV, KV_W, Q_PER_KV * T), BF16),
                   jax.ShapeDtypeStruct((B, nkb_t, tk, KV_W), BF16),
                   jax.ShapeDtypeStruct((B, N_KV, nkb_t, HEAD_DIM, tk), BF16)],
        compiler_params=_cparams(("arbitrary", "arbitrary"), 40),
        name="qkv_prep",
    )(proj, proj, cos_t, sin_t, qn, kn, consts["gq"], consts["gk"], consts["eq"], consts["ek"])


def _attn_kernel(bounded_ref, *refs, nsb, n_tiles, rows):
    q_refs, (k_ref, vt_ref), ag_refs, eye_ref = refs[0:nsb], refs[nsb:nsb + 2], refs[nsb + 2:2 * nsb + 2], refs[2 * nsb + 2]
    o_ref, p_ref, acc_ref, oall_ref, qall_ref = refs[-5:]
    nq = q_refs[0].shape[-1]
    tq = nq // Q_PER_KV
    sl = acc_ref.shape[1] - HEAD_DIM
    U = p_ref.shape[0]
    acc_ref[...] = jnp.zeros_like(acc_ref)

    def stage_of(s):
        if isinstance(s, int):
            return divmod(s, n_tiles)
        return lax.div(s, n_tiles), lax.rem(s, n_tiles)

    def weights(s, slot):
        sb, j = stage_of(s)
        p = jnp.exp2(_dot(k_ref[j, 0:rows, :], qall_ref[sb]))
        p_ref[slot, 0:rows, :] = p.astype(BF16)
        acc_ref[sb, HEAD_DIM:HEAD_DIM + sl, :] += jnp.sum(p.reshape(rows // sl, sl, nq), axis=0)

    def weighted_values(s, slot):
        sb, j = stage_of(s)
        acc_ref[sb, 0:HEAD_DIM, :] += _dot(vt_ref[j, :, 0:rows], p_ref[slot, 0:rows, :])

    def run():
        stages = nsb * n_tiles
        for sb in range(nsb):
            qall_ref[sb] = q_refs[sb][...]
        weights(0, 0)

        def group(i, carry):
            for u in range(U):
                weights(U * i + u + 1, (u + 1) % U)
                weighted_values(U * i + u, u)
            return carry

        groups = (stages - 1) // U
        lax.fori_loop(0, groups, group, 0)
        for s in range(U * groups, stages):
            if s + 1 < stages:
                weights(s + 1, (s + 1) % U)
            weighted_values(s, s % U)

    def run_online(sb):
        for r in range(Q_PER_KV):
            q = q_refs[sb][:, r * tq:(r + 1) * tq]

            def body(j, carry):
                m, den, acc = carry
                s = _dot(k_ref[j, 0:rows, :], q)
                m_new = jnp.maximum(m, jnp.max(s, axis=0, keepdims=True))
                p = jnp.exp2(s - m_new)
                alpha = jnp.exp2(m - m_new)
                return (m_new, alpha * den + jnp.sum(p, axis=0, keepdims=True),
                        alpha * acc + _dot(vt_ref[j, :, 0:rows], p.astype(BF16)))

            init = (jnp.full((1, tq), -jnp.inf, F32), jnp.zeros((1, tq), F32), jnp.zeros((HEAD_DIM, tq), F32))
            _, den, acc = lax.fori_loop(0, n_tiles, body, init)
            acc_ref[sb, 0:HEAD_DIM, r * tq:(r + 1) * tq] = acc
            acc_ref[sb, HEAD_DIM:HEAD_DIM + sl, r * tq:(r + 1) * tq] = jnp.broadcast_to(den * (1.0 / sl), (sl, tq))

    def finish(sb):
        for r in range(Q_PER_KV):
            acc = acc_ref[sb, :, r * tq:(r + 1) * tq]
            den = jnp.sum(acc[HEAD_DIM:HEAD_DIM + sl, :], axis=0, keepdims=True)
            o = acc[0:HEAD_DIM, :] * (1.0 / den)
            oall_ref[sb, r * HEAD_DIM:(r + 1) * HEAD_DIM, :] = o.astype(BF16)
        o_t = _dot_nt(eye_ref[...], oall_ref[sb])
        gate = _silu_from_half(ag_refs[sb][...].astype(F32))
        o_ref[sb * tq:(sb + 1) * tq, :] = (o_t * gate).astype(BF16)

    bounded = bounded_ref[0] == 1

    @pl.when(bounded)
    def _():
        run()
        for sb in range(nsb):
            finish(sb)

    @pl.when(jnp.logical_not(bounded))
    def _():
        for sb in range(nsb):
            run_online(sb)
            finish(sb)


def _attention(bounded, qt, k, vt, proj, eye_tq, q_block0, n_qblocks, nsb, n_tiles, rows):
    B = proj.shape[0]
    nkb, tk = k.shape[1], k.shape[2]
    gw = Q_PER_KV * HEAD_DIM
    nq = Q_PER_KV * ATT_TQ
    assert n_qblocks % nsb == 0

    def qmap(sb):
        return lambda b, g, i, f: (b, g, 0, q_block0 + i * nsb + sb)

    def agmap(sb):
        return lambda b, g, i, f: (b, q_block0 + i * nsb + sb, COL_AG // gw + g)

    in_specs = ([pl.BlockSpec((None, None, KV_W, nq), qmap(sb)) for sb in range(nsb)]
                + [pl.BlockSpec((None, nkb, tk, KV_W), lambda b, g, i, f: (b, 0, 0, 0)),
                   pl.BlockSpec((None, None, nkb, HEAD_DIM, tk), lambda b, g, i, f: (b, g, 0, 0, 0))]
                + [pl.BlockSpec((None, ATT_TQ, gw), agmap(sb)) for sb in range(nsb)]
                + [pl.BlockSpec((ATT_TQ, ATT_TQ), lambda b, g, i, f: (0, 0))])
    args = [bounded] + [qt] * nsb + [k, vt] + [proj] * nsb + [eye_tq]
    grid_spec = pltpu.PrefetchScalarGridSpec(
        num_scalar_prefetch=1,
        grid=(B, N_KV, n_qblocks // nsb),
        in_specs=in_specs,
        out_specs=pl.BlockSpec((None, nsb * ATT_TQ, gw), lambda b, g, i, f: (b, i, g)),
        scratch_shapes=[pltpu.VMEM((ATT_UNROLL, tk, nq), BF16),
                        pltpu.VMEM((nsb, HEAD_DIM + SUBLANES, nq), F32),
                        pltpu.VMEM((nsb, gw, ATT_TQ), BF16),
                        pltpu.VMEM((nsb, KV_W, nq), BF16)])
    return pl.pallas_call(
        functools.partial(_attn_kernel, nsb=nsb, n_tiles=n_tiles, rows=rows),
        grid_spec=grid_spec,
        out_shape=jax.ShapeDtypeStruct((B, n_qblocks * ATT_TQ, ATTN_W), BF16),
        compiler_params=_cparams(("arbitrary", "arbitrary", "arbitrary"), 40),
        name="gqa_attention",
    )(*args)


def _gla_direction(qk, v, pa, a2, ab, st_ref, o_ref, fwd):
    C, c = GLA_C, GLA_SUB
    q = qk[:, 0:GLA_KW].astype(F32) * (GLA_DK ** -0.5)
    k = qk[:, GLA_KW:2 * GLA_KW].astype(F32)
    z = _dot(pa, a2) + ab
    yield
    g =(jnp.minimum(z, 0.0) - jnp.log(1.0 + jnp.exp(-jnp.abs(z)))) * (1.0 / GLA_TEMP)
    row = lax.broadcasted_iota(jnp.int32, (C, C), 0)
    col = lax.broadcasted_iota(jnp.int32, (C, C), 1)
    blk = jnp.bitwise_and(row, -c)
    if fwd:
        tri = col <= row
        inblk = tri & (col >= blk)
    else:
        tri = col >= row
        inblk = tri & (col < blk + c)
    sel = jnp.concatenate([jnp.where(tri, 1.0, 0.0), jnp.where(inblk, 1.0, 0.0)], axis=0).astype(BF16)
    g_hi = g.astype(BF16)
    g_lo = (g - g_hi.astype(F32)).astype(BF16)
    cs = _dot(sel, jnp.concatenate([g_hi, g_lo], axis=1))
    yield
    cs =cs[:, 0:GLA_KW] + cs[:, GLA_KW:2 * GLA_KW]
    b = cs[0:C]
    d = cs[C:2 * C]
    ref = b - d
    total = b[C - 1:C, :] if fwd else b[0:1, :]
    q_in = (q * jnp.exp(d)).astype(BF16)
    q_st = (q * jnp.exp(b)).astype(BF16)
    k_out = (k * jnp.exp(total - b)).astype(BF16)
    w_tot = jnp.exp(total)
    zero = jnp.zeros((), BF16)
    nb = C // c
    k_blk = []
    for i in range(nb):
        lo, hi = (0, (i + 1) * c) if fwd else (i * c, C)
        kb = (k[lo:hi] * jnp.exp(ref[i * c:i * c + 1, :] - b[lo:hi])).astype(BF16)
        pad = jnp.zeros((C - (hi - lo), GLA_KW), BF16)
        k_blk.append(jnp.concatenate(([kb, pad] if fwd else [pad, kb]) if hi - lo < C else [kb], axis=0))
    head_of_lane = jnp.right_shift(lax.broadcasted_iota(jnp.int32, (1, GLA_KW), 1), GLA_DK.bit_length() - 1)
    a_two = []
    for m in range(nb // 2):
        qr = q_in[2 * m * c:(2 * m + 2) * c]
        q_heads = jnp.concatenate([jnp.where(head_of_lane == h, qr, zero) for h in range(GLA_HEADS)], axis=0)
        a_two.append(_dot_nt(q_heads, jnp.concatenate(k_blk[2 * m:2 * m + 2], axis=0)))
    yield

    def a_block(h, i):
        m, r = divmod(i, 2)
        return a_two[m][h * 2 * c + r * c:h * 2 * c + (r + 1) * c, r * C:(r + 1) * C]

    a_cat = jnp.concatenate(
        [jnp.where(tri, jnp.concatenate([a_block(h, i) for i in range(nb)], axis=0), 0.0)
         for h in range(GLA_HEADS)], axis=1).astype(BF16)
    vhead_of_lane = jnp.right_shift(lax.broadcasted_iota(jnp.int32, (1, GLA_VW), 1), GLA_DV.bit_length() - 1)
    v_bd = jnp.concatenate([jnp.where(vhead_of_lane == h, v, zero) for h in range(GLA_HEADS)], axis=0)
    st = st_ref[...]
    o = _dot(a_cat, v_bd) + _dot_nt(q_st, st.astype(BF16))
    upd = lax.dot_general(v, k_out, (((0,), (0,)), ((), ())), preferred_element_type=F32)
    same_head = jnp.right_shift(lax.broadcasted_iota(jnp.int32, (GLA_VW, 1), 0),
                                GLA_DV.bit_length() - 1) == head_of_lane
    st_ref[...] = jnp.where(same_head, st * w_tot + upd, 0.0)
    o_ref[...] = o
    yield


def _gla_kernel(qkf_ref, vf_ref, paf_ref, qkb_ref, vb_ref, pab_ref, a2f_ref, abf_ref, a2b_ref, abb_ref,
                of_ref, ob_ref, stf_ref, stb_ref):
    @pl.when(pl.program_id(0) == 0)
    def _():
        stf_ref[...] = jnp.zeros_like(stf_ref)
        stb_ref[...] = jnp.zeros_like(stb_ref)

    C = GLA_C
    waves = []
    for t in range(GLA_STEPS):
        fs = slice(t * C, (t + 1) * C)
        bs = slice((GLA_STEPS - 1 - t) * C, (GLA_STEPS - t) * C)
        chains = []
        for bi in range(qkf_ref.shape[0]):
            chains.append(_gla_direction(qkf_ref[bi, fs], vf_ref[bi, fs], paf_ref[bi, fs], a2f_ref[...], abf_ref[...],
                                         stf_ref.at[bi], of_ref.at[bi, fs], True))
            chains.append(_gla_direction(qkb_ref[bi, bs], vb_ref[bi, bs], pab_ref[bi, bs], a2b_ref[...], abb_ref[...],
                                         stb_ref.at[bi], ob_ref.at[bi, bs], False))
        waves.append(chains)
    for _ in range(3):
        for chains in waves:
            for chain in chains:
                next(chain)
    for chains in waves:
        for chain in chains:
            next(chain)


def _gla(proj, a2f, abf, a2b, abb, ctx_len):
    B, T = proj.shape[0], proj.shape[1]
    C = GLA_C * GLA_STEPS
    assert T % C == 0 and ctx_len % C == 0
    nch, nctx = T // C, ctx_len // C

    def cb(s):
        return jnp.where(s < nctx, nctx - 1 - s, nch + nctx - 1 - s)

    full = lambda shape: pl.BlockSpec(shape, lambda s: (0,) * len(shape))
    return pl.pallas_call(
        _gla_kernel,
        grid=(nch,),
        in_specs=[pl.BlockSpec((B, C, 2 * GLA_KW), lambda s: (0, s, COL_GQK // (2 * GLA_KW))),
                  pl.BlockSpec((B, C, GLA_VW), lambda s: (0, s, COL_GV // GLA_VW)),
                  pl.BlockSpec((B, C, LANES), lambda s: (0, s, COL_GATE // LANES)),
                  pl.BlockSpec((B, C, 2 * GLA_KW), lambda s: (0, cb(s), COL_GQK // (2 * GLA_KW))),
                  pl.BlockSpec((B, C, GLA_VW), lambda s: (0, cb(s), COL_GV // GLA_VW)),
                  pl.BlockSpec((B, C, LANES), lambda s: (0, cb(s), COL_GATE // LANES)),
                  full((LANES, GLA_KW)), full((1, GLA_KW)), full((LANES, GLA_KW)), full((1, GLA_KW))],
        out_specs=[pl.BlockSpec((B, C, GLA_VW), lambda s: (0, s, 0)),
                   pl.BlockSpec((B, C, GLA_VW), lambda s: (0, cb(s), 0))],
        out_shape=[jax.ShapeDtypeStruct((B, T, GLA_VW), F32),
                   jax.ShapeDtypeStruct((B, T, GLA_VW), F32)],
        scratch_shapes=[pltpu.VMEM((B, GLA_VW, GLA_KW), F32),
                        pltpu.VMEM((B, GLA_VW, GLA_KW), F32)],
        compiler_params=_cparams(("arbitrary",), 32),
        name="gla_scan",
    )(proj, proj, proj, proj, proj, proj, a2f, abf, a2b, abb)


def _merge_kernel(x_ref, mod_ref, ma_ref, mg_ref, ms_ref, su_ref, sv_ref, sg_ref, gg_ref, ya_ref, of_ref, ob_ref,
                  onorm_ref, lng_ref, lnb_ref, sw_ref, sb_ref, wa_ref, wg_ref, ws_ref, wo_ref, pg_ref, pb_ref,
                  o_ref, *, tm, mod_row, alpha):
    D = x_ref.shape[-1]
    ma_ref, mg_ref, ms_ref, su_ref, sv_ref, sg_ref, gg_ref, of_ref, ob_ref = (
        r.at[0] for r in (ma_ref, mg_ref, ms_ref, su_ref, sv_ref, sg_ref, gg_ref, of_ref, ob_ref))
    og = of_ref[...] + ob_ref[...]
    parts = []
    for h in range(GLA_HEADS):
        oh = og[:, h * GLA_DV:(h + 1) * GLA_DV]
        ms = jnp.mean(oh * oh, axis=-1, keepdims=True)
        parts.append(oh * lax.rsqrt(ms + EPS))
    yg = jnp.concatenate(parts, axis=1) * onorm_ref[...] * _silu_from_half(gg_ref[...]).astype(F32)
    sv = sv_ref[...].astype(F32)
    mu = jnp.mean(sv, axis=-1, keepdims=True)
    dv = sv - mu
    var = jnp.mean(dv * dv, axis=-1, keepdims=True)
    vn = (dv * lax.rsqrt(var + EPS) * lng_ref[...] + lnb_ref[...]).astype(BF16)
    gw = SGU_W // SGU_GROUPS
    rows = []
    for n in range(tm // SGU_CHUNK):
        cols = []
        for g in range(SGU_GROUPS):
            blk = vn[n * SGU_CHUNK:(n + 1) * SGU_CHUNK, g * gw:(g + 1) * gw]
            cols.append(_dot(sw_ref[g], blk))
        rows.append(jnp.concatenate(cols, axis=1) + sb_ref[...])
    mixed = jnp.concatenate(rows, axis=0)
    ys = su_ref[...].astype(F32) * mixed * _silu_from_half(sg_ref[...]).astype(F32)

    def gated(g_ref, y, w_ref):
        half = _dot(y, w_ref[...]).astype(BF16)
        return half + half * jnp.tanh(g_ref[...])

    m = (gated(ma_ref, ya_ref[...], wa_ref) + gated(mg_ref, yg.astype(BF16), wg_ref)
         + gated(ms_ref, ys.astype(BF16), ws_ref))
    out = _dot(m, wo_ref[...])
    r = alpha * x_ref[...] + mod_ref[mod_row:mod_row + 1, 2 * D:3 * D] * out
    mu = jnp.mean(r, axis=-1, keepdims=True)
    dr = r - mu
    var = jnp.mean(dr * dr, axis=-1, keepdims=True)
    o_ref[...] = dr * lax.rsqrt(var + EPS) * pg_ref[...] + pb_ref[...]


def _merge(x_seg, ya_seg, row0, mod_row, modb, proj, of, ob, p, tm, alpha):
    B, n_rows, D = x_seg.shape
    assert n_rows % tm == 0 and row0 % 16 == 0

    def stream_rows(w, col0):
        return pl.BlockSpec((pl.Element(1), pl.Element(tm), pl.Element(w)),
                            lambda b, i: (b, pl.multiple_of(row0 + i * tm, 16), col0))

    seg_rows = lambda w: pl.BlockSpec((None, tm, w), lambda b, i: (b, i, 0))
    full = lambda shape: pl.BlockSpec(shape, lambda b, i: (0,) * len(shape))
    return pl.pallas_call(
        functools.partial(_merge_kernel, tm=tm, mod_row=mod_row, alpha=alpha),
        grid=(B, n_rows // tm),
        in_specs=[seg_rows(D),
                  pl.BlockSpec((None, 8, 3 * D), lambda b, i: (b, 0, 0)),
                  stream_rows(D, COL_MA), stream_rows(D, COL_MG), stream_rows(D, COL_MS),
                  stream_rows(SGU_W, COL_SU), stream_rows(SGU_W, COL_SV), stream_rows(SGU_W, COL_SG),
                  stream_rows(GLA_VW, COL_GG),
                  seg_rows(ATTN_W), stream_rows(GLA_VW, 0), stream_rows(GLA_VW, 0),
                  full((1, GLA_VW)), full((1, SGU_W)), full((1, SGU_W)),
                  full((SGU_GROUPS, SGU_CHUNK, SGU_CHUNK)), full((SGU_CHUNK, SGU_W)),
                  full((ATTN_W, D)), full((GLA_VW, D)), full((SGU_W, D)), full((D, D)),
                  full((1, D)), full((1, D))],
        out_specs=pl.BlockSpec((None, tm, D), lambda b, i: (b, i, 0)),
        out_shape=jax.ShapeDtypeStruct((B, n_rows, D), F32),
        compiler_params=_cparams(("arbitrary", "arbitrary"), 56),
        name="merge_out",
    )(x_seg, modb, proj, proj, proj, proj, proj, proj, proj, ya_seg, of, ob,
      p["onorm"], p["lng"], p["lnb"], p["sw"], p["sb"], p["wa"], p["wg"], p["ws"], p["wo"], p["pg"], p["pb"])


W_BLK = 256
GATE_SHIFT = 2 * GLA_RANK


def _w_in_plan():
    sizes = (512, 128, 128, 512, 256, 256, 512, 16, 16, 512, 512, 512, 512, 1024, 1024, 1024)
    names = ("aq", "ak", "av", "ag", "gq", "gk", "gv", "af", "ab", "gg", "su", "sv", "sg", "ma", "mg", "ms")
    halved = ("ag", "gg", "sg", "ma", "mg", "ms")
    src, off = {}, 0
    for n, s in zip(names, sizes):
        src[n] = (off, s)
        off += s
    plan = []
    for n in ("ma", "mg", "ms", "aq", "ag", "gv", "gg", "su", "sv", "sg", "gq", "gk"):
        start, size = src[n]
        shift = start % W_BLK
        assert shift in (0, GATE_SHIFT) and size % W_BLK == 0
        for t in range(size // W_BLK):
            plan.append(((start - shift) // W_BLK + t, 1 if shift else 0, int(n in halved)))
    assert src["ak"][0] % W_BLK == 0 and src["av"][0] == src["ak"][0] + KV_W and 2 * KV_W == W_BLK
    plan.append((src["ak"][0] // W_BLK, 0, 0))
    assert src["af"][0] % W_BLK == 0 and src["ab"][0] == src["af"][0] + GLA_RANK
    plan.append((src["af"][0] // W_BLK, 2, 0))
    assert len(plan) * W_BLK == PROJ_W
    return plan, off


def _wprep_kernel(src_ref, nxt_ref, kind_ref, half_ref, a_ref, b_ref, o_ref):
    j = pl.program_id(1)
    a = a_ref[...]
    ab = jnp.concatenate([a, b_ref[...]], axis=0)
    shifted = ab[GATE_SHIFT:GATE_SHIFT + W_BLK, :]
    row = lax.broadcasted_iota(jnp.int32, (W_BLK, 1), 0)
    gates = jnp.where(row < GATE_SHIFT, a, 0.0)
    kind = kind_ref[j]
    val = jnp.where(kind == 1, shifted, jnp.where(kind == 2, gates, a))
    o_ref[...] = (val * jnp.where(half_ref[j] == 1, 0.5, 1.0)).astype(BF16)


def _regroup_w_in(w_in):
    L, D, width = w_in.shape
    plan, used = _w_in_plan()
    assert used == width
    last_blk = (width - 1) // W_BLK
    src = jnp.array([p[0] for p in plan], jnp.int32)
    kind = jnp.array([p[1] for p in plan], jnp.int32)
    half = jnp.array([p[2] for p in plan], jnp.int32)
    nxt, cur = [], min(plan[0][0] + 1, last_blk)
    for s, k, _ in plan:
        cur = min(s + 1, last_blk) if k == 1 else cur
        nxt.append(cur)
    nxt = jnp.array(nxt, jnp.int32)
    wt = jnp.swapaxes(w_in, 1, 2)
    grid_spec = pltpu.PrefetchScalarGridSpec(
        num_scalar_prefetch=4,
        grid=(L, len(plan)),
        in_specs=[pl.BlockSpec((None, W_BLK, D), lambda l, j, s, n, k, h: (l, s[j], 0)),
                  pl.BlockSpec((None, W_BLK, D), lambda l, j, s, n, k, h: (l, n[j], 0))],
        out_specs=pl.BlockSpec((None, W_BLK, D), lambda l, j, s, n, k, h: (l, j, 0)))
    return pl.pallas_call(
        _wprep_kernel,
        grid_spec=grid_spec,
        out_shape=jax.ShapeDtypeStruct((L, PROJ_W, D), BF16),
        compiler_params=_cparams(("arbitrary", "arbitrary"), 32),
        name="w_in_prep",
    )(src, nxt, kind, half, wt, wt)


def _rope_tables(n_lat, ctx_len):
    rows = n_lat // GRID_W
    row = np.repeat(np.arange(rows, dtype=np.float64), GRID_W)
    col = np.tile(np.arange(GRID_W, dtype=np.float64), rows)
    freqs = ROPE_THETA ** (-np.arange(0, ROPE_AXIS, 2, dtype=np.float64) / ROPE_AXIS)
    ang_r, ang_c = row[:, None] * freqs, col[:, None] * freqs
    cos = np.concatenate([np.cos(ang_r)] * 2 + [np.cos(ang_c)] * 2, axis=1)
    sin = np.concatenate([-np.sin(ang_r), np.sin(ang_r), -np.sin(ang_c), np.sin(ang_c)], axis=1)
    cos = np.concatenate([np.ones((ctx_len, HEAD_DIM)), cos], axis=0)
    sin = np.concatenate([np.zeros((ctx_len, HEAD_DIM)), sin], axis=0)
    return (jnp.asarray(np.concatenate([cos, cos], axis=1), F32),
            jnp.asarray(np.concatenate([sin, sin], axis=1), F32))


def _largest_tile(total, cap, mult):
    best = mult
    for t in range(mult, cap + 1, mult):
        if total % t == 0:
            best = t
    return best


def kernel(x, c, ctx, c_ctx, w_ada, b_ada, w_in, attn_q_norm, attn_k_norm, gla_a2_f, gla_ab_f, gla_a2_b, gla_ab_b, gla_o_norm, sgu_ln_g, sgu_ln_b, sgu_w, sgu_b, w_br_attn, w_br_gla, w_br_sgu, w_out, post_ln_g, post_ln_b):
    B, n_lat, D = x.shape
    ctx_len = ctx.shape[1]
    depth = w_in.shape[0]
    T = ctx_len + n_lat
    assert D == D_MODEL and B <= 7
    assert ctx_len % ATT_TQ == 0 and n_lat % ATT_TQ == 0 and n_lat % GRID_W == 0
    alpha = (2 * depth) ** 0.25

    tm_proj = _largest_tile(T, 2112, 16)
    tn_proj = 1536
    tk_att = _largest_tile(T, ATT_TK_MAX, ATT_TQ)
    assert ctx_len <= tk_att
    tm_prep = tk_att
    tm_merge_lat = _largest_tile(n_lat, 512, SGU_CHUNK)
    tm_merge_ctx = _largest_tile(ctx_len, 512, SGU_CHUNK)

    x_ctx, x_lat = ctx, x
    cc = jnp.zeros((8, D), F32).at[0:B].set(c).at[B].set(c_ctx)
    mod = _ada(cc, w_ada, b_ada.reshape(depth, 1, 3 * D))
    cos_t, sin_t = _rope_tables(n_lat, ctx_len)
    consts = {
        "gq": jnp.kron(jnp.eye(N_HEADS, dtype=F32), jnp.ones((HEAD_DIM, HEAD_DIM), F32)).astype(BF16),
        "gk": jnp.kron(jnp.eye(N_KV, dtype=F32), jnp.ones((HEAD_DIM, HEAD_DIM), F32)).astype(BF16),
        "eq": jnp.eye(ATTN_W, dtype=BF16),
        "ek": jnp.eye(KV_W, dtype=BF16),
    }
    eye_tq = jnp.eye(ATT_TQ, dtype=BF16)
    w_all = _regroup_w_in(w_in)

    for l in range(depth):
        last = l == depth - 1
        modb = jnp.zeros((B, 8, 3 * D), F32).at[:, 0].set(mod[l, 0:B]).at[:, 1].set(mod[l, B][None])
        proj = _inproj(x_lat, x_ctx, modb, w_all, l, tm_proj, tn_proj)
        qn = jnp.concatenate([attn_q_norm[l]] * 2)[None]
        kn = jnp.concatenate([attn_k_norm[l]] * 2)[None]
        qt, kk, vt = _qkprep(proj, cos_t, sin_t, qn, kn, consts, tm_prep, tk_att)
        score_bound = HEAD_DIM * jnp.max(jnp.abs(attn_q_norm[l])) * jnp.max(jnp.abs(attn_k_norm[l])) * Q_SCALE
        bounded = (score_bound <= SCORE_LIMIT).astype(jnp.int32).reshape(1)
        ctx_blocks, lat_blocks = ctx_len // ATT_TQ, n_lat // ATT_TQ
        ya_lat = _attention(bounded, qt, kk, vt, proj, eye_tq, ctx_blocks, lat_blocks, ATT_NSB, T // tk_att, tk_att)
        ya_ctx = None if last else _attention(bounded, qt, kk, vt, proj, eye_tq, 0, ctx_blocks, 1, 1, ctx_len)
        a2f = jnp.zeros((LANES, GLA_KW), F32).at[0:GLA_RANK].set(gla_a2_f[l]).astype(BF16)
        a2b = jnp.zeros((LANES, GLA_KW), F32).at[GLA_RANK:2 * GLA_RANK].set(gla_a2_b[l]).astype(BF16)
        of, ob = _gla(proj, a2f, gla_ab_f[l][None], a2b, gla_ab_b[l][None], ctx_len)
        p = {
            "onorm": jnp.concatenate([gla_o_norm[l]] * GLA_HEADS)[None],
            "lng": sgu_ln_g[l][None], "lnb": sgu_ln_b[l][None],
            "sw": sgu_w[l].astype(BF16),
            "sb": jnp.repeat(sgu_b[l].T, SGU_W // SGU_GROUPS, axis=1),
            "wa": (0.5 * w_br_attn[l]).astype(BF16), "wg": (0.5 * w_br_gla[l]).astype(BF16),
            "ws": (0.5 * w_br_sgu[l]).astype(BF16),
            "wo": w_out[l].astype(BF16), "pg": post_ln_g[l][None], "pb": post_ln_b[l][None],
        }
        if not last:
            x_ctx = _merge(x_ctx, ya_ctx, 0, 1, modb, proj, of, ob, p, tm_merge_ctx, alpha)
        x_lat = _merge(x_lat, ya_lat, ctx_len, 0, modb, proj, of, ob, p, tm_merge_lat, alpha)
    return x_lat
```

```python
import functools

import jax
import jax.numpy as jnp
import numpy as np
from jax import lax
from jax.experimental import pallas as pl
from jax.experimental.pallas import tpu as pltpu

F32 = jnp.float32
BF16 = jnp.bfloat16

D_MODEL = 1024
GRID_W = 64
HEAD_DIM = 64
N_HEADS = 8
N_KV = 2
Q_PER_KV = N_HEADS // N_KV
ATTN_W = N_HEADS * HEAD_DIM
KV_W = N_KV * HEAD_DIM
ROPE_THETA = 10000.0
ROPE_AXIS = HEAD_DIM // 2
GLA_HEADS = 4
GLA_DK = 64
GLA_DV = 128
GLA_KW = GLA_HEADS * GLA_DK
GLA_VW = GLA_HEADS * GLA_DV
GLA_RANK = 16
GLA_TEMP = 16.0
SGU_GROUPS = 4
SGU_CHUNK = 128
SGU_W = 512
EPS = 1e-6

LANES = 128
V7X_VMEM_BYTES = 64 * 1024 * 1024

COL_MA, COL_MG, COL_MS = 0, 1024, 2048
COL_AQ, COL_AG = 3072, 3584
COL_GV, COL_GG = 4096, 4608
COL_SU, COL_SV, COL_SG = 5120, 5632, 6144
COL_GQK = 6656
COL_AKV = 7168
COL_GATE = 7424
PROJ_W = 7680

SUBLANES = 8
ATT_TQ = 256
ATT_NSB = 4
ATT_TK_MAX = 768
GLA_C = 128
GLA_SUB = 16
GLA_STEPS = 2
ATT_UNROLL = 4
Q_SCALE = 1.4426950408889634 * HEAD_DIM ** -0.5
SCORE_LIMIT = 96.0


def _cparams(sem, vmem_mb):
    return pltpu.CompilerParams(dimension_semantics=sem,
                                vmem_limit_bytes=min(vmem_mb * 1024 * 1024, V7X_VMEM_BYTES - (4 << 20)))


def _silu(x):
    return x * jax.nn.sigmoid(x)


def _silu_from_half(u):
    return u + u * jnp.tanh(u)


def _dot(a, b):
    return jnp.dot(a, b, preferred_element_type=F32)


def _dot_nt(a, b):
    return lax.dot_general(a, b, (((1,), (1,)), ((), ())), preferred_element_type=F32)


def _ada_kernel(c_ref, w_ref, b_ref, o_ref):
    c = c_ref[...]
    o_ref[...] = _dot(_silu(c).astype(BF16), w_ref[...].astype(BF16)) + b_ref[...]


def _ada(cc, w_ada, b_ada):
    L, D = w_ada.shape[0], w_ada.shape[1]
    return pl.pallas_call(
        _ada_kernel,
        grid=(L, 3),
        in_specs=[pl.BlockSpec((8, D), lambda l, j: (0, 0)),
                  pl.BlockSpec((None, D, D), lambda l, j: (l, 0, j)),
                  pl.BlockSpec((None, 1, D), lambda l, j: (l, 0, j))],
        out_specs=pl.BlockSpec((None, 8, D), lambda l, j: (l, 0, j)),
        out_shape=jax.ShapeDtypeStruct((L, 8, 3 * D), F32),
        compiler_params=_cparams(("arbitrary", "arbitrary"), 32),
        name="ada_mod",
    )(cc, w_ada, b_ada)


def _inproj_kernel(x_ref, ctx_ref, mod_ref, w_ref, o_ref, h_ref, *, tm, ctx_len):
    i = pl.program_id(1)
    j = pl.program_id(2)
    D = x_ref.shape[-1]

    def modulated(v, r):
        return (v * (1.0 + mod_ref[r:r + 1, D:2 * D]) + mod_ref[r:r + 1, 0:D]).astype(BF16)

    @pl.when((j == 0) & (i == 0))
    def _():
        h_ref[0:ctx_len, :] = modulated(ctx_ref[...], 1)
        h_ref[ctx_len:tm, :] = modulated(x_ref[0, 0:tm - ctx_len, :], 0)

    @pl.when((j == 0) & (i > 0))
    def _():
        h_ref[...] = modulated(x_ref[0], 0)

    o_ref[...] = _dot_nt(h_ref[...], w_ref[...]).astype(BF16)


def _inproj(x, ctx, modb, w_all, layer, tm, tn):
    B, n_lat, D = x.shape
    ctx_len = ctx.shape[1]
    T = n_lat + ctx_len
    assert ctx_len < tm and T % tm == 0 and (tm - ctx_len) % SUBLANES == 0
    x_spec = pl.BlockSpec((pl.Element(1), pl.Element(tm), pl.Element(D)),
                          lambda b, i, j: (b, pl.multiple_of(jnp.maximum(i * tm - ctx_len, 0), SUBLANES), 0))
    return pl.pallas_call(
        functools.partial(_inproj_kernel, tm=tm, ctx_len=ctx_len),
        grid=(B, T // tm, PROJ_W // tn),
        in_specs=[x_spec,
                  pl.BlockSpec((None, ctx_len, D), lambda b, i, j: (b, 0, 0)),
                  pl.BlockSpec((None, 8, 3 * D), lambda b, i, j: (b, 0, 0)),
                  pl.BlockSpec((None, tn, D), lambda b, i, j: (layer, j, 0))],
        out_specs=pl.BlockSpec((None, tm, tn), lambda b, i, j: (b, i, j)),
        out_shape=jax.ShapeDtypeStruct((B, T, PROJ_W), BF16),
        scratch_shapes=[pltpu.VMEM((tm, D), BF16)],
        compiler_params=_cparams(("arbitrary", "arbitrary", "arbitrary"), 52),
        name="in_proj",
    )(x, ctx, modb, w_all)


def _rope(y, cos, sin, width):
    lane = lax.broadcasted_iota(jnp.int32, (1, width), 1)
    first = jnp.bitwise_and(lane, ROPE_AXIS - 1) < (ROPE_AXIS // 2)
    up = pltpu.roll(y, width - ROPE_AXIS // 2, 1)
    dn = pltpu.roll(y, ROPE_AXIS // 2, 1)
    return y * cos + jnp.where(first, up, dn) * sin


def _qkprep_kernel(aq_ref, akv_ref, cos_ref, sin_ref, qn_ref, kn_ref, gq_ref, gk_ref,
                   eq_ref, ek_ref, qt_ref, k_ref, vt_ref, *, tm, tk):
    cos = cos_ref[...]
    sin = sin_ref[...]
    aq = aq_ref[...].astype(F32)
    ssq = _dot((aq * aq).astype(BF16), gq_ref[...])
    yq = aq * lax.rsqrt(ssq * (1.0 / HEAD_DIM) + EPS) * jnp.concatenate([qn_ref[...]] * 4, axis=1)
    yq = _rope(yq, jnp.concatenate([cos] * 4, axis=1), jnp.concatenate([sin] * 4, axis=1), ATTN_W)
    yq = (yq * Q_SCALE).astype(BF16)
    qt = _dot_nt(eq_ref[...], yq).astype(BF16)
    zeros = jnp.zeros((HEAD_DIM, ATT_TQ), BF16)
    for g in range(N_KV):
        for qb in range(tm // ATT_TQ):
            for r in range(Q_PER_KV):
                h = g * Q_PER_KV + r
                cols = slice((qb * Q_PER_KV + r) * ATT_TQ, (qb * Q_PER_KV + r + 1) * ATT_TQ)
                qt_ref[g, g * HEAD_DIM:(g + 1) * HEAD_DIM, cols] = qt[h * HEAD_DIM:(h + 1) * HEAD_DIM,
                                                                      qb * ATT_TQ:(qb + 1) * ATT_TQ]
                qt_ref[g, (1 - g) * HEAD_DIM:(2 - g) * HEAD_DIM, cols] = zeros
    akv = akv_ref[...].astype(F32)
    ak = akv[:, 0:KV_W]
    ssk = _dot((ak * ak).astype(BF16), gk_ref[...])
    yk = ak * lax.rsqrt(ssk * (1.0 / HEAD_DIM) + EPS) * kn_ref[...]
    yk = _rope(yk, cos, sin, KV_W).astype(BF16)
    nkb = tm // tk
    k_ref[...] = yk.reshape(nkb, tk, KV_W)
    av = akv_ref[:, KV_W:2 * KV_W]
    vt = _dot_nt(ek_ref[...], av).astype(BF16)
    for g in range(N_KV):
        for n in range(nkb):
            vt_ref[g, n] = vt[g * HEAD_DIM:(g + 1) * HEAD_DIM, n * tk:(n + 1) * tk]


def _qkprep(proj, cos_t, sin_t, qn, kn, consts, tm, tk):
    B, T, _ = proj.shape
    nkb_t = T // tk
    nkb = tm // tk
    full = lambda shape: pl.BlockSpec(shape, lambda b, i: (0,) * len(shape))
    return pl.pallas_call(
        functools.partial(_qkprep_kernel, tm=tm, tk=tk),
        grid=(B, T // tm),
        in_specs=[pl.BlockSpec((None, tm, ATTN_W), lambda b, i: (b, i, COL_AQ // ATTN_W)),
                  pl.BlockSpec((None, tm, 2 * KV_W), lambda b, i: (b, i, COL_AKV // (2 * KV_W))),
                  pl.BlockSpec((tm, LANES), lambda b, i: (i, 0)),
                  pl.BlockSpec((tm, LANES), lambda b, i: (i, 0)),
                  full((1, LANES)), full((1, LANES)),
                  full((ATTN_W, ATTN_W)), full((KV_W, KV_W)),
                  full((ATTN_W, ATTN_W)), full((KV_W, KV_W))],
        out_specs=[pl.BlockSpec((None, N_KV, KV_W, Q_PER_KV * tm), lambda b, i: (b, 0, 0, i)),
                   pl.BlockSpec((None, nkb, tk, KV_W), lambda b, i: (b, i, 0, 0)),
                   pl.BlockSpec((None, N_KV, nkb, HEAD_DIM, tk), lambda b, i: (b, 0, i, 0, 0))],
        out_shape=[jax.ShapeDtypeStruct((B, N_KV, KV_W, Q_PER_KV * T), BF16),
                   jax.ShapeDtypeStruct((B, nkb_t, tk, KV_W), BF16),
                   jax.ShapeDtypeStruct((B, N_KV, nkb_t, HEAD_DIM, tk), BF16)],
        compiler_params=_cparams(("arbitrary", "arbitrary"), 40),
        name="qkv_prep",
    )(proj, proj, cos_t, sin_t, qn, kn, consts["gq"], consts["gk"], consts["eq"], consts["ek"])


def _attn_kernel(bounded_ref, *refs, nsb, n_tiles, rows):
    q_refs, (k_ref, vt_ref), ag_refs, eye_ref = refs[0:nsb], refs[nsb:nsb + 2], refs[nsb + 2:2 * nsb + 2], refs[2 * nsb + 2]
    o_ref, p_ref, acc_ref, oall_ref, qall_ref = refs[-5:]
    nq = q_refs[0].shape[-1]
    tq = nq // Q_PER_KV
    sl = acc_ref.shape[1] - HEAD_DIM
    U = p_ref.shape[0]
    acc_ref[...] = jnp.zeros_like(acc_ref)

    def stage_of(s):
        if isinstance(s, int):
            return divmod(s, n_tiles)
        return lax.div(s, n_tiles), lax.rem(s, n_tiles)

    def weights(s, slot):
        sb, j = stage_of(s)
        p = jnp.exp2(_dot(k_ref[j, 0:rows, :], qall_ref[sb]))
        p_ref[slot, 0:rows, :] = p.astype(BF16)
        acc_ref[sb, HEAD_DIM:HEAD_DIM + sl, :] += jnp.sum(p.reshape(rows // sl, sl, nq), axis=0)

    def weighted_values(s, slot):
        sb, j = stage_of(s)
        acc_ref[sb, 0:HEAD_DIM, :] += _dot(vt_ref[j, :, 0:rows], p_ref[slot, 0:rows, :])

    def run():
        stages = nsb * n_tiles
        for sb in range(nsb):
            qall_ref[sb] = q_refs[sb][...]
        weights(0, 0)

        def group(i, carry):
            for u in range(U):
                weights(U * i + u + 1, (u + 1) % U)
                weighted_values(U * i + u, u)
            return carry

        groups = (stages - 1) // U
        lax.fori_loop(0, groups, group, 0)
        for s in range(U * groups, stages):
            if s + 1 < stages:
                weights(s + 1, (s + 1) % U)
            weighted_values(s, s % U)

    def run_online(sb):
        for r in range(Q_PER_KV):
            q = q_refs[sb][:, r * tq:(r + 1) * tq]

            def body(j, carry):
                m, den, acc = carry
                s = _dot(k_ref[j, 0:rows, :], q)
                m_new = jnp.maximum(m, jnp.max(s, axis=0, keepdims=True))
                p = jnp.exp2(s - m_new)
                alpha = jnp.exp2(m - m_new)
                return (m_new, alpha * den + jnp.sum(p, axis=0, keepdims=True),
                        alpha * acc + _dot(vt_ref[j, :, 0:rows], p.astype(BF16)))

            init = (jnp.full((1, tq), -jnp.inf, F32), jnp.zeros((1, tq), F32), jnp.zeros((HEAD_DIM, tq), F32))
            _, den, acc = lax.fori_loop(0, n_tiles, body, init)
            acc_ref[sb, 0:HEAD_DIM, r * tq:(r + 1) * tq] = acc
            acc_ref[sb, HEAD_DIM:HEAD_DIM + sl, r * tq:(r + 1) * tq] = jnp.broadcast_to(den * (1.0 / sl), (sl, tq))

    def finish(sb):
        for r in range(Q_PER_KV):
            acc = acc_ref[sb, :, r * tq:(r + 1) * tq]
            den = jnp.sum(acc[HEAD_DIM:HEAD_DIM + sl, :], axis=0, keepdims=True)
            o = acc[0:HEAD_DIM, :] * (1.0 / den)
            oall_ref[sb, r * HEAD_DIM:(r + 1) * HEAD_DIM, :] = o.astype(BF16)
        o_t = _dot_nt(eye_ref[...], oall_ref[sb])
        gate = _silu_from_half(ag_refs[sb][...].astype(F32))
        o_ref[sb * tq:(sb + 1) * tq, :] = (o_t * gate).astype(BF16)

    bounded = bounded_ref[0] == 1

    @pl.when(bounded)
    def _():
        run()
        for sb in range(nsb):
            finish(sb)

    @pl.when(jnp.logical_not(bounded))
    def _():
        for sb in range(nsb):
            run_online(sb)
            finish(sb)


def _attention(bounded, qt, k, vt, proj, eye_tq, q_block0, n_qblocks, nsb, n_tiles, rows):
    B = proj.shape[0]
    nkb, tk = k.shape[1], k.shape[2]
    gw = Q_PER_KV * HEAD_DIM
    nq = Q_PER_KV * ATT_TQ
    assert n_qblocks % nsb == 0

    def qmap(sb):
        return lambda b, g, i, f: (b, g, 0, q_block0 + i * nsb + sb)

    def agmap(sb):
        return lambda b, g, i, f: (b, q_block0 + i * nsb + sb, COL_AG // gw + g)

    in_specs = ([pl.BlockSpec((None, None, KV_W, nq), qmap(sb)) for sb in range(nsb)]
                + [pl.BlockSpec((None, nkb, tk, KV_W), lambda b, g, i, f: (b, 0, 0, 0)),
                   pl.BlockSpec((None, None, nkb, HEAD_DIM, tk), lambda b, g, i, f: (b, g, 0, 0, 0))]
                + [pl.BlockSpec((None, ATT_TQ, gw), agmap(sb)) for sb in range(nsb)]
                + [pl.BlockSpec((ATT_TQ, ATT_TQ), lambda b, g, i, f: (0, 0))])
    args = [bounded] + [qt] * nsb + [k, vt] + [proj] * nsb + [eye_tq]
    grid_spec = pltpu.PrefetchScalarGridSpec(
        num_scalar_prefetch=1,
        grid=(B, N_KV, n_qblocks // nsb),
        in_specs=in_specs,
        out_specs=pl.BlockSpec((None, nsb * ATT_TQ, gw), lambda b, g, i, f: (b, i, g)),
        scratch_shapes=[pltpu.VMEM((ATT_UNROLL, tk, nq), BF16),
                        pltpu.VMEM((nsb, HEAD_DIM + SUBLANES, nq), F32),
                        pltpu.VMEM((nsb, gw, ATT_TQ), BF16),
                        pltpu.VMEM((nsb, KV_W, nq), BF16)])
    return pl.pallas_call(
        functools.partial(_attn_kernel, nsb=nsb, n_tiles=n_tiles, rows=rows),
        grid_spec=grid_spec,
        out_shape=jax.ShapeDtypeStruct((B, n_qblocks * ATT_TQ, ATTN_W), BF16),
        compiler_params=_cparams(("arbitrary", "arbitrary", "arbitrary"), 40),
        name="gqa_attention",
    )(*args)


def _gla_direction(qk, v, pa, a2, ab, st_ref, o_ref, fwd):
    C, c = GLA_C, GLA_SUB
    q = qk[:, 0:GLA_KW].astype(F32) * (GLA_DK ** -0.5)
    k = qk[:, GLA_KW:2 * GLA_KW].astype(F32)
    z = _dot(pa, a2) + ab
    yield
    g =(jnp.minimum(z, 0.0) - jnp.log(1.0 + jnp.exp(-jnp.abs(z)))) * (1.0 / GLA_TEMP)
    row = lax.broadcasted_iota(jnp.int32, (C, C), 0)
    col = lax.broadcasted_iota(jnp.int32, (C, C), 1)
    blk = jnp.bitwise_and(row, -c)
    if fwd:
        tri = col <= row
        inblk = tri & (col >= blk)
    else:
        tri = col >= row
        inblk = tri & (col < blk + c)
    sel = jnp.concatenate([jnp.where(tri, 1.0, 0.0), jnp.where(inblk, 1.0, 0.0)], axis=0).astype(BF16)
    g_hi = g.astype(BF16)
    g_lo = (g - g_hi.astype(F32)).astype(BF16)
    cs = _dot(sel, jnp.concatenate([g_hi, g_lo], axis=1))
    yield
    cs =cs[:, 0:GLA_KW] + cs[:, GLA_KW:2 * GLA_KW]
    b = cs[0:C]
    d = cs[C:2 * C]
    ref = b - d
    total = b[C - 1:C, :] if fwd else b[0:1, :]
    q_in = (q * jnp.exp(d)).astype(BF16)
    q_st = (q * jnp.exp(b)).astype(BF16)
    k_out = (k * jnp.exp(total - b)).astype(BF16)
    w_tot = jnp.exp(total)
    zero = jnp.zeros((), BF16)
    nb = C // c
    k_blk = []
    for i in range(nb):
        lo, hi = (0, (i + 1) * c) if fwd else (i * c, C)
        kb = (k[lo:hi] * jnp.exp(ref[i * c:i * c + 1, :] - b[lo:hi])).astype(BF16)
        pad = jnp.zeros((C - (hi - lo), GLA_KW), BF16)
        k_blk.append(jnp.concatenate(([kb, pad] if fwd else [pad, kb]) if hi - lo < C else [kb], axis=0))
    head_of_lane = jnp.right_shift(lax.broadcasted_iota(jnp.int32, (1, GLA_KW), 1), GLA_DK.bit_length() - 1)
    a_two = []
    for m in range(nb // 2):
        qr = q_in[2 * m * c:(2 * m + 2) * c]
        q_heads = jnp.concatenate([jnp.where(head_of_lane == h, qr, zero) for h in range(GLA_HEADS)], axis=0)
        a_two.append(_dot_nt(q_heads, jnp.concatenate(k_blk[2 * m:2 * m + 2], axis=0)))
    yield

    def a_block(h, i):
        m, r = divmod(i, 2)
        return a_two[m][h * 2 * c + r * c:h * 2 * c + (r + 1) * c, r * C:(r + 1) * C]

    a_cat = jnp.concatenate(
        [jnp.where(tri, jnp.concatenate([a_block(h, i) for i in range(nb)], axis=0), 0.0)
         for h in range(GLA_HEADS)], axis=1).astype(BF16)
    vhead_of_lane = jnp.right_shift(lax.broadcasted_iota(jnp.int32, (1, GLA_VW), 1), GLA_DV.bit_length() - 1)
    v_bd = jnp.concatenate([jnp.where(vhead_of_lane == h, v, zero) for h in range(GLA_HEADS)], axis=0)
    st = st_ref[...]
    o = _dot(a_cat, v_bd) + _dot_nt(q_st, st.astype(BF16))
    upd = lax.dot_general(v, k_out, (((0,), (0,)), ((), ())), preferred_element_type=F32)
    same_head = jnp.right_shift(lax.broadcasted_iota(jnp.int32, (GLA_VW, 1), 0),
                                GLA_DV.bit_length() - 1) == head_of_lane
    st_ref[...] = jnp.where(same_head, st * w_tot + upd, 0.0)
    o_ref[...] = o
    yield


def _gla_kernel(qkf_ref, vf_ref, paf_ref, qkb_ref, vb_ref, pab_ref, a2f_ref, abf_ref, a2b_ref, abb_ref,
                of_ref, ob_ref, stf_ref, stb_ref):
    @pl.when(pl.program_id(0) == 0)
    def _():
        stf_ref[...] = jnp.zeros_like(stf_ref)
        stb_ref[...] = jnp.zeros_like(stb_ref)

    C = GLA_C
    waves = []
    for t in range(GLA_STEPS):
        fs = slice(t * C, (t + 1) * C)
        bs = slice((GLA_STEPS - 1 - t) * C, (GLA_STEPS - t) * C)
        chains = []
        for bi in range(qkf_ref.shape[0]):
            chains.append(_gla_direction(qkf_ref[bi, fs], vf_ref[bi, fs], paf_ref[bi, fs], a2f_ref[...], abf_ref[...],
                                         stf_ref.at[bi], of_ref.at[bi, fs], True))
            chains.append(_gla_direction(qkb_ref[bi, bs], vb_ref[bi, bs], pab_ref[bi, bs], a2b_ref[...], abb_ref[...],
                                         stb_ref.at[bi], ob_ref.at[bi, bs], False))
        waves.append(chains)
    for _ in range(3):
        for chains in waves:
            for chain in chains:
                next(chain)
    for chains in waves:
        for chain in chains:
            next(chain)


def _gla(proj, a2f, abf, a2b, abb, ctx_len):
    B, T = proj.shape[0], proj.shape[1]
    C = GLA_C * GLA_STEPS
    assert T % C == 0 and ctx_len % C == 0
    nch, nctx = T // C, ctx_len // C

    def cb(s):
        return jnp.where(s < nctx, nctx - 1 - s, nch + nctx - 1 - s)

    full = lambda shape: pl.BlockSpec(shape, lambda s: (0,) * len(shape))
    return pl.pallas_call(
        _gla_kernel,
        grid=(nch,),
        in_specs=[pl.BlockSpec((B, C, 2 * GLA_KW), lambda s: (0, s, COL_GQK // (2 * GLA_KW))),
                  pl.BlockSpec((B, C, GLA_VW), lambda s: (0, s, COL_GV // GLA_VW)),
                  pl.BlockSpec((B, C, LANES), lambda s: (0, s, COL_GATE // LANES)),
                  pl.BlockSpec((B, C, 2 * GLA_KW), lambda s: (0, cb(s), COL_GQK // (2 * GLA_KW))),
                  pl.BlockSpec((B, C, GLA_VW), lambda s: (0, cb(s), COL_GV // GLA_VW)),
                  pl.BlockSpec((B, C, LANES), lambda s: (0, cb(s), COL_GATE // LANES)),
                  full((LANES, GLA_KW)), full((1, GLA_KW)), full((LANES, GLA_KW)), full((1, GLA_KW))],
        out_specs=[pl.BlockSpec((B, C, GLA_VW), lambda s: (0, s, 0)),
                   pl.BlockSpec((B, C, GLA_VW), lambda s: (0, cb(s), 0))],
        out_shape=[jax.ShapeDtypeStruct((B, T, GLA_VW), F32),
                   jax.ShapeDtypeStruct((B, T, GLA_VW), F32)],
        scratch_shapes=[pltpu.VMEM((B, GLA_VW, GLA_KW), F32),
                        pltpu.VMEM((B, GLA_VW, GLA_KW), F32)],
        compiler_params=_cparams(("arbitrary",), 32),
        name="gla_scan",
    )(proj, proj, proj, proj, proj, proj, a2f, abf, a2b, abb)


def _merge_kernel(x_ref, mod_ref, ma_ref, mg_ref, ms_ref, su_ref, sv_ref, sg_ref, gg_ref, ya_ref, of_ref, ob_ref,
                  onorm_ref, lng_ref, lnb_ref, sw_ref, sb_ref, wa_ref, wg_ref, ws_ref, wo_ref, pg_ref, pb_ref,
                  o_ref, *, tm, mod_row, alpha):
    D = x_ref.shape[-1]
    ma_ref, mg_ref, ms_ref, su_ref, sv_ref, sg_ref, gg_ref, of_ref, ob_ref = (
        r.at[0] for r in (ma_ref, mg_ref, ms_ref, su_ref, sv_ref, sg_ref, gg_ref, of_ref, ob_ref))
    og = of_ref[...] + ob_ref[...]
    parts = []
    for h in range(GLA_HEADS):
        oh = og[:, h * GLA_DV:(h + 1) * GLA_DV]
        ms = jnp.mean(oh * oh, axis=-1, keepdims=True)
        parts.append(oh * lax.rsqrt(ms + EPS))
    yg = jnp.concatenate(parts, axis=1) * onorm_ref[...] * _silu_from_half(gg_ref[...]).astype(F32)
    sv = sv_ref[...].astype(F32)
    mu = jnp.mean(sv, axis=-1, keepdims=True)
    dv = sv - mu
    var = jnp.mean(dv * dv, axis=-1, keepdims=True)
    vn = (dv * lax.rsqrt(var + EPS) * lng_ref[...] + lnb_ref[...]).astype(BF16)
    gw = SGU_W // SGU_GROUPS
    rows = []
    for n in range(tm // SGU_CHUNK):
        cols = []
        for g in range(SGU_GROUPS):
            blk = vn[n * SGU_CHUNK:(n + 1) * SGU_CHUNK, g * gw:(g + 1) * gw]
            cols.append(_dot(sw_ref[g], blk))
        rows.append(jnp.concatenate(cols, axis=1) + sb_ref[...])
    mixed = jnp.concatenate(rows, axis=0)
    ys = su_ref[...].astype(F32) * mixed * _silu_from_half(sg_ref[...]).astype(F32)

    def gated(g_ref, y, w_ref):
        half = _dot(y, w_ref[...]).astype(BF16)
        return half + half * jnp.tanh(g_ref[...])

    m = (gated(ma_ref, ya_ref[...], wa_ref) + gated(mg_ref, yg.astype(BF16), wg_ref)
         + gated(ms_ref, ys.astype(BF16), ws_ref))
    out = _dot(m, wo_ref[...])
    r = alpha * x_ref[...] + mod_ref[mod_row:mod_row + 1, 2 * D:3 * D] * out
    mu = jnp.mean(r, axis=-1, keepdims=True)
    dr = r - mu
    var = jnp.mean(dr * dr, axis=-1, keepdims=True)
    o_ref[...] = dr * lax.rsqrt(var + EPS) * pg_ref[...] + pb_ref[...]


def _merge(x_seg, ya_seg, row0, mod_row, modb, proj, of, ob, p, tm, alpha):
    B, n_rows, D = x_seg.shape
    assert n_rows % tm == 0 and row0 % 16 == 0

    def stream_rows(w, col0):
        return pl.BlockSpec((pl.Element(1), pl.Element(tm), pl.Element(w)),
                            lambda b, i: (b, pl.multiple_of(row0 + i * tm, 16), col0))

    seg_rows = lambda w: pl.BlockSpec((None, tm, w), lambda b, i: (b, i, 0))
    full = lambda shape: pl.BlockSpec(shape, lambda b, i: (0,) * len(shape))
    return pl.pallas_call(
        functools.partial(_merge_kernel, tm=tm, mod_row=mod_row, alpha=alpha),
        grid=(B, n_rows // tm),
        in_specs=[seg_rows(D),
                  pl.BlockSpec((None, 8, 3 * D), lambda b, i: (b, 0, 0)),
                  stream_rows(D, COL_MA), stream_rows(D, COL_MG), stream_rows(D, COL_MS),
                  stream_rows(SGU_W, COL_SU), stream_rows(SGU_W, COL_SV), stream_rows(SGU_W, COL_SG),
                  stream_rows(GLA_VW, COL_GG),
                  seg_rows(ATTN_W), stream_rows(GLA_VW, 0), stream_rows(GLA_VW, 0),
                  full((1, GLA_VW)), full((1, SGU_W)), full((1, SGU_W)),
                  full((SGU_GROUPS, SGU_CHUNK, SGU_CHUNK)), full((SGU_CHUNK, SGU_W)),
                  full((ATTN_W, D)), full((GLA_VW, D)), full((SGU_W, D)), full((D, D)),
                  full((1, D)), full((1, D))],
        out_specs=pl.BlockSpec((None, tm, D), lambda b, i: (b, i, 0)),
        out_shape=jax.ShapeDtypeStruct((B, n_rows, D), F32),
        compiler_params=_cparams(("arbitrary", "arbitrary"), 56),
        name="merge_out",
    )(x_seg, modb, proj, proj, proj, proj, proj, proj, proj, ya_seg, of, ob,
      p["onorm"], p["lng"], p["lnb"], p["sw"], p["sb"], p["wa"], p["wg"], p["ws"], p["wo"], p["pg"], p["pb"])


W_BLK = 512
N_GATES = 2 * GLA_RANK


def _w_in_plan():
    sizes = (512, 128, 128, 512, 256, 256, 512, 16, 16, 512, 512, 512, 512, 1024, 1024, 1024)
    names = ("aq", "ak", "av", "ag", "gq", "gk", "gv", "af", "ab", "gg", "su", "sv", "sg", "ma", "mg", "ms")
    halved = ("ag", "gg", "sg", "ma", "mg", "ms")
    src, off = {}, 0
    for n, s in zip(names, sizes):
        src[n] = (off, s)
        off += s
    assert src["gk"][0] == src["gq"][0] + src["gq"][1] and src["av"][0] == src["ak"][0] + KV_W
    assert src["ab"][0] == src["af"][0] + GLA_RANK
    plan = []
    for n in ("ma", "mg", "ms", "aq", "ag", "gv", "gg", "su", "sv", "sg"):
        start, size = src[n]
        assert size % W_BLK == 0 and start % SUBLANES == 0
        plan += [(start + t * W_BLK, int(n in halved), 0) for t in range(size // W_BLK)]
    plan.append((src["gq"][0], 0, 0))
    plan.append((src["ak"][0], 0, 1))
    assert len(plan) * W_BLK == PROJ_W and src["ak"][0] + W_BLK <= off
    return plan, src["af"][0], off


def _wprep_kernel(src_ref, half_ref, special_ref, a_ref, g_ref, o_ref):
    j = pl.program_id(1)

    @pl.when(special_ref[j] == 0)
    def _():
        o_ref[...] = (a_ref[0] * jnp.where(half_ref[j] == 1, 0.5, 1.0)).astype(BF16)

    @pl.when(special_ref[j] == 1)
    def _():
        o_ref[0:2 * KV_W, :] = a_ref[0, 0:2 * KV_W, :].astype(BF16)
        o_ref[2 * KV_W:2 * KV_W + N_GATES, :] = g_ref[0].astype(BF16)
        o_ref[2 * KV_W + N_GATES:W_BLK, :] = jnp.zeros((W_BLK - 2 * KV_W - N_GATES, o_ref.shape[-1]), BF16)


def _regroup_w_in(w_in):
    L, D, width = w_in.shape
    plan, gate_col, used = _w_in_plan()
    assert used == width
    src = jnp.array([p[0] for p in plan], jnp.int32)
    half = jnp.array([p[1] for p in plan], jnp.int32)
    special = jnp.array([p[2] for p in plan], jnp.int32)
    wt = jnp.swapaxes(w_in, 1, 2)
    grid_spec = pltpu.PrefetchScalarGridSpec(
        num_scalar_prefetch=3,
        grid=(L, len(plan)),
        in_specs=[pl.BlockSpec((pl.Element(1), pl.Element(W_BLK), pl.Element(D)),
                               lambda l, j, s, h, sp: (l, pl.multiple_of(s[j], SUBLANES), 0)),
                  pl.BlockSpec((pl.Element(1), pl.Element(N_GATES), pl.Element(D)),
                               lambda l, j, s, h, sp: (l, gate_col, 0))],
        out_specs=pl.BlockSpec((None, W_BLK, D), lambda l, j, s, h, sp: (l, j, 0)))
    return pl.pallas_call(
        _wprep_kernel,
        grid_spec=grid_spec,
        out_shape=jax.ShapeDtypeStruct((L, PROJ_W, D), BF16),
        compiler_params=_cparams(("arbitrary", "arbitrary"), 32),
        name="w_in_prep",
    )(src, half, special, wt, wt)


def _rope_tables(n_lat, ctx_len):
    rows = n_lat // GRID_W
    row = np.repeat(np.arange(rows, dtype=np.float64), GRID_W)
    col = np.tile(np.arange(GRID_W, dtype=np.float64), rows)
    freqs = ROPE_THETA ** (-np.arange(0, ROPE_AXIS, 2, dtype=np.float64) / ROPE_AXIS)
    ang_r, ang_c = row[:, None] * freqs, col[:, None] * freqs
    cos = np.concatenate([np.cos(ang_r)] * 2 + [np.cos(ang_c)] * 2, axis=1)
    sin = np.concatenate([-np.sin(ang_r), np.sin(ang_r), -np.sin(ang_c), np.sin(ang_c)], axis=1)
    cos = np.concatenate([np.ones((ctx_len, HEAD_DIM)), cos], axis=0)
    sin = np.concatenate([np.zeros((ctx_len, HEAD_DIM)), sin], axis=0)
    return (jnp.asarray(np.concatenate([cos, cos], axis=1), F32),
            jnp.asarray(np.concatenate([sin, sin], axis=1), F32))


def _largest_tile(total, cap, mult):
    best = mult
    for t in range(mult, cap + 1, mult):
        if total % t == 0:
            best = t
    return best


def kernel(x, c, ctx, c_ctx, w_ada, b_ada, w_in, attn_q_norm, attn_k_norm, gla_a2_f, gla_ab_f, gla_a2_b, gla_ab_b, gla_o_norm, sgu_ln_g, sgu_ln_b, sgu_w, sgu_b, w_br_attn, w_br_gla, w_br_sgu, w_out, post_ln_g, post_ln_b):
    B, n_lat, D = x.shape
    ctx_len = ctx.shape[1]
    depth = w_in.shape[0]
    T = ctx_len + n_lat
    assert D == D_MODEL and B <= 7
    assert ctx_len % ATT_TQ == 0 and n_lat % ATT_TQ == 0 and n_lat % GRID_W == 0
    alpha = (2 * depth) ** 0.25

    tm_proj = _largest_tile(T, 2112, 16)
    tn_proj = 1536
    tk_att = _largest_tile(T, ATT_TK_MAX, ATT_TQ)
    assert ctx_len <= tk_att
    tm_prep = tk_att
    tm_merge_lat = _largest_tile(n_lat, 512, SGU_CHUNK)
    tm_merge_ctx = _largest_tile(ctx_len, 512, SGU_CHUNK)

    x_ctx, x_lat = ctx, x
    cc = jnp.zeros((8, D), F32).at[0:B].set(c).at[B].set(c_ctx)
    mod = _ada(cc, w_ada, b_ada.reshape(depth, 1, 3 * D))
    cos_t, sin_t = _rope_tables(n_lat, ctx_len)
    consts = {
        "gq": jnp.kron(jnp.eye(N_HEADS, dtype=F32), jnp.ones((HEAD_DIM, HEAD_DIM), F32)).astype(BF16),
        "gk": jnp.kron(jnp.eye(N_KV, dtype=F32), jnp.ones((HEAD_DIM, HEAD_DIM), F32)).astype(BF16),
        "eq": jnp.eye(ATTN_W, dtype=BF16),
        "ek": jnp.eye(KV_W, dtype=BF16),
    }
    eye_tq = jnp.eye(ATT_TQ, dtype=BF16)
    w_all = _regroup_w_in(w_in)

    for l in range(depth):
        last = l == depth - 1
        modb = jnp.zeros((B, 8, 3 * D), F32).at[:, 0].set(mod[l, 0:B]).at[:, 1].set(mod[l, B][None])
        proj = _inproj(x_lat, x_ctx, modb, w_all, l, tm_proj, tn_proj)
        qn = jnp.concatenate([attn_q_norm[l]] * 2)[None]
        kn = jnp.concatenate([attn_k_norm[l]] * 2)[None]
        qt, kk, vt = _qkprep(proj, cos_t, sin_t, qn, kn, consts, tm_prep, tk_att)
        score_bound = HEAD_DIM * jnp.max(jnp.abs(attn_q_norm[l])) * jnp.max(jnp.abs(attn_k_norm[l])) * Q_SCALE
        bounded = (score_bound <= SCORE_LIMIT).astype(jnp.int32).reshape(1)
        ctx_blocks, lat_blocks = ctx_len // ATT_TQ, n_lat // ATT_TQ
        ya_lat = _attention(bounded, qt, kk, vt, proj, eye_tq, ctx_blocks, lat_blocks, ATT_NSB, T // tk_att, tk_att)
        ya_ctx = None if last else _attention(bounded, qt, kk, vt, proj, eye_tq, 0, ctx_blocks, 1, 1, ctx_len)
        a2f = jnp.zeros((LANES, GLA_KW), F32).at[0:GLA_RANK].set(gla_a2_f[l]).astype(BF16)
        a2b = jnp.zeros((LANES, GLA_KW), F32).at[GLA_RANK:2 * GLA_RANK].set(gla_a2_b[l]).astype(BF16)
        of, ob = _gla(proj, a2f, gla_ab_f[l][None], a2b, gla_ab_b[l][None], ctx_len)
        p = {
            "onorm": jnp.concatenate([gla_o_norm[l]] * GLA_HEADS)[None],
            "lng": sgu_ln_g[l][None], "lnb": sgu_ln_b[l][None],
            "sw": sgu_w[l].astype(BF16),
            "sb": jnp.repeat(sgu_b[l].T, SGU_W // SGU_GROUPS, axis=1),
            "wa": (0.5 * w_br_attn[l]).astype(BF16), "wg": (0.5 * w_br_gla[l]).astype(BF16),
            "ws": (0.5 * w_br_sgu[l]).astype(BF16),
            "wo": w_out[l].astype(BF16), "pg": post_ln_g[l][None], "pb": post_ln_b[l][None],
        }
        if not last:
            x_ctx = _merge(x_ctx, ya_ctx, 0, 1, modb, proj, of, ob, p, tm_merge_ctx, alpha)
        x_lat = _merge(x_lat, ya_lat, ctx_len, 0, modb, proj, of, ob, p, tm_merge_lat, alpha)
    return x_lat
```

```python
import functools

import jax
import jax.numpy as jnp
import numpy as np
from jax import lax
from jax.experimental import pallas as pl
from jax.experimental.pallas import tpu as pltpu

F32 = jnp.float32
BF16 = jnp.bfloat16

D_MODEL = 1024
GRID_W = 64
HEAD_DIM = 64
N_HEADS = 8
N_KV = 2
Q_PER_KV = N_HEADS // N_KV
ATTN_W = N_HEADS * HEAD_DIM
KV_W = N_KV * HEAD_DIM
ROPE_THETA = 10000.0
ROPE_AXIS = HEAD_DIM // 2
GLA_HEADS = 4
GLA_DK = 64
GLA_DV = 128
GLA_KW = GLA_HEADS * GLA_DK
GLA_VW = GLA_HEADS * GLA_DV
GLA_RANK = 16
GLA_TEMP = 16.0
SGU_GROUPS = 4
SGU_CHUNK = 128
SGU_W = 512
EPS = 1e-6

LANES = 128
V7X_VMEM_BYTES = 64 * 1024 * 1024

COL_MA, COL_MG, COL_MS = 0, 1024, 2048
COL_AQ, COL_AG = 3072, 3584
COL_GV, COL_GG = 4096, 4608
COL_SU, COL_SV, COL_SG = 5120, 5632, 6144
COL_GQK = 6656
COL_AKV = 7168
COL_GATE = 7424
PROJ_W = 7680

SUBLANES = 8
ATT_TQ = 256
ATT_NSB = 4
ATT_TK_MAX = 768
GLA_C = 128
GLA_SUB = 16
GLA_STEPS = 2
ATT_UNROLL = 4
Q_SCALE = 1.4426950408889634 * HEAD_DIM ** -0.5
SCORE_LIMIT = 96.0


def _cparams(sem, vmem_mb):
    return pltpu.CompilerParams(dimension_semantics=sem,
                                vmem_limit_bytes=min(vmem_mb * 1024 * 1024, V7X_VMEM_BYTES - (4 << 20)))


def _silu(x):
    return x * jax.nn.sigmoid(x)


def _silu_from_half(u):
    return u + u * jnp.tanh(u)


def _dot(a, b):
    return jnp.dot(a, b, preferred_element_type=F32)


def _dot_nt(a, b):
    return lax.dot_general(a, b, (((1,), (1,)), ((), ())), preferred_element_type=F32)


def _ada_kernel(c_ref, w_ref, b_ref, o_ref):
    c = c_ref[...]
    o_ref[...] = _dot(_silu(c).astype(BF16), w_ref[...].astype(BF16)) + b_ref[...]


def _ada(cc, w_ada, b_ada):
    L, D = w_ada.shape[0], w_ada.shape[1]
    return pl.pallas_call(
        _ada_kernel,
        grid=(L, 3),
        in_specs=[pl.BlockSpec((8, D), lambda l, j: (0, 0)),
                  pl.BlockSpec((None, D, D), lambda l, j: (l, 0, j)),
                  pl.BlockSpec((None, 1, D), lambda l, j: (l, 0, j))],
        out_specs=pl.BlockSpec((None, 8, D), lambda l, j: (l, 0, j)),
        out_shape=jax.ShapeDtypeStruct((L, 8, 3 * D), F32),
        compiler_params=_cparams(("arbitrary", "arbitrary"), 32),
        name="ada_mod",
    )(cc, w_ada, b_ada)


def _inproj_kernel(x_ref, ctx_ref, mod_ref, w_ref, o_ref, h_ref, *, tm, ctx_len):
    i = pl.program_id(1)
    j = pl.program_id(2)
    D = x_ref.shape[-1]

    def modulated(v, r):
        return (v * (1.0 + mod_ref[r:r + 1, D:2 * D]) + mod_ref[r:r + 1, 0:D]).astype(BF16)

    @pl.when((j == 0) & (i == 0))
    def _():
        h_ref[0:ctx_len, :] = modulated(ctx_ref[...], 1)
        h_ref[ctx_len:tm, :] = modulated(x_ref[0, 0:tm - ctx_len, :], 0)

    @pl.when((j == 0) & (i > 0))
    def _():
        h_ref[...] = modulated(x_ref[0], 0)

    o_ref[...] = _dot_nt(h_ref[...], w_ref[...]).astype(BF16)


def _inproj(x, ctx, modb, w_all, layer, tm, tn):
    B, n_lat, D = x.shape
    ctx_len = ctx.shape[1]
    T = n_lat + ctx_len
    assert ctx_len < tm and T % tm == 0 and (tm - ctx_len) % SUBLANES == 0
    x_spec = pl.BlockSpec((pl.Element(1), pl.Element(tm), pl.Element(D)),
                          lambda b, i, j: (b, pl.multiple_of(jnp.maximum(i * tm - ctx_len, 0), SUBLANES), 0))
    return pl.pallas_call(
        functools.partial(_inproj_kernel, tm=tm, ctx_len=ctx_len),
        grid=(B, T // tm, PROJ_W // tn),
        in_specs=[x_spec,
                  pl.BlockSpec((None, ctx_len, D), lambda b, i, j: (b, 0, 0)),
                  pl.BlockSpec((None, 8, 3 * D), lambda b, i, j: (b, 0, 0)),
                  pl.BlockSpec((None, tn, D), lambda b, i, j: (layer, j, 0))],
        out_specs=pl.BlockSpec((None, tm, tn), lambda b, i, j: (b, i, j)),
        out_shape=jax.ShapeDtypeStruct((B, T, PROJ_W), BF16),
        scratch_shapes=[pltpu.VMEM((tm, D), BF16)],
        compiler_params=_cparams(("arbitrary", "arbitrary", "arbitrary"), 52),
        name="in_proj",
    )(x, ctx, modb, w_all)


def _rope(y, cos, sin, width):
    lane = lax.broadcasted_iota(jnp.int32, (1, width), 1)
    first = jnp.bitwise_and(lane, ROPE_AXIS - 1) < (ROPE_AXIS // 2)
    up = pltpu.roll(y, width - ROPE_AXIS // 2, 1)
    dn = pltpu.roll(y, ROPE_AXIS // 2, 1)
    return y * cos + jnp.where(first, up, dn) * sin


def _qkprep_kernel(aq_ref, akv_ref, cos_ref, sin_ref, qn_ref, kn_ref, gq_ref, gk_ref,
                   eq_ref, ek_ref, qt_ref, k_ref, vt_ref, *, tm, tk):
    cos = cos_ref[...]
    sin = sin_ref[...]
    aq = aq_ref[...].astype(F32)
    ssq = _dot((aq * aq).astype(BF16), gq_ref[...])
    akv = akv_ref[...].astype(F32)
    ak = akv[:, 0:KV_W]
    ssk = _dot((ak * ak).astype(BF16), gk_ref[...])
    vt = _dot_nt(ek_ref[...], akv_ref[:, KV_W:2 * KV_W]).astype(BF16)
    yq = aq * lax.rsqrt(ssq * (1.0 / HEAD_DIM) + EPS) * jnp.concatenate([qn_ref[...]] * 4, axis=1)
    yq = _rope(yq, jnp.concatenate([cos] * 4, axis=1), jnp.concatenate([sin] * 4, axis=1), ATTN_W)
    yq = (yq * Q_SCALE).astype(BF16)
    qt = _dot_nt(eq_ref[...], yq).astype(BF16)
    zeros = jnp.zeros((HEAD_DIM, ATT_TQ), BF16)
    for g in range(N_KV):
        for qb in range(tm // ATT_TQ):
            for r in range(Q_PER_KV):
                h = g * Q_PER_KV + r
                cols = slice((qb * Q_PER_KV + r) * ATT_TQ, (qb * Q_PER_KV + r + 1) * ATT_TQ)
                qt_ref[g, g * HEAD_DIM:(g + 1) * HEAD_DIM, cols] = qt[h * HEAD_DIM:(h + 1) * HEAD_DIM,
                                                                      qb * ATT_TQ:(qb + 1) * ATT_TQ]
                qt_ref[g, (1 - g) * HEAD_DIM:(2 - g) * HEAD_DIM, cols] = zeros
    yk = ak * lax.rsqrt(ssk * (1.0 / HEAD_DIM) + EPS) * kn_ref[...]
    yk = _rope(yk, cos, sin, KV_W).astype(BF16)
    nkb = tm // tk
    k_ref[...] = yk.reshape(nkb, tk, KV_W)
    for g in range(N_KV):
        for n in range(nkb):
            vt_ref[g, n] = vt[g * HEAD_DIM:(g + 1) * HEAD_DIM, n * tk:(n + 1) * tk]


def _qkprep(proj, cos_t, sin_t, qn, kn, consts, tm, tk):
    B, T, _ = proj.shape
    nkb_t = T // tk
    nkb = tm // tk
    full = lambda shape: pl.BlockSpec(shape, lambda b, i: (0,) * len(shape))
    return pl.pallas_call(
        functools.partial(_qkprep_kernel, tm=tm, tk=tk),
        grid=(B, T // tm),
        in_specs=[pl.BlockSpec((None, tm, ATTN_W), lambda b, i: (b, i, COL_AQ // ATTN_W)),
                  pl.BlockSpec((None, tm, 2 * KV_W), lambda b, i: (b, i, COL_AKV // (2 * KV_W))),
                  pl.BlockSpec((tm, LANES), lambda b, i: (i, 0)),
                  pl.BlockSpec((tm, LANES), lambda b, i: (i, 0)),
                  full((1, LANES)), full((1, LANES)),
                  full((ATTN_W, ATTN_W)), full((KV_W, KV_W)),
                  full((ATTN_W, ATTN_W)), full((KV_W, KV_W))],
        out_specs=[pl.BlockSpec((None, N_KV, KV_W, Q_PER_KV * tm), lambda b, i: (b, 0, 0, i)),
                   pl.BlockSpec((None, nkb, tk, KV_W), lambda b, i: (b, i, 0, 0)),
                   pl.BlockSpec((None, N_KV, nkb, HEAD_DIM, tk), lambda b, i: (b, 0, i, 0, 0))],
        out_shape=[jax.ShapeDtypeStruct((B, N_KV, KV_W, Q_PER_KV * T), BF16),
                   jax.ShapeDtypeStruct((B, nkb_t, tk, KV_W), BF16),
                   jax.ShapeDtypeStruct((B, N_KV, nkb_t, HEAD_DIM, tk), BF16)],
        compiler_params=_cparams(("arbitrary", "arbitrary"), 40),
        name="qkv_prep",
    )(proj, proj, cos_t, sin_t, qn, kn, consts["gq"], consts["gk"], consts["eq"], consts["ek"])


def _attn_kernel(bounded_ref, *refs, nsb, n_tiles, rows):
    q_refs, (k_ref, vt_ref), ag_refs, eye_ref = refs[0:nsb], refs[nsb:nsb + 2], refs[nsb + 2:2 * nsb + 2], refs[2 * nsb + 2]
    o_ref, p_ref, acc_ref, oall_ref, qall_ref = refs[-5:]
    nq = q_refs[0].shape[-1]
    tq = nq // Q_PER_KV
    sl = acc_ref.shape[1] - HEAD_DIM
    U = p_ref.shape[0]
    acc_ref[...] = jnp.zeros_like(acc_ref)

    def stage_of(s):
        if isinstance(s, int):
            return divmod(s, n_tiles)
        return lax.div(s, n_tiles), lax.rem(s, n_tiles)

    def weights(s, slot):
        sb, j = stage_of(s)
        p = jnp.exp2(_dot(k_ref[j, 0:rows, :], qall_ref[sb]))
        p_ref[slot, 0:rows, :] = p.astype(BF16)
        acc_ref[sb, HEAD_DIM:HEAD_DIM + sl, :] += jnp.sum(p.reshape(rows // sl, sl, nq), axis=0)

    def weighted_values(s, slot):
        sb, j = stage_of(s)
        acc_ref[sb, 0:HEAD_DIM, :] += _dot(vt_ref[j, :, 0:rows], p_ref[slot, 0:rows, :])

    def run():
        stages = nsb * n_tiles
        for sb in range(nsb):
            qall_ref[sb] = q_refs[sb][...]
        weights(0, 0)

        def group(i, carry):
            for u in range(U):
                weights(U * i + u + 1, (u + 1) % U)
                weighted_values(U * i + u, u)
            return carry

        groups = (stages - 1) // U
        lax.fori_loop(0, groups, group, 0)
        for s in range(U * groups, stages):
            if s + 1 < stages:
                weights(s + 1, (s + 1) % U)
            weighted_values(s, s % U)

    def run_online(sb):
        for r in range(Q_PER_KV):
            q = q_refs[sb][:, r * tq:(r + 1) * tq]

            def body(j, carry):
                m, den, acc = carry
                s = _dot(k_ref[j, 0:rows, :], q)
                m_new = jnp.maximum(m, jnp.max(s, axis=0, keepdims=True))
                p = jnp.exp2(s - m_new)
                alpha = jnp.exp2(m - m_new)
                return (m_new, alpha * den + jnp.sum(p, axis=0, keepdims=True),
                        alpha * acc + _dot(vt_ref[j, :, 0:rows], p.astype(BF16)))

            init = (jnp.full((1, tq), -jnp.inf, F32), jnp.zeros((1, tq), F32), jnp.zeros((HEAD_DIM, tq), F32))
            _, den, acc = lax.fori_loop(0, n_tiles, body, init)
            acc_ref[sb, 0:HEAD_DIM, r * tq:(r + 1) * tq] = acc
            acc_ref[sb, HEAD_DIM:HEAD_DIM + sl, r * tq:(r + 1) * tq] = jnp.broadcast_to(den * (1.0 / sl), (sl, tq))

    def finish(sb):
        for r in range(Q_PER_KV):
            acc = acc_ref[sb, :, r * tq:(r + 1) * tq]
            den = jnp.sum(acc[HEAD_DIM:HEAD_DIM + sl, :], axis=0, keepdims=True)
            o = acc[0:HEAD_DIM, :] * (1.0 / den)
            oall_ref[sb, r * HEAD_DIM:(r + 1) * HEAD_DIM, :] = o.astype(BF16)
        o_t = _dot_nt(eye_ref[...], oall_ref[sb])
        gate = _silu_from_half(ag_refs[sb][...].astype(F32))
        o_ref[sb * tq:(sb + 1) * tq, :] = (o_t * gate).astype(BF16)

    bounded = bounded_ref[0] == 1

    @pl.when(bounded)
    def _():
        run()
        for sb in range(nsb):
            finish(sb)

    @pl.when(jnp.logical_not(bounded))
    def _():
        for sb in range(nsb):
            run_online(sb)
            finish(sb)


def _attention(bounded, qt, k, vt, proj, eye_tq, q_block0, n_qblocks, nsb, n_tiles, rows):
    B = proj.shape[0]
    nkb, tk = k.shape[1], k.shape[2]
    gw = Q_PER_KV * HEAD_DIM
    nq = Q_PER_KV * ATT_TQ
    assert n_qblocks % nsb == 0

    def qmap(sb):
        return lambda b, g, i, f: (b, g, 0, q_block0 + i * nsb + sb)

    def agmap(sb):
        return lambda b, g, i, f: (b, q_block0 + i * nsb + sb, COL_AG // gw + g)

    in_specs = ([pl.BlockSpec((None, None, KV_W, nq), qmap(sb)) for sb in range(nsb)]
                + [pl.BlockSpec((None, nkb, tk, KV_W), lambda b, g, i, f: (b, 0, 0, 0)),
                   pl.BlockSpec((None, None, nkb, HEAD_DIM, tk), lambda b, g, i, f: (b, g, 0, 0, 0))]
                + [pl.BlockSpec((None, ATT_TQ, gw), agmap(sb)) for sb in range(nsb)]
                + [pl.BlockSpec((ATT_TQ, ATT_TQ), lambda b, g, i, f: (0, 0))])
    args = [bounded] + [qt] * nsb + [k, vt] + [proj] * nsb + [eye_tq]
    grid_spec = pltpu.PrefetchScalarGridSpec(
        num_scalar_prefetch=1,
        grid=(B, N_KV, n_qblocks // nsb),
        in_specs=in_specs,
        out_specs=pl.BlockSpec((None, nsb * ATT_TQ, gw), lambda b, g, i, f: (b, i, g)),
        scratch_shapes=[pltpu.VMEM((ATT_UNROLL, tk, nq), BF16),
                        pltpu.VMEM((nsb, HEAD_DIM + SUBLANES, nq), F32),
                        pltpu.VMEM((nsb, gw, ATT_TQ), BF16),
                        pltpu.VMEM((nsb, KV_W, nq), BF16)])
    return pl.pallas_call(
        functools.partial(_attn_kernel, nsb=nsb, n_tiles=n_tiles, rows=rows),
        grid_spec=grid_spec,
        out_shape=jax.ShapeDtypeStruct((B, n_qblocks * ATT_TQ, ATTN_W), BF16),
        compiler_params=_cparams(("arbitrary", "arbitrary", "arbitrary"), 40),
        name="gqa_attention",
    )(*args)


def _gla_direction(qk, v, pa, a2, ab, st_ref, o_ref, fwd):
    C, c = GLA_C, GLA_SUB
    q = qk[:, 0:GLA_KW].astype(F32) * (GLA_DK ** -0.5)
    k = qk[:, GLA_KW:2 * GLA_KW].astype(F32)
    z = _dot(pa, a2) + ab
    yield
    g =(jnp.minimum(z, 0.0) - jnp.log(1.0 + jnp.exp(-jnp.abs(z)))) * (1.0 / GLA_TEMP)
    row = lax.broadcasted_iota(jnp.int32, (C, C), 0)
    col = lax.broadcasted_iota(jnp.int32, (C, C), 1)
    blk = jnp.bitwise_and(row, -c)
    if fwd:
        tri = col <= row
        inblk = tri & (col >= blk)
    else:
        tri = col >= row
        inblk = tri & (col < blk + c)
    sel = jnp.concatenate([jnp.where(tri, 1.0, 0.0), jnp.where(inblk, 1.0, 0.0)], axis=0).astype(BF16)
    g_hi = g.astype(BF16)
    g_lo = (g - g_hi.astype(F32)).astype(BF16)
    cs = _dot(sel, jnp.concatenate([g_hi, g_lo], axis=1))
    yield
    cs =cs[:, 0:GLA_KW] + cs[:, GLA_KW:2 * GLA_KW]
    b = cs[0:C]
    d = cs[C:2 * C]
    ref = b - d
    total = b[C - 1:C, :] if fwd else b[0:1, :]
    q_in = (q * jnp.exp(d)).astype(BF16)
    q_st = (q * jnp.exp(b)).astype(BF16)
    k_out = (k * jnp.exp(total - b)).astype(BF16)
    w_tot = jnp.exp(total)
    zero = jnp.zeros((), BF16)
    nb = C // c
    k_blk = []
    for i in range(nb):
        lo, hi = (0, (i + 1) * c) if fwd else (i * c, C)
        kb = (k[lo:hi] * jnp.exp(ref[i * c:i * c + 1, :] - b[lo:hi])).astype(BF16)
        pad = jnp.zeros((C - (hi - lo), GLA_KW), BF16)
        k_blk.append(jnp.concatenate(([kb, pad] if fwd else [pad, kb]) if hi - lo < C else [kb], axis=0))
    head_of_lane = jnp.right_shift(lax.broadcasted_iota(jnp.int32, (1, GLA_KW), 1), GLA_DK.bit_length() - 1)
    a_two = []
    for m in range(nb // 2):
        qr = q_in[2 * m * c:(2 * m + 2) * c]
        q_heads = jnp.concatenate([jnp.where(head_of_lane == h, qr, zero) for h in range(GLA_HEADS)], axis=0)
        a_two.append(_dot_nt(q_heads, jnp.concatenate(k_blk[2 * m:2 * m + 2], axis=0)))
    yield

    def a_block(h, i):
        m, r = divmod(i, 2)
        return a_two[m][h * 2 * c + r * c:h * 2 * c + (r + 1) * c, r * C:(r + 1) * C]

    a_cat = jnp.concatenate(
        [jnp.where(tri, jnp.concatenate([a_block(h, i) for i in range(nb)], axis=0), 0.0)
         for h in range(GLA_HEADS)], axis=1).astype(BF16)
    vhead_of_lane = jnp.right_shift(lax.broadcasted_iota(jnp.int32, (1, GLA_VW), 1), GLA_DV.bit_length() - 1)
    v_bd = jnp.concatenate([jnp.where(vhead_of_lane == h, v, zero) for h in range(GLA_HEADS)], axis=0)
    st = st_ref[...]
    o = _dot(a_cat, v_bd) + _dot_nt(q_st, st.astype(BF16))
    upd = lax.dot_general(v, k_out, (((0,), (0,)), ((), ())), preferred_element_type=F32)
    same_head = jnp.right_shift(lax.broadcasted_iota(jnp.int32, (GLA_VW, 1), 0),
                                GLA_DV.bit_length() - 1) == head_of_lane
    st_ref[...] = jnp.where(same_head, st * w_tot + upd, 0.0)
    o_ref[...] = o
    yield


def _gla_kernel(qkf_ref, vf_ref, paf_ref, qkb_ref, vb_ref, pab_ref, a2f_ref, abf_ref, a2b_ref, abb_ref,
                of_ref, ob_ref, stf_ref, stb_ref):
    @pl.when(pl.program_id(0) == 0)
    def _():
        stf_ref[...] = jnp.zeros_like(stf_ref)
        stb_ref[...] = jnp.zeros_like(stb_ref)

    C = GLA_C
    waves = []
    for t in range(GLA_STEPS):
        fs = slice(t * C, (t + 1) * C)
        bs = slice((GLA_STEPS - 1 - t) * C, (GLA_STEPS - t) * C)
        chains = []
        for bi in range(qkf_ref.shape[0]):
            chains.append(_gla_direction(qkf_ref[bi, fs], vf_ref[bi, fs], paf_ref[bi, fs], a2f_ref[...], abf_ref[...],
                                         stf_ref.at[bi], of_ref.at[bi, fs], True))
            chains.append(_gla_direction(qkb_ref[bi, bs], vb_ref[bi, bs], pab_ref[bi, bs], a2b_ref[...], abb_ref[...],
                                         stb_ref.at[bi], ob_ref.at[bi, bs], False))
        waves.append(chains)
    for _ in range(3):
        for chains in waves:
            for chain in chains:
                next(chain)
    for chains in waves:
        for chain in chains:
            next(chain)


def _gla(proj, a2f, abf, a2b, abb, ctx_len):
    B, T = proj.shape[0], proj.shape[1]
    C = GLA_C * GLA_STEPS
    assert T % C == 0 and ctx_len % C == 0
    nch, nctx = T // C, ctx_len // C

    def cb(s):
        return jnp.where(s < nctx, nctx - 1 - s, nch + nctx - 1 - s)

    full = lambda shape: pl.BlockSpec(shape, lambda s: (0,) * len(shape))
    return pl.pallas_call(
        _gla_kernel,
        grid=(nch,),
        in_specs=[pl.BlockSpec((B, C, 2 * GLA_KW), lambda s: (0, s, COL_GQK // (2 * GLA_KW))),
                  pl.BlockSpec((B, C, GLA_VW), lambda s: (0, s, COL_GV // GLA_VW)),
                  pl.BlockSpec((B, C, LANES), lambda s: (0, s, COL_GATE // LANES)),
                  pl.BlockSpec((B, C, 2 * GLA_KW), lambda s: (0, cb(s), COL_GQK // (2 * GLA_KW))),
                  pl.BlockSpec((B, C, GLA_VW), lambda s: (0, cb(s), COL_GV // GLA_VW)),
                  pl.BlockSpec((B, C, LANES), lambda s: (0, cb(s), COL_GATE // LANES)),
                  full((LANES, GLA_KW)), full((1, GLA_KW)), full((LANES, GLA_KW)), full((1, GLA_KW))],
        out_specs=[pl.BlockSpec((B, C, GLA_VW), lambda s: (0, s, 0)),
                   pl.BlockSpec((B, C, GLA_VW), lambda s: (0, cb(s), 0))],
        out_shape=[jax.ShapeDtypeStruct((B, T, GLA_VW), F32),
                   jax.ShapeDtypeStruct((B, T, GLA_VW), F32)],
        scratch_shapes=[pltpu.VMEM((B, GLA_VW, GLA_KW), F32),
                        pltpu.VMEM((B, GLA_VW, GLA_KW), F32)],
        compiler_params=_cparams(("arbitrary",), 32),
        name="gla_scan",
    )(proj, proj, proj, proj, proj, proj, a2f, abf, a2b, abb)


def _merge_kernel(x_ref, mod_ref, ma_ref, mg_ref, ms_ref, su_ref, sv_ref, sg_ref, gg_ref, ya_ref, of_ref, ob_ref,
                  onorm_ref, lng_ref, lnb_ref, sw_ref, sb_ref, wa_ref, wg_ref, ws_ref, wo_ref, pg_ref, pb_ref,
                  o_ref, *, tm, mod_row, alpha):
    D = x_ref.shape[-1]
    ma_ref, mg_ref, ms_ref, su_ref, sv_ref, sg_ref, gg_ref, of_ref, ob_ref = (
        r.at[0] for r in (ma_ref, mg_ref, ms_ref, su_ref, sv_ref, sg_ref, gg_ref, of_ref, ob_ref))
    og = of_ref[...] + ob_ref[...]
    parts = []
    for h in range(GLA_HEADS):
        oh = og[:, h * GLA_DV:(h + 1) * GLA_DV]
        ms = jnp.mean(oh * oh, axis=-1, keepdims=True)
        parts.append(oh * lax.rsqrt(ms + EPS))
    yg = jnp.concatenate(parts, axis=1) * onorm_ref[...] * _silu_from_half(gg_ref[...]).astype(F32)
    sv = sv_ref[...].astype(F32)
    mu = jnp.mean(sv, axis=-1, keepdims=True)
    dv = sv - mu
    var = jnp.mean(dv * dv, axis=-1, keepdims=True)
    vn = (dv * lax.rsqrt(var + EPS) * lng_ref[...] + lnb_ref[...]).astype(BF16)
    gw = SGU_W // SGU_GROUPS
    rows = []
    for n in range(tm // SGU_CHUNK):
        cols = []
        for g in range(SGU_GROUPS):
            blk = vn[n * SGU_CHUNK:(n + 1) * SGU_CHUNK, g * gw:(g + 1) * gw]
            cols.append(_dot(sw_ref[g], blk))
        rows.append(jnp.concatenate(cols, axis=1) + sb_ref[...])
    mixed = jnp.concatenate(rows, axis=0)
    ys = su_ref[...].astype(F32) * mixed * _silu_from_half(sg_ref[...]).astype(F32)

    def gated(g_ref, y, w_ref):
        half = _dot(y, w_ref[...]).astype(BF16)
        return half + half * jnp.tanh(g_ref[...])

    m = (gated(ma_ref, ya_ref[...], wa_ref) + gated(mg_ref, yg.astype(BF16), wg_ref)
         + gated(ms_ref, ys.astype(BF16), ws_ref))
    out = _dot(m, wo_ref[...])
    r = alpha * x_ref[...] + mod_ref[mod_row:mod_row + 1, 2 * D:3 * D] * out
    mu = jnp.mean(r, axis=-1, keepdims=True)
    dr = r - mu
    var = jnp.mean(dr * dr, axis=-1, keepdims=True)
    o_ref[...] = dr * lax.rsqrt(var + EPS) * pg_ref[...] + pb_ref[...]


def _merge(x_seg, ya_seg, row0, mod_row, modb, proj, of, ob, p, tm, alpha):
    B, n_rows, D = x_seg.shape
    assert n_rows % tm == 0 and row0 % 16 == 0

    def stream_rows(w, col0):
        return pl.BlockSpec((pl.Element(1), pl.Element(tm), pl.Element(w)),
                            lambda b, i: (b, pl.multiple_of(row0 + i * tm, 16), col0))

    seg_rows = lambda w: pl.BlockSpec((None, tm, w), lambda b, i: (b, i, 0))
    full = lambda shape: pl.BlockSpec(shape, lambda b, i: (0,) * len(shape))
    return pl.pallas_call(
        functools.partial(_merge_kernel, tm=tm, mod_row=mod_row, alpha=alpha),
        grid=(B, n_rows // tm),
        in_specs=[seg_rows(D),
                  pl.BlockSpec((None, 8, 3 * D), lambda b, i: (b, 0, 0)),
                  stream_rows(D, COL_MA), stream_rows(D, COL_MG), stream_rows(D, COL_MS),
                  stream_rows(SGU_W, COL_SU), stream_rows(SGU_W, COL_SV), stream_rows(SGU_W, COL_SG),
                  stream_rows(GLA_VW, COL_GG),
                  seg_rows(ATTN_W), stream_rows(GLA_VW, 0), stream_rows(GLA_VW, 0),
                  full((1, GLA_VW)), full((1, SGU_W)), full((1, SGU_W)),
                  full((SGU_GROUPS, SGU_CHUNK, SGU_CHUNK)), full((SGU_CHUNK, SGU_W)),
                  full((ATTN_W, D)), full((GLA_VW, D)), full((SGU_W, D)), full((D, D)),
                  full((1, D)), full((1, D))],
        out_specs=pl.BlockSpec((None, tm, D), lambda b, i: (b, i, 0)),
        out_shape=jax.ShapeDtypeStruct((B, n_rows, D), F32),
        compiler_params=_cparams(("arbitrary", "arbitrary"), 56),
        name="merge_out",
    )(x_seg, modb, proj, proj, proj, proj, proj, proj, proj, ya_seg, of, ob,
      p["onorm"], p["lng"], p["lnb"], p["sw"], p["sb"], p["wa"], p["wg"], p["ws"], p["wo"], p["pg"], p["pb"])


W_BLK = 512
N_GATES = 2 * GLA_RANK


def _w_in_plan():
    sizes = (512, 128, 128, 512, 256, 256, 512, 16, 16, 512, 512, 512, 512, 1024, 1024, 1024)
    names = ("aq", "ak", "av", "ag", "gq", "gk", "gv", "af", "ab", "gg", "su", "sv", "sg", "ma", "mg", "ms")
    halved = ("ag", "gg", "sg", "ma", "mg", "ms")
    src, off = {}, 0
    for n, s in zip(names, sizes):
        src[n] = (off, s)
        off += s
    assert src["gk"][0] == src["gq"][0] + src["gq"][1] and src["av"][0] == src["ak"][0] + KV_W
    assert src["ab"][0] == src["af"][0] + GLA_RANK
    plan = []
    for n in ("ma", "mg", "ms", "aq", "ag", "gv", "gg", "su", "sv", "sg"):
        start, size = src[n]
        assert size % W_BLK == 0 and start % SUBLANES == 0
        plan += [(start + t * W_BLK, int(n in halved), 0) for t in range(size // W_BLK)]
    plan.append((src["gq"][0], 0, 0))
    plan.append((src["ak"][0], 0, 1))
    assert len(plan) * W_BLK == PROJ_W and src["ak"][0] + W_BLK <= off
    return plan, src["af"][0], off


def _wprep_kernel(src_ref, half_ref, special_ref, a_ref, g_ref, o_ref):
    j = pl.program_id(1)

    @pl.when(special_ref[j] == 0)
    def _():
        o_ref[...] = (a_ref[0] * jnp.where(half_ref[j] == 1, 0.5, 1.0)).astype(BF16)

    @pl.when(special_ref[j] == 1)
    def _():
        o_ref[0:2 * KV_W, :] = a_ref[0, 0:2 * KV_W, :].astype(BF16)
        o_ref[2 * KV_W:2 * KV_W + N_GATES, :] = g_ref[0].astype(BF16)
        o_ref[2 * KV_W + N_GATES:W_BLK, :] = jnp.zeros((W_BLK - 2 * KV_W - N_GATES, o_ref.shape[-1]), BF16)


def _regroup_w_in(w_in):
    L, D, width = w_in.shape
    plan, gate_col, used = _w_in_plan()
    assert used == width
    src = jnp.array([p[0] for p in plan], jnp.int32)
    half = jnp.array([p[1] for p in plan], jnp.int32)
    special = jnp.array([p[2] for p in plan], jnp.int32)
    wt = jnp.swapaxes(w_in, 1, 2)
    grid_spec = pltpu.PrefetchScalarGridSpec(
        num_scalar_prefetch=3,
        grid=(L, len(plan)),
        in_specs=[pl.BlockSpec((pl.Element(1), pl.Element(W_BLK), pl.Element(D)),
                               lambda l, j, s, h, sp: (l, pl.multiple_of(s[j], SUBLANES), 0)),
                  pl.BlockSpec((pl.Element(1), pl.Element(N_GATES), pl.Element(D)),
                               lambda l, j, s, h, sp: (l, gate_col, 0))],
        out_specs=pl.BlockSpec((None, W_BLK, D), lambda l, j, s, h, sp: (l, j, 0)))
    return pl.pallas_call(
        _wprep_kernel,
        grid_spec=grid_spec,
        out_shape=jax.ShapeDtypeStruct((L, PROJ_W, D), BF16),
        compiler_params=_cparams(("arbitrary", "arbitrary"), 32),
        name="w_in_prep",
    )(src, half, special, wt, wt)


def _rope_tables(n_lat, ctx_len):
    rows = n_lat // GRID_W
    row = np.repeat(np.arange(rows, dtype=np.float64), GRID_W)
    col = np.tile(np.arange(GRID_W, dtype=np.float64), rows)
    freqs = ROPE_THETA ** (-np.arange(0, ROPE_AXIS, 2, dtype=np.float64) / ROPE_AXIS)
    ang_r, ang_c = row[:, None] * freqs, col[:, None] * freqs
    cos = np.concatenate([np.cos(ang_r)] * 2 + [np.cos(ang_c)] * 2, axis=1)
    sin = np.concatenate([-np.sin(ang_r), np.sin(ang_r), -np.sin(ang_c), np.sin(ang_c)], axis=1)
    cos = np.concatenate([np.ones((ctx_len, HEAD_DIM)), cos], axis=0)
    sin = np.concatenate([np.zeros((ctx_len, HEAD_DIM)), sin], axis=0)
    return (jnp.asarray(np.concatenate([cos, cos], axis=1), F32),
            jnp.asarray(np.concatenate([sin, sin], axis=1), F32))


def _largest_tile(total, cap, mult):
    best = mult
    for t in range(mult, cap + 1, mult):
        if total % t == 0:
            best = t
    return best


def kernel(x, c, ctx, c_ctx, w_ada, b_ada, w_in, attn_q_norm, attn_k_norm, gla_a2_f, gla_ab_f, gla_a2_b, gla_ab_b, gla_o_norm, sgu_ln_g, sgu_ln_b, sgu_w, sgu_b, w_br_attn, w_br_gla, w_br_sgu, w_out, post_ln_g, post_ln_b):
    B, n_lat, D = x.shape
    ctx_len = ctx.shape[1]
    depth = w_in.shape[0]
    T = ctx_len + n_lat
    assert D == D_MODEL and B <= 7
    assert ctx_len % ATT_TQ == 0 and n_lat % ATT_TQ == 0 and n_lat % GRID_W == 0
    alpha = (2 * depth) ** 0.25

    tm_proj = _largest_tile(T, 2112, 16)
    tn_proj = 1536
    tk_att = _largest_tile(T, ATT_TK_MAX, ATT_TQ)
    assert ctx_len <= tk_att
    tm_prep = tk_att
    tm_merge_lat = _largest_tile(n_lat, 512, SGU_CHUNK)
    tm_merge_ctx = _largest_tile(ctx_len, 512, SGU_CHUNK)

    x_ctx, x_lat = ctx, x
    cc = jnp.zeros((8, D), F32).at[0:B].set(c).at[B].set(c_ctx)
    mod = _ada(cc, w_ada, b_ada.reshape(depth, 1, 3 * D))
    cos_t, sin_t = _rope_tables(n_lat, ctx_len)
    consts = {
        "gq": jnp.kron(jnp.eye(N_HEADS, dtype=F32), jnp.ones((HEAD_DIM, HEAD_DIM), F32)).astype(BF16),
        "gk": jnp.kron(jnp.eye(N_KV, dtype=F32), jnp.ones((HEAD_DIM, HEAD_DIM), F32)).astype(BF16),
        "eq": jnp.eye(ATTN_W, dtype=BF16),
        "ek": jnp.eye(KV_W, dtype=BF16),
    }
    eye_tq = jnp.eye(ATT_TQ, dtype=BF16)
    w_all = _regroup_w_in(w_in)

    for l in range(depth):
        last = l == depth - 1
        modb = jnp.zeros((B, 8, 3 * D), F32).at[:, 0].set(mod[l, 0:B]).at[:, 1].set(mod[l, B][None])
        proj = _inproj(x_lat, x_ctx, modb, w_all, l, tm_proj, tn_proj)
        qn = jnp.concatenate([attn_q_norm[l]] * 2)[None]
        kn = jnp.concatenate([attn_k_norm[l]] * 2)[None]
        qt, kk, vt = _qkprep(proj, cos_t, sin_t, qn, kn, consts, tm_prep, tk_att)
        score_bound = HEAD_DIM * jnp.max(jnp.abs(attn_q_norm[l])) * jnp.max(jnp.abs(attn_k_norm[l])) * Q_SCALE
        bounded = (score_bound <= SCORE_LIMIT).astype(jnp.int32).reshape(1)
        ctx_blocks, lat_blocks = ctx_len // ATT_TQ, n_lat // ATT_TQ
        ya_lat = _attention(bounded, qt, kk, vt, proj, eye_tq, ctx_blocks, lat_blocks, ATT_NSB, T // tk_att, tk_att)
        ya_ctx = None if last else _attention(bounded, qt, kk, vt, proj, eye_tq, 0, ctx_blocks, 1, 1, ctx_len)
        a2f = jnp.zeros((LANES, GLA_KW), F32).at[0:GLA_RANK].set(gla_a2_f[l]).astype(BF16)
        a2b = jnp.zeros((LANES, GLA_KW), F32).at[GLA_RANK:2 * GLA_RANK].set(gla_a2_b[l]).astype(BF16)
        of, ob = _gla(proj, a2f, gla_ab_f[l][None], a2b, gla_ab_b[l][None], ctx_len)
        p = {
            "onorm": jnp.concatenate([gla_o_norm[l]] * GLA_HEADS)[None],
            "lng": sgu_ln_g[l][None], "lnb": sgu_ln_b[l][None],
            "sw": sgu_w[l].astype(BF16),
            "sb": jnp.repeat(sgu_b[l].T, SGU_W // SGU_GROUPS, axis=1),
            "wa": (0.5 * w_br_attn[l]).astype(BF16), "wg": (0.5 * w_br_gla[l]).astype(BF16),
            "ws": (0.5 * w_br_sgu[l]).astype(BF16),
            "wo": w_out[l].astype(BF16), "pg": post_ln_g[l][None], "pb": post_ln_b[l][None],
        }
        if not last:
            x_ctx = _merge(x_ctx, ya_ctx, 0, 1, modb, proj, of, ob, p, tm_merge_ctx, alpha)
        x_lat = _merge(x_lat, ya_lat, ctx_len, 0, modb, proj, of, ob, p, tm_merge_lat, alpha)
    return x_lat
```

```python
import functools

import jax
import jax.numpy as jnp
import numpy as np
from jax import lax
from jax.experimental import pallas as pl
from jax.experimental.pallas import tpu as pltpu

F32 = jnp.float32
BF16 = jnp.bfloat16

D_MODEL = 1024
GRID_W = 64
HEAD_DIM = 64
N_HEADS = 8
N_KV = 2
Q_PER_KV = N_HEADS // N_KV
ATTN_W = N_HEADS * HEAD_DIM
KV_W = N_KV * HEAD_DIM
ROPE_THETA = 10000.0
ROPE_AXIS = HEAD_DIM // 2
GLA_HEADS = 4
GLA_DK = 64
GLA_DV = 128
GLA_KW = GLA_HEADS * GLA_DK
GLA_VW = GLA_HEADS * GLA_DV
GLA_RANK = 16
GLA_TEMP = 16.0
SGU_GROUPS = 4
SGU_CHUNK = 128
SGU_W = 512
EPS = 1e-6

LANES = 128
V7X_VMEM_BYTES = 64 * 1024 * 1024

COL_MA, COL_MG, COL_MS = 0, 1024, 2048
COL_AQ, COL_AG = 3072, 3584
COL_GV, COL_GG = 4096, 4608
COL_SU, COL_SV, COL_SG = 5120, 5632, 6144
COL_GQK = 6656
COL_AKV = 7168
COL_GATE = 7424
PROJ_W = 7680

SUBLANES = 8
ATT_TQ = 256
ATT_NSB = 4
ATT_TK_MAX = 768
GLA_C = 128
GLA_SUB = 16
GLA_STEPS = 2
ATT_UNROLL = 4
Q_SCALE = 1.4426950408889634 * HEAD_DIM ** -0.5
SCORE_LIMIT = 96.0


def _cparams(sem, vmem_mb):
    return pltpu.CompilerParams(dimension_semantics=sem,
                                vmem_limit_bytes=min(vmem_mb * 1024 * 1024, V7X_VMEM_BYTES - (4 << 20)))


def _silu(x):
    return x * jax.nn.sigmoid(x)


def _silu_from_half(u):
    return u + u * jnp.tanh(u)


def _dot(a, b):
    return jnp.dot(a, b, preferred_element_type=F32)


def _dot_nt(a, b):
    return lax.dot_general(a, b, (((1,), (1,)), ((), ())), preferred_element_type=F32)


def _ada_kernel(c_ref, w_ref, b_ref, o_ref):
    c = c_ref[...]
    o_ref[...] = _dot(_silu(c).astype(BF16), w_ref[...].astype(BF16)) + b_ref[...]


def _ada(cc, w_ada, b_ada):
    L, D = w_ada.shape[0], w_ada.shape[1]
    return pl.pallas_call(
        _ada_kernel,
        grid=(L, 3),
        in_specs=[pl.BlockSpec((8, D), lambda l, j: (0, 0)),
                  pl.BlockSpec((None, D, D), lambda l, j: (l, 0, j)),
                  pl.BlockSpec((None, 1, D), lambda l, j: (l, 0, j))],
        out_specs=pl.BlockSpec((None, 8, D), lambda l, j: (l, 0, j)),
        out_shape=jax.ShapeDtypeStruct((L, 8, 3 * D), F32),
        compiler_params=_cparams(("arbitrary", "arbitrary"), 32),
        name="ada_mod",
    )(cc, w_ada, b_ada)


def _inproj_kernel(x_ref, ctx_ref, mod_ref, w_ref, o_ref, h_ref, *, tm, ctx_len):
    i = pl.program_id(1)
    j = pl.program_id(2)
    D = x_ref.shape[-1]

    def modulated(v, r):
        return (v * (1.0 + mod_ref[r:r + 1, D:2 * D]) + mod_ref[r:r + 1, 0:D]).astype(BF16)

    @pl.when((j == 0) & (i == 0))
    def _():
        h_ref[0:ctx_len, :] = modulated(ctx_ref[...], 1)
        h_ref[ctx_len:tm, :] = modulated(x_ref[0, 0:tm - ctx_len, :], 0)

    @pl.when((j == 0) & (i > 0))
    def _():
        h_ref[...] = modulated(x_ref[0], 0)

    o_ref[...] = _dot_nt(h_ref[...], w_ref[...]).astype(BF16)


def _inproj(x, ctx, modb, w_all, layer, tm, tn):
    B, n_lat, D = x.shape
    ctx_len = ctx.shape[1]
    T = n_lat + ctx_len
    assert ctx_len < tm and T % tm == 0 and (tm - ctx_len) % SUBLANES == 0
    x_spec = pl.BlockSpec((pl.Element(1), pl.Element(tm), pl.Element(D)),
                          lambda b, i, j: (b, pl.multiple_of(jnp.maximum(i * tm - ctx_len, 0), SUBLANES), 0))
    return pl.pallas_call(
        functools.partial(_inproj_kernel, tm=tm, ctx_len=ctx_len),
        grid=(B, T // tm, PROJ_W // tn),
        in_specs=[x_spec,
                  pl.BlockSpec((None, ctx_len, D), lambda b, i, j: (b, 0, 0)),
                  pl.BlockSpec((None, 8, 3 * D), lambda b, i, j: (b, 0, 0)),
                  pl.BlockSpec((None, tn, D), lambda b, i, j: (layer, j, 0))],
        out_specs=pl.BlockSpec((None, tm, tn), lambda b, i, j: (b, i, j)),
        out_shape=jax.ShapeDtypeStruct((B, T, PROJ_W), BF16),
        scratch_shapes=[pltpu.VMEM((tm, D), BF16)],
        compiler_params=_cparams(("arbitrary", "arbitrary", "arbitrary"), 52),
        name="in_proj",
    )(x, ctx, modb, w_all)


def _rope(y, cos, sin, width):
    lane = lax.broadcasted_iota(jnp.int32, (1, width), 1)
    first = jnp.bitwise_and(lane, ROPE_AXIS - 1) < (ROPE_AXIS // 2)
    up = pltpu.roll(y, width - ROPE_AXIS // 2, 1)
    dn = pltpu.roll(y, ROPE_AXIS // 2, 1)
    return y * cos + jnp.where(first, up, dn) * sin


def _qkprep_kernel(aq_ref, akv_ref, cos_ref, sin_ref, qn_ref, kn_ref, gq_ref, gk_ref,
                   eq_ref, ek_ref, qt_ref, k_ref, vt_ref, *, tm, tk):
    cos = cos_ref[...]
    sin = sin_ref[...]
    aq = aq_ref[...].astype(F32)
    ssq = _dot((aq * aq).astype(BF16), gq_ref[...])
    akv = akv_ref[...].astype(F32)
    ak = akv[:, 0:KV_W]
    ssk = _dot((ak * ak).astype(BF16), gk_ref[...])
    vt = _dot_nt(ek_ref[...], akv_ref[:, KV_W:2 * KV_W]).astype(BF16)
    yq = aq * lax.rsqrt(ssq * (1.0 / HEAD_DIM) + EPS) * jnp.concatenate([qn_ref[...]] * 4, axis=1)
    yq = _rope(yq, jnp.concatenate([cos] * 4, axis=1), jnp.concatenate([sin] * 4, axis=1), ATTN_W)
    yq = (yq * Q_SCALE).astype(BF16)
    qt = _dot_nt(eq_ref[...], yq).astype(BF16)
    zeros = jnp.zeros((HEAD_DIM, ATT_TQ), BF16)
    for g in range(N_KV):
        for qb in range(tm // ATT_TQ):
            for r in range(Q_PER_KV):
                h = g * Q_PER_KV + r
                cols = slice((qb * Q_PER_KV + r) * ATT_TQ, (qb * Q_PER_KV + r + 1) * ATT_TQ)
                qt_ref[g, g * HEAD_DIM:(g + 1) * HEAD_DIM, cols] = qt[h * HEAD_DIM:(h + 1) * HEAD_DIM,
                                                                      qb * ATT_TQ:(qb + 1) * ATT_TQ]
                qt_ref[g, (1 - g) * HEAD_DIM:(2 - g) * HEAD_DIM, cols] = zeros
    yk = ak * lax.rsqrt(ssk * (1.0 / HEAD_DIM) + EPS) * kn_ref[...]
    yk = _rope(yk, cos, sin, KV_W).astype(BF16)
    nkb = tm // tk
    k_ref[...] = yk.reshape(nkb, tk, KV_W)
    for g in range(N_KV):
        for n in range(nkb):
            vt_ref[g, n] = vt[g * HEAD_DIM:(g + 1) * HEAD_DIM, n * tk:(n + 1) * tk]


def _qkprep(proj, cos_t, sin_t, qn, kn, consts, tm, tk):
    B, T, _ = proj.shape
    nkb_t = T // tk
    nkb = tm // tk
    full = lambda shape: pl.BlockSpec(shape, lambda b, i: (0,) * len(shape))
    return pl.pallas_call(
        functools.partial(_qkprep_kernel, tm=tm, tk=tk),
        grid=(B, T // tm),
        in_specs=[pl.BlockSpec((None, tm, ATTN_W), lambda b, i: (b, i, COL_AQ // ATTN_W)),
                  pl.BlockSpec((None, tm, 2 * KV_W), lambda b, i: (b, i, COL_AKV // (2 * KV_W))),
                  pl.BlockSpec((tm, LANES), lambda b, i: (i, 0)),
                  pl.BlockSpec((tm, LANES), lambda b, i: (i, 0)),
                  full((1, LANES)), full((1, LANES)),
                  full((ATTN_W, ATTN_W)), full((KV_W, KV_W)),
                  full((ATTN_W, ATTN_W)), full((KV_W, KV_W))],
        out_specs=[pl.BlockSpec((None, N_KV, KV_W, Q_PER_KV * tm), lambda b, i: (b, 0, 0, i)),
                   pl.BlockSpec((None, nkb, tk, KV_W), lambda b, i: (b, i, 0, 0)),
                   pl.BlockSpec((None, N_KV, nkb, HEAD_DIM, tk), lambda b, i: (b, 0, i, 0, 0))],
        out_shape=[jax.ShapeDtypeStruct((B, N_KV, KV_W, Q_PER_KV * T), BF16),
                   jax.ShapeDtypeStruct((B, nkb_t, tk, KV_W), BF16),
                   jax.ShapeDtypeStruct((B, N_KV, nkb_t, HEAD_DIM, tk), BF16)],
        compiler_params=_cparams(("arbitrary", "arbitrary"), 40),
        name="qkv_prep",
    )(proj, proj, cos_t, sin_t, qn, kn, consts["gq"], consts["gk"], consts["eq"], consts["ek"])


def _attn_kernel(bounded_ref, *refs, nsb, n_tiles, rows):
    q_refs, (k_ref, vt_ref), ag_refs, eye_ref = refs[0:nsb], refs[nsb:nsb + 2], refs[nsb + 2:2 * nsb + 2], refs[2 * nsb + 2]
    o_ref, p_ref, acc_ref, oall_ref, qall_ref = refs[-5:]
    nq = q_refs[0].shape[-1]
    tq = nq // Q_PER_KV
    sl = acc_ref.shape[1] - HEAD_DIM
    U = p_ref.shape[0]
    acc_ref[...] = jnp.zeros_like(acc_ref)

    def stage_of(s):
        if isinstance(s, int):
            return divmod(s, n_tiles)
        return lax.div(s, n_tiles), lax.rem(s, n_tiles)

    def weights(s, slot):
        sb, j = stage_of(s)
        p = jnp.exp2(_dot(k_ref[j, 0:rows, :], qall_ref[sb]))
        p_ref[slot, 0:rows, :] = p.astype(BF16)
        acc_ref[sb, HEAD_DIM:HEAD_DIM + sl, :] += jnp.sum(p.reshape(rows // sl, sl, nq), axis=0)

    def weighted_values(s, slot):
        sb, j = stage_of(s)
        acc_ref[sb, 0:HEAD_DIM, :] += _dot(vt_ref[j, :, 0:rows], p_ref[slot, 0:rows, :])

    def run():
        stages = nsb * n_tiles
        for sb in range(nsb):
            qall_ref[sb] = q_refs[sb][...]
        weights(0, 0)

        def group(i, carry):
            for u in range(U):
                weights(U * i + u + 1, (u + 1) % U)
                weighted_values(U * i + u, u)
            return carry

        groups = (stages - 1) // U
        lax.fori_loop(0, groups, group, 0)
        for s in range(U * groups, stages):
            if s + 1 < stages:
                weights(s + 1, (s + 1) % U)
            weighted_values(s, s % U)

    def run_online(sb):
        for r in range(Q_PER_KV):
            q = q_refs[sb][:, r * tq:(r + 1) * tq]

            def body(j, carry):
                m, den, acc = carry
                s = _dot(k_ref[j, 0:rows, :], q)
                m_new = jnp.maximum(m, jnp.max(s, axis=0, keepdims=True))
                p = jnp.exp2(s - m_new)
                alpha = jnp.exp2(m - m_new)
                return (m_new, alpha * den + jnp.sum(p, axis=0, keepdims=True),
                        alpha * acc + _dot(vt_ref[j, :, 0:rows], p.astype(BF16)))

            init = (jnp.full((1, tq), -jnp.inf, F32), jnp.zeros((1, tq), F32), jnp.zeros((HEAD_DIM, tq), F32))
            _, den, acc = lax.fori_loop(0, n_tiles, body, init)
            acc_ref[sb, 0:HEAD_DIM, r * tq:(r + 1) * tq] = acc
            acc_ref[sb, HEAD_DIM:HEAD_DIM + sl, r * tq:(r + 1) * tq] = jnp.broadcast_to(den * (1.0 / sl), (sl, tq))

    def finish(sb):
        for r in range(Q_PER_KV):
            acc = acc_ref[sb, :, r * tq:(r + 1) * tq]
            den = jnp.sum(acc[HEAD_DIM:HEAD_DIM + sl, :], axis=0, keepdims=True)
            o = acc[0:HEAD_DIM, :] * (1.0 / den)
            oall_ref[sb, r * HEAD_DIM:(r + 1) * HEAD_DIM, :] = o.astype(BF16)
        o_t = _dot_nt(eye_ref[...], oall_ref[sb])
        gate = _silu_from_half(ag_refs[sb][...].astype(F32))
        o_ref[sb * tq:(sb + 1) * tq, :] = (o_t * gate).astype(BF16)

    bounded = bounded_ref[0] == 1

    @pl.when(bounded)
    def _():
        run()
        for sb in range(nsb):
            finish(sb)

    @pl.when(jnp.logical_not(bounded))
    def _():
        for sb in range(nsb):
            run_online(sb)
            finish(sb)


def _attention(bounded, qt, k, vt, proj, eye_tq, q_block0, n_qblocks, nsb, n_tiles, rows):
    B = proj.shape[0]
    nkb, tk = k.shape[1], k.shape[2]
    gw = Q_PER_KV * HEAD_DIM
    nq = Q_PER_KV * ATT_TQ
    assert n_qblocks % nsb == 0

    def qmap(sb):
        return lambda b, g, i, f: (b, g, 0, q_block0 + i * nsb + sb)

    def agmap(sb):
        return lambda b, g, i, f: (b, q_block0 + i * nsb + sb, COL_AG // gw + g)

    in_specs = ([pl.BlockSpec((None, None, KV_W, nq), qmap(sb)) for sb in range(nsb)]
                + [pl.BlockSpec((None, nkb, tk, KV_W), lambda b, g, i, f: (b, 0, 0, 0)),
                   pl.BlockSpec((None, None, nkb, HEAD_DIM, tk), lambda b, g, i, f: (b, g, 0, 0, 0))]
                + [pl.BlockSpec((None, ATT_TQ, gw), agmap(sb)) for sb in range(nsb)]
                + [pl.BlockSpec((ATT_TQ, ATT_TQ), lambda b, g, i, f: (0, 0))])
    args = [bounded] + [qt] * nsb + [k, vt] + [proj] * nsb + [eye_tq]
    grid_spec = pltpu.PrefetchScalarGridSpec(
        num_scalar_prefetch=1,
        grid=(B, N_KV, n_qblocks // nsb),
        in_specs=in_specs,
        out_specs=pl.BlockSpec((None, nsb * ATT_TQ, gw), lambda b, g, i, f: (b, i, g)),
        scratch_shapes=[pltpu.VMEM((ATT_UNROLL, tk, nq), BF16),
                        pltpu.VMEM((nsb, HEAD_DIM + SUBLANES, nq), F32),
                        pltpu.VMEM((nsb, gw, ATT_TQ), BF16),
                        pltpu.VMEM((nsb, KV_W, nq), BF16)])
    return pl.pallas_call(
        functools.partial(_attn_kernel, nsb=nsb, n_tiles=n_tiles, rows=rows),
        grid_spec=grid_spec,
        out_shape=jax.ShapeDtypeStruct((B, n_qblocks * ATT_TQ, ATTN_W), BF16),
        compiler_params=_cparams(("arbitrary", "arbitrary", "arbitrary"), 40),
        name="gqa_attention",
    )(*args)


def _gla_direction(qk, v, pa, a2, ab, st_ref, o_ref, fwd):
    C, c = GLA_C, GLA_SUB
    q = qk[:, 0:GLA_KW].astype(F32) * (GLA_DK ** -0.5)
    k = qk[:, GLA_KW:2 * GLA_KW].astype(F32)
    z = _dot(pa, a2) + ab
    yield
    g = (jnp.minimum(z, 0.0) - jnp.log(1.0 + jnp.exp(-jnp.abs(z)))) * (1.0 / GLA_TEMP)
    row = lax.broadcasted_iota(jnp.int32, (C, C), 0)
    col = lax.broadcasted_iota(jnp.int32, (C, C), 1)
    blk = jnp.bitwise_and(row, -c)
    if fwd:
        tri = col <= row
        inblk = tri & (col >= blk)
    else:
        tri = col >= row
        inblk = tri & (col < blk + c)
    sel = jnp.concatenate([jnp.where(tri, 1.0, 0.0), jnp.where(inblk, 1.0, 0.0)], axis=0).astype(BF16)
    g_hi = g.astype(BF16)
    g_lo = (g - g_hi.astype(F32)).astype(BF16)
    cs = _dot(sel, jnp.concatenate([g_hi, g_lo], axis=1))
    yield
    cs = cs[:, 0:GLA_KW] + cs[:, GLA_KW:2 * GLA_KW]
    b = cs[0:C]
    d = cs[C:2 * C]
    ref = b - d
    total = b[C - 1:C, :] if fwd else b[0:1, :]
    q_in = (q * jnp.exp(d)).astype(BF16)
    q_st = (q * jnp.exp(b)).astype(BF16)
    k_out = (k * jnp.exp(total - b)).astype(BF16)
    w_tot = jnp.exp(total)
    zero = jnp.zeros((), BF16)
    nb = C // c
    k_blk = []
    for i in range(nb):
        lo, hi = (0, (i + 1) * c) if fwd else (i * c, C)
        kb = (k[lo:hi] * jnp.exp(ref[i * c:i * c + 1, :] - b[lo:hi])).astype(BF16)
        pad = jnp.zeros((C - (hi - lo), GLA_KW), BF16)
        k_blk.append(jnp.concatenate(([kb, pad] if fwd else [pad, kb]) if hi - lo < C else [kb], axis=0))
    head_of_lane = jnp.right_shift(lax.broadcasted_iota(jnp.int32, (1, GLA_KW), 1), GLA_DK.bit_length() - 1)
    a_two = []
    for m in range(nb // 2):
        qr = q_in[2 * m * c:(2 * m + 2) * c]
        q_heads = jnp.concatenate([jnp.where(head_of_lane == h, qr, zero) for h in range(GLA_HEADS)], axis=0)
        a_two.append(_dot_nt(q_heads, jnp.concatenate(k_blk[2 * m:2 * m + 2], axis=0)))
    yield

    def a_block(h, i):
        m, r = divmod(i, 2)
        return a_two[m][h * 2 * c + r * c:h * 2 * c + (r + 1) * c, r * C:(r + 1) * C]

    a_cat = jnp.concatenate(
        [jnp.where(tri, jnp.concatenate([a_block(h, i) for i in range(nb)], axis=0), 0.0)
         for h in range(GLA_HEADS)], axis=1).astype(BF16)
    vhead_of_lane = jnp.right_shift(lax.broadcasted_iota(jnp.int32, (1, GLA_VW), 1), GLA_DV.bit_length() - 1)
    v_bd = jnp.concatenate([jnp.where(vhead_of_lane == h, v, zero) for h in range(GLA_HEADS)], axis=0)
    st = st_ref[...]
    o = _dot(a_cat, v_bd) + _dot_nt(q_st, st.astype(BF16))
    upd = lax.dot_general(v, k_out, (((0,), (0,)), ((), ())), preferred_element_type=F32)
    same_head = jnp.right_shift(lax.broadcasted_iota(jnp.int32, (GLA_VW, 1), 0),
                                GLA_DV.bit_length() - 1) == head_of_lane
    st_ref[...] = jnp.where(same_head, st * w_tot + upd, 0.0)
    o_ref[...] = o
    yield


def _gla_kernel(qkf_ref, vf_ref, paf_ref, qkb_ref, vb_ref, pab_ref, a2f_ref, abf_ref, a2b_ref, abb_ref,
                of_ref, ob_ref, stf_ref, stb_ref):
    @pl.when(pl.program_id(0) == 0)
    def _():
        stf_ref[...] = jnp.zeros_like(stf_ref)
        stb_ref[...] = jnp.zeros_like(stb_ref)

    C = GLA_C
    waves = []
    for t in range(GLA_STEPS):
        fs = slice(t * C, (t + 1) * C)
        bs = slice((GLA_STEPS - 1 - t) * C, (GLA_STEPS - t) * C)
        chains = []
        for bi in range(qkf_ref.shape[0]):
            chains.append(_gla_direction(qkf_ref[bi, fs], vf_ref[bi, fs], paf_ref[bi, fs], a2f_ref[...], abf_ref[...],
                                         stf_ref.at[bi], of_ref.at[bi, fs], True))
            chains.append(_gla_direction(qkb_ref[bi, bs], vb_ref[bi, bs], pab_ref[bi, bs], a2b_ref[...], abb_ref[...],
                                         stb_ref.at[bi], ob_ref.at[bi, bs], False))
        waves.append(chains)
    for _ in range(3):
        for chains in waves:
            for chain in chains:
                next(chain)
    for chains in waves:
        for chain in chains:
            next(chain)


def _gla(proj, a2f, abf, a2b, abb, ctx_len):
    B, T = proj.shape[0], proj.shape[1]
    C = GLA_C * GLA_STEPS
    assert T % C == 0 and ctx_len % C == 0
    nch, nctx = T // C, ctx_len // C

    def cb(s):
        return jnp.where(s < nctx, nctx - 1 - s, nch + nctx - 1 - s)

    full = lambda shape: pl.BlockSpec(shape, lambda s: (0,) * len(shape))
    return pl.pallas_call(
        _gla_kernel,
        grid=(nch,),
        in_specs=[pl.BlockSpec((B, C, 2 * GLA_KW), lambda s: (0, s, COL_GQK // (2 * GLA_KW))),
                  pl.BlockSpec((B, C, GLA_VW), lambda s: (0, s, COL_GV // GLA_VW)),
                  pl.BlockSpec((B, C, LANES), lambda s: (0, s, COL_GATE // LANES)),
                  pl.BlockSpec((B, C, 2 * GLA_KW), lambda s: (0, cb(s), COL_GQK // (2 * GLA_KW))),
                  pl.BlockSpec((B, C, GLA_VW), lambda s: (0, cb(s), COL_GV // GLA_VW)),
                  pl.BlockSpec((B, C, LANES), lambda s: (0, cb(s), COL_GATE // LANES)),
                  full((LANES, GLA_KW)), full((1, GLA_KW)), full((LANES, GLA_KW)), full((1, GLA_KW))],
        out_specs=[pl.BlockSpec((B, C, GLA_VW), lambda s: (0, s, 0)),
                   pl.BlockSpec((B, C, GLA_VW), lambda s: (0, cb(s), 0))],
        out_shape=[jax.ShapeDtypeStruct((B, T, GLA_VW), F32),
                   jax.ShapeDtypeStruct((B, T, GLA_VW), F32)],
        scratch_shapes=[pltpu.VMEM((B, GLA_VW, GLA_KW), F32),
                        pltpu.VMEM((B, GLA_VW, GLA_KW), F32)],
        compiler_params=_cparams(("arbitrary",), 32),
        name="gla_scan",
    )(proj, proj, proj, proj, proj, proj, a2f, abf, a2b, abb)


def _merge_kernel(x_ref, mod_ref, ma_ref, mg_ref, ms_ref, su_ref, sv_ref, sg_ref, gg_ref, ya_ref, of_ref, ob_ref,
                  onorm_ref, lng_ref, lnb_ref, sw_ref, sb_ref, wa_ref, wg_ref, ws_ref, wo_ref, pg_ref, pb_ref,
                  o_ref, *, tm, mod_row, alpha):
    D = x_ref.shape[-1]
    ma_ref, mg_ref, ms_ref, su_ref, sv_ref, sg_ref, gg_ref, of_ref, ob_ref = (
        r.at[0] for r in (ma_ref, mg_ref, ms_ref, su_ref, sv_ref, sg_ref, gg_ref, of_ref, ob_ref))
    og = of_ref[...] + ob_ref[...]
    parts = []
    for h in range(GLA_HEADS):
        oh = og[:, h * GLA_DV:(h + 1) * GLA_DV]
        ms = jnp.mean(oh * oh, axis=-1, keepdims=True)
        parts.append(oh * lax.rsqrt(ms + EPS))
    yg = jnp.concatenate(parts, axis=1) * onorm_ref[...] * _silu_from_half(gg_ref[...]).astype(F32)
    sv = sv_ref[...].astype(F32)
    mu = jnp.mean(sv, axis=-1, keepdims=True)
    dv = sv - mu
    var = jnp.mean(dv * dv, axis=-1, keepdims=True)
    vn = (dv * lax.rsqrt(var + EPS) * lng_ref[...] + lnb_ref[...]).astype(BF16)
    gw = SGU_W // SGU_GROUPS
    rows = []
    for n in range(tm // SGU_CHUNK):
        cols = []
        for g in range(SGU_GROUPS):
            blk = vn[n * SGU_CHUNK:(n + 1) * SGU_CHUNK, g * gw:(g + 1) * gw]
            cols.append(_dot(sw_ref[g], blk))
        rows.append(jnp.concatenate(cols, axis=1) + sb_ref[...])
    mixed = jnp.concatenate(rows, axis=0)
    ys = su_ref[...].astype(F32) * mixed * _silu_from_half(sg_ref[...]).astype(F32)

    def gated(g_ref, y, w_ref):
        half = _dot(y, w_ref[...]).astype(BF16)
        return half + half * jnp.tanh(g_ref[...])

    m = (gated(ma_ref, ya_ref[...], wa_ref) + gated(mg_ref, yg.astype(BF16), wg_ref)
         + gated(ms_ref, ys.astype(BF16), ws_ref))
    out = _dot(m, wo_ref[...])
    r = alpha * x_ref[...] + mod_ref[mod_row:mod_row + 1, 2 * D:3 * D] * out
    mu = jnp.mean(r, axis=-1, keepdims=True)
    dr = r - mu
    var = jnp.mean(dr * dr, axis=-1, keepdims=True)
    o_ref[...] = dr * lax.rsqrt(var + EPS) * pg_ref[...] + pb_ref[...]


def _merge(x_seg, ya_seg, row0, mod_row, modb, proj, of, ob, p, tm, alpha):
    B, n_rows, D = x_seg.shape
    assert n_rows % tm == 0 and row0 % 16 == 0

    def stream_rows(w, col0):
        return pl.BlockSpec((pl.Element(1), pl.Element(tm), pl.Element(w)),
                            lambda b, i: (b, pl.multiple_of(row0 + i * tm, 16), col0))

    seg_rows = lambda w: pl.BlockSpec((None, tm, w), lambda b, i: (b, i, 0))
    full = lambda shape: pl.BlockSpec(shape, lambda b, i: (0,) * len(shape))
    return pl.pallas_call(
        functools.partial(_merge_kernel, tm=tm, mod_row=mod_row, alpha=alpha),
        grid=(B, n_rows // tm),
        in_specs=[seg_rows(D),
                  pl.BlockSpec((None, 8, 3 * D), lambda b, i: (b, 0, 0)),
                  stream_rows(D, COL_MA), stream_rows(D, COL_MG), stream_rows(D, COL_MS),
                  stream_rows(SGU_W, COL_SU), stream_rows(SGU_W, COL_SV), stream_rows(SGU_W, COL_SG),
                  stream_rows(GLA_VW, COL_GG),
                  seg_rows(ATTN_W), stream_rows(GLA_VW, 0), stream_rows(GLA_VW, 0),
                  full((1, GLA_VW)), full((1, SGU_W)), full((1, SGU_W)),
                  full((SGU_GROUPS, SGU_CHUNK, SGU_CHUNK)), full((SGU_CHUNK, SGU_W)),
                  full((ATTN_W, D)), full((GLA_VW, D)), full((SGU_W, D)), full((D, D)),
                  full((1, D)), full((1, D))],
        out_specs=pl.BlockSpec((None, tm, D), lambda b, i: (b, i, 0)),
        out_shape=jax.ShapeDtypeStruct((B, n_rows, D), F32),
        compiler_params=_cparams(("arbitrary", "arbitrary"), 56),
        name="merge_out",
    )(x_seg, modb, proj, proj, proj, proj, proj, proj, proj, ya_seg, of, ob,
      p["onorm"], p["lng"], p["lnb"], p["sw"], p["sb"], p["wa"], p["wg"], p["ws"], p["wo"], p["pg"], p["pb"])


W_BLK = 512
N_GATES = 2 * GLA_RANK


def _w_in_plan():
    sizes = (512, 128, 128, 512, 256, 256, 512, 16, 16, 512, 512, 512, 512, 1024, 1024, 1024)
    names = ("aq", "ak", "av", "ag", "gq", "gk", "gv", "af", "ab", "gg", "su", "sv", "sg", "ma", "mg", "ms")
    halved = ("ag", "gg", "sg", "ma", "mg", "ms")
    src, off = {}, 0
    for n, s in zip(names, sizes):
        src[n] = (off, s)
        off += s
    assert src["gk"][0] == src["gq"][0] + src["gq"][1] and src["av"][0] == src["ak"][0] + KV_W
    assert src["ab"][0] == src["af"][0] + GLA_RANK
    plan = []
    for n in ("ma", "mg", "ms", "aq", "ag", "gv", "gg", "su", "sv", "sg"):
        start, size = src[n]
        assert size % W_BLK == 0 and start % SUBLANES == 0
        plan += [(start + t * W_BLK, int(n in halved), 0) for t in range(size // W_BLK)]
    plan.append((src["gq"][0], 0, 0))
    plan.append((src["ak"][0], 0, 1))
    assert len(plan) * W_BLK == PROJ_W and src["ak"][0] + W_BLK <= off
    return plan, src["af"][0], off


def _wprep_kernel(src_ref, half_ref, special_ref, a_ref, g_ref, o_ref):
    j = pl.program_id(1)

    @pl.when(special_ref[j] == 0)
    def _():
        o_ref[...] = (a_ref[0] * jnp.where(half_ref[j] == 1, 0.5, 1.0)).astype(BF16)

    @pl.when(special_ref[j] == 1)
    def _():
        o_ref[0:2 * KV_W, :] = a_ref[0, 0:2 * KV_W, :].astype(BF16)
        o_ref[2 * KV_W:2 * KV_W + N_GATES, :] = g_ref[0].astype(BF16)
        o_ref[2 * KV_W + N_GATES:W_BLK, :] = jnp.zeros((W_BLK - 2 * KV_W - N_GATES, o_ref.shape[-1]), BF16)


def _regroup_w_in(w_in):
    L, D, width = w_in.shape
    plan, gate_col, used = _w_in_plan()
    assert used == width
    src = jnp.array([p[0] for p in plan], jnp.int32)
    half = jnp.array([p[1] for p in plan], jnp.int32)
    special = jnp.array([p[2] for p in plan], jnp.int32)
    wt = jnp.swapaxes(w_in, 1, 2)
    grid_spec = pltpu.PrefetchScalarGridSpec(
        num_scalar_prefetch=3,
        grid=(L, len(plan)),
        in_specs=[pl.BlockSpec((pl.Element(1), pl.Element(W_BLK), pl.Element(D)),
                               lambda l, j, s, h, sp: (l, pl.multiple_of(s[j], SUBLANES), 0)),
                  pl.BlockSpec((pl.Element(1), pl.Element(N_GATES), pl.Element(D)),
                               lambda l, j, s, h, sp: (l, gate_col, 0))],
        out_specs=pl.BlockSpec((None, W_BLK, D), lambda l, j, s, h, sp: (l, j, 0)))
    return pl.pallas_call(
        _wprep_kernel,
        grid_spec=grid_spec,
        out_shape=jax.ShapeDtypeStruct((L, PROJ_W, D), BF16),
        compiler_params=_cparams(("arbitrary", "arbitrary"), 32),
        name="w_in_prep",
    )(src, half, special, wt, wt)


def _rope_tables(n_lat, ctx_len):
    rows = n_lat // GRID_W
    row = np.repeat(np.arange(rows, dtype=np.float64), GRID_W)
    col = np.tile(np.arange(GRID_W, dtype=np.float64), rows)
    freqs = ROPE_THETA ** (-np.arange(0, ROPE_AXIS, 2, dtype=np.float64) / ROPE_AXIS)
    ang_r, ang_c = row[:, None] * freqs, col[:, None] * freqs
    cos = np.concatenate([np.cos(ang_r)] * 2 + [np.cos(ang_c)] * 2, axis=1)
    sin = np.concatenate([-np.sin(ang_r), np.sin(ang_r), -np.sin(ang_c), np.sin(ang_c)], axis=1)
    cos = np.concatenate([np.ones((ctx_len, HEAD_DIM)), cos], axis=0)
    sin = np.concatenate([np.zeros((ctx_len, HEAD_DIM)), sin], axis=0)
    return (jnp.asarray(np.concatenate([cos, cos], axis=1), F32),
            jnp.asarray(np.concatenate([sin, sin], axis=1), F32))


def _largest_tile(total, cap, mult):
    best = mult
    for t in range(mult, cap + 1, mult):
        if total % t == 0:
            best = t
    return best


def kernel(x, c, ctx, c_ctx, w_ada, b_ada, w_in, attn_q_norm, attn_k_norm, gla_a2_f, gla_ab_f, gla_a2_b, gla_ab_b, gla_o_norm, sgu_ln_g, sgu_ln_b, sgu_w, sgu_b, w_br_attn, w_br_gla, w_br_sgu, w_out, post_ln_g, post_ln_b):
    B, n_lat, D = x.shape
    ctx_len = ctx.shape[1]
    depth = w_in.shape[0]
    T = ctx_len + n_lat
    assert D == D_MODEL and B <= 7
    assert ctx_len % ATT_TQ == 0 and n_lat % ATT_TQ == 0 and n_lat % GRID_W == 0
    alpha = (2 * depth) ** 0.25

    tm_proj = _largest_tile(T, 2112, 16)
    tn_proj = 1536
    tk_att = _largest_tile(T, ATT_TK_MAX, ATT_TQ)
    assert ctx_len <= tk_att
    tm_prep = tk_att
    tm_merge_lat = _largest_tile(n_lat, 512, SGU_CHUNK)
    tm_merge_ctx = _largest_tile(ctx_len, 512, SGU_CHUNK)

    x_ctx, x_lat = ctx, x
    cc = jnp.zeros((8, D), F32).at[0:B].set(c).at[B].set(c_ctx)
    mod = _ada(cc, w_ada, b_ada.reshape(depth, 1, 3 * D))
    cos_t, sin_t = _rope_tables(n_lat, ctx_len)
    consts = {
        "gq": jnp.kron(jnp.eye(N_HEADS, dtype=F32), jnp.ones((HEAD_DIM, HEAD_DIM), F32)).astype(BF16),
        "gk": jnp.kron(jnp.eye(N_KV, dtype=F32), jnp.ones((HEAD_DIM, HEAD_DIM), F32)).astype(BF16),
        "eq": jnp.eye(ATTN_W, dtype=BF16),
        "ek": jnp.eye(KV_W, dtype=BF16),
    }
    eye_tq = jnp.eye(ATT_TQ, dtype=BF16)
    w_all = _regroup_w_in(w_in)

    for l in range(depth):
        last = l == depth - 1
        modb = jnp.zeros((B, 8, 3 * D), F32).at[:, 0].set(mod[l, 0:B]).at[:, 1].set(mod[l, B][None])
        proj = _inproj(x_lat, x_ctx, modb, w_all, l, tm_proj, tn_proj)
        qn = jnp.concatenate([attn_q_norm[l]] * 2)[None]
        kn = jnp.concatenate([attn_k_norm[l]] * 2)[None]
        qt, kk, vt = _qkprep(proj, cos_t, sin_t, qn, kn, consts, tm_prep, tk_att)
        score_bound = HEAD_DIM * jnp.max(jnp.abs(attn_q_norm[l])) * jnp.max(jnp.abs(attn_k_norm[l])) * Q_SCALE
        bounded = (score_bound <= SCORE_LIMIT).astype(jnp.int32).reshape(1)
        ctx_blocks, lat_blocks = ctx_len // ATT_TQ, n_lat // ATT_TQ
        ya_lat = _attention(bounded, qt, kk, vt, proj, eye_tq, ctx_blocks, lat_blocks, ATT_NSB, T // tk_att, tk_att)
        ya_ctx = None if last else _attention(bounded, qt, kk, vt, proj, eye_tq, 0, ctx_blocks, 1, 1, ctx_len)
        a2f = jnp.zeros((LANES, GLA_KW), F32).at[0:GLA_RANK].set(gla_a2_f[l]).astype(BF16)
        a2b = jnp.zeros((LANES, GLA_KW), F32).at[GLA_RANK:2 * GLA_RANK].set(gla_a2_b[l]).astype(BF16)
        of, ob = _gla(proj, a2f, gla_ab_f[l][None], a2b, gla_ab_b[l][None], ctx_len)
        p = {
            "onorm": jnp.concatenate([gla_o_norm[l]] * GLA_HEADS)[None],
            "lng": sgu_ln_g[l][None], "lnb": sgu_ln_b[l][None],
            "sw": sgu_w[l].astype(BF16),
            "sb": jnp.repeat(sgu_b[l].T, SGU_W // SGU_GROUPS, axis=1),
            "wa": (0.5 * w_br_attn[l]).astype(BF16), "wg": (0.5 * w_br_gla[l]).astype(BF16),
            "ws": (0.5 * w_br_sgu[l]).astype(BF16),
            "wo": w_out[l].astype(BF16), "pg": post_ln_g[l][None], "pb": post_ln_b[l][None],
        }
        if not last:
            x_ctx = _merge(x_ctx, ya_ctx, 0, 1, modb, proj, of, ob, p, tm_merge_ctx, alpha)
        x_lat = _merge(x_lat, ya_lat, ctx_len, 0, modb, proj, of, ob, p, tm_merge_lat, alpha)
    return x_lat
```

```python
import functools

import jax
import jax.numpy as jnp
import numpy as np
from jax import lax
from jax.experimental import pallas as pl
from jax.experimental.pallas import tpu as pltpu

F32 = jnp.float32
BF16 = jnp.bfloat16

D_MODEL = 1024
GRID_W = 64
HEAD_DIM = 64
N_HEADS = 8
N_KV = 2
Q_PER_KV = N_HEADS // N_KV
ATTN_W = N_HEADS * HEAD_DIM
KV_W = N_KV * HEAD_DIM
ROPE_THETA = 10000.0
ROPE_AXIS = HEAD_DIM // 2
GLA_HEADS = 4
GLA_DK = 64
GLA_DV = 128
GLA_KW = GLA_HEADS * GLA_DK
GLA_VW = GLA_HEADS * GLA_DV
GLA_RANK = 16
GLA_TEMP = 16.0
SGU_GROUPS = 4
SGU_CHUNK = 128
SGU_W = 512
EPS = 1e-6

LANES = 128
V7X_VMEM_BYTES = 64 * 1024 * 1024

COL_MA, COL_MG, COL_MS = 0, 1024, 2048
COL_AQ, COL_AG = 3072, 3584
COL_GV, COL_GG = 4096, 4608
COL_SU, COL_SV, COL_SG = 5120, 5632, 6144
COL_GQK = 6656
COL_AKV = 7168
COL_GATE = 7424
PROJ_W = 7680

SUBLANES = 8
ATT_TQ = 256
ATT_NSB = 4
ATT_TK_MAX = 768
GLA_C = 128
GLA_SUB = 16
GLA_STEPS = 2
ATT_UNROLL = 4
Q_SCALE = 1.4426950408889634 * HEAD_DIM ** -0.5
SCORE_LIMIT = 96.0


def _cparams(sem, vmem_mb):
    return pltpu.CompilerParams(dimension_semantics=sem,
                                vmem_limit_bytes=min(vmem_mb * 1024 * 1024, V7X_VMEM_BYTES - (4 << 20)))


def _silu(x):
    return x * jax.nn.sigmoid(x)


def _silu_from_half(u):
    return u + u * jnp.tanh(u)


def _dot(a, b):
    return jnp.dot(a, b, preferred_element_type=F32)


def _dot_nt(a, b):
    return lax.dot_general(a, b, (((1,), (1,)), ((), ())), preferred_element_type=F32)


def _ada_kernel(c_ref, w_ref, b_ref, o_ref):
    c = c_ref[...]
    o_ref[...] = _dot(_silu(c).astype(BF16), w_ref[...].astype(BF16)) + b_ref[...]


def _ada(cc, w_ada, b_ada):
    L, D = w_ada.shape[0], w_ada.shape[1]
    return pl.pallas_call(
        _ada_kernel,
        grid=(L, 3),
        in_specs=[pl.BlockSpec((8, D), lambda l, j: (0, 0)),
                  pl.BlockSpec((None, D, D), lambda l, j: (l, 0, j)),
                  pl.BlockSpec((None, 1, D), lambda l, j: (l, 0, j))],
        out_specs=pl.BlockSpec((None, 8, D), lambda l, j: (l, 0, j)),
        out_shape=jax.ShapeDtypeStruct((L, 8, 3 * D), F32),
        compiler_params=_cparams(("arbitrary", "arbitrary"), 32),
        name="ada_mod",
    )(cc, w_ada, b_ada)


def _inproj_kernel(x_ref, ctx_ref, mod_ref, w_ref, o_ref, h_ref, *, tm, ctx_len):
    i = pl.program_id(1)
    j = pl.program_id(2)
    D = x_ref.shape[-1]

    def modulated(v, r):
        return (v * (1.0 + mod_ref[r:r + 1, D:2 * D]) + mod_ref[r:r + 1, 0:D]).astype(BF16)

    @pl.when((j == 0) & (i == 0))
    def _():
        h_ref[0:ctx_len, :] = modulated(ctx_ref[...], 1)
        h_ref[ctx_len:tm, :] = modulated(x_ref[0, 0:tm - ctx_len, :], 0)

    @pl.when((j == 0) & (i > 0))
    def _():
        h_ref[...] = modulated(x_ref[0], 0)

    o_ref[...] = _dot_nt(h_ref[...], w_ref[...]).astype(BF16)


def _inproj(x, ctx, modb, w_all, layer, tm, tn):
    B, n_lat, D = x.shape
    ctx_len = ctx.shape[1]
    T = n_lat + ctx_len
    assert ctx_len < tm and T % tm == 0 and (tm - ctx_len) % SUBLANES == 0
    x_spec = pl.BlockSpec((pl.Element(1), pl.Element(tm), pl.Element(D)),
                          lambda b, i, j: (b, pl.multiple_of(jnp.maximum(i * tm - ctx_len, 0), SUBLANES), 0))
    return pl.pallas_call(
        functools.partial(_inproj_kernel, tm=tm, ctx_len=ctx_len),
        grid=(B, T // tm, PROJ_W // tn),
        in_specs=[x_spec,
                  pl.BlockSpec((None, ctx_len, D), lambda b, i, j: (b, 0, 0)),
                  pl.BlockSpec((None, 8, 3 * D), lambda b, i, j: (b, 0, 0)),
                  pl.BlockSpec((None, tn, D), lambda b, i, j: (layer, j, 0))],
        out_specs=pl.BlockSpec((None, tm, tn), lambda b, i, j: (b, i, j)),
        out_shape=jax.ShapeDtypeStruct((B, T, PROJ_W), BF16),
        scratch_shapes=[pltpu.VMEM((tm, D), BF16)],
        compiler_params=_cparams(("arbitrary", "arbitrary", "arbitrary"), 52),
        name="in_proj",
    )(x, ctx, modb, w_all)


def _rope(y, cos, sin, width):
    lane = lax.broadcasted_iota(jnp.int32, (1, width), 1)
    first = jnp.bitwise_and(lane, ROPE_AXIS - 1) < (ROPE_AXIS // 2)
    up = pltpu.roll(y, width - ROPE_AXIS // 2, 1)
    dn = pltpu.roll(y, ROPE_AXIS // 2, 1)
    return y * cos + jnp.where(first, up, dn) * sin


def _qkprep_kernel(aq_ref, akv_ref, cos_ref, sin_ref, qn_ref, kn_ref, gq_ref, gk_ref,
                   eq_ref, ek_ref, qt_ref, k_ref, vt_ref, *, tm, tk):
    cos = cos_ref[...]
    sin = sin_ref[...]
    aq = aq_ref[...].astype(F32)
    ssq = _dot((aq * aq).astype(BF16), gq_ref[...])
    akv = akv_ref[...].astype(F32)
    ak = akv[:, 0:KV_W]
    ssk = _dot((ak * ak).astype(BF16), gk_ref[...])
    vt = _dot_nt(ek_ref[...], akv_ref[:, KV_W:2 * KV_W]).astype(BF16)
    yq = aq * lax.rsqrt(ssq * (1.0 / HEAD_DIM) + EPS) * jnp.concatenate([qn_ref[...]] * 4, axis=1)
    yq = _rope(yq, jnp.concatenate([cos] * 4, axis=1), jnp.concatenate([sin] * 4, axis=1), ATTN_W)
    yq = (yq * Q_SCALE).astype(BF16)
    qt = _dot_nt(eq_ref[...], yq).astype(BF16)
    zeros = jnp.zeros((HEAD_DIM, ATT_TQ), BF16)
    for g in range(N_KV):
        for qb in range(tm // ATT_TQ):
            for r in range(Q_PER_KV):
                h = g * Q_PER_KV + r
                cols = slice((qb * Q_PER_KV + r) * ATT_TQ, (qb * Q_PER_KV + r + 1) * ATT_TQ)
                qt_ref[g, g * HEAD_DIM:(g + 1) * HEAD_DIM, cols] = qt[h * HEAD_DIM:(h + 1) * HEAD_DIM,
                                                                      qb * ATT_TQ:(qb + 1) * ATT_TQ]
                qt_ref[g, (1 - g) * HEAD_DIM:(2 - g) * HEAD_DIM, cols] = zeros
    yk = ak * lax.rsqrt(ssk * (1.0 / HEAD_DIM) + EPS) * kn_ref[...]
    yk = _rope(yk, cos, sin, KV_W).astype(BF16)
    nkb = tm // tk
    k_ref[...] = yk.reshape(nkb, tk, KV_W)
    for g in range(N_KV):
        for n in range(nkb):
            vt_ref[g, n] = vt[g * HEAD_DIM:(g + 1) * HEAD_DIM, n * tk:(n + 1) * tk]


def _qkprep(proj, cos_t, sin_t, qn, kn, consts, tm, tk):
    B, T, _ = proj.shape
    nkb_t = T // tk
    nkb = tm // tk
    full = lambda shape: pl.BlockSpec(shape, lambda b, i: (0,) * len(shape))
    return pl.pallas_call(
        functools.partial(_qkprep_kernel, tm=tm, tk=tk),
        grid=(B, T // tm),
        in_specs=[pl.BlockSpec((None, tm, ATTN_W), lambda b, i: (b, i, COL_AQ // ATTN_W)),
                  pl.BlockSpec((None, tm, 2 * KV_W), lambda b, i: (b, i, COL_AKV // (2 * KV_W))),
                  pl.BlockSpec((tm, LANES), lambda b, i: (i, 0)),
                  pl.BlockSpec((tm, LANES), lambda b, i: (i, 0)),
                  full((1, LANES)), full((1, LANES)),
                  full((ATTN_W, ATTN_W)), full((KV_W, KV_W)),
                  full((ATTN_W, ATTN_W)), full((KV_W, KV_W))],
        out_specs=[pl.BlockSpec((None, N_KV, KV_W, Q_PER_KV * tm), lambda b, i: (b, 0, 0, i)),
                   pl.BlockSpec((None, nkb, tk, KV_W), lambda b, i: (b, i, 0, 0)),
                   pl.BlockSpec((None, N_KV, nkb, HEAD_DIM, tk), lambda b, i: (b, 0, i, 0, 0))],
        out_shape=[jax.ShapeDtypeStruct((B, N_KV, KV_W, Q_PER_KV * T), BF16),
                   jax.ShapeDtypeStruct((B, nkb_t, tk, KV_W), BF16),
                   jax.ShapeDtypeStruct((B, N_KV, nkb_t, HEAD_DIM, tk), BF16)],
        compiler_params=_cparams(("arbitrary", "arbitrary"), 40),
        name="qkv_prep",
    )(proj, proj, cos_t, sin_t, qn, kn, consts["gq"], consts["gk"], consts["eq"], consts["ek"])


def _attn_kernel(bounded_ref, *refs, nsb, n_tiles, rows):
    q_refs, (k_ref, vt_ref), ag_refs, eye_ref = refs[0:nsb], refs[nsb:nsb + 2], refs[nsb + 2:2 * nsb + 2], refs[2 * nsb + 2]
    o_ref, p_ref, acc_ref, oall_ref, qall_ref = refs[-5:]
    nq = q_refs[0].shape[-1]
    tq = nq // Q_PER_KV
    sl = acc_ref.shape[1] - HEAD_DIM
    U = p_ref.shape[0]
    acc_ref[...] = jnp.zeros_like(acc_ref)

    def stage_of(s):
        if isinstance(s, int):
            return divmod(s, n_tiles)
        return lax.div(s, n_tiles), lax.rem(s, n_tiles)

    def weights(s, slot):
        sb, j = stage_of(s)
        p = jnp.exp2(_dot(k_ref[j, 0:rows, :], qall_ref[sb]))
        p_ref[slot, 0:rows, :] = p.astype(BF16)
        acc_ref[sb, HEAD_DIM:HEAD_DIM + sl, :] += jnp.sum(p.reshape(rows // sl, sl, nq), axis=0)

    def weighted_values(s, slot):
        sb, j = stage_of(s)
        acc_ref[sb, 0:HEAD_DIM, :] += _dot(vt_ref[j, :, 0:rows], p_ref[slot, 0:rows, :])

    def run():
        stages = nsb * n_tiles
        for sb in range(nsb):
            qall_ref[sb] = q_refs[sb][...]
        weights(0, 0)

        def group(i, carry):
            for u in range(U):
                weights(U * i + u + 1, (u + 1) % U)
                weighted_values(U * i + u, u)
            return carry

        groups = (stages - 1) // U
        lax.fori_loop(0, groups, group, 0)
        for s in range(U * groups, stages):
            if s + 1 < stages:
                weights(s + 1, (s + 1) % U)
            weighted_values(s, s % U)

    def run_online(sb):
        for r in range(Q_PER_KV):
            q = q_refs[sb][:, r * tq:(r + 1) * tq]

            def body(j, carry):
                m, den, acc = carry
                s = _dot(k_ref[j, 0:rows, :], q)
                m_new = jnp.maximum(m, jnp.max(s, axis=0, keepdims=True))
                p = jnp.exp2(s - m_new)
                alpha = jnp.exp2(m - m_new)
                return (m_new, alpha * den + jnp.sum(p, axis=0, keepdims=True),
                        alpha * acc + _dot(vt_ref[j, :, 0:rows], p.astype(BF16)))

            init = (jnp.full((1, tq), -jnp.inf, F32), jnp.zeros((1, tq), F32), jnp.zeros((HEAD_DIM, tq), F32))
            _, den, acc = lax.fori_loop(0, n_tiles, body, init)
            acc_ref[sb, 0:HEAD_DIM, r * tq:(r + 1) * tq] = acc
            acc_ref[sb, HEAD_DIM:HEAD_DIM + sl, r * tq:(r + 1) * tq] = jnp.broadcast_to(den * (1.0 / sl), (sl, tq))

    def finish(sb):
        for r in range(Q_PER_KV):
            acc = acc_ref[sb, :, r * tq:(r + 1) * tq]
            den = jnp.sum(acc[HEAD_DIM:HEAD_DIM + sl, :], axis=0, keepdims=True)
            o = acc[0:HEAD_DIM, :] * (1.0 / den)
            oall_ref[sb, r * HEAD_DIM:(r + 1) * HEAD_DIM, :] = o.astype(BF16)
        o_t = _dot_nt(eye_ref[...], oall_ref[sb])
        gate = _silu_from_half(ag_refs[sb][...].astype(F32))
        o_ref[sb * tq:(sb + 1) * tq, :] = (o_t * gate).astype(BF16)

    bounded = bounded_ref[0] == 1

    @pl.when(bounded)
    def _():
        run()
        for sb in range(nsb):
            finish(sb)

    @pl.when(jnp.logical_not(bounded))
    def _():
        for sb in range(nsb):
            run_online(sb)
            finish(sb)


def _attention(bounded, qt, k, vt, proj, eye_tq, q_block0, n_qblocks, nsb, n_tiles, rows):
    B = proj.shape[0]
    nkb, tk = k.shape[1], k.shape[2]
    gw = Q_PER_KV * HEAD_DIM
    nq = Q_PER_KV * ATT_TQ
    assert n_qblocks % nsb == 0

    def qmap(sb):
        return lambda b, g, i, f: (b, g, 0, q_block0 + i * nsb + sb)

    def agmap(sb):
        return lambda b, g, i, f: (b, q_block0 + i * nsb + sb, COL_AG // gw + g)

    in_specs = ([pl.BlockSpec((None, None, KV_W, nq), qmap(sb)) for sb in range(nsb)]
                + [pl.BlockSpec((None, nkb, tk, KV_W), lambda b, g, i, f: (b, 0, 0, 0)),
                   pl.BlockSpec((None, None, nkb, HEAD_DIM, tk), lambda b, g, i, f: (b, g, 0, 0, 0))]
                + [pl.BlockSpec((None, ATT_TQ, gw), agmap(sb)) for sb in range(nsb)]
                + [pl.BlockSpec((ATT_TQ, ATT_TQ), lambda b, g, i, f: (0, 0))])
    args = [bounded] + [qt] * nsb + [k, vt] + [proj] * nsb + [eye_tq]
    grid_spec = pltpu.PrefetchScalarGridSpec(
        num_scalar_prefetch=1,
        grid=(B, N_KV, n_qblocks // nsb),
        in_specs=in_specs,
        out_specs=pl.BlockSpec((None, nsb * ATT_TQ, gw), lambda b, g, i, f: (b, i, g)),
        scratch_shapes=[pltpu.VMEM((ATT_UNROLL, tk, nq), BF16),
                        pltpu.VMEM((nsb, HEAD_DIM + SUBLANES, nq), F32),
                        pltpu.VMEM((nsb, gw, ATT_TQ), BF16),
                        pltpu.VMEM((nsb, KV_W, nq), BF16)])
    return pl.pallas_call(
        functools.partial(_attn_kernel, nsb=nsb, n_tiles=n_tiles, rows=rows),
        grid_spec=grid_spec,
        out_shape=jax.ShapeDtypeStruct((B, n_qblocks * ATT_TQ, ATTN_W), BF16),
        compiler_params=_cparams(("arbitrary", "arbitrary", "arbitrary"), 40),
        name="gqa_attention",
    )(*args)


def _gla_direction(qk, v, pa, a2, ab, st_ref, o_ref, fwd):
    C, c = GLA_C, GLA_SUB
    q = qk[:, 0:GLA_KW].astype(F32) * (GLA_DK ** -0.5)
    k = qk[:, GLA_KW:2 * GLA_KW].astype(F32)
    z = _dot(pa, a2) + ab
    yield
    g = (jnp.minimum(z, 0.0) - jnp.log(1.0 + jnp.exp(-jnp.abs(z)))) * (1.0 / GLA_TEMP)
    row = lax.broadcasted_iota(jnp.int32, (C, C), 0)
    col = lax.broadcasted_iota(jnp.int32, (C, C), 1)
    blk = jnp.bitwise_and(row, -c)
    if fwd:
        tri = col <= row
        inblk = tri & (col >= blk)
    else:
        tri = col >= row
        inblk = tri & (col < blk + c)
    sel = jnp.concatenate([jnp.where(tri, 1.0, 0.0), jnp.where(inblk, 1.0, 0.0)], axis=0).astype(BF16)
    g_hi = g.astype(BF16)
    g_lo = (g - g_hi.astype(F32)).astype(BF16)
    cs = _dot(sel, jnp.concatenate([g_hi, g_lo], axis=1))
    yield
    cs = cs[:, 0:GLA_KW] + cs[:, GLA_KW:2 * GLA_KW]
    b = cs[0:C]
    d = cs[C:2 * C]
    ref = b - d
    total = b[C - 1:C, :] if fwd else b[0:1, :]
    q_in = (q * jnp.exp(d)).astype(BF16)
    q_st = (q * jnp.exp(b)).astype(BF16)
    k_out = (k * jnp.exp(total - b)).astype(BF16)
    w_tot = jnp.exp(total)
    zero = jnp.zeros((), BF16)
    nb = C // c
    k_blk = []
    for i in range(nb):
        lo, hi = (0, (i + 1) * c) if fwd else (i * c, C)
        kb = (k[lo:hi] * jnp.exp(ref[i * c:i * c + 1, :] - b[lo:hi])).astype(BF16)
        pad = jnp.zeros((C - (hi - lo), GLA_KW), BF16)
        k_blk.append(jnp.concatenate(([kb, pad] if fwd else [pad, kb]) if hi - lo < C else [kb], axis=0))
    head_of_lane = jnp.right_shift(lax.broadcasted_iota(jnp.int32, (1, GLA_KW), 1), GLA_DK.bit_length() - 1)
    a_two = []
    for m in range(nb // 2):
        qr = q_in[2 * m * c:(2 * m + 2) * c]
        q_heads = jnp.concatenate([jnp.where(head_of_lane == h, qr, zero) for h in range(GLA_HEADS)], axis=0)
        a_two.append(_dot_nt(q_heads, jnp.concatenate(k_blk[2 * m:2 * m + 2], axis=0)))
    yield

    def a_block(h, i):
        m, r = divmod(i, 2)
        return a_two[m][h * 2 * c + r * c:h * 2 * c + (r + 1) * c, r * C:(r + 1) * C]

    a_cat = jnp.concatenate(
        [jnp.where(tri, jnp.concatenate([a_block(h, i) for i in range(nb)], axis=0), 0.0)
         for h in range(GLA_HEADS)], axis=1).astype(BF16)
    vhead_of_lane = jnp.right_shift(lax.broadcasted_iota(jnp.int32, (1, GLA_VW), 1), GLA_DV.bit_length() - 1)
    v_bd = jnp.concatenate([jnp.where(vhead_of_lane == h, v, zero) for h in range(GLA_HEADS)], axis=0)
    st = st_ref[...]
    o = _dot(a_cat, v_bd) + _dot_nt(q_st, st.astype(BF16))
    upd = lax.dot_general(v, k_out, (((0,), (0,)), ((), ())), preferred_element_type=F32)
    same_head = jnp.right_shift(lax.broadcasted_iota(jnp.int32, (GLA_VW, 1), 0),
                                GLA_DV.bit_length() - 1) == head_of_lane
    st_ref[...] = jnp.where(same_head, st * w_tot + upd, 0.0)
    o_ref[...] = o.astype(o_ref.dtype)
    yield


def _gla_kernel(qkf_ref, vf_ref, paf_ref, qkb_ref, vb_ref, pab_ref, a2f_ref, abf_ref, a2b_ref, abb_ref,
                of_ref, ob_ref, stf_ref, stb_ref):
    @pl.when(pl.program_id(0) == 0)
    def _():
        stf_ref[...] = jnp.zeros_like(stf_ref)
        stb_ref[...] = jnp.zeros_like(stb_ref)

    C = GLA_C
    waves = []
    for t in range(GLA_STEPS):
        fs = slice(t * C, (t + 1) * C)
        bs = slice((GLA_STEPS - 1 - t) * C, (GLA_STEPS - t) * C)
        chains = []
        for bi in range(qkf_ref.shape[0]):
            chains.append(_gla_direction(qkf_ref[bi, fs], vf_ref[bi, fs], paf_ref[bi, fs], a2f_ref[...], abf_ref[...],
                                         stf_ref.at[bi], of_ref.at[bi, fs], True))
            chains.append(_gla_direction(qkb_ref[bi, bs], vb_ref[bi, bs], pab_ref[bi, bs], a2b_ref[...], abb_ref[...],
                                         stb_ref.at[bi], ob_ref.at[bi, bs], False))
        waves.append(chains)
    for _ in range(3):
        for chains in waves:
            for chain in chains:
                next(chain)
    for chains in waves:
        for chain in chains:
            next(chain)


def _gla(proj, a2f, abf, a2b, abb, ctx_len):
    B, T = proj.shape[0], proj.shape[1]
    C = GLA_C * GLA_STEPS
    assert T % C == 0 and ctx_len % C == 0
    nch, nctx = T // C, ctx_len // C

    def cb(s):
        return jnp.where(s < nctx, nctx - 1 - s, nch + nctx - 1 - s)

    full = lambda shape: pl.BlockSpec(shape, lambda s: (0,) * len(shape))
    return pl.pallas_call(
        _gla_kernel,
        grid=(nch,),
        in_specs=[pl.BlockSpec((B, C, 2 * GLA_KW), lambda s: (0, s, COL_GQK // (2 * GLA_KW))),
                  pl.BlockSpec((B, C, GLA_VW), lambda s: (0, s, COL_GV // GLA_VW)),
                  pl.BlockSpec((B, C, LANES), lambda s: (0, s, COL_GATE // LANES)),
                  pl.BlockSpec((B, C, 2 * GLA_KW), lambda s: (0, cb(s), COL_GQK // (2 * GLA_KW))),
                  pl.BlockSpec((B, C, GLA_VW), lambda s: (0, cb(s), COL_GV // GLA_VW)),
                  pl.BlockSpec((B, C, LANES), lambda s: (0, cb(s), COL_GATE // LANES)),
                  full((LANES, GLA_KW)), full((1, GLA_KW)), full((LANES, GLA_KW)), full((1, GLA_KW))],
        out_specs=[pl.BlockSpec((B, C, GLA_VW), lambda s: (0, s, 0)),
                   pl.BlockSpec((B, C, GLA_VW), lambda s: (0, cb(s), 0))],
        out_shape=[jax.ShapeDtypeStruct((B, T, GLA_VW), BF16),
                   jax.ShapeDtypeStruct((B, T, GLA_VW), BF16)],
        scratch_shapes=[pltpu.VMEM((B, GLA_VW, GLA_KW), F32),
                        pltpu.VMEM((B, GLA_VW, GLA_KW), F32)],
        compiler_params=_cparams(("arbitrary",), 32),
        name="gla_scan",
    )(proj, proj, proj, proj, proj, proj, a2f, abf, a2b, abb)


def _merge_kernel(x_ref, mod_ref, ma_ref, mg_ref, ms_ref, su_ref, sv_ref, sg_ref, gg_ref, ya_ref, of_ref, ob_ref,
                  onorm_ref, lng_ref, lnb_ref, sw_ref, sb_ref, wa_ref, wg_ref, ws_ref, wo_ref, pg_ref, pb_ref,
                  o_ref, *, tm, mod_row, alpha):
    D = x_ref.shape[-1]
    ma_ref, mg_ref, ms_ref, su_ref, sv_ref, sg_ref, gg_ref, of_ref, ob_ref = (
        r.at[0] for r in (ma_ref, mg_ref, ms_ref, su_ref, sv_ref, sg_ref, gg_ref, of_ref, ob_ref))
    og = of_ref[...].astype(F32) + ob_ref[...].astype(F32)
    parts = []
    for h in range(GLA_HEADS):
        oh = og[:, h * GLA_DV:(h + 1) * GLA_DV]
        ms = jnp.mean(oh * oh, axis=-1, keepdims=True)
        parts.append(oh * lax.rsqrt(ms + EPS))
    yg = jnp.concatenate(parts, axis=1) * onorm_ref[...] * _silu_from_half(gg_ref[...]).astype(F32)
    sv = sv_ref[...].astype(F32)
    mu = jnp.mean(sv, axis=-1, keepdims=True)
    dv = sv - mu
    var = jnp.mean(dv * dv, axis=-1, keepdims=True)
    vn = (dv * lax.rsqrt(var + EPS) * lng_ref[...] + lnb_ref[...]).astype(BF16)
    gw = SGU_W // SGU_GROUPS
    rows = []
    for n in range(tm // SGU_CHUNK):
        cols = []
        for g in range(SGU_GROUPS):
            blk = vn[n * SGU_CHUNK:(n + 1) * SGU_CHUNK, g * gw:(g + 1) * gw]
            cols.append(_dot(sw_ref[g], blk))
        rows.append(jnp.concatenate(cols, axis=1) + sb_ref[...])
    mixed = jnp.concatenate(rows, axis=0)
    ys = su_ref[...].astype(F32) * mixed * _silu_from_half(sg_ref[...]).astype(F32)

    def gated(g_ref, y, w_ref):
        half = _dot(y, w_ref[...]).astype(BF16)
        return half + half * jnp.tanh(g_ref[...])

    m = (gated(ma_ref, ya_ref[...], wa_ref) + gated(mg_ref, yg.astype(BF16), wg_ref)
         + gated(ms_ref, ys.astype(BF16), ws_ref))
    out = _dot(m, wo_ref[...])
    r = alpha * x_ref[...] + mod_ref[mod_row:mod_row + 1, 2 * D:3 * D] * out
    mu = jnp.mean(r, axis=-1, keepdims=True)
    dr = r - mu
    var = jnp.mean(dr * dr, axis=-1, keepdims=True)
    o_ref[...] = dr * lax.rsqrt(var + EPS) * pg_ref[...] + pb_ref[...]


def _merge(x_seg, ya_seg, row0, mod_row, modb, proj, of, ob, p, tm, alpha):
    B, n_rows, D = x_seg.shape
    assert n_rows % tm == 0 and row0 % 16 == 0

    def stream_rows(w, col0):
        return pl.BlockSpec((pl.Element(1), pl.Element(tm), pl.Element(w)),
                            lambda b, i: (b, pl.multiple_of(row0 + i * tm, 16), col0))

    seg_rows = lambda w: pl.BlockSpec((None, tm, w), lambda b, i: (b, i, 0))
    full = lambda shape: pl.BlockSpec(shape, lambda b, i: (0,) * len(shape))
    return pl.pallas_call(
        functools.partial(_merge_kernel, tm=tm, mod_row=mod_row, alpha=alpha),
        grid=(B, n_rows // tm),
        in_specs=[seg_rows(D),
                  pl.BlockSpec((None, 8, 3 * D), lambda b, i: (b, 0, 0)),
                  stream_rows(D, COL_MA), stream_rows(D, COL_MG), stream_rows(D, COL_MS),
                  stream_rows(SGU_W, COL_SU), stream_rows(SGU_W, COL_SV), stream_rows(SGU_W, COL_SG),
                  stream_rows(GLA_VW, COL_GG),
                  seg_rows(ATTN_W), stream_rows(GLA_VW, 0), stream_rows(GLA_VW, 0),
                  full((1, GLA_VW)), full((1, SGU_W)), full((1, SGU_W)),
                  full((SGU_GROUPS, SGU_CHUNK, SGU_CHUNK)), full((SGU_CHUNK, SGU_W)),
                  full((ATTN_W, D)), full((GLA_VW, D)), full((SGU_W, D)), full((D, D)),
                  full((1, D)), full((1, D))],
        out_specs=pl.BlockSpec((None, tm, D), lambda b, i: (b, i, 0)),
        out_shape=jax.ShapeDtypeStruct((B, n_rows, D), F32),
        compiler_params=_cparams(("arbitrary", "arbitrary"), 56),
        name="merge_out",
    )(x_seg, modb, proj, proj, proj, proj, proj, proj, proj, ya_seg, of, ob,
      p["onorm"], p["lng"], p["lnb"], p["sw"], p["sb"], p["wa"], p["wg"], p["ws"], p["wo"], p["pg"], p["pb"])


W_BLK = 512
N_GATES = 2 * GLA_RANK


def _w_in_plan():
    sizes = (512, 128, 128, 512, 256, 256, 512, 16, 16, 512, 512, 512, 512, 1024, 1024, 1024)
    names = ("aq", "ak", "av", "ag", "gq", "gk", "gv", "af", "ab", "gg", "su", "sv", "sg", "ma", "mg", "ms")
    halved = ("ag", "gg", "sg", "ma", "mg", "ms")
    src, off = {}, 0
    for n, s in zip(names, sizes):
        src[n] = (off, s)
        off += s
    assert src["gk"][0] == src["gq"][0] + src["gq"][1] and src["av"][0] == src["ak"][0] + KV_W
    assert src["ab"][0] == src["af"][0] + GLA_RANK
    plan = []
    for n in ("ma", "mg", "ms", "aq", "ag", "gv", "gg", "su", "sv", "sg"):
        start, size = src[n]
        assert size % W_BLK == 0 and start % SUBLANES == 0
        plan += [(start + t * W_BLK, int(n in halved), 0) for t in range(size // W_BLK)]
    plan.append((src["gq"][0], 0, 0))
    plan.append((src["ak"][0], 0, 1))
    assert len(plan) * W_BLK == PROJ_W and src["ak"][0] + W_BLK <= off
    return plan, src["af"][0], off


def _wprep_kernel(src_ref, half_ref, special_ref, a_ref, g_ref, o_ref):
    j = pl.program_id(1)

    @pl.when(special_ref[j] == 0)
    def _():
        o_ref[...] = (a_ref[0] * jnp.where(half_ref[j] == 1, 0.5, 1.0)).astype(BF16)

    @pl.when(special_ref[j] == 1)
    def _():
        o_ref[0:2 * KV_W, :] = a_ref[0, 0:2 * KV_W, :].astype(BF16)
        o_ref[2 * KV_W:2 * KV_W + N_GATES, :] = g_ref[0].astype(BF16)
        o_ref[2 * KV_W + N_GATES:W_BLK, :] = jnp.zeros((W_BLK - 2 * KV_W - N_GATES, o_ref.shape[-1]), BF16)


def _regroup_w_in(w_in):
    L, D, width = w_in.shape
    plan, gate_col, used = _w_in_plan()
    assert used == width
    src = jnp.array([p[0] for p in plan], jnp.int32)
    half = jnp.array([p[1] for p in plan], jnp.int32)
    special = jnp.array([p[2] for p in plan], jnp.int32)
    wt = jnp.swapaxes(w_in, 1, 2)
    grid_spec = pltpu.PrefetchScalarGridSpec(
        num_scalar_prefetch=3,
        grid=(L, len(plan)),
        in_specs=[pl.BlockSpec((pl.Element(1), pl.Element(W_BLK), pl.Element(D)),
                               lambda l, j, s, h, sp: (l, pl.multiple_of(s[j], SUBLANES), 0)),
                  pl.BlockSpec((pl.Element(1), pl.Element(N_GATES), pl.Element(D)),
                               lambda l, j, s, h, sp: (l, gate_col, 0))],
        out_specs=pl.BlockSpec((None, W_BLK, D), lambda l, j, s, h, sp: (l, j, 0)))
    return pl.pallas_call(
        _wprep_kernel,
        grid_spec=grid_spec,
        out_shape=jax.ShapeDtypeStruct((L, PROJ_W, D), BF16),
        compiler_params=_cparams(("arbitrary", "arbitrary"), 32),
        name="w_in_prep",
    )(src, half, special, wt, wt)


def _rope_tables(n_lat, ctx_len):
    rows = n_lat // GRID_W
    row = np.repeat(np.arange(rows, dtype=np.float64), GRID_W)
    col = np.tile(np.arange(GRID_W, dtype=np.float64), rows)
    freqs = ROPE_THETA ** (-np.arange(0, ROPE_AXIS, 2, dtype=np.float64) / ROPE_AXIS)
    ang_r, ang_c = row[:, None] * freqs, col[:, None] * freqs
    cos = np.concatenate([np.cos(ang_r)] * 2 + [np.cos(ang_c)] * 2, axis=1)
    sin = np.concatenate([-np.sin(ang_r), np.sin(ang_r), -np.sin(ang_c), np.sin(ang_c)], axis=1)
    cos = np.concatenate([np.ones((ctx_len, HEAD_DIM)), cos], axis=0)
    sin = np.concatenate([np.zeros((ctx_len, HEAD_DIM)), sin], axis=0)
    return (jnp.asarray(np.concatenate([cos, cos], axis=1), F32),
            jnp.asarray(np.concatenate([sin, sin], axis=1), F32))


def _largest_tile(total, cap, mult):
    best = mult
    for t in range(mult, cap + 1, mult):
        if total % t == 0:
            best = t
    return best


def kernel(x, c, ctx, c_ctx, w_ada, b_ada, w_in, attn_q_norm, attn_k_norm, gla_a2_f, gla_ab_f, gla_a2_b, gla_ab_b, gla_o_norm, sgu_ln_g, sgu_ln_b, sgu_w, sgu_b, w_br_attn, w_br_gla, w_br_sgu, w_out, post_ln_g, post_ln_b):
    B, n_lat, D = x.shape
    ctx_len = ctx.shape[1]
    depth = w_in.shape[0]
    T = ctx_len + n_lat
    assert D == D_MODEL and B <= 7
    assert ctx_len % ATT_TQ == 0 and n_lat % ATT_TQ == 0 and n_lat % GRID_W == 0
    alpha = (2 * depth) ** 0.25

    tm_proj = _largest_tile(T, 2112, 16)
    tn_proj = 1536
    tk_att = _largest_tile(T, ATT_TK_MAX, ATT_TQ)
    assert ctx_len <= tk_att
    tm_prep = tk_att
    tm_merge_lat = _largest_tile(n_lat, 512, SGU_CHUNK)
    tm_merge_ctx = _largest_tile(ctx_len, 512, SGU_CHUNK)

    x_ctx, x_lat = ctx, x
    cc = jnp.zeros((8, D), F32).at[0:B].set(c).at[B].set(c_ctx)
    mod = _ada(cc, w_ada, b_ada.reshape(depth, 1, 3 * D))
    cos_t, sin_t = _rope_tables(n_lat, ctx_len)
    consts = {
        "gq": jnp.kron(jnp.eye(N_HEADS, dtype=F32), jnp.ones((HEAD_DIM, HEAD_DIM), F32)).astype(BF16),
        "gk": jnp.kron(jnp.eye(N_KV, dtype=F32), jnp.ones((HEAD_DIM, HEAD_DIM), F32)).astype(BF16),
        "eq": jnp.eye(ATTN_W, dtype=BF16),
        "ek": jnp.eye(KV_W, dtype=BF16),
    }
    eye_tq = jnp.eye(ATT_TQ, dtype=BF16)
    w_all = _regroup_w_in(w_in)

    for l in range(depth):
        last = l == depth - 1
        modb = jnp.zeros((B, 8, 3 * D), F32).at[:, 0].set(mod[l, 0:B]).at[:, 1].set(mod[l, B][None])
        proj = _inproj(x_lat, x_ctx, modb, w_all, l, tm_proj, tn_proj)
        qn = jnp.concatenate([attn_q_norm[l]] * 2)[None]
        kn = jnp.concatenate([attn_k_norm[l]] * 2)[None]
        qt, kk, vt = _qkprep(proj, cos_t, sin_t, qn, kn, consts, tm_prep, tk_att)
        score_bound = HEAD_DIM * jnp.max(jnp.abs(attn_q_norm[l])) * jnp.max(jnp.abs(attn_k_norm[l])) * Q_SCALE
        bounded = (score_bound <= SCORE_LIMIT).astype(jnp.int32).reshape(1)
        ctx_blocks, lat_blocks = ctx_len // ATT_TQ, n_lat // ATT_TQ
        ya_lat = _attention(bounded, qt, kk, vt, proj, eye_tq, ctx_blocks, lat_blocks, ATT_NSB, T // tk_att, tk_att)
        ya_ctx = None if last else _attention(bounded, qt, kk, vt, proj, eye_tq, 0, ctx_blocks, 1, 1, ctx_len)
        a2f = jnp.zeros((LANES, GLA_KW), F32).at[0:GLA_RANK].set(gla_a2_f[l]).astype(BF16)
        a2b = jnp.zeros((LANES, GLA_KW), F32).at[GLA_RANK:2 * GLA_RANK].set(gla_a2_b[l]).astype(BF16)
        of, ob = _gla(proj, a2f, gla_ab_f[l][None], a2b, gla_ab_b[l][None], ctx_len)
        p = {
            "onorm": jnp.concatenate([gla_o_norm[l]] * GLA_HEADS)[None],
            "lng": sgu_ln_g[l][None], "lnb": sgu_ln_b[l][None],
            "sw": sgu_w[l].astype(BF16),
            "sb": jnp.repeat(sgu_b[l].T, SGU_W // SGU_GROUPS, axis=1),
            "wa": (0.5 * w_br_attn[l]).astype(BF16), "wg": (0.5 * w_br_gla[l]).astype(BF16),
            "ws": (0.5 * w_br_sgu[l]).astype(BF16),
            "wo": w_out[l].astype(BF16), "pg": post_ln_g[l][None], "pb": post_ln_b[l][None],
        }
        if not last:
            x_ctx = _merge(x_ctx, ya_ctx, 0, 1, modb, proj, of, ob, p, tm_merge_ctx, alpha)
        x_lat = _merge(x_lat, ya_lat, ctx_len, 0, modb, proj, of, ob, p, tm_merge_lat, alpha)
    return x_lat
```

```python
import functools

import jax
import jax.numpy as jnp
import numpy as np
from jax import lax
from jax.experimental import pallas as pl
from jax.experimental.pallas import tpu as pltpu

F32 = jnp.float32
BF16 = jnp.bfloat16

D_MODEL = 1024
GRID_W = 64
HEAD_DIM = 64
N_HEADS = 8
N_KV = 2
Q_PER_KV = N_HEADS // N_KV
ATTN_W = N_HEADS * HEAD_DIM
KV_W = N_KV * HEAD_DIM
ROPE_THETA = 10000.0
ROPE_AXIS = HEAD_DIM // 2
GLA_HEADS = 4
GLA_DK = 64
GLA_DV = 128
GLA_KW = GLA_HEADS * GLA_DK
GLA_VW = GLA_HEADS * GLA_DV
GLA_RANK = 16
GLA_TEMP = 16.0
SGU_GROUPS = 4
SGU_CHUNK = 128
SGU_W = 512
EPS = 1e-6

LANES = 128
V7X_VMEM_BYTES = 64 * 1024 * 1024

COL_MA, COL_MG, COL_MS = 0, 1024, 2048
COL_AQ, COL_AG = 3072, 3584
COL_GV, COL_GG = 4096, 4608
COL_SU, COL_SV, COL_SG = 5120, 5632, 6144
COL_GQK = 6656
COL_AKV = 7168
COL_GATE = 7424
PROJ_W = 7680

SUBLANES = 8
ATT_TQ = 256
ATT_NSB = 4
ATT_TK_MAX = 768
GLA_C = 128
GLA_SUB = 16
GLA_STEPS = 2
ATT_UNROLL = 4
Q_SCALE = 1.4426950408889634 * HEAD_DIM ** -0.5
SCORE_LIMIT = 96.0


def _cparams(sem, vmem_mb):
    return pltpu.CompilerParams(dimension_semantics=sem,
                                vmem_limit_bytes=min(vmem_mb * 1024 * 1024, V7X_VMEM_BYTES - (4 << 20)))


def _silu(x):
    return x * jax.nn.sigmoid(x)


def _silu_from_half(u):
    return u + u * jnp.tanh(u)


def _dot(a, b):
    return jnp.dot(a, b, preferred_element_type=F32)


def _dot_nt(a, b):
    return lax.dot_general(a, b, (((1,), (1,)), ((), ())), preferred_element_type=F32)


def _ada_kernel(c_ref, w_ref, b_ref, o_ref):
    c = c_ref[...]
    o_ref[...] = _dot(_silu(c).astype(BF16), w_ref[...].astype(BF16)) + b_ref[...]


def _ada(cc, w_ada, b_ada):
    L, D = w_ada.shape[0], w_ada.shape[1]
    return pl.pallas_call(
        _ada_kernel,
        grid=(L, 3),
        in_specs=[pl.BlockSpec((8, D), lambda l, j: (0, 0)),
                  pl.BlockSpec((None, D, D), lambda l, j: (l, 0, j)),
                  pl.BlockSpec((None, 1, D), lambda l, j: (l, 0, j))],
        out_specs=pl.BlockSpec((None, 8, D), lambda l, j: (l, 0, j)),
        out_shape=jax.ShapeDtypeStruct((L, 8, 3 * D), F32),
        compiler_params=_cparams(("arbitrary", "arbitrary"), 32),
        name="ada_mod",
    )(cc, w_ada, b_ada)


def _inproj_kernel(x_ref, ctx_ref, mod_ref, w_ref, o_ref, h_ref, *, tm, ctx_len):
    i = pl.program_id(1)
    j = pl.program_id(2)
    D = x_ref.shape[-1]

    def modulated(v, r):
        return (v * (1.0 + mod_ref[r:r + 1, D:2 * D]) + mod_ref[r:r + 1, 0:D]).astype(BF16)

    @pl.when((j == 0) & (i == 0))
    def _():
        h_ref[0:ctx_len, :] = modulated(ctx_ref[...], 1)
        h_ref[ctx_len:tm, :] = modulated(x_ref[0, 0:tm - ctx_len, :], 0)

    @pl.when((j == 0) & (i > 0))
    def _():
        h_ref[...] = modulated(x_ref[0], 0)

    o_ref[...] = _dot_nt(h_ref[...], w_ref[...]).astype(BF16)


def _inproj(x, ctx, modb, w_all, layer, tm, tn):
    B, n_lat, D = x.shape
    ctx_len = ctx.shape[1]
    T = n_lat + ctx_len
    assert ctx_len < tm and T % tm == 0 and (tm - ctx_len) % SUBLANES == 0
    x_spec = pl.BlockSpec((pl.Element(1), pl.Element(tm), pl.Element(D)),
                          lambda b, i, j: (b, pl.multiple_of(jnp.maximum(i * tm - ctx_len, 0), SUBLANES), 0))
    return pl.pallas_call(
        functools.partial(_inproj_kernel, tm=tm, ctx_len=ctx_len),
        grid=(B, T // tm, PROJ_W // tn),
        in_specs=[x_spec,
                  pl.BlockSpec((None, ctx_len, D), lambda b, i, j: (b, 0, 0)),
                  pl.BlockSpec((None, 8, 3 * D), lambda b, i, j: (b, 0, 0)),
                  pl.BlockSpec((None, tn, D), lambda b, i, j: (layer, j, 0))],
        out_specs=pl.BlockSpec((None, tm, tn), lambda b, i, j: (b, i, j)),
        out_shape=jax.ShapeDtypeStruct((B, T, PROJ_W), BF16),
        scratch_shapes=[pltpu.VMEM((tm, D), BF16)],
        compiler_params=_cparams(("arbitrary", "arbitrary", "arbitrary"), 52),
        name="in_proj",
    )(x, ctx, modb, w_all)


def _rope(y, cos, sin, width):
    lane = lax.broadcasted_iota(jnp.int32, (1, width), 1)
    first = jnp.bitwise_and(lane, ROPE_AXIS - 1) < (ROPE_AXIS // 2)
    up = pltpu.roll(y, width - ROPE_AXIS // 2, 1)
    dn = pltpu.roll(y, ROPE_AXIS // 2, 1)
    return y * cos + jnp.where(first, up, dn) * sin


def _qkprep_kernel(aq_ref, akv_ref, cos_ref, sin_ref, qn_ref, kn_ref, gq_ref, gk_ref,
                   eq_ref, ek_ref, pq_ref, qt_ref, k_ref, vt_ref, *, tm, tk):
    cos = cos_ref[...]
    sin = sin_ref[...]
    aq = aq_ref[...].astype(F32)
    ssq = _dot((aq * aq).astype(BF16), gq_ref[...])
    akv = akv_ref[...].astype(F32)
    ak = akv[:, 0:KV_W]
    ssk = _dot((ak * ak).astype(BF16), gk_ref[...])
    vt = _dot_nt(ek_ref[...], akv_ref[:, KV_W:2 * KV_W]).astype(BF16)
    yq = aq * lax.rsqrt(ssq * (1.0 / HEAD_DIM) + EPS) * (jnp.concatenate([qn_ref[...]] * 4, axis=1) * Q_SCALE)
    yc = (yq * jnp.concatenate([cos] * 4, axis=1)).astype(BF16)
    ys = (yq * jnp.concatenate([-sin] * 4, axis=1)).astype(BF16)
    qt = (_dot_nt(eq_ref[...], yc) + _dot_nt(pq_ref[...], ys)).astype(BF16)
    zeros = jnp.zeros((HEAD_DIM, ATT_TQ), BF16)
    for g in range(N_KV):
        for qb in range(tm // ATT_TQ):
            for r in range(Q_PER_KV):
                h = g * Q_PER_KV + r
                cols = slice((qb * Q_PER_KV + r) * ATT_TQ, (qb * Q_PER_KV + r + 1) * ATT_TQ)
                qt_ref[g, g * HEAD_DIM:(g + 1) * HEAD_DIM, cols] = qt[h * HEAD_DIM:(h + 1) * HEAD_DIM,
                                                                      qb * ATT_TQ:(qb + 1) * ATT_TQ]
                qt_ref[g, (1 - g) * HEAD_DIM:(2 - g) * HEAD_DIM, cols] = zeros
    yk = ak * lax.rsqrt(ssk * (1.0 / HEAD_DIM) + EPS) * kn_ref[...]
    yk = _rope(yk, cos, sin, KV_W).astype(BF16)
    nkb = tm // tk
    k_ref[...] = yk.reshape(nkb, tk, KV_W)
    for g in range(N_KV):
        for n in range(nkb):
            vt_ref[g, n] = vt[g * HEAD_DIM:(g + 1) * HEAD_DIM, n * tk:(n + 1) * tk]


def _qkprep(proj, cos_t, sin_t, qn, kn, consts, tm, tk):
    B, T, _ = proj.shape
    nkb_t = T // tk
    nkb = tm // tk
    full = lambda shape: pl.BlockSpec(shape, lambda b, i: (0,) * len(shape))
    return pl.pallas_call(
        functools.partial(_qkprep_kernel, tm=tm, tk=tk),
        grid=(B, T // tm),
        in_specs=[pl.BlockSpec((None, tm, ATTN_W), lambda b, i: (b, i, COL_AQ // ATTN_W)),
                  pl.BlockSpec((None, tm, 2 * KV_W), lambda b, i: (b, i, COL_AKV // (2 * KV_W))),
                  pl.BlockSpec((tm, LANES), lambda b, i: (i, 0)),
                  pl.BlockSpec((tm, LANES), lambda b, i: (i, 0)),
                  full((1, LANES)), full((1, LANES)),
                  full((ATTN_W, ATTN_W)), full((KV_W, KV_W)),
                  full((ATTN_W, ATTN_W)), full((KV_W, KV_W)), full((ATTN_W, ATTN_W))],
        out_specs=[pl.BlockSpec((None, N_KV, KV_W, Q_PER_KV * tm), lambda b, i: (b, 0, 0, i)),
                   pl.BlockSpec((None, nkb, tk, KV_W), lambda b, i: (b, i, 0, 0)),
                   pl.BlockSpec((None, N_KV, nkb, HEAD_DIM, tk), lambda b, i: (b, 0, i, 0, 0))],
        out_shape=[jax.ShapeDtypeStruct((B, N_KV, KV_W, Q_PER_KV * T), BF16),
                   jax.ShapeDtypeStruct((B, nkb_t, tk, KV_W), BF16),
                   jax.ShapeDtypeStruct((B, N_KV, nkb_t, HEAD_DIM, tk), BF16)],
        compiler_params=_cparams(("arbitrary", "arbitrary"), 40),
        name="qkv_prep",
    )(proj, proj, cos_t, sin_t, qn, kn, consts["gq"], consts["gk"], consts["eq"], consts["ek"], consts["pq"])


def _attn_kernel(bounded_ref, *refs, nsb, n_tiles, rows):
    q_refs, (k_ref, vt_ref), ag_refs, eye_ref = refs[0:nsb], refs[nsb:nsb + 2], refs[nsb + 2:2 * nsb + 2], refs[2 * nsb + 2]
    o_ref, p_ref, acc_ref, oall_ref, qall_ref = refs[-5:]
    nq = q_refs[0].shape[-1]
    tq = nq // Q_PER_KV
    sl = acc_ref.shape[1] - HEAD_DIM
    U = p_ref.shape[0]
    acc_ref[...] = jnp.zeros_like(acc_ref)

    def stage_of(s):
        if isinstance(s, int):
            return divmod(s, n_tiles)
        return lax.div(s, n_tiles), lax.rem(s, n_tiles)

    def weights(s, slot):
        sb, j = stage_of(s)
        p = jnp.exp2(_dot(k_ref[j, 0:rows, :], qall_ref[sb]))
        p_ref[slot, 0:rows, :] = p.astype(BF16)
        acc_ref[sb, HEAD_DIM:HEAD_DIM + sl, :] += jnp.sum(p.reshape(rows // sl, sl, nq), axis=0)

    def weighted_values(s, slot):
        sb, j = stage_of(s)
        acc_ref[sb, 0:HEAD_DIM, :] += _dot(vt_ref[j, :, 0:rows], p_ref[slot, 0:rows, :])

    def run():
        stages = nsb * n_tiles
        for sb in range(nsb):
            qall_ref[sb] = q_refs[sb][...]
        weights(0, 0)

        def group(i, carry):
            for u in range(U):
                weights(U * i + u + 1, (u + 1) % U)
                weighted_values(U * i + u, u)
            return carry

        groups = (stages - 1) // U
        lax.fori_loop(0, groups, group, 0)
        for s in range(U * groups, stages):
            if s + 1 < stages:
                weights(s + 1, (s + 1) % U)
            weighted_values(s, s % U)

    def run_online(sb):
        for r in range(Q_PER_KV):
            q = q_refs[sb][:, r * tq:(r + 1) * tq]

            def body(j, carry):
                m, den, acc = carry
                s = _dot(k_ref[j, 0:rows, :], q)
                m_new = jnp.maximum(m, jnp.max(s, axis=0, keepdims=True))
                p = jnp.exp2(s - m_new)
                alpha = jnp.exp2(m - m_new)
                return (m_new, alpha * den + jnp.sum(p, axis=0, keepdims=True),
                        alpha * acc + _dot(vt_ref[j, :, 0:rows], p.astype(BF16)))

            init = (jnp.full((1, tq), -jnp.inf, F32), jnp.zeros((1, tq), F32), jnp.zeros((HEAD_DIM, tq), F32))
            _, den, acc = lax.fori_loop(0, n_tiles, body, init)
            acc_ref[sb, 0:HEAD_DIM, r * tq:(r + 1) * tq] = acc
            acc_ref[sb, HEAD_DIM:HEAD_DIM + sl, r * tq:(r + 1) * tq] = jnp.broadcast_to(den * (1.0 / sl), (sl, tq))

    def finish(sb):
        for r in range(Q_PER_KV):
            acc = acc_ref[sb, :, r * tq:(r + 1) * tq]
            den = jnp.sum(acc[HEAD_DIM:HEAD_DIM + sl, :], axis=0, keepdims=True)
            o = acc[0:HEAD_DIM, :] * (1.0 / den)
            oall_ref[sb, r * HEAD_DIM:(r + 1) * HEAD_DIM, :] = o.astype(BF16)
        o_t = _dot_nt(eye_ref[...], oall_ref[sb])
        gate = _silu_from_half(ag_refs[sb][...].astype(F32))
        o_ref[sb * tq:(sb + 1) * tq, :] = (o_t * gate).astype(BF16)

    bounded = bounded_ref[0] == 1

    @pl.when(bounded)
    def _():
        run()
        for sb in range(nsb):
            finish(sb)

    @pl.when(jnp.logical_not(bounded))
    def _():
        for sb in range(nsb):
            run_online(sb)
            finish(sb)


def _attention(bounded, qt, k, vt, proj, eye_tq, q_block0, n_qblocks, nsb, n_tiles, rows):
    B = proj.shape[0]
    nkb, tk = k.shape[1], k.shape[2]
    gw = Q_PER_KV * HEAD_DIM
    nq = Q_PER_KV * ATT_TQ
    assert n_qblocks % nsb == 0

    def qmap(sb):
        return lambda b, g, i, f: (b, g, 0, q_block0 + i * nsb + sb)

    def agmap(sb):
        return lambda b, g, i, f: (b, q_block0 + i * nsb + sb, COL_AG // gw + g)

    in_specs = ([pl.BlockSpec((None, None, KV_W, nq), qmap(sb)) for sb in range(nsb)]
                + [pl.BlockSpec((None, nkb, tk, KV_W), lambda b, g, i, f: (b, 0, 0, 0)),
                   pl.BlockSpec((None, None, nkb, HEAD_DIM, tk), lambda b, g, i, f: (b, g, 0, 0, 0))]
                + [pl.BlockSpec((None, ATT_TQ, gw), agmap(sb)) for sb in range(nsb)]
                + [pl.BlockSpec((ATT_TQ, ATT_TQ), lambda b, g, i, f: (0, 0))])
    args = [bounded] + [qt] * nsb + [k, vt] + [proj] * nsb + [eye_tq]
    grid_spec = pltpu.PrefetchScalarGridSpec(
        num_scalar_prefetch=1,
        grid=(B, N_KV, n_qblocks // nsb),
        in_specs=in_specs,
        out_specs=pl.BlockSpec((None, nsb * ATT_TQ, gw), lambda b, g, i, f: (b, i, g)),
        scratch_shapes=[pltpu.VMEM((ATT_UNROLL, tk, nq), BF16),
                        pltpu.VMEM((nsb, HEAD_DIM + SUBLANES, nq), F32),
                        pltpu.VMEM((nsb, gw, ATT_TQ), BF16),
                        pltpu.VMEM((nsb, KV_W, nq), BF16)])
    return pl.pallas_call(
        functools.partial(_attn_kernel, nsb=nsb, n_tiles=n_tiles, rows=rows),
        grid_spec=grid_spec,
        out_shape=jax.ShapeDtypeStruct((B, n_qblocks * ATT_TQ, ATTN_W), BF16),
        compiler_params=_cparams(("arbitrary", "arbitrary", "arbitrary"), 40),
        name="gqa_attention",
    )(*args)


def _gla_direction(qk, v, pa, a2, ab, st_ref, o_ref, fwd):
    C, c = GLA_C, GLA_SUB
    q = qk[:, 0:GLA_KW].astype(F32) * (GLA_DK ** -0.5)
    k = qk[:, GLA_KW:2 * GLA_KW].astype(F32)
    z = _dot(pa, a2) + ab
    yield
    g = (jnp.minimum(z, 0.0) - jnp.log(1.0 + jnp.exp(-jnp.abs(z)))) * (1.0 / GLA_TEMP)
    row = lax.broadcasted_iota(jnp.int32, (C, C), 0)
    col = lax.broadcasted_iota(jnp.int32, (C, C), 1)
    blk = jnp.bitwise_and(row, -c)
    if fwd:
        tri = col <= row
        inblk = tri & (col >= blk)
    else:
        tri = col >= row
        inblk = tri & (col < blk + c)
    sel = jnp.concatenate([jnp.where(tri, 1.0, 0.0), jnp.where(inblk, 1.0, 0.0)], axis=0).astype(BF16)
    g_hi = g.astype(BF16)
    g_lo = (g - g_hi.astype(F32)).astype(BF16)
    cs = _dot(sel, jnp.concatenate([g_hi, g_lo], axis=1))
    yield
    cs = cs[:, 0:GLA_KW] + cs[:, GLA_KW:2 * GLA_KW]
    b = cs[0:C]
    d = cs[C:2 * C]
    ref = b - d
    total = b[C - 1:C, :] if fwd else b[0:1, :]
    q_in = (q * jnp.exp(d)).astype(BF16)
    q_st = (q * jnp.exp(b)).astype(BF16)
    k_out = (k * jnp.exp(total - b)).astype(BF16)
    w_tot = jnp.exp(total)
    zero = jnp.zeros((), BF16)
    nb = C // c
    k_blk = []
    for i in range(nb):
        lo, hi = (0, (i + 1) * c) if fwd else (i * c, C)
        kb = (k[lo:hi] * jnp.exp(ref[i * c:i * c + 1, :] - b[lo:hi])).astype(BF16)
        pad = jnp.zeros((C - (hi - lo), GLA_KW), BF16)
        k_blk.append(jnp.concatenate(([kb, pad] if fwd else [pad, kb]) if hi - lo < C else [kb], axis=0))
    head_of_lane = jnp.right_shift(lax.broadcasted_iota(jnp.int32, (1, GLA_KW), 1), GLA_DK.bit_length() - 1)
    a_two = []
    for m in range(nb // 2):
        qr = q_in[2 * m * c:(2 * m + 2) * c]
        q_heads = jnp.concatenate([jnp.where(head_of_lane == h, qr, zero) for h in range(GLA_HEADS)], axis=0)
        a_two.append(_dot_nt(q_heads, jnp.concatenate(k_blk[2 * m:2 * m + 2], axis=0)))
    yield

    def a_block(h, i):
        m, r = divmod(i, 2)
        return a_two[m][h * 2 * c + r * c:h * 2 * c + (r + 1) * c, r * C:(r + 1) * C]

    a_cat = jnp.concatenate(
        [jnp.where(tri, jnp.concatenate([a_block(h, i) for i in range(nb)], axis=0), 0.0)
         for h in range(GLA_HEADS)], axis=1).astype(BF16)
    vhead_of_lane = jnp.right_shift(lax.broadcasted_iota(jnp.int32, (1, GLA_VW), 1), GLA_DV.bit_length() - 1)
    v_bd = jnp.concatenate([jnp.where(vhead_of_lane == h, v, zero) for h in range(GLA_HEADS)], axis=0)
    st = st_ref[...]
    o = _dot(a_cat, v_bd) + _dot_nt(q_st, st.astype(BF16))
    upd = lax.dot_general(v, k_out, (((0,), (0,)), ((), ())), preferred_element_type=F32)
    same_head = jnp.right_shift(lax.broadcasted_iota(jnp.int32, (GLA_VW, 1), 0),
                                GLA_DV.bit_length() - 1) == head_of_lane
    st_ref[...] = jnp.where(same_head, st * w_tot + upd, 0.0)
    o_ref[...] = o
    yield


def _gla_kernel(qkf_ref, vf_ref, paf_ref, qkb_ref, vb_ref, pab_ref, a2f_ref, abf_ref, a2b_ref, abb_ref,
                of_ref, ob_ref, stf_ref, stb_ref):
    @pl.when(pl.program_id(0) == 0)
    def _():
        stf_ref[...] = jnp.zeros_like(stf_ref)
        stb_ref[...] = jnp.zeros_like(stb_ref)

    C = GLA_C
    waves = []
    for t in range(GLA_STEPS):
        fs = slice(t * C, (t + 1) * C)
        bs = slice((GLA_STEPS - 1 - t) * C, (GLA_STEPS - t) * C)
        chains = []
        for bi in range(qkf_ref.shape[0]):
            chains.append(_gla_direction(qkf_ref[bi, fs], vf_ref[bi, fs], paf_ref[bi, fs], a2f_ref[...], abf_ref[...],
                                         stf_ref.at[bi], of_ref.at[bi, fs], True))
            chains.append(_gla_direction(qkb_ref[bi, bs], vb_ref[bi, bs], pab_ref[bi, bs], a2b_ref[...], abb_ref[...],
                                         stb_ref.at[bi], ob_ref.at[bi, bs], False))
        waves.append(chains)
    for _ in range(3):
        for chains in waves:
            for chain in chains:
                next(chain)
    for chains in waves:
        for chain in chains:
            next(chain)


def _gla(proj, a2f, abf, a2b, abb, ctx_len):
    B, T = proj.shape[0], proj.shape[1]
    C = GLA_C * GLA_STEPS
    assert T % C == 0 and ctx_len % C == 0
    nch, nctx = T // C, ctx_len // C

    def cb(s):
        return jnp.where(s < nctx, nctx - 1 - s, nch + nctx - 1 - s)

    full = lambda shape: pl.BlockSpec(shape, lambda s: (0,) * len(shape))
    return pl.pallas_call(
        _gla_kernel,
        grid=(nch,),
        in_specs=[pl.BlockSpec((B, C, 2 * GLA_KW), lambda s: (0, s, COL_GQK // (2 * GLA_KW))),
                  pl.BlockSpec((B, C, GLA_VW), lambda s: (0, s, COL_GV // GLA_VW)),
                  pl.BlockSpec((B, C, LANES), lambda s: (0, s, COL_GATE // LANES)),
                  pl.BlockSpec((B, C, 2 * GLA_KW), lambda s: (0, cb(s), COL_GQK // (2 * GLA_KW))),
                  pl.BlockSpec((B, C, GLA_VW), lambda s: (0, cb(s), COL_GV // GLA_VW)),
                  pl.BlockSpec((B, C, LANES), lambda s: (0, cb(s), COL_GATE // LANES)),
                  full((LANES, GLA_KW)), full((1, GLA_KW)), full((LANES, GLA_KW)), full((1, GLA_KW))],
        out_specs=[pl.BlockSpec((B, C, GLA_VW), lambda s: (0, s, 0)),
                   pl.BlockSpec((B, C, GLA_VW), lambda s: (0, cb(s), 0))],
        out_shape=[jax.ShapeDtypeStruct((B, T, GLA_VW), F32),
                   jax.ShapeDtypeStruct((B, T, GLA_VW), F32)],
        scratch_shapes=[pltpu.VMEM((B, GLA_VW, GLA_KW), F32),
                        pltpu.VMEM((B, GLA_VW, GLA_KW), F32)],
        compiler_params=_cparams(("arbitrary",), 32),
        name="gla_scan",
    )(proj, proj, proj, proj, proj, proj, a2f, abf, a2b, abb)


def _merge_kernel(x_ref, mod_ref, ma_ref, mg_ref, ms_ref, su_ref, sv_ref, sg_ref, gg_ref, ya_ref, of_ref, ob_ref,
                  onorm_ref, lng_ref, lnb_ref, sw_ref, sb_ref, wa_ref, wg_ref, ws_ref, wo_ref, pg_ref, pb_ref,
                  o_ref, *, tm, mod_row, alpha):
    D = x_ref.shape[-1]
    ma_ref, mg_ref, ms_ref, su_ref, sv_ref, sg_ref, gg_ref, of_ref, ob_ref = (
        r.at[0] for r in (ma_ref, mg_ref, ms_ref, su_ref, sv_ref, sg_ref, gg_ref, of_ref, ob_ref))
    og = of_ref[...] + ob_ref[...]
    parts = []
    for h in range(GLA_HEADS):
        oh = og[:, h * GLA_DV:(h + 1) * GLA_DV]
        ms = jnp.mean(oh * oh, axis=-1, keepdims=True)
        parts.append(oh * lax.rsqrt(ms + EPS))
    yg = jnp.concatenate(parts, axis=1) * onorm_ref[...] * _silu_from_half(gg_ref[...]).astype(F32)
    sv = sv_ref[...].astype(F32)
    mu = jnp.mean(sv, axis=-1, keepdims=True)
    dv = sv - mu
    var = jnp.mean(dv * dv, axis=-1, keepdims=True)
    vn = (dv * lax.rsqrt(var + EPS) * lng_ref[...] + lnb_ref[...]).astype(BF16)
    gw = SGU_W // SGU_GROUPS
    rows = []
    for n in range(tm // SGU_CHUNK):
        cols = []
        for g in range(SGU_GROUPS):
            blk = vn[n * SGU_CHUNK:(n + 1) * SGU_CHUNK, g * gw:(g + 1) * gw]
            cols.append(_dot(sw_ref[g], blk))
        rows.append(jnp.concatenate(cols, axis=1) + sb_ref[...])
    mixed = jnp.concatenate(rows, axis=0)
    ys = su_ref[...].astype(F32) * mixed * _silu_from_half(sg_ref[...]).astype(F32)

    def gated(g_ref, y, w_ref):
        half = _dot(y, w_ref[...]).astype(BF16)
        return half + half * jnp.tanh(g_ref[...])

    m = (gated(ma_ref, ya_ref[...], wa_ref) + gated(mg_ref, yg.astype(BF16), wg_ref)
         + gated(ms_ref, ys.astype(BF16), ws_ref))
    out = _dot(m, wo_ref[...])
    r = alpha * x_ref[...] + mod_ref[mod_row:mod_row + 1, 2 * D:3 * D] * out
    mu = jnp.mean(r, axis=-1, keepdims=True)
    dr = r - mu
    var = jnp.mean(dr * dr, axis=-1, keepdims=True)
    o_ref[...] = dr * lax.rsqrt(var + EPS) * pg_ref[...] + pb_ref[...]


def _merge(x_seg, ya_seg, row0, mod_row, modb, proj, of, ob, p, tm, alpha):
    B, n_rows, D = x_seg.shape
    assert n_rows % tm == 0 and row0 % 16 == 0

    def stream_rows(w, col0):
        return pl.BlockSpec((pl.Element(1), pl.Element(tm), pl.Element(w)),
                            lambda b, i: (b, pl.multiple_of(row0 + i * tm, 16), col0))

    seg_rows = lambda w: pl.BlockSpec((None, tm, w), lambda b, i: (b, i, 0))
    full = lambda shape: pl.BlockSpec(shape, lambda b, i: (0,) * len(shape))
    return pl.pallas_call(
        functools.partial(_merge_kernel, tm=tm, mod_row=mod_row, alpha=alpha),
        grid=(B, n_rows // tm),
        in_specs=[seg_rows(D),
                  pl.BlockSpec((None, 8, 3 * D), lambda b, i: (b, 0, 0)),
                  stream_rows(D, COL_MA), stream_rows(D, COL_MG), stream_rows(D, COL_MS),
                  stream_rows(SGU_W, COL_SU), stream_rows(SGU_W, COL_SV), stream_rows(SGU_W, COL_SG),
                  stream_rows(GLA_VW, COL_GG),
                  seg_rows(ATTN_W), stream_rows(GLA_VW, 0), stream_rows(GLA_VW, 0),
                  full((1, GLA_VW)), full((1, SGU_W)), full((1, SGU_W)),
                  full((SGU_GROUPS, SGU_CHUNK, SGU_CHUNK)), full((SGU_CHUNK, SGU_W)),
                  full((ATTN_W, D)), full((GLA_VW, D)), full((SGU_W, D)), full((D, D)),
                  full((1, D)), full((1, D))],
        out_specs=pl.BlockSpec((None, tm, D), lambda b, i: (b, i, 0)),
        out_shape=jax.ShapeDtypeStruct((B, n_rows, D), F32),
        compiler_params=_cparams(("arbitrary", "arbitrary"), 56),
        name="merge_out",
    )(x_seg, modb, proj, proj, proj, proj, proj, proj, proj, ya_seg, of, ob,
      p["onorm"], p["lng"], p["lnb"], p["sw"], p["sb"], p["wa"], p["wg"], p["ws"], p["wo"], p["pg"], p["pb"])


W_BLK = 512
N_GATES = 2 * GLA_RANK


def _w_in_plan():
    sizes = (512, 128, 128, 512, 256, 256, 512, 16, 16, 512, 512, 512, 512, 1024, 1024, 1024)
    names = ("aq", "ak", "av", "ag", "gq", "gk", "gv", "af", "ab", "gg", "su", "sv", "sg", "ma", "mg", "ms")
    halved = ("ag", "gg", "sg", "ma", "mg", "ms")
    src, off = {}, 0
    for n, s in zip(names, sizes):
        src[n] = (off, s)
        off += s
    assert src["gk"][0] == src["gq"][0] + src["gq"][1] and src["av"][0] == src["ak"][0] + KV_W
    assert src["ab"][0] == src["af"][0] + GLA_RANK
    plan = []
    for n in ("ma", "mg", "ms", "aq", "ag", "gv", "gg", "su", "sv", "sg"):
        start, size = src[n]
        assert size % W_BLK == 0 and start % SUBLANES == 0
        plan += [(start + t * W_BLK, int(n in halved), 0) for t in range(size // W_BLK)]
    plan.append((src["gq"][0], 0, 0))
    plan.append((src["ak"][0], 0, 1))
    assert len(plan) * W_BLK == PROJ_W and src["ak"][0] + W_BLK <= off
    return plan, src["af"][0], off


def _wprep_kernel(src_ref, half_ref, special_ref, a_ref, g_ref, o_ref):
    j = pl.program_id(1)

    @pl.when(special_ref[j] == 0)
    def _():
        o_ref[...] = (a_ref[0] * jnp.where(half_ref[j] == 1, 0.5, 1.0)).astype(BF16)

    @pl.when(special_ref[j] == 1)
    def _():
        o_ref[0:2 * KV_W, :] = a_ref[0, 0:2 * KV_W, :].astype(BF16)
        o_ref[2 * KV_W:2 * KV_W + N_GATES, :] = g_ref[0].astype(BF16)
        o_ref[2 * KV_W + N_GATES:W_BLK, :] = jnp.zeros((W_BLK - 2 * KV_W - N_GATES, o_ref.shape[-1]), BF16)


def _regroup_w_in(w_in):
    L, D, width = w_in.shape
    plan, gate_col, used = _w_in_plan()
    assert used == width
    src = jnp.array([p[0] for p in plan], jnp.int32)
    half = jnp.array([p[1] for p in plan], jnp.int32)
    special = jnp.array([p[2] for p in plan], jnp.int32)
    wt = jnp.swapaxes(w_in, 1, 2)
    grid_spec = pltpu.PrefetchScalarGridSpec(
        num_scalar_prefetch=3,
        grid=(L, len(plan)),
        in_specs=[pl.BlockSpec((pl.Element(1), pl.Element(W_BLK), pl.Element(D)),
                               lambda l, j, s, h, sp: (l, pl.multiple_of(s[j], SUBLANES), 0)),
                  pl.BlockSpec((pl.Element(1), pl.Element(N_GATES), pl.Element(D)),
                               lambda l, j, s, h, sp: (l, gate_col, 0))],
        out_specs=pl.BlockSpec((None, W_BLK, D), lambda l, j, s, h, sp: (l, j, 0)))
    return pl.pallas_call(
        _wprep_kernel,
        grid_spec=grid_spec,
        out_shape=jax.ShapeDtypeStruct((L, PROJ_W, D), BF16),
        compiler_params=_cparams(("arbitrary", "arbitrary"), 32),
        name="w_in_prep",
    )(src, half, special, wt, wt)


def _rope_tables(n_lat, ctx_len):
    rows = n_lat // GRID_W
    row = np.repeat(np.arange(rows, dtype=np.float64), GRID_W)
    col = np.tile(np.arange(GRID_W, dtype=np.float64), rows)
    freqs = ROPE_THETA ** (-np.arange(0, ROPE_AXIS, 2, dtype=np.float64) / ROPE_AXIS)
    ang_r, ang_c = row[:, None] * freqs, col[:, None] * freqs
    cos = np.concatenate([np.cos(ang_r)] * 2 + [np.cos(ang_c)] * 2, axis=1)
    sin = np.concatenate([-np.sin(ang_r), np.sin(ang_r), -np.sin(ang_c), np.sin(ang_c)], axis=1)
    cos = np.concatenate([np.ones((ctx_len, HEAD_DIM)), cos], axis=0)
    sin = np.concatenate([np.zeros((ctx_len, HEAD_DIM)), sin], axis=0)
    return (jnp.asarray(np.concatenate([cos, cos], axis=1), F32),
            jnp.asarray(np.concatenate([sin, sin], axis=1), F32))


def _largest_tile(total, cap, mult):
    best = mult
    for t in range(mult, cap + 1, mult):
        if total % t == 0:
            best = t
    return best


def kernel(x, c, ctx, c_ctx, w_ada, b_ada, w_in, attn_q_norm, attn_k_norm, gla_a2_f, gla_ab_f, gla_a2_b, gla_ab_b, gla_o_norm, sgu_ln_g, sgu_ln_b, sgu_w, sgu_b, w_br_attn, w_br_gla, w_br_sgu, w_out, post_ln_g, post_ln_b):
    B, n_lat, D = x.shape
    ctx_len = ctx.shape[1]
    depth = w_in.shape[0]
    T = ctx_len + n_lat
    assert D == D_MODEL and B <= 7
    assert ctx_len % ATT_TQ == 0 and n_lat % ATT_TQ == 0 and n_lat % GRID_W == 0
    alpha = (2 * depth) ** 0.25

    tm_proj = _largest_tile(T, 2112, 16)
    tn_proj = 1536
    tk_att = _largest_tile(T, ATT_TK_MAX, ATT_TQ)
    assert ctx_len <= tk_att
    tm_prep = tk_att
    tm_merge_lat = _largest_tile(n_lat, 512, SGU_CHUNK)
    tm_merge_ctx = _largest_tile(ctx_len, 512, SGU_CHUNK)

    x_ctx, x_lat = ctx, x
    cc = jnp.zeros((8, D), F32).at[0:B].set(c).at[B].set(c_ctx)
    mod = _ada(cc, w_ada, b_ada.reshape(depth, 1, 3 * D))
    cos_t, sin_t = _rope_tables(n_lat, ctx_len)
    consts = {
        "gq": jnp.kron(jnp.eye(N_HEADS, dtype=F32), jnp.ones((HEAD_DIM, HEAD_DIM), F32)).astype(BF16),
        "gk": jnp.kron(jnp.eye(N_KV, dtype=F32), jnp.ones((HEAD_DIM, HEAD_DIM), F32)).astype(BF16),
        "eq": jnp.eye(ATTN_W, dtype=BF16),
        "ek": jnp.eye(KV_W, dtype=BF16),
        "pq": jnp.kron(jnp.eye(ATTN_W // ROPE_AXIS, dtype=F32),
                       jnp.kron(jnp.array([[0.0, 1.0], [1.0, 0.0]], F32),
                                jnp.eye(ROPE_AXIS // 2, dtype=F32))).astype(BF16),
    }
    eye_tq = jnp.eye(ATT_TQ, dtype=BF16)
    w_all = _regroup_w_in(w_in)

    for l in range(depth):
        last = l == depth - 1
        modb = jnp.zeros((B, 8, 3 * D), F32).at[:, 0].set(mod[l, 0:B]).at[:, 1].set(mod[l, B][None])
        proj = _inproj(x_lat, x_ctx, modb, w_all, l, tm_proj, tn_proj)
        qn = jnp.concatenate([attn_q_norm[l]] * 2)[None]
        kn = jnp.concatenate([attn_k_norm[l]] * 2)[None]
        qt, kk, vt = _qkprep(proj, cos_t, sin_t, qn, kn, consts, tm_prep, tk_att)
        score_bound = HEAD_DIM * jnp.max(jnp.abs(attn_q_norm[l])) * jnp.max(jnp.abs(attn_k_norm[l])) * Q_SCALE
        bounded = (score_bound <= SCORE_LIMIT).astype(jnp.int32).reshape(1)
        ctx_blocks, lat_blocks = ctx_len // ATT_TQ, n_lat // ATT_TQ
        ya_lat = _attention(bounded, qt, kk, vt, proj, eye_tq, ctx_blocks, lat_blocks, ATT_NSB, T // tk_att, tk_att)
        ya_ctx = None if last else _attention(bounded, qt, kk, vt, proj, eye_tq, 0, ctx_blocks, 1, 1, ctx_len)
        a2f = jnp.zeros((LANES, GLA_KW), F32).at[0:GLA_RANK].set(gla_a2_f[l]).astype(BF16)
        a2b = jnp.zeros((LANES, GLA_KW), F32).at[GLA_RANK:2 * GLA_RANK].set(gla_a2_b[l]).astype(BF16)
        of, ob = _gla(proj, a2f, gla_ab_f[l][None], a2b, gla_ab_b[l][None], ctx_len)
        p = {
            "onorm": jnp.concatenate([gla_o_norm[l]] * GLA_HEADS)[None],
            "lng": sgu_ln_g[l][None], "lnb": sgu_ln_b[l][None],
            "sw": sgu_w[l].astype(BF16),
            "sb": jnp.repeat(sgu_b[l].T, SGU_W // SGU_GROUPS, axis=1),
            "wa": (0.5 * w_br_attn[l]).astype(BF16), "wg": (0.5 * w_br_gla[l]).astype(BF16),
            "ws": (0.5 * w_br_sgu[l]).astype(BF16),
            "wo": w_out[l].astype(BF16), "pg": post_ln_g[l][None], "pb": post_ln_b[l][None],
        }
        if not last:
            x_ctx = _merge(x_ctx, ya_ctx, 0, 1, modb, proj, of, ob, p, tm_merge_ctx, alpha)
        x_lat = _merge(x_lat, ya_lat, ctx_len, 0, modb, proj, of, ob, p, tm_merge_lat, alpha)
    return x_lat
```
